```python
import math
import jax, jax.numpy as jnp
from jax import lax
import numpy as np

D_MODEL = 1024
BATCH = 8
SEQ = 2048
DEPTH = 1
DEC_BATCH = 128
DEC_SEQ = 1
PAST_LEN = 16384
PAGE_SIZE = 128

RWKV_WIDTH = D_MODEL // 2
RWKV_HEAD = 64
RWKV_HEADS = RWKV_WIDTH // RWKV_HEAD
DECAY_LORA = 32
AAA_LORA = 32
GATE_LORA = 64
LNX_EPS = 64e-5
S5_WIDTH = D_MODEL // 2
S5_GROUP = 16
S5_GROUPS = S5_WIDTH // S5_GROUP
S5_STATE = 64
D_FF = ((8 * D_MODEL // 3 + 255) // 256) * 256
NORM_EPS = 1e-6
SHIFT_COLS = 3 * RWKV_WIDTH + DECAY_LORA + AAA_LORA + GATE_LORA
PROJ_COLS = SHIFT_COLS + S5_WIDTH + 2 * D_MODEL

kernel_name = "rwkv7_s5_gated_parallel_decode_step"


def _rms_norm(x, g):
    xf = x.astype(jnp.float32)
    y = xf * lax.rsqrt(jnp.mean(xf * xf, axis=-1, keepdims=True) + NORM_EPS)
    return (y * g.astype(jnp.float32)).astype(x.dtype)


def _rwkv7_mix(xr, wkv0, w0, w_decay_up, a0, w_aaa_up, w_gate_up, k_k, k_a, r_k, lnx_g, lnx_b):
    Bn, T, _ = xr.shape
    H, N = RWKV_HEADS, RWKV_HEAD
    c0 = 3 * RWKV_WIDTH
    r = xr[..., :RWKV_WIDTH]
    k = xr[..., RWKV_WIDTH:2 * RWKV_WIDTH]
    v = xr[..., 2 * RWKV_WIDTH:c0]
    w_lo = xr[..., c0:c0 + DECAY_LORA]
    a_lo = xr[..., c0 + DECAY_LORA:c0 + DECAY_LORA + AAA_LORA]
    g_lo = xr[..., c0 + DECAY_LORA + AAA_LORA:]
    w = -jax.nn.softplus(-(w0 + jnp.tanh(w_lo) @ w_decay_up)) - 0.5
    decay = jnp.exp(-jnp.exp(w))
    a = jax.nn.sigmoid(a0 + a_lo @ w_aaa_up)
    g = jax.nn.sigmoid(g_lo) @ w_gate_up
    heads = lambda t: t.reshape(Bn, T, H, N)
    kk = heads(k * k_k)
    kk = kk * lax.rsqrt(jnp.maximum(jnp.sum(kk * kk, axis=-1, keepdims=True), 1e-24))
    k = heads(k * (1.0 + (a - 1.0) * k_a))
    r, v, decay, a = heads(r), heads(v), heads(decay), heads(a)

    def step(S, inp):
        r_t, w_t, k_t, v_t, kk_t, b_t = inp
        sa = jnp.einsum('bhvk,bhk->bhv', S, -kk_t)
        S = S * w_t[:, :, None, :] + sa[..., :, None] * b_t[..., None, :] + v_t[..., :, None] * k_t[..., None, :]
        return S, jnp.einsum('bhvk,bhk->bhv', S, r_t)

    tm = lambda t: jnp.moveaxis(t, 1, 0)
    wkv1, o = lax.scan(step, wkv0.astype(jnp.float32), (tm(r), tm(decay), tm(k), tm(v), tm(kk), tm(kk * a)))
    o = jnp.moveaxis(o, 0, 1)
    mu = jnp.mean(o, axis=-1, keepdims=True)
    var = jnp.mean(jnp.square(o - mu), axis=-1, keepdims=True)
    o = ((o - mu) * lax.rsqrt(var + LNX_EPS)).reshape(Bn, T, RWKV_WIDTH) * lnx_g + lnx_b
    bonus = jnp.sum(r * k * r_k.reshape(H, N), axis=-1, keepdims=True) * v
    o = (o + bonus.reshape(Bn, T, RWKV_WIDTH)) * g
    return o, wkv1


def _s5_mix(u, re0, im0, lam_re, lam_im, log_dt, b_re, b_im, c_re, c_im, d_skip, glu_w1, glu_b1, glu_w2, glu_b2):
    Bn, T, _ = u.shape
    lam_re = lam_re.astype(jnp.float32)
    lam_im = lam_im.astype(jnp.float32)
    dt = jnp.exp(log_dt.astype(jnp.float32))[:, None]
    mag = jnp.exp(lam_re * dt)
    ang = lam_im * dt
    lb_re, lb_im = mag * jnp.cos(ang), mag * jnp.sin(ang)
    nr, ni = lb_re - 1.0, lb_im
    den = lam_re * lam_re + lam_im * lam_im
    f_re = (nr * lam_re + ni * lam_im) / den
    f_im = (ni * lam_re - nr * lam_im) / den
    bb_re = f_re[..., None] * b_re - f_im[..., None] * b_im
    bb_im = f_re[..., None] * b_im + f_im[..., None] * b_re
    ug = u.reshape(Bn, T, S5_GROUPS, S5_GROUP)
    bu_re = jnp.einsum('gpc,btgc->btgp', bb_re, ug)
    bu_im = jnp.einsum('gpc,btgc->btgp', bb_im, ug)
    re0 = re0.astype(jnp.float32)
    im0 = im0.astype(jnp.float32)
    bu_re = bu_re.at[:, 0].add(lb_re * re0 - lb_im * im0)
    bu_im = bu_im.at[:, 0].add(lb_re * im0 + lb_im * re0)
    a_re = jnp.broadcast_to(lb_re, bu_re.shape)
    a_im = jnp.broadcast_to(lb_im, bu_im.shape)

    def combine(e1, e2):
        a1r, a1i, b1r, b1i = e1
        a2r, a2i, b2r, b2i = e2
        return (a1r * a2r - a1i * a2i, a1r * a2i + a1i * a2r,
                a2r * b1r - a2i * b1i + b2r, a2r * b1i + a2i * b1r + b2i)

    _, _, xs_re, xs_im = lax.associative_scan(combine, (a_re, a_im, bu_re, bu_im), axis=1)
    y = jnp.einsum('gcp,btgp->btgc', c_re, xs_re) - jnp.einsum('gcp,btgp->btgc', c_im, xs_im)
    y = y.reshape(Bn, T, S5_WIDTH) + d_skip * u
    hgl = jax.nn.gelu(y)
    out = (hgl @ glu_w1 + glu_b1) * jax.nn.sigmoid(hgl @ glu_w2 + glu_b2)
    return out, xs_re[:, -1], xs_im[:, -1]


def _layer(x, shift0, wkv0, s5re0, s5im0, norm_pre_mix, norm_post_mix, norm_pre_ffn, norm_post_ffn,
           w_in, mu_shift, w0, w_decay_up, a0, w_aaa_up, w_gate_up, k_k, k_a, r_k, lnx_g, lnx_b, w_rwkv_out,
           s5_lam_re, s5_lam_im, s5_log_dt, s5_b_re, s5_b_im, s5_c_re, s5_c_im, s5_d,
           glu_w1, glu_b1, glu_w2, glu_b2, w_merge_out, w_ffn_gate, w_ffn_up, w_ffn_down):
    h = _rms_norm(x, norm_pre_mix)
    p = (h @ w_in).astype(jnp.float32)
    pr = p[..., :SHIFT_COLS]
    pr_prev = jnp.concatenate([shift0.astype(jnp.float32)[:, None, :], pr[:, :-1]], axis=1)
    xr = pr + (pr_prev - pr) * mu_shift
    oa, wkv1 = _rwkv7_mix(xr, wkv0, w0, w_decay_up, a0, w_aaa_up, w_gate_up, k_k, k_a, r_k, lnx_g, lnx_b)
    u = p[..., SHIFT_COLS:SHIFT_COLS + S5_WIDTH]
    ob, re1, im1 = _s5_mix(u, s5re0, s5im0, s5_lam_re, s5_lam_im, s5_log_dt, s5_b_re, s5_b_im,
                           s5_c_re, s5_c_im, s5_d, glu_w1, glu_b1, glu_w2, glu_b2)
    g_off = SHIFT_COLS + S5_WIDTH
    gate_a = jax.nn.sigmoid(p[..., g_off:g_off + D_MODEL])
    gate_b = jax.nn.sigmoid(p[..., g_off + D_MODEL:g_off + 2 * D_MODEL])
    merged = gate_a * (oa @ w_rwkv_out) + gate_b * ob
    mix = merged @ w_merge_out
    x = x + _rms_norm(mix, norm_post_mix).astype(x.dtype)
    h = _rms_norm(x, norm_pre_ffn)
    f = (jax.nn.silu(h @ w_ffn_gate) * (h @ w_ffn_up)) @ w_ffn_down
    x = x + _rms_norm(f, norm_post_ffn).astype(x.dtype)
    return x, pr[:, -1], wkv1, re1, im1


def setup_inputs(seed: int = 0) -> dict:
    key = jax.random.key(seed)
    ks = iter(jax.random.split(key, 64))
    L = DEPTH

    def nrm(shape, s):
        return jax.random.normal(next(ks), shape, jnp.float32) * s

    def unif(shape, lo, hi):
        return jax.random.uniform(next(ks), shape, jnp.float32, lo, hi)

    ramp = jnp.arange(RWKV_WIDTH, dtype=jnp.float32) / (RWKV_WIDTH - 1)
    n_state = jnp.arange(S5_STATE, dtype=jnp.float32)
    return {
        "x_prompt": nrm((BATCH, SEQ, D_MODEL), 1.0),
        "x_sample": nrm((DEC_BATCH, DEC_SEQ, D_MODEL), 1.0),
        "state_shift": nrm((L, DEC_BATCH, SHIFT_COLS), 1.0),
        "state_wkv": nrm((L, DEC_BATCH, RWKV_HEADS, RWKV_HEAD, RWKV_HEAD), 0.3),
        "state_s5_re": nrm((L, DEC_BATCH, S5_GROUPS, S5_STATE), 0.5),
        "state_s5_im": nrm((L, DEC_BATCH, S5_GROUPS, S5_STATE), 0.5),
        "norm_pre_mix": 1.0 + nrm((L, D_MODEL), 0.05),
        "norm_post_mix": 1.0 + nrm((L, D_MODEL), 0.05),
        "norm_pre_ffn": 1.0 + nrm((L, D_MODEL), 0.05),
        "norm_post_ffn": 1.0 + nrm((L, D_MODEL), 0.05),
        "w_in": nrm((L, D_MODEL, PROJ_COLS), D_MODEL ** -0.5),
        "mu_shift": unif((L, SHIFT_COLS), 0.0, 1.0),
        "w0": (-6.0 + 5.0 * ramp ** 0.7)[None, :] + nrm((L, RWKV_WIDTH), 0.1),
        "w_decay_up": nrm((L, DECAY_LORA, RWKV_WIDTH), 0.1),
        "a0": nrm((L, RWKV_WIDTH), 0.1),
        "w_aaa_up": nrm((L, AAA_LORA, RWKV_WIDTH), 0.1),
        "w_gate_up": nrm((L, GATE_LORA, RWKV_WIDTH), GATE_LORA ** -0.5),
        "k_k": 0.85 + nrm((L, RWKV_WIDTH), 0.05),
        "k_a": 1.0 + nrm((L, RWKV_WIDTH), 0.05),
        "r_k": nrm((L, RWKV_WIDTH), 0.1),
        "lnx_g": 1.0 + nrm((L, RWKV_WIDTH), 0.05),
        "lnx_b": nrm((L, RWKV_WIDTH), 0.01),
        "w_rwkv_out": nrm((L, RWKV_WIDTH, D_MODEL), RWKV_WIDTH ** -0.5),
        "s5_lam_re": -0.5 + nrm((L, S5_GROUPS, S5_STATE), 0.01),
        "s5_lam_im": (math.pi * n_state)[None, None, :] + nrm((L, S5_GROUPS, S5_STATE), 0.01),
        "s5_log_dt": unif((L, S5_GROUPS), math.log(1e-3), math.log(1e-1)),
        "s5_b_re": nrm((L, S5_GROUPS, S5_STATE, S5_GROUP), (2 * S5_GROUP) ** -0.5),
        "s5_b_im": nrm((L, S5_GROUPS, S5_STATE, S5_GROUP), (2 * S5_GROUP) ** -0.5),
        "s5_c_re": nrm((L, S5_GROUPS, S5_GROUP, S5_STATE), S5_STATE ** -0.5),
        "s5_c_im": nrm((L, S5_GROUPS, S5_GROUP, S5_STATE), S5_STATE ** -0.5),
        "s5_d": nrm((L, S5_WIDTH), 1.0),
        "glu_w1": nrm((L, S5_WIDTH, D_MODEL), S5_WIDTH ** -0.5),
        "glu_b1": nrm((L, D_MODEL), 0.01),
        "glu_w2": nrm((L, S5_WIDTH, D_MODEL), S5_WIDTH ** -0.5),
        "glu_b2": nrm((L, D_MODEL), 0.01),
        "w_merge_out": nrm((L, D_MODEL, D_MODEL), D_MODEL ** -0.5),
        "w_ffn_gate": nrm((L, D_MODEL, D_FF), D_MODEL ** -0.5),
        "w_ffn_up": nrm((L, D_MODEL, D_FF), D_MODEL ** -0.5),
        "w_ffn_down": nrm((L, D_FF, D_MODEL), D_FF ** -0.5),
    }


def reference(x_prompt, x_sample, state_shift, state_wkv, state_s5_re, state_s5_im,
              norm_pre_mix, norm_post_mix, norm_pre_ffn, norm_post_ffn, w_in, mu_shift,
              w0, w_decay_up, a0, w_aaa_up, w_gate_up, k_k, k_a, r_k, lnx_g, lnx_b, w_rwkv_out,
              s5_lam_re, s5_lam_im, s5_log_dt, s5_b_re, s5_b_im, s5_c_re, s5_c_im, s5_d,
              glu_w1, glu_b1, glu_w2, glu_b2, w_merge_out, w_ffn_gate, w_ffn_up, w_ffn_down):
    yp, ys = x_prompt, x_sample
    sh_p, wkv_p, re_p, im_p = [], [], [], []
    sh_s, wkv_s, re_s, im_s = [], [], [], []
    for l in range(DEPTH):
        lw = (norm_pre_mix[l], norm_post_mix[l], norm_pre_ffn[l], norm_post_ffn[l], w_in[l], mu_shift[l],
              w0[l], w_decay_up[l], a0[l], w_aaa_up[l], w_gate_up[l], k_k[l], k_a[l], r_k[l], lnx_g[l], lnx_b[l],
              w_rwkv_out[l], s5_lam_re[l], s5_lam_im[l], s5_log_dt[l], s5_b_re[l], s5_b_im[l], s5_c_re[l],
              s5_c_im[l], s5_d[l], glu_w1[l], glu_b1[l], glu_w2[l], glu_b2[l], w_merge_out[l],
              w_ffn_gate[l], w_ffn_up[l], w_ffn_down[l])
        zp_shift = jnp.zeros((x_prompt.shape[0], SHIFT_COLS), jnp.float32)
        zp_wkv = jnp.zeros((x_prompt.shape[0], RWKV_HEADS, RWKV_HEAD, RWKV_HEAD), jnp.float32)
        zp_s5 = jnp.zeros((x_prompt.shape[0], S5_GROUPS, S5_STATE), jnp.float32)
        yp, a1, a2, a3, a4 = _layer(yp, zp_shift, zp_wkv, zp_s5, zp_s5, *lw)
        sh_p.append(a1); wkv_p.append(a2); re_p.append(a3); im_p.append(a4)
        ys, b1, b2, b3, b4 = _layer(ys, state_shift[l], state_wkv[l], state_s5_re[l], state_s5_im[l], *lw)
        sh_s.append(b1); wkv_s.append(b2); re_s.append(b3); im_s.append(b4)
    dt_p, dt_s = x_prompt.dtype, x_sample.dtype
    new_shift_prompt = jnp.stack(sh_p).astype(dt_p)
    new_wkv_prompt = jnp.stack(wkv_p).astype(dt_p)
    new_s5_re_prompt = jnp.stack(re_p).astype(dt_p)
    new_s5_im_prompt = jnp.stack(im_p).astype(dt_p)
    new_shift_sample = jnp.stack(sh_s).astype(dt_s)
    new_wkv_sample = jnp.stack(wkv_s).astype(dt_s)
    new_s5_re_sample = jnp.stack(re_s).astype(dt_s)
    new_s5_im_sample = jnp.stack(im_s).astype(dt_s)
    return (yp, ys, new_shift_prompt, new_wkv_prompt, new_s5_re_prompt, new_s5_im_prompt,
            new_shift_sample, new_wkv_sample, new_s5_re_sample, new_s5_im_sample)
```

```python
import functools
import math

import jax
import jax.numpy as jnp
from jax import lax
from jax.experimental import pallas as pl
from jax.experimental.pallas import tpu as pltpu

F32 = jnp.float32
BF16 = jnp.bfloat16

NORM_EPS = 1e-6
LNX_EPS = 64e-5
HEAD = 64
PAIR = 2 * HEAD
LORA_PAD = 128
S5_GROUP = 16
S5_STATE = 64
SLAB_GROUPS = 8
SUBLANES = 8
LANES = 128
VMEM_LIMIT = 56 * 1024 * 1024

NN = (((1,), (0,)), ((), ()))
NT = (((1,), (1,)), ((), ()))


def _dot(a, b, dims=NN):
    return lax.dot_general(a, b, dims, preferred_element_type=F32)


def _split2(x):
    hi = x.astype(BF16)
    lo = (x - hi.astype(F32)).astype(BF16)
    return hi, lo


def _split3(x):
    hi = x.astype(BF16)
    r1 = x - hi.astype(F32)
    mid = r1.astype(BF16)
    lo = (r1 - mid.astype(F32)).astype(BF16)
    return hi, mid, lo


def _mm1(a, b, dims=NN):
    return _dot(a.astype(BF16), b.astype(BF16), dims)


def _mm3(a, b, dims=NN):
    ah, al = _split2(a)
    bh, bl = _split2(b)
    return _dot(ah, bh, dims) + (_dot(ah, bl, dims) + _dot(al, bh, dims))


def _mm_exact_lhs(a_bf16, b):
    h, m, l = _split3(b)
    return _dot(a_bf16, h) + (_dot(a_bf16, m) + _dot(a_bf16, l))


def _mm_exact_rhs(a, b_bf16):
    h, m, l = _split3(a)
    return _dot(h, b_bf16) + (_dot(m, b_bf16) + _dot(l, b_bf16))


def _rms(x, g):
    return x * lax.rsqrt(jnp.mean(x * x, axis=-1, keepdims=True) + NORM_EPS) * g


def _sigmoid(x):
    return 1.0 / (1.0 + jnp.exp(-x))


def _softplus(x):
    return jnp.maximum(x, 0.0) + jnp.log(1.0 + jnp.exp(-jnp.abs(x)))


def _gelu_tanh(x):
    c = math.sqrt(2.0 / math.pi)
    return 0.5 * x * (1.0 + jnp.tanh(c * (x + 0.044715 * (x * x * x))))


def _const_spec(shape, single_buffer=False):
    idx = lambda *_: (0,) * len(shape)
    if single_buffer:
        return pl.BlockSpec(shape, idx, pipeline_mode=pl.Buffered(1))
    return pl.BlockSpec(shape, idx)


def _proj_kernel(x_ref, g_ref, w_ref, pr_ref, u_ref, gt_ref, *, c_shift, c_u):
    hb = _rms(x_ref[...], g_ref[...]).astype(BF16)
    pr_ref[...] = _dot(hb, w_ref[:, :c_shift])
    u_ref[...] = _dot(hb, w_ref[:, c_shift:c_shift + c_u])
    gt_ref[...] = _dot(hb, w_ref[:, c_shift + c_u:])


def _proj(x2d, g, w_in_bf16, c_shift, c_u, tm):
    rows, d = x2d.shape
    cols = w_in_bf16.shape[1]
    c_g = cols - c_shift - c_u
    row = lambda w: pl.BlockSpec((tm, w), lambda i: (i, 0))
    return pl.pallas_call(
        functools.partial(_proj_kernel, c_shift=c_shift, c_u=c_u),
        grid=(rows // tm,),
        in_specs=[row(d), _const_spec((1, d)), _const_spec((d, cols), True)],
        out_specs=[row(c_shift), row(c_u), row(c_g)],
        out_shape=[jax.ShapeDtypeStruct((rows, c_shift), F32),
                   jax.ShapeDtypeStruct((rows, c_u), F32),
                   jax.ShapeDtypeStruct((rows, c_g), F32)],
        compiler_params=pltpu.CompilerParams(dimension_semantics=("parallel",),
                                             vmem_limit_bytes=VMEM_LIMIT),
        name="proj",
    )(x2d, g, w_in_bf16)


def _head_sum(x, e_ref):
    return _mm_exact_rhs(x, e_ref[...])


def _rwkv_token_prep(xr, w, e_ref):
    width = w["w0"].shape[-1]
    r = xr[:, :width]
    k = xr[:, width:2 * width]
    v = xr[:, 2 * width:3 * width]
    lo = xr[:, 3 * width:3 * width + LORA_PAD]
    wl = w["w0"][...] + _mm1(jnp.tanh(lo), w["wd"][...])
    lw = -jnp.exp(-_softplus(-wl) - 0.5)
    a = _sigmoid(w["a0"][...] + _mm1(lo, w["wa"][...]))
    g = _mm1(_sigmoid(lo), w["wg"][...])
    kk = k * w["k_k"][...]
    kk = kk * lax.rsqrt(jnp.maximum(_head_sum(kk * kk, e_ref), 1e-24))
    kmod = k * (1.0 + (a - 1.0) * w["k_a"][...])
    return r, kmod, v, kk, kk * a, lw, g


def _rwkv_post(o, r, kmod, v, g, w, e_ref):
    inv_n = 1.0 / HEAD
    mu = _head_sum(o, e_ref) * inv_n
    oc = o - mu
    var = _head_sum(oc * oc, e_ref) * inv_n
    on = oc * lax.rsqrt(var + LNX_EPS) * w["lnx_g"][...] + w["lnx_b"][...]
    bonus = _head_sum(r * kmod * w["r_k"][...], e_ref) * v
    return (on + bonus) * g


_RWKV_W_NAMES = ("mu", "w0", "a0", "k_k", "k_a", "r_k", "lnx_g", "lnx_b", "wd", "wa", "wg")


def _tri_inverse(a_strict_lower, n):
    rows = lax.broadcasted_iota(jnp.int32, (n, n), 0)
    cols = lax.broadcasted_iota(jnp.int32, (n, n), 1)
    eye = jnp.where(rows == cols, 1.0, 0.0).astype(F32)
    p = -a_strict_lower
    t = eye + p
    covered = 2
    while covered < n:
        p = _mm3(p, p)
        t = t + _mm3(t, p)
        covered *= 2
    return t


def _rwkv_chunk_kernel(pr_ref, shift0_ref, s0_ref, *rest, chunk, width):
    n_w = len(_RWKV_W_NAMES)
    w = dict(zip(_RWKV_W_NAMES, rest[:n_w]))
    e_ref, tri_ref = rest[n_w], rest[n_w + 1]
    oa_ref, s1_ref = rest[n_w + 2], rest[n_w + 3]
    carry_ref, z_ref = rest[n_w + 4], rest[n_w + 5]
    c = pl.program_id(1)
    n_pairs = width // PAIR
    C = chunk

    lane_in_pair = lax.broadcasted_iota(jnp.int32, (1, PAIR), 1)
    head_masks = (lane_in_pair < HEAD, lane_in_pair >= HEAD)

    @pl.when(c == 0)
    def _():
        carry_ref[...] = shift0_ref[0]
        zero = jnp.zeros((HEAD, HEAD), F32)
        for p in range(n_pairs):
            top = jnp.concatenate([s0_ref[0, 2 * p], zero], axis=1)
            bot = jnp.concatenate([zero, s0_ref[0, 2 * p + 1]], axis=1)
            z_ref[p] = jnp.concatenate([top, bot], axis=0)

    pr = pr_ref[0]
    row_id = lax.broadcasted_iota(jnp.int32, pr.shape, 0)
    pr_prev = jnp.where(row_id == 0, carry_ref[...], pltpu.roll(pr, 1, axis=0))
    carry_ref[...] = pr[C - 1:C, :]
    xr = pr + (pr_prev - pr) * w["mu"][...]
    r, kmod, v, kk, bvec, lw, g = _rwkv_token_prep(xr, w, e_ref)

    cum = _mm_exact_lhs(tri_ref[...], lw)
    cum_last = cum[C - 1:C, :]
    g_in = jnp.exp(cum)
    g_ex = jnp.exp(cum - lw)
    g_neg = jnp.exp(-cum)
    g_end = jnp.exp(cum_last - cum)
    g_all = jnp.exp(cum_last)
    r_t = r * g_in
    k_t = kmod * g_neg
    b_t = bvec * g_neg
    kap_t = kk * g_ex
    k_end = kmod * g_end
    b_end = bvec * g_end

    ri = lax.broadcasted_iota(jnp.int32, (C, C), 0)
    ci = lax.broadcasted_iota(jnp.int32, (C, C), 1)
    strict = ri > ci
    incl = ri >= ci
    bd_rows = lax.broadcasted_iota(jnp.int32, (PAIR, PAIR), 0) < HEAD
    bd_cols = lax.broadcasted_iota(jnp.int32, (PAIR, PAIR), 1) < HEAD
    block_diag = bd_rows == bd_cols

    outs = []
    for p in range(n_pairs):
        sl = slice(p * PAIR, (p + 1) * PAIR)
        vp = v[:, sl]
        left = jnp.concatenate([kap_t[:, sl], r_t[:, sl]], axis=0)
        a_rk, a_rb, t_inv, a_k = [], [], [], []
        for m in head_masks:
            lm = jnp.where(m, left, 0.0)
            gk = _mm3(lm, k_t[:, sl], NT)
            gb = _mm3(lm, b_t[:, sl], NT)
            a_k.append(jnp.where(strict, gk[:C], 0.0))
            a_rk.append(jnp.where(incl, gk[C:], 0.0))
            a_rb.append(jnp.where(incl, gb[C:], 0.0))
            t_inv.append(_tri_inverse(jnp.where(strict, gb[:C], 0.0), C))

        def per_head(mats, rhs):
            m0 = head_masks[0] if rhs.shape[1] == PAIR else jnp.concatenate([head_masks[0]] * (rhs.shape[1] // PAIR), axis=1)
            return jnp.where(m0, _mm3(mats[0], rhs), _mm3(mats[1], rhs))

        akv = per_head(a_k, vp)
        tw = per_head(t_inv, jnp.concatenate([kap_t[:, sl], akv], axis=1))
        k_hat, v_hat = tw[:, :PAIR], tw[:, PAIR:]
        arkv = per_head(a_rk, vp)

        z = z_ref[p]
        pz = _mm3(jnp.concatenate([k_hat, r_t[:, sl]], axis=0), z, NT)
        u = pz[:C] + v_hat
        o = pz[C:] + arkv - per_head(a_rb, u)
        outs.append(o)
        vu_t = jnp.concatenate([vp, u], axis=0).T
        kb = jnp.concatenate([k_end[:, sl], -b_end[:, sl]], axis=0)
        z_ref[p] = z * g_all[:, sl] + jnp.where(block_diag, _mm3(vu_t, kb), 0.0)

    o = jnp.concatenate(outs, axis=1)
    oa_ref[0] = _rwkv_post(o, r, kmod, v, g, w, e_ref).astype(oa_ref.dtype)

    @pl.when(c == pl.num_programs(1) - 1)
    def _():
        for p in range(n_pairs):
            z = z_ref[p]
            s1_ref[0, 2 * p] = z[:HEAD, :HEAD]
            s1_ref[0, 2 * p + 1] = z[HEAD:, HEAD:]


def _rwkv_weight_inputs(wts):
    return [wts[n] for n in _RWKV_W_NAMES]


def _rwkv_weight_specs(wts):
    return [_const_spec(wts[n].shape) for n in _RWKV_W_NAMES]


def _rwkv_chunked(pr3d, shift0, wkv0, wts, e_mat, chunk):
    bsz, t, c_shift = pr3d.shape
    heads = wkv0.shape[1]
    width = heads * HEAD
    tri = jnp.tril(jnp.ones((chunk, chunk), F32)).astype(BF16)
    kern = functools.partial(_rwkv_chunk_kernel, chunk=chunk, width=width)
    return pl.pallas_call(
        kern,
        grid=(bsz, t // chunk),
        in_specs=[pl.BlockSpec((1, chunk, c_shift), lambda b, c: (b, c, 0)),
                  pl.BlockSpec((1, 1, c_shift), lambda b, c: (b, 0, 0)),
                  pl.BlockSpec((1, heads, HEAD, HEAD), lambda b, c: (b, 0, 0, 0))]
                 + _rwkv_weight_specs(wts)
                 + [_const_spec(e_mat.shape), _const_spec(tri.shape)],
        out_specs=[pl.BlockSpec((1, chunk, width), lambda b, c: (b, c, 0)),
                   pl.BlockSpec((1, heads, HEAD, HEAD), lambda b, c: (b, 0, 0, 0))],
        out_shape=[jax.ShapeDtypeStruct((bsz, t, width), BF16),
                   jax.ShapeDtypeStruct((bsz, heads, HEAD, HEAD), F32)],
        scratch_shapes=[pltpu.VMEM((1, c_shift), F32),
                        pltpu.VMEM((width // PAIR, PAIR, PAIR), F32)],
        compiler_params=pltpu.CompilerParams(dimension_semantics=("parallel", "arbitrary"),
                                             vmem_limit_bytes=VMEM_LIMIT),
        name="rwkv_chunk",
    )(pr3d, shift0[:, None, :], wkv0, *_rwkv_weight_inputs(wts), e_mat, tri)


def _rwkv_step_prep_kernel(pr_ref, shift0_ref, *rest):
    n_w = len(_RWKV_W_NAMES)
    w = dict(zip(_RWKV_W_NAMES, rest[:n_w]))
    e_ref = rest[n_w]
    r_ref, k_ref, v_ref, kk_ref, b_ref, dec_ref, g_ref = rest[n_w + 1:]
    pr = pr_ref[...]
    xr = pr + (shift0_ref[...] - pr) * w["mu"][...]
    r, kmod, v, kk, bvec, lw, g = _rwkv_token_prep(xr, w, e_ref)
    r_ref[...] = r
    k_ref[...] = kmod
    v_ref[...] = v
    kk_ref[...] = kk
    b_ref[...] = bvec
    dec_ref[...] = jnp.exp(lw)
    g_ref[...] = g


def _rwkv_step_state_kernel(s_ref, kk_ref, b_ref, dec_ref, k_ref, r_ref, v_ref, s1_ref, o_ref):
    s = s_ref[...]
    sa = -jnp.sum(s * kk_ref[...], axis=-1, keepdims=True)
    s1 = s * dec_ref[...] + sa * b_ref[...] + v_ref[...] * k_ref[...]
    s1_ref[...] = s1
    o_ref[...] = jnp.sum(s1 * r_ref[...], axis=-1, keepdims=True)


def _rwkv_step_post_kernel(o_ref, r_ref, k_ref, v_ref, g_ref, *rest):
    n_w = len(_RWKV_W_NAMES)
    w = dict(zip(_RWKV_W_NAMES, rest[:n_w]))
    e_ref, oa_ref = rest[n_w], rest[n_w + 1]
    oa_ref[...] = _rwkv_post(o_ref[...], r_ref[...], k_ref[...], v_ref[...], g_ref[...], w, e_ref).astype(oa_ref.dtype)


def _rwkv_step(pr2d, shift0, wkv0, wts, e_mat, nblk):
    bsz, c_shift = pr2d.shape
    heads = wkv0.shape[1]
    width = heads * HEAD
    full = lambda a: _const_spec(a.shape)
    vec = jax.ShapeDtypeStruct((bsz, width), F32)
    cp = pltpu.CompilerParams(vmem_limit_bytes=VMEM_LIMIT)
    r, kmod, v, kk, bvec, dec, g = pl.pallas_call(
        _rwkv_step_prep_kernel,
        grid=(1,),
        in_specs=[full(pr2d), full(shift0)] + _rwkv_weight_specs(wts) + [full(e_mat)],
        out_specs=[_const_spec((bsz, width))] * 7,
        out_shape=[vec] * 7,
        compiler_params=cp,
        name="rwkv_step_prep",
    )(pr2d, shift0, *_rwkv_weight_inputs(wts), e_mat)

    n = bsz * heads
    rowv = lambda a: a.reshape(n, 1, HEAD)
    colv = lambda a: a.reshape(n, HEAD, 1)
    mat_spec = pl.BlockSpec((nblk, HEAD, HEAD), lambda i: (i, 0, 0))
    row_spec = pl.BlockSpec((nblk, 1, HEAD), lambda i: (i, 0, 0))
    col_spec = pl.BlockSpec((nblk, HEAD, 1), lambda i: (i, 0, 0))
    s1, o = pl.pallas_call(
        _rwkv_step_state_kernel,
        grid=(n // nblk,),
        in_specs=[mat_spec] + [row_spec] * 5 + [col_spec],
        out_specs=[mat_spec, col_spec],
        out_shape=[jax.ShapeDtypeStruct((n, HEAD, HEAD), F32), jax.ShapeDtypeStruct((n, HEAD, 1), F32)],
        compiler_params=pltpu.CompilerParams(dimension_semantics=("parallel",), vmem_limit_bytes=VMEM_LIMIT),
        name="rwkv_step_state",
    )(wkv0.reshape(n, HEAD, HEAD), rowv(kk), rowv(bvec), rowv(dec), rowv(kmod), rowv(r), colv(v))

    o2d = o.reshape(bsz, width)
    oa = pl.pallas_call(
        _rwkv_step_post_kernel,
        grid=(1,),
        in_specs=[_const_spec((bsz, width))] * 5 + _rwkv_weight_specs(wts) + [full(e_mat)],
        out_specs=_const_spec((bsz, width)),
        out_shape=jax.ShapeDtypeStruct((bsz, width), BF16),
        compiler_params=cp,
        name="rwkv_step_post",
    )(o2d, r, kmod, v, g, *_rwkv_weight_inputs(wts), e_mat)
    return oa, s1.reshape(wkv0.shape)


def _s5_disc_kernel(lre_ref, lim_ref, ldt_ref, bre_ref, bim_ref, lbr_ref, lbi_ref, bbr_ref, bbi_ref):
    lam_re, lam_im = lre_ref[...], lim_ref[...]
    dt = jnp.exp(ldt_ref[...])
    mag = jnp.exp(lam_re * dt)
    ang = lam_im * dt
    lb_re, lb_im = mag * jnp.cos(ang), mag * jnp.sin(ang)
    nr, ni = lb_re - 1.0, lb_im
    den = lam_re * lam_re + lam_im * lam_im
    f_re = (nr * lam_re + ni * lam_im) / den
    f_im = (ni * lam_re - nr * lam_im) / den
    b_re, b_im = bre_ref[...], bim_ref[...]
    lbr_ref[...] = lb_re
    lbi_ref[...] = lb_im
    bbr_ref[...] = f_re * b_re - f_im * b_im
    bbi_ref[...] = f_re * b_im + f_im * b_re


def _s5_discretise(lam_re, lam_im, log_dt, b_re, b_im):
    g, p = lam_re.shape
    full = lambda a: _const_spec(a.shape)
    args = (lam_re[..., None], lam_im[..., None], log_dt[:, None, None], b_re, b_im)
    return pl.pallas_call(
        _s5_disc_kernel,
        grid=(1,),
        in_specs=[full(a) for a in args],
        out_specs=[_const_spec((g, p, 1))] * 2 + [full(b_re)] * 2,
        out_shape=[jax.ShapeDtypeStruct((g, p, 1), F32)] * 2 + [jax.ShapeDtypeStruct(b_re.shape, F32)] * 2,
        name="s5_discretise",
    )(*args)


def _s5_kernel(u_ref, re0_ref, im0_ref, lbr_ref, lbi_ref, wb_ref, wc_ref, d_ref,
               h_ref, re1_ref, im1_ref, bu_ref, xs_ref, *, tt, n_slabs):
    t_blk = pl.program_id(1)
    rows = SUBLANES * tt
    s_w = SLAB_GROUPS * S5_STATE
    u_w = SLAB_GROUPS * S5_GROUP

    @pl.when(t_blk == 0)
    def _():
        re1_ref[...] = re0_ref[...]
        im1_ref[...] = im0_ref[...]

    u = u_ref[...].reshape(rows, n_slabs * u_w)
    n_lb = s_w // LANES
    lane_blk = lambda a, j: a[:, j * LANES:(j + 1) * LANES]
    ys = []
    for s in range(n_slabs):
        us = u[:, s * u_w:(s + 1) * u_w]
        bu = _mm1(us, wb_ref[s])
        for j in range(2 * n_lb):
            bu_ref[j] = lane_blk(bu, j)
        st = slice(s * s_w, (s + 1) * s_w)
        lbr = jnp.broadcast_to(lbr_ref[:, st], (SUBLANES, s_w))
        lbi = jnp.broadcast_to(lbi_ref[:, st], (SUBLANES, s_w))

        def step(t, carry):
            xr, xi = carry
            at_t = pl.ds(t, SUBLANES, stride=tt) if tt > 1 else pl.ds(0, SUBLANES)
            b_re = jnp.concatenate([bu_ref[j, at_t, :] for j in range(n_lb)], axis=1)
            b_im = jnp.concatenate([bu_ref[n_lb + j, at_t, :] for j in range(n_lb)], axis=1)
            nr = lbr * xr - lbi * xi + b_re
            ni = lbr * xi + lbi * xr + b_im
            for j in range(n_lb):
                xs_ref[j, at_t, :] = lane_blk(nr, j)
                xs_ref[n_lb + j, at_t, :] = lane_blk(ni, j)
            return nr, ni

        xr, xi = lax.fori_loop(0, tt, step, (re1_ref[:, st], im1_ref[:, st]), unroll=min(tt, 8))
        re1_ref[:, st] = xr
        im1_ref[:, st] = xi
        xs = jnp.concatenate([xs_ref[j] for j in range(2 * n_lb)], axis=1)
        y = _mm1(xs, wc_ref[s]) + d_ref[:, s * u_w:(s + 1) * u_w] * us
        ys.append(_gelu_tanh(y))
    h_ref[...] = jnp.concatenate(ys, axis=1).astype(h_ref.dtype).reshape(h_ref.shape)


def _s5(u_blocks, blk, re0, im0, lb_re, lb_im, wb, wc, d_skip, tt):
    n_slabs = wb.shape[0]
    width = u_blocks.shape[-1]
    n_state = re0.shape[1]
    grid = (u_blocks.shape[0] // blk[0], u_blocks.shape[1] // blk[1])
    rows = SUBLANES * tt
    u_spec = pl.BlockSpec(blk, lambda i, j: (i, j, 0))
    st_spec = pl.BlockSpec((SUBLANES, n_state), lambda i, j: (i, 0))
    full = lambda a: _const_spec(a.shape)
    return pl.pallas_call(
        functools.partial(_s5_kernel, tt=tt, n_slabs=n_slabs),
        grid=grid,
        in_specs=[u_spec, st_spec, st_spec, full(lb_re), full(lb_im), full(wb), full(wc), full(d_skip)],
        out_specs=[u_spec, st_spec, st_spec],
        out_shape=[jax.ShapeDtypeStruct(u_blocks.shape, BF16),
                   jax.ShapeDtypeStruct(re0.shape, F32), jax.ShapeDtypeStruct(im0.shape, F32)],
        scratch_shapes=[pltpu.VMEM((2 * SLAB_GROUPS * S5_STATE // LANES, rows, LANES), F32)] * 2,
        compiler_params=pltpu.CompilerParams(dimension_semantics=("parallel", "arbitrary"),
                                             vmem_limit_bytes=VMEM_LIMIT),
        name="s5_scan",
    )(u_blocks, re0, im0, lb_re, lb_im, wb, wc, d_skip)


def _block_diag_slabs(m):
    g, a, b = m.shape
    eye = jnp.eye(SLAB_GROUPS, dtype=m.dtype)
    m4 = m.reshape(g // SLAB_GROUPS, SLAB_GROUPS, a, b)
    return jnp.einsum("sgab,gh->sgahb", m4, eye).reshape(g // SLAB_GROUPS, SLAB_GROUPS * a, SLAB_GROUPS * b)


def _tail_kernel(x_ref, oa_ref, hg_ref, gt_ref, wro_ref, w1_ref, b1_ref, w2_ref, b2_ref, wmo_ref,
                 npm_ref, nf_ref, npf_ref, wg_ref, wu_ref, wd_ref, y_ref):
    d = x_ref.shape[-1]
    hg = hg_ref[...]
    a_out = _dot(oa_ref[...], wro_ref[...])
    b_out = (_dot(hg, w1_ref[...]) + b1_ref[...]) * _sigmoid(_dot(hg, w2_ref[...]) + b2_ref[...])
    merged = _sigmoid(gt_ref[:, :d]) * a_out + _sigmoid(gt_ref[:, d:]) * b_out
    mix = _dot(merged.astype(BF16), wmo_ref[...])
    x1 = x_ref[...] + _rms(mix, npm_ref[...])
    hb = _rms(x1, nf_ref[...]).astype(BF16)
    gate = _dot(hb, wg_ref[...])
    up = _dot(hb, wu_ref[...])
    f = _dot((gate * _sigmoid(gate) * up).astype(BF16), wd_ref[...])
    y_ref[...] = x1 + _rms(f, npf_ref[...])


def _tail(x2d, oa, hg, gates, tw, tm):
    rows, d = x2d.shape
    row = lambda a: pl.BlockSpec((tm, a.shape[1]), lambda i: (i, 0))
    wnames = ("wro", "w1", "b1", "w2", "b2", "wmo", "npm", "nf", "npf", "wg", "wu", "wd")
    wargs = [tw[n] for n in wnames]
    return pl.pallas_call(
        _tail_kernel,
        grid=(rows // tm,),
        in_specs=[row(x2d), row(oa), row(hg), row(gates)] + [_const_spec(a.shape, True) for a in wargs],
        out_specs=pl.BlockSpec((tm, d), lambda i: (i, 0)),
        out_shape=jax.ShapeDtypeStruct((rows, d), F32),
        compiler_params=pltpu.CompilerParams(dimension_semantics=("parallel",),
                                             vmem_limit_bytes=VMEM_LIMIT),
        name="tail",
    )(x2d, oa, hg, gates, *wargs)


def _pick_tile(rows, target):
    t = min(rows, target)
    assert rows % t == 0
    return t


def _layer(x, shift0, wkv0, re0, im0, lw, *, chunk, s5_tt, row_tile):
    bsz, t, d = x.shape
    rows = bsz * t
    c_shift = shift0.shape[-1]
    c_u = lw["d_skip"].shape[-1]
    tm = _pick_tile(rows, row_tile)
    x2d = x.reshape(rows, d)
    pr, u, gates = _proj(x2d, lw["norm_pre_mix"], lw["w_in"], c_shift, c_u, tm)

    if t == 1:
        oa, wkv1 = _rwkv_step(pr, shift0, wkv0, lw["rwkv"], lw["e_mat"], nblk=64)
        pr_last = pr
        u_blocks = u.reshape(bsz // SUBLANES, SUBLANES, c_u)
        blk = (1, SUBLANES, c_u)
    else:
        oa, wkv1 = _rwkv_chunked(pr.reshape(bsz, t, c_shift), shift0, wkv0, lw["rwkv"], lw["e_mat"], chunk)
        oa = oa.reshape(rows, -1)
        pr_last = pr.reshape(bsz, t, c_shift)[:, -1]
        u_blocks = u.reshape(bsz, t, c_u)
        blk = (SUBLANES, s5_tt, c_u)
    n_state = re0.shape[1] * re0.shape[2]
    hg, re1, im1 = _s5(u_blocks, blk, re0.reshape(bsz, n_state), im0.reshape(bsz, n_state),
                       lw["lb_re"], lw["lb_im"], lw["wb"], lw["wc"], lw["d_skip"],
                       tt=1 if t == 1 else s5_tt)
    y = _tail(x2d, oa, hg.reshape(rows, c_u), gates, lw["tail"], tm)
    return y.reshape(x.shape), pr_last, wkv1, re1.reshape(re0.shape), im1.reshape(im0.shape)


def _prepare_layer_weights(l, p):
    row = lambda a: a[l][None, :].astype(F32)
    width = p["w0"].shape[-1]
    n_dec, n_aaa, n_gate = p["w_decay_up"].shape[1], p["w_aaa_up"].shape[1], p["w_gate_up"].shape[1]
    assert n_dec + n_aaa + n_gate == LORA_PAD

    def lora_pad(wup, start):
        return jnp.zeros((LORA_PAD, width), F32).at[start:start + wup.shape[0]].set(wup).astype(BF16)

    rwkv = {
        "mu": row(p["mu_shift"]), "w0": row(p["w0"]), "a0": row(p["a0"]), "k_k": row(p["k_k"]),
        "k_a": row(p["k_a"]), "r_k": row(p["r_k"]), "lnx_g": row(p["lnx_g"]), "lnx_b": row(p["lnx_b"]),
        "wd": lora_pad(p["w_decay_up"][l], 0),
        "wa": lora_pad(p["w_aaa_up"][l], n_dec),
        "wg": lora_pad(p["w_gate_up"][l], n_dec + n_aaa),
    }
    head_id = jnp.arange(width) // HEAD
    e_mat = (head_id[:, None] == head_id[None, :]).astype(BF16)

    lb_re, lb_im, bb_re, bb_im = _s5_discretise(p["s5_lam_re"][l], p["s5_lam_im"][l], p["s5_log_dt"][l],
                                                p["s5_b_re"][l], p["s5_b_im"][l])
    n_state = lb_re.shape[0] * lb_re.shape[1]
    to_in = lambda bb: _block_diag_slabs(jnp.swapaxes(bb, 1, 2))
    to_out = lambda cc: _block_diag_slabs(jnp.swapaxes(cc, 1, 2))
    wb = jnp.concatenate([to_in(bb_re), to_in(bb_im)], axis=-1).astype(BF16)
    wc = jnp.concatenate([to_out(p["s5_c_re"][l]), -to_out(p["s5_c_im"][l])], axis=1).astype(BF16)

    bf = lambda a: a[l].astype(BF16)
    tail = {
        "wro": bf(p["w_rwkv_out"]), "w1": bf(p["glu_w1"]), "b1": row(p["glu_b1"]), "w2": bf(p["glu_w2"]),
        "b2": row(p["glu_b2"]), "wmo": bf(p["w_merge_out"]), "npm": row(p["norm_post_mix"]),
        "nf": row(p["norm_pre_ffn"]), "npf": row(p["norm_post_ffn"]),
        "wg": bf(p["w_ffn_gate"]), "wu": bf(p["w_ffn_up"]), "wd": bf(p["w_ffn_down"]),
    }
    return {
        "norm_pre_mix": row(p["norm_pre_mix"]), "w_in": bf(p["w_in"]), "rwkv": rwkv, "e_mat": e_mat,
        "lb_re": lb_re.reshape(1, n_state), "lb_im": lb_im.reshape(1, n_state), "wb": wb, "wc": wc,
        "d_skip": row(p["s5_d"]), "tail": tail,
    }


_PARAM_NAMES = ("norm_pre_mix", "norm_post_mix", "norm_pre_ffn", "norm_post_ffn", "w_in", "mu_shift",
                "w0", "w_decay_up", "a0", "w_aaa_up", "w_gate_up", "k_k", "k_a", "r_k", "lnx_g", "lnx_b",
                "w_rwkv_out", "s5_lam_re", "s5_lam_im", "s5_log_dt", "s5_b_re", "s5_b_im", "s5_c_re",
                "s5_c_im", "s5_d", "glu_w1", "glu_b1", "glu_w2", "glu_b2", "w_merge_out",
                "w_ffn_gate", "w_ffn_up", "w_ffn_down")


def _forward(x_prompt, x_sample, state_shift, state_wkv, state_s5_re, state_s5_im, params,
             *, chunk=64, s5_tt=128, row_tile=256):
    depth = params["w_in"].shape[0]
    heads = state_wkv.shape[2]
    bp = x_prompt.shape[0]
    yp, ys = x_prompt, x_sample
    outs_p, outs_s = [], []
    for l in range(depth):
        lw = _prepare_layer_weights(l, params)
        zp_shift = jnp.zeros((bp, state_shift.shape[-1]), F32)
        zp_wkv = jnp.zeros((bp, heads, HEAD, HEAD), F32)
        zp_s5 = jnp.zeros((bp,) + state_s5_re.shape[2:], F32)
        yp, *st_p = _layer(yp, zp_shift, zp_wkv, zp_s5, zp_s5, lw, chunk=chunk, s5_tt=s5_tt, row_tile=row_tile)
        ys, *st_s = _layer(ys, state_shift[l], state_wkv[l], state_s5_re[l], state_s5_im[l], lw,
                           chunk=chunk, s5_tt=s5_tt, row_tile=row_tile)
        outs_p.append(st_p)
        outs_s.append(st_s)
    stack = lambda outs, i, dt: jnp.stack([o[i] for o in outs]).astype(dt)
    dt_p, dt_s = x_prompt.dtype, x_sample.dtype
    return (yp, ys,
            stack(outs_p, 0, dt_p), stack(outs_p, 1, dt_p), stack(outs_p, 2, dt_p), stack(outs_p, 3, dt_p),
            stack(outs_s, 0, dt_s), stack(outs_s, 1, dt_s), stack(outs_s, 2, dt_s), stack(outs_s, 3, dt_s))


def kernel(x_prompt, x_sample, state_shift, state_wkv, state_s5_re, state_s5_im, norm_pre_mix, norm_post_mix, norm_pre_ffn, norm_post_ffn, w_in, mu_shift, w0, w_decay_up, a0, w_aaa_up, w_gate_up, k_k, k_a, r_k, lnx_g, lnx_b, w_rwkv_out, s5_lam_re, s5_lam_im, s5_log_dt, s5_b_re, s5_b_im, s5_c_re, s5_c_im, s5_d, glu_w1, glu_b1, glu_w2, glu_b2, w_merge_out, w_ffn_gate, w_ffn_up, w_ffn_down):
    params = dict(zip(_PARAM_NAMES, (norm_pre_mix, norm_post_mix, norm_pre_ffn, norm_post_ffn, w_in, mu_shift,
                                     w0, w_decay_up, a0, w_aaa_up, w_gate_up, k_k, k_a, r_k, lnx_g, lnx_b,
                                     w_rwkv_out, s5_lam_re, s5_lam_im, s5_log_dt, s5_b_re, s5_b_im, s5_c_re,
                                     s5_c_im, s5_d, glu_w1, glu_b1, glu_w2, glu_b2, w_merge_out,
                                     w_ffn_gate, w_ffn_up, w_ffn_down)))
    return _forward(x_prompt, x_sample, state_shift, state_wkv, state_s5_re, state_s5_im, params)
```

```python
import functools
import math

import jax
import jax.numpy as jnp
from jax import lax
from jax.experimental import pallas as pl
from jax.experimental.pallas import tpu as pltpu

F32 = jnp.float32
BF16 = jnp.bfloat16

NORM_EPS = 1e-6
LNX_EPS = 64e-5
HEAD = 64
GROUP_HEADS = 4
GROUP_W = GROUP_HEADS * HEAD
GRAM_PASSES = 1
INV_PASSES = 1
APPLY_PASSES = 1
STATE_PASSES = 1
LORA_PAD = 128
S5_GROUP = 16
S5_STATE = 64
SLAB_GROUPS = 8
SUBLANES = 8
LANES = 128
VMEM_LIMIT = 56 * 1024 * 1024

NN = (((1,), (0,)), ((), ()))
NT = (((1,), (1,)), ((), ()))


def _dot(a, b, dims=NN):
    return lax.dot_general(a, b, dims, preferred_element_type=F32)


def _split2(x):
    hi = x.astype(BF16)
    lo = (x - hi.astype(F32)).astype(BF16)
    return hi, lo


def _split3(x):
    hi = x.astype(BF16)
    r1 = x - hi.astype(F32)
    mid = r1.astype(BF16)
    lo = (r1 - mid.astype(F32)).astype(BF16)
    return hi, mid, lo


def _mm1(a, b, dims=NN):
    return _dot(a.astype(BF16), b.astype(BF16), dims)


def _mm3(a, b, dims=NN):
    ah, al = _split2(a)
    bh, bl = _split2(b)
    return _dot(ah, bh, dims) + (_dot(ah, bl, dims) + _dot(al, bh, dims))


def _mm_exact_lhs(a_bf16, b):
    h, m, l = _split3(b)
    return _dot(a_bf16, h) + (_dot(a_bf16, m) + _dot(a_bf16, l))


def _mm_exact_rhs(a, b_bf16):
    h, m, l = _split3(a)
    return _dot(h, b_bf16) + (_dot(m, b_bf16) + _dot(l, b_bf16))


def _rms(x, g):
    return x * lax.rsqrt(jnp.mean(x * x, axis=-1, keepdims=True) + NORM_EPS) * g


def _sigmoid(x):
    return 1.0 / (1.0 + jnp.exp(-x))


def _softplus(x):
    return jnp.maximum(x, 0.0) + jnp.log(1.0 + jnp.exp(-jnp.abs(x)))


def _gelu_tanh(x):
    c = math.sqrt(2.0 / math.pi)
    return 0.5 * x * (1.0 + jnp.tanh(c * (x + 0.044715 * (x * x * x))))


def _const_spec(shape, single_buffer=False):
    idx = lambda *_: (0,) * len(shape)
    if single_buffer:
        return pl.BlockSpec(shape, idx, pipeline_mode=pl.Buffered(1))
    return pl.BlockSpec(shape, idx)


def _proj_kernel(x_ref, g_ref, w_ref, pr_ref, u_ref, gt_ref, *, c_shift, c_u):
    hb = _rms(x_ref[...], g_ref[...]).astype(BF16)
    pr_ref[...] = _dot(hb, w_ref[:, :c_shift])
    u_ref[...] = _dot(hb, w_ref[:, c_shift:c_shift + c_u])
    gt_ref[...] = _dot(hb, w_ref[:, c_shift + c_u:])


def _proj(x2d, g, w_in_bf16, c_shift, c_u, tm):
    rows, d = x2d.shape
    cols = w_in_bf16.shape[1]
    c_g = cols - c_shift - c_u
    row = lambda w: pl.BlockSpec((tm, w), lambda i: (i, 0))
    return pl.pallas_call(
        functools.partial(_proj_kernel, c_shift=c_shift, c_u=c_u),
        grid=(rows // tm,),
        in_specs=[row(d), _const_spec((1, d)), _const_spec((d, cols), True)],
        out_specs=[row(c_shift), row(c_u), row(c_g)],
        out_shape=[jax.ShapeDtypeStruct((rows, c_shift), F32),
                   jax.ShapeDtypeStruct((rows, c_u), F32),
                   jax.ShapeDtypeStruct((rows, c_g), F32)],
        compiler_params=pltpu.CompilerParams(dimension_semantics=("parallel",),
                                             vmem_limit_bytes=VMEM_LIMIT),
        name="proj",
    )(x2d, g, w_in_bf16)


def _head_sum(x, e_ref):
    return _mm_exact_rhs(x, e_ref[...])


def _rwkv_token_prep(xr, w, e_ref):
    width = w["w0"].shape[-1]
    r = xr[:, :width]
    k = xr[:, width:2 * width]
    v = xr[:, 2 * width:3 * width]
    lo = xr[:, 3 * width:3 * width + LORA_PAD]
    wl = w["w0"][...] + _mm1(jnp.tanh(lo), w["wd"][...])
    lw = -jnp.exp(-_softplus(-wl) - 0.5)
    a = _sigmoid(w["a0"][...] + _mm1(lo, w["wa"][...]))
    g = _mm1(_sigmoid(lo), w["wg"][...])
    kk = k * w["k_k"][...]
    kk = kk * lax.rsqrt(jnp.maximum(_head_sum(kk * kk, e_ref), 1e-24))
    kmod = k * (1.0 + (a - 1.0) * w["k_a"][...])
    return r, kmod, v, kk, kk * a, lw, g


def _rwkv_post(o, r, kmod, v, g, w, e_ref):
    inv_n = 1.0 / HEAD
    mu = _head_sum(o, e_ref) * inv_n
    oc = o - mu
    var = _head_sum(oc * oc, e_ref) * inv_n
    on = oc * lax.rsqrt(var + LNX_EPS) * w["lnx_g"][...] + w["lnx_b"][...]
    bonus = _head_sum(r * kmod * w["r_k"][...], e_ref) * v
    return (on + bonus) * g


_RWKV_W_NAMES = ("mu", "w0", "a0", "k_k", "k_a", "r_k", "lnx_g", "lnx_b", "wd", "wa", "wg")


def _mm(a, b, dims=NN, passes=1):
    return _mm3(a, b, dims) if passes == 3 else _mm1(a, b, dims)


def _tri_inverse(a_strict_lower, nilpotency, passes):
    n = a_strict_lower.shape[0]
    rows = lax.broadcasted_iota(jnp.int32, (n, n), 0)
    cols = lax.broadcasted_iota(jnp.int32, (n, n), 1)
    eye = jnp.where(rows == cols, 1.0, 0.0).astype(F32)
    p = -a_strict_lower
    t = eye + p
    covered = 2
    while covered < nilpotency:
        p = _mm(p, p, NN, passes)
        t = t + _mm(t, p, NN, passes)
        covered *= 2
    return t


def _rwkv_chunk_kernel(pr_ref, shift0_ref, s0_ref, *rest, chunk, width):
    n_w = len(_RWKV_W_NAMES)
    w = dict(zip(_RWKV_W_NAMES, rest[:n_w]))
    e_ref, tri_ref = rest[n_w], rest[n_w + 1]
    oa_ref, s1_ref = rest[n_w + 2], rest[n_w + 3]
    carry_ref, z_ref = rest[n_w + 4], rest[n_w + 5]
    c = pl.program_id(1)
    n_groups = width // GROUP_W
    C = chunk
    GC = GROUP_HEADS * C

    lane_head = lax.broadcasted_iota(jnp.int32, (1, GROUP_W), 1) // HEAD
    head_masks = [lane_head == h for h in range(GROUP_HEADS)]

    def stack(x):
        return jnp.concatenate([jnp.where(m, x, 0.0) for m in head_masks], axis=0)

    def collapse(xs):
        out = xs[:C]
        for h in range(1, GROUP_HEADS):
            out = out + xs[h * C:(h + 1) * C]
        return out

    @pl.when(c == 0)
    def _():
        carry_ref[...] = shift0_ref[0]
        for gi in range(n_groups):
            z_ref[gi] = jnp.zeros((GROUP_W, GROUP_W), F32)
            for h in range(GROUP_HEADS):
                z_ref[gi, h * HEAD:(h + 1) * HEAD, h * HEAD:(h + 1) * HEAD] = s0_ref[0, gi * GROUP_HEADS + h]

    pr = pr_ref[0]
    row_id = lax.broadcasted_iota(jnp.int32, pr.shape, 0)
    pr_prev = jnp.where(row_id == 0, carry_ref[...], pltpu.roll(pr, 1, axis=0))
    carry_ref[...] = pr[C - 1:C, :]
    xr = pr + (pr_prev - pr) * w["mu"][...]
    r, kmod, v, kk, bvec, lw, g = _rwkv_token_prep(xr, w, e_ref)

    cum = _mm_exact_lhs(tri_ref[...], lw)
    cum_last = cum[C - 1:C, :]
    g_in = jnp.exp(cum)
    g_ex = jnp.exp(cum - lw)
    g_neg = jnp.exp(-cum)
    g_end = jnp.exp(cum_last - cum)
    g_all = jnp.exp(cum_last)
    r_t = r * g_in
    k_t = kmod * g_neg
    b_t = bvec * g_neg
    kap_t = kk * g_ex
    k_end = kmod * g_end
    b_end = bvec * g_end

    ri = lax.broadcasted_iota(jnp.int32, (GC, GC), 0)
    ci = lax.broadcasted_iota(jnp.int32, (GC, GC), 1)
    same_head = (ri // C) == (ci // C)
    strict = same_head & (ri > ci)
    incl = same_head & (ri >= ci)
    zi = lax.broadcasted_iota(jnp.int32, (GROUP_W, GROUP_W), 0) // HEAD
    zj = lax.broadcasted_iota(jnp.int32, (GROUP_W, GROUP_W), 1) // HEAD
    block_diag = zi == zj

    outs = []
    for gi in range(n_groups):
        sl = slice(gi * GROUP_W, (gi + 1) * GROUP_W)
        kap_s, v_s = stack(kap_t[:, sl]), stack(v[:, sl])
        left = jnp.concatenate([kap_s, stack(r_t[:, sl])], axis=0)
        right = jnp.concatenate([k_t[:, sl]] * GROUP_HEADS + [b_t[:, sl]] * GROUP_HEADS, axis=0)
        gram = _mm(left, right, NT, GRAM_PASSES)
        a_k = jnp.where(strict, gram[:GC, :GC], 0.0)
        a_b = jnp.where(strict, gram[:GC, GC:], 0.0)
        a_rk = jnp.where(incl, gram[GC:, :GC], 0.0)
        a_rb = jnp.where(incl, gram[GC:, GC:], 0.0)
        t_inv = _tri_inverse(a_b, C, INV_PASSES)

        akv = _mm(a_k, v_s, NN, APPLY_PASSES)
        tw = _mm(t_inv, jnp.concatenate([kap_s, akv], axis=1), NN, APPLY_PASSES)
        k_hat, v_hat = collapse(tw[:, :GROUP_W]), collapse(tw[:, GROUP_W:])
        arkv = collapse(_mm(a_rk, v_s, NN, APPLY_PASSES))

        z = z_ref[gi]
        pz = _mm(jnp.concatenate([k_hat, r_t[:, sl]], axis=0), z, NT, STATE_PASSES)
        u = pz[:C] + v_hat
        outs.append(pz[C:] + arkv - collapse(_mm(a_rb, stack(u), NN, APPLY_PASSES)))
        vu_t = jnp.concatenate([v[:, sl], u], axis=0).T
        kb = jnp.concatenate([k_end[:, sl], -b_end[:, sl]], axis=0)
        z_ref[gi] = z * g_all[:, sl] + jnp.where(block_diag, _mm(vu_t, kb, NN, STATE_PASSES), 0.0)

    o = jnp.concatenate(outs, axis=1)
    oa_ref[0] = _rwkv_post(o, r, kmod, v, g, w, e_ref).astype(oa_ref.dtype)

    @pl.when(c == pl.num_programs(1) - 1)
    def _():
        for gi in range(n_groups):
            for h in range(GROUP_HEADS):
                s1_ref[0, gi * GROUP_HEADS + h] = z_ref[gi, h * HEAD:(h + 1) * HEAD, h * HEAD:(h + 1) * HEAD]


def _rwkv_weight_inputs(wts):
    return [wts[n] for n in _RWKV_W_NAMES]


def _rwkv_weight_specs(wts):
    return [_const_spec(wts[n].shape) for n in _RWKV_W_NAMES]


def _rwkv_chunked(pr3d, shift0, wkv0, wts, e_mat, chunk):
    bsz, t, c_shift = pr3d.shape
    heads = wkv0.shape[1]
    width = heads * HEAD
    tri = jnp.tril(jnp.ones((chunk, chunk), F32)).astype(BF16)
    kern = functools.partial(_rwkv_chunk_kernel, chunk=chunk, width=width)
    return pl.pallas_call(
        kern,
        grid=(bsz, t // chunk),
        in_specs=[pl.BlockSpec((1, chunk, c_shift), lambda b, c: (b, c, 0)),
                  pl.BlockSpec((1, 1, c_shift), lambda b, c: (b, 0, 0)),
                  pl.BlockSpec((1, heads, HEAD, HEAD), lambda b, c: (b, 0, 0, 0))]
                 + _rwkv_weight_specs(wts)
                 + [_const_spec(e_mat.shape), _const_spec(tri.shape)],
        out_specs=[pl.BlockSpec((1, chunk, width), lambda b, c: (b, c, 0)),
                   pl.BlockSpec((1, heads, HEAD, HEAD), lambda b, c: (b, 0, 0, 0))],
        out_shape=[jax.ShapeDtypeStruct((bsz, t, width), BF16),
                   jax.ShapeDtypeStruct((bsz, heads, HEAD, HEAD), F32)],
        scratch_shapes=[pltpu.VMEM((1, c_shift), F32),
                        pltpu.VMEM((width // GROUP_W, GROUP_W, GROUP_W), F32)],
        compiler_params=pltpu.CompilerParams(dimension_semantics=("parallel", "arbitrary"),
                                             vmem_limit_bytes=VMEM_LIMIT),
        name="rwkv_chunk",
    )(pr3d, shift0[:, None, :], wkv0, *_rwkv_weight_inputs(wts), e_mat, tri)


def _rwkv_step_prep_kernel(pr_ref, shift0_ref, *rest):
    n_w = len(_RWKV_W_NAMES)
    w = dict(zip(_RWKV_W_NAMES, rest[:n_w]))
    e_ref = rest[n_w]
    r_ref, k_ref, v_ref, kk_ref, b_ref, dec_ref, g_ref = rest[n_w + 1:]
    pr = pr_ref[...]
    xr = pr + (shift0_ref[...] - pr) * w["mu"][...]
    r, kmod, v, kk, bvec, lw, g = _rwkv_token_prep(xr, w, e_ref)
    r_ref[...] = r
    k_ref[...] = kmod
    v_ref[...] = v
    kk_ref[...] = kk
    b_ref[...] = bvec
    dec_ref[...] = jnp.exp(lw)
    g_ref[...] = g


def _rwkv_step_state_kernel(s_ref, kk_ref, b_ref, dec_ref, k_ref, r_ref, v_ref, s1_ref, o_ref):
    s = s_ref[...]
    sa = -jnp.sum(s * kk_ref[...], axis=-1, keepdims=True)
    s1 = s * dec_ref[...] + sa * b_ref[...] + v_ref[...] * k_ref[...]
    s1_ref[...] = s1
    o_ref[...] = jnp.sum(s1 * r_ref[...], axis=-1, keepdims=True)


def _rwkv_step_post_kernel(o_ref, r_ref, k_ref, v_ref, g_ref, *rest):
    n_w = len(_RWKV_W_NAMES)
    w = dict(zip(_RWKV_W_NAMES, rest[:n_w]))
    e_ref, oa_ref = rest[n_w], rest[n_w + 1]
    oa_ref[...] = _rwkv_post(o_ref[...], r_ref[...], k_ref[...], v_ref[...], g_ref[...], w, e_ref).astype(oa_ref.dtype)


def _rwkv_step(pr2d, shift0, wkv0, wts, e_mat, nblk):
    bsz, c_shift = pr2d.shape
    heads = wkv0.shape[1]
    width = heads * HEAD
    full = lambda a: _const_spec(a.shape)
    vec = jax.ShapeDtypeStruct((bsz, width), F32)
    cp = pltpu.CompilerParams(vmem_limit_bytes=VMEM_LIMIT)
    r, kmod, v, kk, bvec, dec, g = pl.pallas_call(
        _rwkv_step_prep_kernel,
        grid=(1,),
        in_specs=[full(pr2d), full(shift0)] + _rwkv_weight_specs(wts) + [full(e_mat)],
        out_specs=[_const_spec((bsz, width))] * 7,
        out_shape=[vec] * 7,
        compiler_params=cp,
        name="rwkv_step_prep",
    )(pr2d, shift0, *_rwkv_weight_inputs(wts), e_mat)

    n = bsz * heads
    rowv = lambda a: a.reshape(n, 1, HEAD)
    colv = lambda a: a.reshape(n, HEAD, 1)
    mat_spec = pl.BlockSpec((nblk, HEAD, HEAD), lambda i: (i, 0, 0))
    row_spec = pl.BlockSpec((nblk, 1, HEAD), lambda i: (i, 0, 0))
    col_spec = pl.BlockSpec((nblk, HEAD, 1), lambda i: (i, 0, 0))
    s1, o = pl.pallas_call(
        _rwkv_step_state_kernel,
        grid=(n // nblk,),
        in_specs=[mat_spec] + [row_spec] * 5 + [col_spec],
        out_specs=[mat_spec, col_spec],
        out_shape=[jax.ShapeDtypeStruct((n, HEAD, HEAD), F32), jax.ShapeDtypeStruct((n, HEAD, 1), F32)],
        compiler_params=pltpu.CompilerParams(dimension_semantics=("parallel",), vmem_limit_bytes=VMEM_LIMIT),
        name="rwkv_step_state",
    )(wkv0.reshape(n, HEAD, HEAD), rowv(kk), rowv(bvec), rowv(dec), rowv(kmod), rowv(r), colv(v))

    o2d = o.reshape(bsz, width)
    oa = pl.pallas_call(
        _rwkv_step_post_kernel,
        grid=(1,),
        in_specs=[_const_spec((bsz, width))] * 5 + _rwkv_weight_specs(wts) + [full(e_mat)],
        out_specs=_const_spec((bsz, width)),
        out_shape=jax.ShapeDtypeStruct((bsz, width), BF16),
        compiler_params=cp,
        name="rwkv_step_post",
    )(o2d, r, kmod, v, g, *_rwkv_weight_inputs(wts), e_mat)
    return oa, s1.reshape(wkv0.shape)


def _s5_disc_kernel(lre_ref, lim_ref, ldt_ref, bre_ref, bim_ref, lbr_ref, lbi_ref, bbr_ref, bbi_ref):
    lam_re, lam_im = lre_ref[...], lim_ref[...]
    dt = jnp.exp(ldt_ref[...])
    mag = jnp.exp(lam_re * dt)
    ang = lam_im * dt
    lb_re, lb_im = mag * jnp.cos(ang), mag * jnp.sin(ang)
    nr, ni = lb_re - 1.0, lb_im
    den = lam_re * lam_re + lam_im * lam_im
    f_re = (nr * lam_re + ni * lam_im) / den
    f_im = (ni * lam_re - nr * lam_im) / den
    b_re, b_im = bre_ref[...], bim_ref[...]
    lbr_ref[...] = lb_re
    lbi_ref[...] = lb_im
    bbr_ref[...] = f_re * b_re - f_im * b_im
    bbi_ref[...] = f_re * b_im + f_im * b_re


def _s5_discretise(lam_re, lam_im, log_dt, b_re, b_im):
    g, p = lam_re.shape
    full = lambda a: _const_spec(a.shape)
    args = (lam_re[..., None], lam_im[..., None], log_dt[:, None, None], b_re, b_im)
    return pl.pallas_call(
        _s5_disc_kernel,
        grid=(1,),
        in_specs=[full(a) for a in args],
        out_specs=[_const_spec((g, p, 1))] * 2 + [full(b_re)] * 2,
        out_shape=[jax.ShapeDtypeStruct((g, p, 1), F32)] * 2 + [jax.ShapeDtypeStruct(b_re.shape, F32)] * 2,
        name="s5_discretise",
    )(*args)


def _s5_kernel(u_ref, re0_ref, im0_ref, lbr_ref, lbi_ref, wb_ref, wc_ref, d_ref,
               h_ref, re1_ref, im1_ref, bu_ref, xs_ref, *, tt, n_slabs):
    t_blk = pl.program_id(1)
    rows = SUBLANES * tt
    s_w = SLAB_GROUPS * S5_STATE
    u_w = SLAB_GROUPS * S5_GROUP

    @pl.when(t_blk == 0)
    def _():
        re1_ref[...] = re0_ref[...]
        im1_ref[...] = im0_ref[...]

    u = u_ref[...].reshape(rows, n_slabs * u_w)
    n_lb = s_w // LANES
    lane_blk = lambda a, j: a[:, j * LANES:(j + 1) * LANES]
    ys = []
    for s in range(n_slabs):
        us = u[:, s * u_w:(s + 1) * u_w]
        bu = _mm1(us, wb_ref[s])
        for j in range(2 * n_lb):
            bu_ref[j] = lane_blk(bu, j)
        st = slice(s * s_w, (s + 1) * s_w)
        lbr = jnp.broadcast_to(lbr_ref[:, st], (SUBLANES, s_w))
        lbi = jnp.broadcast_to(lbi_ref[:, st], (SUBLANES, s_w))

        def step(t, carry):
            xr, xi = carry
            at_t = pl.ds(t, SUBLANES, stride=tt) if tt > 1 else pl.ds(0, SUBLANES)
            b_re = jnp.concatenate([bu_ref[j, at_t, :] for j in range(n_lb)], axis=1)
            b_im = jnp.concatenate([bu_ref[n_lb + j, at_t, :] for j in range(n_lb)], axis=1)
            nr = lbr * xr - lbi * xi + b_re
            ni = lbr * xi + lbi * xr + b_im
            for j in range(n_lb):
                xs_ref[j, at_t, :] = lane_blk(nr, j)
                xs_ref[n_lb + j, at_t, :] = lane_blk(ni, j)
            return nr, ni

        xr, xi = lax.fori_loop(0, tt, step, (re1_ref[:, st], im1_ref[:, st]), unroll=min(tt, 8))
        re1_ref[:, st] = xr
        im1_ref[:, st] = xi
        xs = jnp.concatenate([xs_ref[j] for j in range(2 * n_lb)], axis=1)
        y = _mm1(xs, wc_ref[s]) + d_ref[:, s * u_w:(s + 1) * u_w] * us
        ys.append(_gelu_tanh(y))
    h_ref[...] = jnp.concatenate(ys, axis=1).astype(h_ref.dtype).reshape(h_ref.shape)


def _s5(u_blocks, blk, re0, im0, lb_re, lb_im, wb, wc, d_skip, tt):
    n_slabs = wb.shape[0]
    width = u_blocks.shape[-1]
    n_state = re0.shape[1]
    grid = (u_blocks.shape[0] // blk[0], u_blocks.shape[1] // blk[1])
    rows = SUBLANES * tt
    u_spec = pl.BlockSpec(blk, lambda i, j: (i, j, 0))
    st_spec = pl.BlockSpec((SUBLANES, n_state), lambda i, j: (i, 0))
    full = lambda a: _const_spec(a.shape)
    return pl.pallas_call(
        functools.partial(_s5_kernel, tt=tt, n_slabs=n_slabs),
        grid=grid,
        in_specs=[u_spec, st_spec, st_spec, full(lb_re), full(lb_im), full(wb), full(wc), full(d_skip)],
        out_specs=[u_spec, st_spec, st_spec],
        out_shape=[jax.ShapeDtypeStruct(u_blocks.shape, BF16),
                   jax.ShapeDtypeStruct(re0.shape, F32), jax.ShapeDtypeStruct(im0.shape, F32)],
        scratch_shapes=[pltpu.VMEM((2 * SLAB_GROUPS * S5_STATE // LANES, rows, LANES), F32)] * 2,
        compiler_params=pltpu.CompilerParams(dimension_semantics=("parallel", "arbitrary"),
                                             vmem_limit_bytes=VMEM_LIMIT),
        name="s5_scan",
    )(u_blocks, re0, im0, lb_re, lb_im, wb, wc, d_skip)


def _block_diag_slabs(m):
    g, a, b = m.shape
    eye = jnp.eye(SLAB_GROUPS, dtype=m.dtype)
    m4 = m.reshape(g // SLAB_GROUPS, SLAB_GROUPS, a, b)
    return jnp.einsum("sgab,gh->sgahb", m4, eye).reshape(g // SLAB_GROUPS, SLAB_GROUPS * a, SLAB_GROUPS * b)


def _tail_kernel(x_ref, oa_ref, hg_ref, gt_ref, wro_ref, w1_ref, b1_ref, w2_ref, b2_ref, wmo_ref,
                 npm_ref, nf_ref, npf_ref, wg_ref, wu_ref, wd_ref, y_ref):
    d = x_ref.shape[-1]
    hg = hg_ref[...]
    a_out = _dot(oa_ref[...], wro_ref[...])
    b_out = (_dot(hg, w1_ref[...]) + b1_ref[...]) * _sigmoid(_dot(hg, w2_ref[...]) + b2_ref[...])
    merged = _sigmoid(gt_ref[:, :d]) * a_out + _sigmoid(gt_ref[:, d:]) * b_out
    mix = _dot(merged.astype(BF16), wmo_ref[...])
    x1 = x_ref[...] + _rms(mix, npm_ref[...])
    hb = _rms(x1, nf_ref[...]).astype(BF16)
    gate = _dot(hb, wg_ref[...])
    up = _dot(hb, wu_ref[...])
    f = _dot((gate * _sigmoid(gate) * up).astype(BF16), wd_ref[...])
    y_ref[...] = x1 + _rms(f, npf_ref[...])


def _tail(x2d, oa, hg, gates, tw, tm):
    rows, d = x2d.shape
    row = lambda a: pl.BlockSpec((tm, a.shape[1]), lambda i: (i, 0))
    wnames = ("wro", "w1", "b1", "w2", "b2", "wmo", "npm", "nf", "npf", "wg", "wu", "wd")
    wargs = [tw[n] for n in wnames]
    return pl.pallas_call(
        _tail_kernel,
        grid=(rows // tm,),
        in_specs=[row(x2d), row(oa), row(hg), row(gates)] + [_const_spec(a.shape, True) for a in wargs],
        out_specs=pl.BlockSpec((tm, d), lambda i: (i, 0)),
        out_shape=jax.ShapeDtypeStruct((rows, d), F32),
        compiler_params=pltpu.CompilerParams(dimension_semantics=("parallel",),
                                             vmem_limit_bytes=VMEM_LIMIT),
        name="tail",
    )(x2d, oa, hg, gates, *wargs)


def _pick_tile(rows, target):
    t = min(rows, target)
    assert rows % t == 0
    return t


def _layer(x, shift0, wkv0, re0, im0, lw, *, chunk, s5_tt, row_tile):
    bsz, t, d = x.shape
    rows = bsz * t
    c_shift = shift0.shape[-1]
    c_u = lw["d_skip"].shape[-1]
    tm = _pick_tile(rows, row_tile)
    x2d = x.reshape(rows, d)
    pr, u, gates = _proj(x2d, lw["norm_pre_mix"], lw["w_in"], c_shift, c_u, tm)

    if t == 1:
        oa, wkv1 = _rwkv_step(pr, shift0, wkv0, lw["rwkv"], lw["e_mat"], nblk=64)
        pr_last = pr
        u_blocks = u.reshape(bsz // SUBLANES, SUBLANES, c_u)
        blk = (1, SUBLANES, c_u)
    else:
        oa, wkv1 = _rwkv_chunked(pr.reshape(bsz, t, c_shift), shift0, wkv0, lw["rwkv"], lw["e_mat"], chunk)
        oa = oa.reshape(rows, -1)
        pr_last = pr.reshape(bsz, t, c_shift)[:, -1]
        u_blocks = u.reshape(bsz, t, c_u)
        blk = (SUBLANES, s5_tt, c_u)
    n_state = re0.shape[1] * re0.shape[2]
    hg, re1, im1 = _s5(u_blocks, blk, re0.reshape(bsz, n_state), im0.reshape(bsz, n_state),
                       lw["lb_re"], lw["lb_im"], lw["wb"], lw["wc"], lw["d_skip"],
                       tt=1 if t == 1 else s5_tt)
    y = _tail(x2d, oa, hg.reshape(rows, c_u), gates, lw["tail"], tm)
    return y.reshape(x.shape), pr_last, wkv1, re1.reshape(re0.shape), im1.reshape(im0.shape)


def _prepare_layer_weights(l, p):
    row = lambda a: a[l][None, :].astype(F32)
    width = p["w0"].shape[-1]
    n_dec, n_aaa, n_gate = p["w_decay_up"].shape[1], p["w_aaa_up"].shape[1], p["w_gate_up"].shape[1]
    assert n_dec + n_aaa + n_gate == LORA_PAD

    def lora_pad(wup, start):
        return jnp.zeros((LORA_PAD, width), F32).at[start:start + wup.shape[0]].set(wup).astype(BF16)

    rwkv = {
        "mu": row(p["mu_shift"]), "w0": row(p["w0"]), "a0": row(p["a0"]), "k_k": row(p["k_k"]),
        "k_a": row(p["k_a"]), "r_k": row(p["r_k"]), "lnx_g": row(p["lnx_g"]), "lnx_b": row(p["lnx_b"]),
        "wd": lora_pad(p["w_decay_up"][l], 0),
        "wa": lora_pad(p["w_aaa_up"][l], n_dec),
        "wg": lora_pad(p["w_gate_up"][l], n_dec + n_aaa),
    }
    head_id = jnp.arange(width) // HEAD
    e_mat = (head_id[:, None] == head_id[None, :]).astype(BF16)

    lb_re, lb_im, bb_re, bb_im = _s5_discretise(p["s5_lam_re"][l], p["s5_lam_im"][l], p["s5_log_dt"][l],
                                                p["s5_b_re"][l], p["s5_b_im"][l])
    n_state = lb_re.shape[0] * lb_re.shape[1]
    to_in = lambda bb: _block_diag_slabs(jnp.swapaxes(bb, 1, 2))
    to_out = lambda cc: _block_diag_slabs(jnp.swapaxes(cc, 1, 2))
    wb = jnp.concatenate([to_in(bb_re), to_in(bb_im)], axis=-1).astype(BF16)
    wc = jnp.concatenate([to_out(p["s5_c_re"][l]), -to_out(p["s5_c_im"][l])], axis=1).astype(BF16)

    bf = lambda a: a[l].astype(BF16)
    tail = {
        "wro": bf(p["w_rwkv_out"]), "w1": bf(p["glu_w1"]), "b1": row(p["glu_b1"]), "w2": bf(p["glu_w2"]),
        "b2": row(p["glu_b2"]), "wmo": bf(p["w_merge_out"]), "npm": row(p["norm_post_mix"]),
        "nf": row(p["norm_pre_ffn"]), "npf": row(p["norm_post_ffn"]),
        "wg": bf(p["w_ffn_gate"]), "wu": bf(p["w_ffn_up"]), "wd": bf(p["w_ffn_down"]),
    }
    return {
        "norm_pre_mix": row(p["norm_pre_mix"]), "w_in": bf(p["w_in"]), "rwkv": rwkv, "e_mat": e_mat,
        "lb_re": lb_re.reshape(1, n_state), "lb_im": lb_im.reshape(1, n_state), "wb": wb, "wc": wc,
        "d_skip": row(p["s5_d"]), "tail": tail,
    }


_PARAM_NAMES = ("norm_pre_mix", "norm_post_mix", "norm_pre_ffn", "norm_post_ffn", "w_in", "mu_shift",
                "w0", "w_decay_up", "a0", "w_aaa_up", "w_gate_up", "k_k", "k_a", "r_k", "lnx_g", "lnx_b",
                "w_rwkv_out", "s5_lam_re", "s5_lam_im", "s5_log_dt", "s5_b_re", "s5_b_im", "s5_c_re",
                "s5_c_im", "s5_d", "glu_w1", "glu_b1", "glu_w2", "glu_b2", "w_merge_out",
                "w_ffn_gate", "w_ffn_up", "w_ffn_down")


def _forward(x_prompt, x_sample, state_shift, state_wkv, state_s5_re, state_s5_im, params,
             *, chunk=64, s5_tt=128, row_tile=256):
    depth = params["w_in"].shape[0]
    heads = state_wkv.shape[2]
    bp = x_prompt.shape[0]
    yp, ys = x_prompt, x_sample
    outs_p, outs_s = [], []
    for l in range(depth):
        lw = _prepare_layer_weights(l, params)
        zp_shift = jnp.zeros((bp, state_shift.shape[-1]), F32)
        zp_wkv = jnp.zeros((bp, heads, HEAD, HEAD), F32)
        zp_s5 = jnp.zeros((bp,) + state_s5_re.shape[2:], F32)
        yp, *st_p = _layer(yp, zp_shift, zp_wkv, zp_s5, zp_s5, lw, chunk=chunk, s5_tt=s5_tt, row_tile=row_tile)
        ys, *st_s = _layer(ys, state_shift[l], state_wkv[l], state_s5_re[l], state_s5_im[l], lw,
                           chunk=chunk, s5_tt=s5_tt, row_tile=row_tile)
        outs_p.append(st_p)
        outs_s.append(st_s)
    stack = lambda outs, i, dt: jnp.stack([o[i] for o in outs]).astype(dt)
    dt_p, dt_s = x_prompt.dtype, x_sample.dtype
    return (yp, ys,
            stack(outs_p, 0, dt_p), stack(outs_p, 1, dt_p), stack(outs_p, 2, dt_p), stack(outs_p, 3, dt_p),
            stack(outs_s, 0, dt_s), stack(outs_s, 1, dt_s), stack(outs_s, 2, dt_s), stack(outs_s, 3, dt_s))


def kernel(x_prompt, x_sample, state_shift, state_wkv, state_s5_re, state_s5_im, norm_pre_mix, norm_post_mix, norm_pre_ffn, norm_post_ffn, w_in, mu_shift, w0, w_decay_up, a0, w_aaa_up, w_gate_up, k_k, k_a, r_k, lnx_g, lnx_b, w_rwkv_out, s5_lam_re, s5_lam_im, s5_log_dt, s5_b_re, s5_b_im, s5_c_re, s5_c_im, s5_d, glu_w1, glu_b1, glu_w2, glu_b2, w_merge_out, w_ffn_gate, w_ffn_up, w_ffn_down):
    params = dict(zip(_PARAM_NAMES, (norm_pre_mix, norm_post_mix, norm_pre_ffn, norm_post_ffn, w_in, mu_shift,
                                     w0, w_decay_up, a0, w_aaa_up, w_gate_up, k_k, k_a, r_k, lnx_g, lnx_b,
                                     w_rwkv_out, s5_lam_re, s5_lam_im, s5_log_dt, s5_b_re, s5_b_im, s5_c_re,
                                     s5_c_im, s5_d, glu_w1, glu_b1, glu_w2, glu_b2, w_merge_out,
                                     w_ffn_gate, w_ffn_up, w_ffn_down)))
    return _forward(x_prompt, x_sample, state_shift, state_wkv, state_s5_re, state_s5_im, params)
```

```python
import functools
import math

import jax
import jax.numpy as jnp
from jax import lax
from jax.experimental import pallas as pl
from jax.experimental.pallas import tpu as pltpu

F32 = jnp.float32
BF16 = jnp.bfloat16

NORM_EPS = 1e-6
LNX_EPS = 64e-5
HEAD = 64
GROUP_HEADS = 2
GROUP_W = GROUP_HEADS * HEAD
GRAM_PASSES = 1
INV_PASSES = 1
APPLY_PASSES = 1
STATE_PASSES = 1
LORA_PAD = 128
S5_GROUP = 16
S5_STATE = 64
SLAB_GROUPS = 8
SUBLANES = 8
LANES = 128
VMEM_LIMIT = 56 * 1024 * 1024

NN = (((1,), (0,)), ((), ()))
NT = (((1,), (1,)), ((), ()))


def _dot(a, b, dims=NN):
    return lax.dot_general(a, b, dims, preferred_element_type=F32)


def _split2(x):
    hi = x.astype(BF16)
    lo = (x - hi.astype(F32)).astype(BF16)
    return hi, lo


def _split3(x):
    hi = x.astype(BF16)
    r1 = x - hi.astype(F32)
    mid = r1.astype(BF16)
    lo = (r1 - mid.astype(F32)).astype(BF16)
    return hi, mid, lo


def _mm1(a, b, dims=NN):
    return _dot(a.astype(BF16), b.astype(BF16), dims)


def _mm3(a, b, dims=NN):
    ah, al = _split2(a)
    bh, bl = _split2(b)
    return _dot(ah, bh, dims) + (_dot(ah, bl, dims) + _dot(al, bh, dims))


def _mm_exact_lhs(a_bf16, b):
    h, m, l = _split3(b)
    return _dot(a_bf16, h) + (_dot(a_bf16, m) + _dot(a_bf16, l))


def _rms(x, g):
    return x * lax.rsqrt(jnp.mean(x * x, axis=-1, keepdims=True) + NORM_EPS) * g


def _sigmoid(x):
    return 1.0 / (1.0 + jnp.exp(-x))


def _softplus(x):
    return jnp.maximum(x, 0.0) + jnp.log(1.0 + jnp.exp(-jnp.abs(x)))


def _gelu_tanh(x):
    c = math.sqrt(2.0 / math.pi)
    return 0.5 * x * (1.0 + jnp.tanh(c * (x + 0.044715 * (x * x * x))))


def _const_spec(shape, single_buffer=False):
    idx = lambda *_: (0,) * len(shape)
    if single_buffer:
        return pl.BlockSpec(shape, idx, pipeline_mode=pl.Buffered(1))
    return pl.BlockSpec(shape, idx)


def _proj_kernel(x_ref, g_ref, w_ref, pr_ref, u_ref, gt_ref, *, c_shift, c_u):
    hb = _rms(x_ref[...], g_ref[...]).astype(BF16)
    pr_ref[...] = _dot(hb, w_ref[:, :c_shift])
    u_ref[...] = _dot(hb, w_ref[:, c_shift:c_shift + c_u])
    gt_ref[...] = _dot(hb, w_ref[:, c_shift + c_u:])


def _proj(x2d, g, w_in_bf16, c_shift, c_u, tm):
    rows, d = x2d.shape
    cols = w_in_bf16.shape[1]
    c_g = cols - c_shift - c_u
    row = lambda w: pl.BlockSpec((tm, w), lambda i: (i, 0))
    return pl.pallas_call(
        functools.partial(_proj_kernel, c_shift=c_shift, c_u=c_u),
        grid=(rows // tm,),
        in_specs=[row(d), _const_spec((1, d)), _const_spec((d, cols), True)],
        out_specs=[row(c_shift), row(c_u), row(c_g)],
        out_shape=[jax.ShapeDtypeStruct((rows, c_shift), F32),
                   jax.ShapeDtypeStruct((rows, c_u), F32),
                   jax.ShapeDtypeStruct((rows, c_g), F32)],
        compiler_params=pltpu.CompilerParams(dimension_semantics=("parallel",),
                                             vmem_limit_bytes=VMEM_LIMIT),
        name="proj",
    )(x2d, g, w_in_bf16)


def _head_sum(x, e_ref):
    rows, width = x.shape
    n_lg = width // LANES
    hi, lo = _split2(x)
    stacked = jnp.concatenate([part[:, j * LANES:(j + 1) * LANES] for part in (hi, lo) for j in range(n_lg)], axis=0)
    sums = _dot(stacked, e_ref[...])
    return jnp.concatenate([sums[j * rows:(j + 1) * rows] + sums[(n_lg + j) * rows:(n_lg + j + 1) * rows]
                            for j in range(n_lg)], axis=1)


def _rwkv_token_prep(xr, w, e_ref):
    width = w["w0"].shape[-1]
    r = xr[:, :width]
    k = xr[:, width:2 * width]
    v = xr[:, 2 * width:3 * width]
    lo = xr[:, 3 * width:3 * width + LORA_PAD]
    wl = w["w0"][...] + _mm1(jnp.tanh(lo), w["wd"][...])
    lw = -jnp.exp(-_softplus(-wl) - 0.5)
    a = _sigmoid(w["a0"][...] + _mm1(lo, w["wa"][...]))
    g = _mm1(_sigmoid(lo), w["wg"][...])
    kk = k * w["k_k"][...]
    kk = kk * lax.rsqrt(jnp.maximum(_head_sum(kk * kk, e_ref), 1e-24))
    kmod = k * (1.0 + (a - 1.0) * w["k_a"][...])
    return r, kmod, v, kk, kk * a, lw, g


def _rwkv_post(o, r, kmod, v, g, w, e_ref):
    inv_n = 1.0 / HEAD
    mu = _head_sum(o, e_ref) * inv_n
    oc = o - mu
    var = _head_sum(oc * oc, e_ref) * inv_n
    on = oc * lax.rsqrt(var + LNX_EPS) * w["lnx_g"][...] + w["lnx_b"][...]
    bonus = _head_sum(r * kmod * w["r_k"][...], e_ref) * v
    return (on + bonus) * g


_RWKV_W_NAMES = ("mu", "w0", "a0", "k_k", "k_a", "r_k", "lnx_g", "lnx_b", "wd", "wa", "wg")


def _mm(a, b, dims=NN, passes=1):
    return _mm3(a, b, dims) if passes == 3 else _mm1(a, b, dims)


def _tri_inverses(mats, nilpotency, passes):
    n = mats[0].shape[0]
    rows = lax.broadcasted_iota(jnp.int32, (n, n), 0)
    cols = lax.broadcasted_iota(jnp.int32, (n, n), 1)
    eye = jnp.where(rows == cols, 1.0, 0.0).astype(F32)
    ps = [-a for a in mats]
    ts = [eye + p for p in ps]
    covered = 2
    while covered < nilpotency:
        ps = [_mm(p, p, NN, passes) for p in ps]
        ts = [t + _mm(t, p, NN, passes) for t, p in zip(ts, ps)]
        covered *= 2
    return ts


def _rwkv_chunk_kernel(pr_ref, shift0_ref, s0_ref, *rest, chunk, width, n_seq):
    n_w = len(_RWKV_W_NAMES)
    w = dict(zip(_RWKV_W_NAMES, rest[:n_w]))
    e_ref, tri_ref = rest[n_w], rest[n_w + 1]
    oa_ref, s1_ref = rest[n_w + 2], rest[n_w + 3]
    carry_ref, z_ref = rest[n_w + 4], rest[n_w + 5]
    c = pl.program_id(1)
    n_groups = width // GROUP_W
    C = chunk
    GC = GROUP_HEADS * C

    lane_head = lax.broadcasted_iota(jnp.int32, (1, GROUP_W), 1) // HEAD
    head_masks = [lane_head == h for h in range(GROUP_HEADS)]

    def stack(x):
        return jnp.concatenate([jnp.where(m, x, 0.0) for m in head_masks], axis=0)

    def collapse(xs):
        out = xs[:C]
        for h in range(1, GROUP_HEADS):
            out = out + xs[h * C:(h + 1) * C]
        return out

    @pl.when(c == 0)
    def _():
        carry_ref[...] = shift0_ref[...]
        for q in range(n_seq):
            for gi in range(n_groups):
                z_ref[q, gi] = jnp.zeros((GROUP_W, GROUP_W), F32)
                for h in range(GROUP_HEADS):
                    z_ref[q, gi, h * HEAD:(h + 1) * HEAD, h * HEAD:(h + 1) * HEAD] = s0_ref[q, gi * GROUP_HEADS + h]

    pr = pr_ref[...].reshape(n_seq * C, pr_ref.shape[-1])
    row_id = lax.broadcasted_iota(jnp.int32, pr.shape, 0)
    pr_prev = pltpu.roll(pr, 1, axis=0)
    for q in range(n_seq):
        pr_prev = jnp.where(row_id == q * C, carry_ref[q], pr_prev)
        carry_ref[q] = pr[(q + 1) * C - 1:(q + 1) * C, :]
    xr = pr + (pr_prev - pr) * w["mu"][...]
    r, kmod, v, kk, bvec, lw, g = _rwkv_token_prep(xr, w, e_ref)

    cum = _mm_exact_lhs(tri_ref[...], lw)
    g_in = jnp.exp(cum)
    g_ex = jnp.exp(cum - lw)
    g_neg = jnp.exp(-cum)
    r_t = r * g_in
    k_t = kmod * g_neg
    b_t = bvec * g_neg
    kap_t = kk * g_ex

    ri = lax.broadcasted_iota(jnp.int32, (GC, GC), 0)
    ci = lax.broadcasted_iota(jnp.int32, (GC, GC), 1)
    same_head = (ri // C) == (ci // C)
    strict = same_head & (ri > ci)
    incl = same_head & (ri >= ci)
    zi = lax.broadcasted_iota(jnp.int32, (GROUP_W, GROUP_W), 0) // HEAD
    zj = lax.broadcasted_iota(jnp.int32, (GROUP_W, GROUP_W), 1) // HEAD
    block_diag = zi == zj

    chains = [(q, gi) for q in range(n_seq) for gi in range(n_groups)]
    blk = lambda a, q, gi: a[q * C:(q + 1) * C, gi * GROUP_W:(gi + 1) * GROUP_W]
    kap_s = [stack(blk(kap_t, q, gi)) for q, gi in chains]
    v_s = [stack(blk(v, q, gi)) for q, gi in chains]
    grams = []
    for i, (q, gi) in enumerate(chains):
        left = jnp.concatenate([kap_s[i], stack(blk(r_t, q, gi))], axis=0)
        right = jnp.concatenate([blk(k_t, q, gi)] * GROUP_HEADS + [blk(b_t, q, gi)] * GROUP_HEADS, axis=0)
        grams.append(_mm(left, right, NT, GRAM_PASSES))
    a_k = [jnp.where(strict, gm[:GC, :GC], 0.0) for gm in grams]
    a_b = [jnp.where(strict, gm[:GC, GC:], 0.0) for gm in grams]
    a_rk = [jnp.where(incl, gm[GC:, :GC], 0.0) for gm in grams]
    a_rb = [jnp.where(incl, gm[GC:, GC:], 0.0) for gm in grams]
    t_inv = _tri_inverses(a_b, C, INV_PASSES)
    akv = [_mm(a, vs, NN, APPLY_PASSES) for a, vs in zip(a_k, v_s)]
    tw = [_mm(t, jnp.concatenate([ks, x], axis=1), NN, APPLY_PASSES) for t, ks, x in zip(t_inv, kap_s, akv)]
    k_hat = [collapse(x[:, :GROUP_W]) for x in tw]
    v_hat = [collapse(x[:, GROUP_W:]) for x in tw]
    arkv = [collapse(_mm(a, vs, NN, APPLY_PASSES)) for a, vs in zip(a_rk, v_s)]

    zs = [z_ref[q, gi] for q, gi in chains]
    pz = [_mm(jnp.concatenate([kh, blk(r_t, q, gi)], axis=0), z, NT, STATE_PASSES)
          for kh, z, (q, gi) in zip(k_hat, zs, chains)]
    us = [p[:C] + vh for p, vh in zip(pz, v_hat)]
    arbu = [collapse(_mm(a, stack(u), NN, APPLY_PASSES)) for a, u in zip(a_rb, us)]
    o_blk = [p[C:] + x - y for p, x, y in zip(pz, arkv, arbu)]
    for i, (q, gi) in enumerate(chains):
        cum_blk = blk(cum, q, gi)
        cum_last = cum_blk[C - 1:C, :]
        g_end = jnp.exp(cum_last - cum_blk)
        vu_t = jnp.concatenate([blk(v, q, gi), us[i]], axis=0).T
        kb = jnp.concatenate([blk(kmod, q, gi) * g_end, -blk(bvec, q, gi) * g_end], axis=0)
        z_ref[q, gi] = zs[i] * jnp.exp(cum_last) + jnp.where(block_diag, _mm(vu_t, kb, NN, STATE_PASSES), 0.0)

    o = jnp.concatenate([jnp.concatenate(o_blk[q * n_groups:(q + 1) * n_groups], axis=1) for q in range(n_seq)], axis=0)
    oa_ref[...] = _rwkv_post(o, r, kmod, v, g, w, e_ref).astype(oa_ref.dtype).reshape(oa_ref.shape)

    @pl.when(c == pl.num_programs(1) - 1)
    def _():
        for q in range(n_seq):
            for gi in range(n_groups):
                for h in range(GROUP_HEADS):
                    s1_ref[q, gi * GROUP_HEADS + h] = z_ref[q, gi, h * HEAD:(h + 1) * HEAD, h * HEAD:(h + 1) * HEAD]


def _rwkv_weight_inputs(wts):
    return [wts[n] for n in _RWKV_W_NAMES]


def _rwkv_weight_specs(wts):
    return [_const_spec(wts[n].shape) for n in _RWKV_W_NAMES]


def _rwkv_chunked(pr3d, shift0, wkv0, wts, e_mat, chunk, n_seq):
    bsz, t, c_shift = pr3d.shape
    heads = wkv0.shape[1]
    width = heads * HEAD
    tri = jnp.kron(jnp.eye(n_seq, dtype=F32), jnp.tril(jnp.ones((chunk, chunk), F32))).astype(BF16)
    kern = functools.partial(_rwkv_chunk_kernel, chunk=chunk, width=width, n_seq=n_seq)
    return pl.pallas_call(
        kern,
        grid=(bsz // n_seq, t // chunk),
        in_specs=[pl.BlockSpec((n_seq, chunk, c_shift), lambda b, c: (b, c, 0)),
                  pl.BlockSpec((n_seq, 1, c_shift), lambda b, c: (b, 0, 0)),
                  pl.BlockSpec((n_seq, heads, HEAD, HEAD), lambda b, c: (b, 0, 0, 0))]
                 + _rwkv_weight_specs(wts)
                 + [_const_spec(e_mat.shape), _const_spec(tri.shape)],
        out_specs=[pl.BlockSpec((n_seq, chunk, width), lambda b, c: (b, c, 0)),
                   pl.BlockSpec((n_seq, heads, HEAD, HEAD), lambda b, c: (b, 0, 0, 0))],
        out_shape=[jax.ShapeDtypeStruct((bsz, t, width), BF16),
                   jax.ShapeDtypeStruct((bsz, heads, HEAD, HEAD), F32)],
        scratch_shapes=[pltpu.VMEM((n_seq, 1, c_shift), F32),
                        pltpu.VMEM((n_seq, width // GROUP_W, GROUP_W, GROUP_W), F32)],
        compiler_params=pltpu.CompilerParams(dimension_semantics=("parallel", "arbitrary"),
                                             vmem_limit_bytes=VMEM_LIMIT),
        name="rwkv_chunk",
    )(pr3d, shift0[:, None, :], wkv0, *_rwkv_weight_inputs(wts), e_mat, tri)


def _rwkv_step_prep_kernel(pr_ref, shift0_ref, *rest):
    n_w = len(_RWKV_W_NAMES)
    w = dict(zip(_RWKV_W_NAMES, rest[:n_w]))
    e_ref = rest[n_w]
    r_ref, k_ref, v_ref, kk_ref, b_ref, dec_ref, g_ref = rest[n_w + 1:]
    pr = pr_ref[...]
    xr = pr + (shift0_ref[...] - pr) * w["mu"][...]
    r, kmod, v, kk, bvec, lw, g = _rwkv_token_prep(xr, w, e_ref)
    r_ref[...] = r
    k_ref[...] = kmod
    v_ref[...] = v
    kk_ref[...] = kk
    b_ref[...] = bvec
    dec_ref[...] = jnp.exp(lw)
    g_ref[...] = g


def _rwkv_step_state_kernel(s_ref, kk_ref, b_ref, dec_ref, k_ref, r_ref, v_ref, s1_ref, o_ref):
    s = s_ref[...]
    sa = -jnp.sum(s * kk_ref[...], axis=-1, keepdims=True)
    s1 = s * dec_ref[...] + sa * b_ref[...] + v_ref[...] * k_ref[...]
    s1_ref[...] = s1
    o_ref[...] = jnp.sum(s1 * r_ref[...], axis=-1, keepdims=True)


def _rwkv_step_post_kernel(o_ref, r_ref, k_ref, v_ref, g_ref, *rest):
    n_w = len(_RWKV_W_NAMES)
    w = dict(zip(_RWKV_W_NAMES, rest[:n_w]))
    e_ref, oa_ref = rest[n_w], rest[n_w + 1]
    oa_ref[...] = _rwkv_post(o_ref[...], r_ref[...], k_ref[...], v_ref[...], g_ref[...], w, e_ref).astype(oa_ref.dtype)


def _rwkv_step(pr2d, shift0, wkv0, wts, e_mat, nblk):
    bsz, c_shift = pr2d.shape
    heads = wkv0.shape[1]
    width = heads * HEAD
    full = lambda a: _const_spec(a.shape)
    vec = jax.ShapeDtypeStruct((bsz, width), F32)
    cp = pltpu.CompilerParams(vmem_limit_bytes=VMEM_LIMIT)
    r, kmod, v, kk, bvec, dec, g = pl.pallas_call(
        _rwkv_step_prep_kernel,
        grid=(1,),
        in_specs=[full(pr2d), full(shift0)] + _rwkv_weight_specs(wts) + [full(e_mat)],
        out_specs=[_const_spec((bsz, width))] * 7,
        out_shape=[vec] * 7,
        compiler_params=cp,
        name="rwkv_step_prep",
    )(pr2d, shift0, *_rwkv_weight_inputs(wts), e_mat)

    n = bsz * heads
    rowv = lambda a: a.reshape(n, 1, HEAD)
    colv = lambda a: a.reshape(n, HEAD, 1)
    mat_spec = pl.BlockSpec((nblk, HEAD, HEAD), lambda i: (i, 0, 0))
    row_spec = pl.BlockSpec((nblk, 1, HEAD), lambda i: (i, 0, 0))
    col_spec = pl.BlockSpec((nblk, HEAD, 1), lambda i: (i, 0, 0))
    s1, o = pl.pallas_call(
        _rwkv_step_state_kernel,
        grid=(n // nblk,),
        in_specs=[mat_spec] + [row_spec] * 5 + [col_spec],
        out_specs=[mat_spec, col_spec],
        out_shape=[jax.ShapeDtypeStruct((n, HEAD, HEAD), F32), jax.ShapeDtypeStruct((n, HEAD, 1), F32)],
        compiler_params=pltpu.CompilerParams(dimension_semantics=("parallel",), vmem_limit_bytes=VMEM_LIMIT),
        name="rwkv_step_state",
    )(wkv0.reshape(n, HEAD, HEAD), rowv(kk), rowv(bvec), rowv(dec), rowv(kmod), rowv(r), colv(v))

    o2d = o.reshape(bsz, width)
    oa = pl.pallas_call(
        _rwkv_step_post_kernel,
        grid=(1,),
        in_specs=[_const_spec((bsz, width))] * 5 + _rwkv_weight_specs(wts) + [full(e_mat)],
        out_specs=_const_spec((bsz, width)),
        out_shape=jax.ShapeDtypeStruct((bsz, width), BF16),
        compiler_params=cp,
        name="rwkv_step_post",
    )(o2d, r, kmod, v, g, *_rwkv_weight_inputs(wts), e_mat)
    return oa, s1.reshape(wkv0.shape)


def _s5_disc_kernel(lre_ref, lim_ref, ldt_ref, bre_ref, bim_ref, lbr_ref, lbi_ref, bbr_ref, bbi_ref):
    lam_re, lam_im = lre_ref[...], lim_ref[...]
    dt = jnp.exp(ldt_ref[...])
    mag = jnp.exp(lam_re * dt)
    ang = lam_im * dt
    lb_re, lb_im = mag * jnp.cos(ang), mag * jnp.sin(ang)
    nr, ni = lb_re - 1.0, lb_im
    den = lam_re * lam_re + lam_im * lam_im
    f_re = (nr * lam_re + ni * lam_im) / den
    f_im = (ni * lam_re - nr * lam_im) / den
    b_re, b_im = bre_ref[...], bim_ref[...]
    lbr_ref[...] = lb_re
    lbi_ref[...] = lb_im
    bbr_ref[...] = f_re * b_re - f_im * b_im
    bbi_ref[...] = f_re * b_im + f_im * b_re


def _s5_discretise(lam_re, lam_im, log_dt, b_re, b_im):
    g, p = lam_re.shape
    full = lambda a: _const_spec(a.shape)
    args = (lam_re[..., None], lam_im[..., None], log_dt[:, None, None], b_re, b_im)
    return pl.pallas_call(
        _s5_disc_kernel,
        grid=(1,),
        in_specs=[full(a) for a in args],
        out_specs=[_const_spec((g, p, 1))] * 2 + [full(b_re)] * 2,
        out_shape=[jax.ShapeDtypeStruct((g, p, 1), F32)] * 2 + [jax.ShapeDtypeStruct(b_re.shape, F32)] * 2,
        name="s5_discretise",
    )(*args)


def _s5_kernel(u_ref, re0_ref, im0_ref, lbr_ref, lbi_ref, wb_ref, wc_ref, d_ref,
               h_ref, re1_ref, im1_ref, bu_ref, xs_ref, *, tt, n_slabs):
    t_blk = pl.program_id(1)
    rows = SUBLANES * tt
    s_w = SLAB_GROUPS * S5_STATE
    u_w = SLAB_GROUPS * S5_GROUP

    @pl.when(t_blk == 0)
    def _():
        re1_ref[...] = re0_ref[...]
        im1_ref[...] = im0_ref[...]

    u_blk = u_ref[...]
    if tt > 1:
        u_blk = jnp.swapaxes(u_blk, 0, 1)
    u = u_blk.reshape(rows, n_slabs * u_w)
    ys = []
    for s in range(n_slabs):
        us = u[:, s * u_w:(s + 1) * u_w]
        bu_ref[...] = _mm1(us, wb_ref[s])
        st = slice(s * s_w, (s + 1) * s_w)
        lbr = jnp.broadcast_to(lbr_ref[:, st], (SUBLANES, s_w))
        lbi = jnp.broadcast_to(lbi_ref[:, st], (SUBLANES, s_w))

        def step(t, carry):
            xr, xi = carry
            at_t = pl.ds(pl.multiple_of(t * SUBLANES, SUBLANES), SUBLANES)
            nr = lbr * xr - lbi * xi + bu_ref[at_t, :s_w]
            ni = lbr * xi + lbi * xr + bu_ref[at_t, s_w:]
            xs_ref[at_t, :s_w] = nr
            xs_ref[at_t, s_w:] = ni
            return nr, ni

        xr, xi = lax.fori_loop(0, tt, step, (re1_ref[:, st], im1_ref[:, st]), unroll=min(tt, 8))
        re1_ref[:, st] = xr
        im1_ref[:, st] = xi
        y = _mm1(xs_ref[...], wc_ref[s]) + d_ref[:, s * u_w:(s + 1) * u_w] * us
        ys.append(_gelu_tanh(y))
    h = jnp.concatenate(ys, axis=1)
    if tt > 1:
        h = jnp.swapaxes(h.reshape(tt, SUBLANES, n_slabs * u_w), 0, 1)
    h_ref[...] = h.reshape(h_ref.shape).astype(h_ref.dtype)


def _s5(u_blocks, blk, re0, im0, lb_re, lb_im, wb, wc, d_skip, tt):
    n_slabs = wb.shape[0]
    width = u_blocks.shape[-1]
    n_state = re0.shape[1]
    grid = (u_blocks.shape[0] // blk[0], u_blocks.shape[1] // blk[1])
    rows = SUBLANES * tt
    u_spec = pl.BlockSpec(blk, lambda i, j: (i, j, 0))
    st_spec = pl.BlockSpec((SUBLANES, n_state), lambda i, j: (i, 0))
    full = lambda a: _const_spec(a.shape)
    return pl.pallas_call(
        functools.partial(_s5_kernel, tt=tt, n_slabs=n_slabs),
        grid=grid,
        in_specs=[u_spec, st_spec, st_spec, full(lb_re), full(lb_im), full(wb), full(wc), full(d_skip)],
        out_specs=[u_spec, st_spec, st_spec],
        out_shape=[jax.ShapeDtypeStruct(u_blocks.shape, BF16),
                   jax.ShapeDtypeStruct(re0.shape, F32), jax.ShapeDtypeStruct(im0.shape, F32)],
        scratch_shapes=[pltpu.VMEM((rows, 2 * SLAB_GROUPS * S5_STATE), F32)] * 2,
        compiler_params=pltpu.CompilerParams(dimension_semantics=("parallel", "arbitrary"),
                                             vmem_limit_bytes=VMEM_LIMIT),
        name="s5_scan",
    )(u_blocks, re0, im0, lb_re, lb_im, wb, wc, d_skip)


def _block_diag_slabs(m):
    g, a, b = m.shape
    eye = jnp.eye(SLAB_GROUPS, dtype=m.dtype)
    m4 = m.reshape(g // SLAB_GROUPS, SLAB_GROUPS, a, b)
    return jnp.einsum("sgab,gh->sgahb", m4, eye).reshape(g // SLAB_GROUPS, SLAB_GROUPS * a, SLAB_GROUPS * b)


def _tail_kernel(x_ref, oa_ref, hg_ref, gt_ref, wro_ref, w1_ref, b1_ref, w2_ref, b2_ref, wmo_ref,
                 npm_ref, nf_ref, npf_ref, wg_ref, wu_ref, wd_ref, y_ref):
    d = x_ref.shape[-1]
    hg = hg_ref[...]
    a_out = _dot(oa_ref[...], wro_ref[...])
    b_out = (_dot(hg, w1_ref[...]) + b1_ref[...]) * _sigmoid(_dot(hg, w2_ref[...]) + b2_ref[...])
    merged = _sigmoid(gt_ref[:, :d]) * a_out + _sigmoid(gt_ref[:, d:]) * b_out
    mix = _dot(merged.astype(BF16), wmo_ref[...])
    x1 = x_ref[...] + _rms(mix, npm_ref[...])
    hb = _rms(x1, nf_ref[...]).astype(BF16)
    gate = _dot(hb, wg_ref[...])
    up = _dot(hb, wu_ref[...])
    f = _dot((gate * _sigmoid(gate) * up).astype(BF16), wd_ref[...])
    y_ref[...] = x1 + _rms(f, npf_ref[...])


def _tail(x2d, oa, hg, gates, tw, tm):
    rows, d = x2d.shape
    row = lambda a: pl.BlockSpec((tm, a.shape[1]), lambda i: (i, 0))
    wnames = ("wro", "w1", "b1", "w2", "b2", "wmo", "npm", "nf", "npf", "wg", "wu", "wd")
    wargs = [tw[n] for n in wnames]
    return pl.pallas_call(
        _tail_kernel,
        grid=(rows // tm,),
        in_specs=[row(x2d), row(oa), row(hg), row(gates)] + [_const_spec(a.shape, True) for a in wargs],
        out_specs=pl.BlockSpec((tm, d), lambda i: (i, 0)),
        out_shape=jax.ShapeDtypeStruct((rows, d), F32),
        compiler_params=pltpu.CompilerParams(dimension_semantics=("parallel",),
                                             vmem_limit_bytes=VMEM_LIMIT),
        name="tail",
    )(x2d, oa, hg, gates, *wargs)


def _pick_tile(rows, target):
    t = min(rows, target)
    assert rows % t == 0
    return t


def _layer(x, shift0, wkv0, re0, im0, lw, *, chunk, n_seq, s5_tt, row_tile):
    bsz, t, d = x.shape
    rows = bsz * t
    c_shift = shift0.shape[-1]
    c_u = lw["d_skip"].shape[-1]
    tm = _pick_tile(rows, row_tile)
    x2d = x.reshape(rows, d)
    pr, u, gates = _proj(x2d, lw["norm_pre_mix"], lw["w_in"], c_shift, c_u, tm)

    if t == 1:
        oa, wkv1 = _rwkv_step(pr, shift0, wkv0, lw["rwkv"], lw["e_mat"], nblk=64)
        pr_last = pr
        u_blocks = u.reshape(bsz // SUBLANES, SUBLANES, c_u)
        blk = (1, SUBLANES, c_u)
    else:
        oa, wkv1 = _rwkv_chunked(pr.reshape(bsz, t, c_shift), shift0, wkv0, lw["rwkv"], lw["e_mat"], chunk, n_seq)
        oa = oa.reshape(rows, -1)
        pr_last = pr.reshape(bsz, t, c_shift)[:, -1]
        u_blocks = u.reshape(bsz, t, c_u)
        blk = (SUBLANES, s5_tt, c_u)
    n_state = re0.shape[1] * re0.shape[2]
    hg, re1, im1 = _s5(u_blocks, blk, re0.reshape(bsz, n_state), im0.reshape(bsz, n_state),
                       lw["lb_re"], lw["lb_im"], lw["wb"], lw["wc"], lw["d_skip"],
                       tt=1 if t == 1 else s5_tt)
    y = _tail(x2d, oa, hg.reshape(rows, c_u), gates, lw["tail"], tm)
    return y.reshape(x.shape), pr_last, wkv1, re1.reshape(re0.shape), im1.reshape(im0.shape)


def _prepare_layer_weights(l, p):
    row = lambda a: a[l][None, :].astype(F32)
    width = p["w0"].shape[-1]
    n_dec, n_aaa, n_gate = p["w_decay_up"].shape[1], p["w_aaa_up"].shape[1], p["w_gate_up"].shape[1]
    assert n_dec + n_aaa + n_gate == LORA_PAD

    def lora_pad(wup, start):
        return jnp.zeros((LORA_PAD, width), F32).at[start:start + wup.shape[0]].set(wup).astype(BF16)

    rwkv = {
        "mu": row(p["mu_shift"]), "w0": row(p["w0"]), "a0": row(p["a0"]), "k_k": row(p["k_k"]),
        "k_a": row(p["k_a"]), "r_k": row(p["r_k"]), "lnx_g": row(p["lnx_g"]), "lnx_b": row(p["lnx_b"]),
        "wd": lora_pad(p["w_decay_up"][l], 0),
        "wa": lora_pad(p["w_aaa_up"][l], n_dec),
        "wg": lora_pad(p["w_gate_up"][l], n_dec + n_aaa),
    }
    head_id = jnp.arange(LANES) // HEAD
    e_mat = (head_id[:, None] == head_id[None, :]).astype(BF16)

    lb_re, lb_im, bb_re, bb_im = _s5_discretise(p["s5_lam_re"][l], p["s5_lam_im"][l], p["s5_log_dt"][l],
                                                p["s5_b_re"][l], p["s5_b_im"][l])
    n_state = lb_re.shape[0] * lb_re.shape[1]
    to_in = lambda bb: _block_diag_slabs(jnp.swapaxes(bb, 1, 2))
    to_out = lambda cc: _block_diag_slabs(jnp.swapaxes(cc, 1, 2))
    wb = jnp.concatenate([to_in(bb_re), to_in(bb_im)], axis=-1).astype(BF16)
    wc = jnp.concatenate([to_out(p["s5_c_re"][l]), -to_out(p["s5_c_im"][l])], axis=1).astype(BF16)

    bf = lambda a: a[l].astype(BF16)
    tail = {
        "wro": bf(p["w_rwkv_out"]), "w1": bf(p["glu_w1"]), "b1": row(p["glu_b1"]), "w2": bf(p["glu_w2"]),
        "b2": row(p["glu_b2"]), "wmo": bf(p["w_merge_out"]), "npm": row(p["norm_post_mix"]),
        "nf": row(p["norm_pre_ffn"]), "npf": row(p["norm_post_ffn"]),
        "wg": bf(p["w_ffn_gate"]), "wu": bf(p["w_ffn_up"]), "wd": bf(p["w_ffn_down"]),
    }
    return {
        "norm_pre_mix": row(p["norm_pre_mix"]), "w_in": bf(p["w_in"]), "rwkv": rwkv, "e_mat": e_mat,
        "lb_re": lb_re.reshape(1, n_state), "lb_im": lb_im.reshape(1, n_state), "wb": wb, "wc": wc,
        "d_skip": row(p["s5_d"]), "tail": tail,
    }


_PARAM_NAMES = ("norm_pre_mix", "norm_post_mix", "norm_pre_ffn", "norm_post_ffn", "w_in", "mu_shift",
                "w0", "w_decay_up", "a0", "w_aaa_up", "w_gate_up", "k_k", "k_a", "r_k", "lnx_g", "lnx_b",
                "w_rwkv_out", "s5_lam_re", "s5_lam_im", "s5_log_dt", "s5_b_re", "s5_b_im", "s5_c_re",
                "s5_c_im", "s5_d", "glu_w1", "glu_b1", "glu_w2", "glu_b2", "w_merge_out",
                "w_ffn_gate", "w_ffn_up", "w_ffn_down")


def _forward(x_prompt, x_sample, state_shift, state_wkv, state_s5_re, state_s5_im, params,
             *, chunk=64, n_seq=4, s5_tt=128, row_tile=256):
    depth = params["w_in"].shape[0]
    heads = state_wkv.shape[2]
    bp = x_prompt.shape[0]
    yp, ys = x_prompt, x_sample
    outs_p, outs_s = [], []
    for l in range(depth):
        lw = _prepare_layer_weights(l, params)
        zp_shift = jnp.zeros((bp, state_shift.shape[-1]), F32)
        zp_wkv = jnp.zeros((bp, heads, HEAD, HEAD), F32)
        zp_s5 = jnp.zeros((bp,) + state_s5_re.shape[2:], F32)
        yp, *st_p = _layer(yp, zp_shift, zp_wkv, zp_s5, zp_s5, lw, chunk=chunk, n_seq=n_seq, s5_tt=s5_tt, row_tile=row_tile)
        ys, *st_s = _layer(ys, state_shift[l], state_wkv[l], state_s5_re[l], state_s5_im[l], lw,
                           chunk=chunk, n_seq=n_seq, s5_tt=s5_tt, row_tile=row_tile)
        outs_p.append(st_p)
        outs_s.append(st_s)
    stack = lambda outs, i, dt: jnp.stack([o[i] for o in outs]).astype(dt)
    dt_p, dt_s = x_prompt.dtype, x_sample.dtype
    return (yp, ys,
            stack(outs_p, 0, dt_p), stack(outs_p, 1, dt_p), stack(outs_p, 2, dt_p), stack(outs_p, 3, dt_p),
            stack(outs_s, 0, dt_s), stack(outs_s, 1, dt_s), stack(outs_s, 2, dt_s), stack(outs_s, 3, dt_s))


def kernel(x_prompt, x_sample, state_shift, state_wkv, state_s5_re, state_s5_im, norm_pre_mix, norm_post_mix, norm_pre_ffn, norm_post_ffn, w_in, mu_shift, w0, w_decay_up, a0, w_aaa_up, w_gate_up, k_k, k_a, r_k, lnx_g, lnx_b, w_rwkv_out, s5_lam_re, s5_lam_im, s5_log_dt, s5_b_re, s5_b_im, s5_c_re, s5_c_im, s5_d, glu_w1, glu_b1, glu_w2, glu_b2, w_merge_out, w_ffn_gate, w_ffn_up, w_ffn_down):
    params = dict(zip(_PARAM_NAMES, (norm_pre_mix, norm_post_mix, norm_pre_ffn, norm_post_ffn, w_in, mu_shift,
                                     w0, w_decay_up, a0, w_aaa_up, w_gate_up, k_k, k_a, r_k, lnx_g, lnx_b,
                                     w_rwkv_out, s5_lam_re, s5_lam_im, s5_log_dt, s5_b_re, s5_b_im, s5_c_re,
                                     s5_c_im, s5_d, glu_w1, glu_b1, glu_w2, glu_b2, w_merge_out,
                                     w_ffn_gate, w_ffn_up, w_ffn_down)))
    return _forward(x_prompt, x_sample, state_shift, state_wkv, state_s5_re, state_s5_im, params)
```

```python
import functools
import math

import jax
import jax.numpy as jnp
from jax import lax
from jax.experimental import pallas as pl
from jax.experimental.pallas import tpu as pltpu

F32 = jnp.float32
BF16 = jnp.bfloat16

NORM_EPS = 1e-6
LNX_EPS = 64e-5
HEAD = 64
GROUP_HEADS = 2
GROUP_W = GROUP_HEADS * HEAD
GRAM_PASSES = 1
INV_PASSES = 1
APPLY_PASSES = 1
STATE_PASSES = 1
LORA_PAD = 128
S5_GROUP = 16
S5_STATE = 64
SLAB_GROUPS = 8
SUBLANES = 8
LANES = 128
STEP_SEQS = 16
VMEM_LIMIT = 56 * 1024 * 1024

NN = (((1,), (0,)), ((), ()))
NT = (((1,), (1,)), ((), ()))


def _dot(a, b, dims=NN):
    return lax.dot_general(a, b, dims, preferred_element_type=F32)


def _split2(x):
    hi = x.astype(BF16)
    lo = (x - hi.astype(F32)).astype(BF16)
    return hi, lo


def _split3(x):
    hi = x.astype(BF16)
    r1 = x - hi.astype(F32)
    mid = r1.astype(BF16)
    lo = (r1 - mid.astype(F32)).astype(BF16)
    return hi, mid, lo


def _mm1(a, b, dims=NN):
    return _dot(a.astype(BF16), b.astype(BF16), dims)


def _mm3(a, b, dims=NN):
    ah, al = _split2(a)
    bh, bl = _split2(b)
    return _dot(ah, bh, dims) + (_dot(ah, bl, dims) + _dot(al, bh, dims))


def _mm_exact_lhs(a_bf16, b):
    h, m, l = _split3(b)
    return _dot(a_bf16, h) + (_dot(a_bf16, m) + _dot(a_bf16, l))


def _rms(x, g):
    return x * lax.rsqrt(jnp.mean(x * x, axis=-1, keepdims=True) + NORM_EPS) * g


def _sigmoid(x):
    return 1.0 / (1.0 + jnp.exp(-x))


def _softplus(x):
    return jnp.maximum(x, 0.0) + jnp.log(1.0 + jnp.exp(-jnp.abs(x)))


def _gelu_tanh(x):
    c = math.sqrt(2.0 / math.pi)
    return 0.5 * x * (1.0 + jnp.tanh(c * (x + 0.044715 * (x * x * x))))


def _const_spec(shape, single_buffer=False):
    idx = lambda *_: (0,) * len(shape)
    if single_buffer:
        return pl.BlockSpec(shape, idx, pipeline_mode=pl.Buffered(1))
    return pl.BlockSpec(shape, idx)


def _proj_kernel(x_ref, g_ref, w_ref, pr_ref, u_ref, gt_ref, *, c_shift, c_u):
    hb = _rms(x_ref[...], g_ref[...]).astype(BF16)
    pr_ref[...] = _dot(hb, w_ref[:, :c_shift])
    u_ref[...] = _dot(hb, w_ref[:, c_shift:c_shift + c_u])
    gt_ref[...] = _dot(hb, w_ref[:, c_shift + c_u:])


def _proj(x2d, g, w_in_bf16, c_shift, c_u, tm):
    rows, d = x2d.shape
    cols = w_in_bf16.shape[1]
    c_g = cols - c_shift - c_u
    row = lambda w: pl.BlockSpec((tm, w), lambda i: (i, 0))
    return pl.pallas_call(
        functools.partial(_proj_kernel, c_shift=c_shift, c_u=c_u),
        grid=(rows // tm,),
        in_specs=[row(d), _const_spec((1, d)), _const_spec((d, cols), True)],
        out_specs=[row(c_shift), row(c_u), row(c_g)],
        out_shape=[jax.ShapeDtypeStruct((rows, c_shift), F32),
                   jax.ShapeDtypeStruct((rows, c_u), F32),
                   jax.ShapeDtypeStruct((rows, c_g), F32)],
        compiler_params=pltpu.CompilerParams(dimension_semantics=("parallel",),
                                             vmem_limit_bytes=VMEM_LIMIT),
        name="proj",
    )(x2d, g, w_in_bf16)


def _head_sum(x, e_ref):
    rows, width = x.shape
    n_lg = width // LANES
    hi, lo = _split2(x)
    stacked = jnp.concatenate([part[:, j * LANES:(j + 1) * LANES] for part in (hi, lo) for j in range(n_lg)], axis=0)
    sums = _dot(stacked, e_ref[...])
    return jnp.concatenate([sums[j * rows:(j + 1) * rows] + sums[(n_lg + j) * rows:(n_lg + j + 1) * rows]
                            for j in range(n_lg)], axis=1)


def _rwkv_token_prep(xr, w, e_ref):
    width = w["w0"].shape[-1]
    r = xr[:, :width]
    k = xr[:, width:2 * width]
    v = xr[:, 2 * width:3 * width]
    lo = xr[:, 3 * width:3 * width + LORA_PAD]
    wl = w["w0"][...] + _mm1(jnp.tanh(lo), w["wd"][...])
    lw = -jnp.exp(-_softplus(-wl) - 0.5)
    a = _sigmoid(w["a0"][...] + _mm1(lo, w["wa"][...]))
    g = _mm1(_sigmoid(lo), w["wg"][...])
    kk = k * w["k_k"][...]
    kk = kk * lax.rsqrt(jnp.maximum(_head_sum(kk * kk, e_ref), 1e-24))
    kmod = k * (1.0 + (a - 1.0) * w["k_a"][...])
    return r, kmod, v, kk, kk * a, lw, g


def _rwkv_post(o, r, kmod, v, g, w, e_ref):
    inv_n = 1.0 / HEAD
    mu = _head_sum(o, e_ref) * inv_n
    oc = o - mu
    var = _head_sum(oc * oc, e_ref) * inv_n
    on = oc * lax.rsqrt(var + LNX_EPS) * w["lnx_g"][...] + w["lnx_b"][...]
    bonus = _head_sum(r * kmod * w["r_k"][...], e_ref) * v
    return (on + bonus) * g


_RWKV_W_NAMES = ("mu", "w0", "a0", "k_k", "k_a", "r_k", "lnx_g", "lnx_b", "wd", "wa", "wg")


def _mm(a, b, dims=NN, passes=1):
    return _mm3(a, b, dims) if passes == 3 else _mm1(a, b, dims)


def _tri_inverses(mats, nilpotency, passes):
    n = mats[0].shape[0]
    rows = lax.broadcasted_iota(jnp.int32, (n, n), 0)
    cols = lax.broadcasted_iota(jnp.int32, (n, n), 1)
    eye = jnp.where(rows == cols, 1.0, 0.0).astype(F32)
    ps = [-a for a in mats]
    ts = [eye + p for p in ps]
    covered = 2
    while covered < nilpotency:
        ps = [_mm(p, p, NN, passes) for p in ps]
        ts = [t + _mm(t, p, NN, passes) for t, p in zip(ts, ps)]
        covered *= 2
    return ts


def _rwkv_chunk_kernel(pr_ref, shift0_ref, s0_ref, *rest, chunk, width, n_seq):
    n_w = len(_RWKV_W_NAMES)
    w = dict(zip(_RWKV_W_NAMES, rest[:n_w]))
    e_ref, tri_ref = rest[n_w], rest[n_w + 1]
    oa_ref, s1_ref = rest[n_w + 2], rest[n_w + 3]
    carry_ref, z_ref = rest[n_w + 4], rest[n_w + 5]
    c = pl.program_id(1)
    n_groups = width // GROUP_W
    C = chunk
    GC = GROUP_HEADS * C

    lane_head = lax.broadcasted_iota(jnp.int32, (1, GROUP_W), 1) // HEAD
    head_masks = [lane_head == h for h in range(GROUP_HEADS)]

    def stack(x):
        return jnp.concatenate([jnp.where(m, x, 0.0) for m in head_masks], axis=0)

    def collapse(xs):
        out = xs[:C]
        for h in range(1, GROUP_HEADS):
            out = out + xs[h * C:(h + 1) * C]
        return out

    @pl.when(c == 0)
    def _():
        carry_ref[...] = shift0_ref[...]
        for q in range(n_seq):
            for gi in range(n_groups):
                z_ref[q, gi] = jnp.zeros((GROUP_W, GROUP_W), F32)
                for h in range(GROUP_HEADS):
                    z_ref[q, gi, h * HEAD:(h + 1) * HEAD, h * HEAD:(h + 1) * HEAD] = s0_ref[q, gi * GROUP_HEADS + h]

    pr = pr_ref[...].reshape(n_seq * C, pr_ref.shape[-1])
    row_id = lax.broadcasted_iota(jnp.int32, pr.shape, 0)
    pr_prev = pltpu.roll(pr, 1, axis=0)
    for q in range(n_seq):
        pr_prev = jnp.where(row_id == q * C, carry_ref[q], pr_prev)
        carry_ref[q] = pr[(q + 1) * C - 1:(q + 1) * C, :]
    xr = pr + (pr_prev - pr) * w["mu"][...]
    r, kmod, v, kk, bvec, lw, g = _rwkv_token_prep(xr, w, e_ref)

    cum = _mm_exact_lhs(tri_ref[...], lw)
    g_in = jnp.exp(cum)
    g_ex = jnp.exp(cum - lw)
    g_neg = jnp.exp(-cum)
    r_t = r * g_in
    k_t = kmod * g_neg
    b_t = bvec * g_neg
    kap_t = kk * g_ex

    ri = lax.broadcasted_iota(jnp.int32, (GC, GC), 0)
    ci = lax.broadcasted_iota(jnp.int32, (GC, GC), 1)
    same_head = (ri // C) == (ci // C)
    strict = same_head & (ri > ci)
    incl = same_head & (ri >= ci)
    zi = lax.broadcasted_iota(jnp.int32, (GROUP_W, GROUP_W), 0) // HEAD
    zj = lax.broadcasted_iota(jnp.int32, (GROUP_W, GROUP_W), 1) // HEAD
    block_diag = zi == zj

    chains = [(q, gi) for q in range(n_seq) for gi in range(n_groups)]
    blk = lambda a, q, gi: a[q * C:(q + 1) * C, gi * GROUP_W:(gi + 1) * GROUP_W]
    kap_s = [stack(blk(kap_t, q, gi)) for q, gi in chains]
    v_s = [stack(blk(v, q, gi)) for q, gi in chains]
    grams = []
    for i, (q, gi) in enumerate(chains):
        left = jnp.concatenate([kap_s[i], stack(blk(r_t, q, gi))], axis=0)
        right = jnp.concatenate([blk(k_t, q, gi)] * GROUP_HEADS + [blk(b_t, q, gi)] * GROUP_HEADS, axis=0)
        grams.append(_mm(left, right, NT, GRAM_PASSES))
    a_k = [jnp.where(strict, gm[:GC, :GC], 0.0) for gm in grams]
    a_b = [jnp.where(strict, gm[:GC, GC:], 0.0) for gm in grams]
    a_rk = [jnp.where(incl, gm[GC:, :GC], 0.0) for gm in grams]
    a_rb = [jnp.where(incl, gm[GC:, GC:], 0.0) for gm in grams]
    t_inv = _tri_inverses(a_b, C, INV_PASSES)
    akv = [_mm(a, vs, NN, APPLY_PASSES) for a, vs in zip(a_k, v_s)]
    tw = [_mm(t, jnp.concatenate([ks, x], axis=1), NN, APPLY_PASSES) for t, ks, x in zip(t_inv, kap_s, akv)]
    k_hat = [collapse(x[:, :GROUP_W]) for x in tw]
    v_hat = [collapse(x[:, GROUP_W:]) for x in tw]
    arkv = [collapse(_mm(a, vs, NN, APPLY_PASSES)) for a, vs in zip(a_rk, v_s)]

    zs = [z_ref[q, gi] for q, gi in chains]
    pz = [_mm(jnp.concatenate([kh, blk(r_t, q, gi)], axis=0), z, NT, STATE_PASSES)
          for kh, z, (q, gi) in zip(k_hat, zs, chains)]
    us = [p[:C] + vh for p, vh in zip(pz, v_hat)]
    arbu = [collapse(_mm(a, stack(u), NN, APPLY_PASSES)) for a, u in zip(a_rb, us)]
    o_blk = [p[C:] + x - y for p, x, y in zip(pz, arkv, arbu)]
    for i, (q, gi) in enumerate(chains):
        cum_blk = blk(cum, q, gi)
        cum_last = cum_blk[C - 1:C, :]
        g_end = jnp.exp(cum_last - cum_blk)
        vu_t = jnp.concatenate([blk(v, q, gi), us[i]], axis=0).T
        kb = jnp.concatenate([blk(kmod, q, gi) * g_end, -blk(bvec, q, gi) * g_end], axis=0)
        z_ref[q, gi] = zs[i] * jnp.exp(cum_last) + jnp.where(block_diag, _mm(vu_t, kb, NN, STATE_PASSES), 0.0)

    o = jnp.concatenate([jnp.concatenate(o_blk[q * n_groups:(q + 1) * n_groups], axis=1) for q in range(n_seq)], axis=0)
    oa_ref[...] = _rwkv_post(o, r, kmod, v, g, w, e_ref).astype(oa_ref.dtype).reshape(oa_ref.shape)

    @pl.when(c == pl.num_programs(1) - 1)
    def _():
        for q in range(n_seq):
            for gi in range(n_groups):
                for h in range(GROUP_HEADS):
                    s1_ref[q, gi * GROUP_HEADS + h] = z_ref[q, gi, h * HEAD:(h + 1) * HEAD, h * HEAD:(h + 1) * HEAD]


def _rwkv_weight_inputs(wts):
    return [wts[n] for n in _RWKV_W_NAMES]


def _rwkv_weight_specs(wts):
    return [_const_spec(wts[n].shape) for n in _RWKV_W_NAMES]


def _rwkv_chunked(pr3d, shift0, wkv0, wts, e_mat, chunk, n_seq):
    bsz, t, c_shift = pr3d.shape
    heads = wkv0.shape[1]
    width = heads * HEAD
    tri = jnp.kron(jnp.eye(n_seq, dtype=F32), jnp.tril(jnp.ones((chunk, chunk), F32))).astype(BF16)
    kern = functools.partial(_rwkv_chunk_kernel, chunk=chunk, width=width, n_seq=n_seq)
    return pl.pallas_call(
        kern,
        grid=(bsz // n_seq, t // chunk),
        in_specs=[pl.BlockSpec((n_seq, chunk, c_shift), lambda b, c: (b, c, 0)),
                  pl.BlockSpec((n_seq, 1, c_shift), lambda b, c: (b, 0, 0)),
                  pl.BlockSpec((n_seq, heads, HEAD, HEAD), lambda b, c: (b, 0, 0, 0))]
                 + _rwkv_weight_specs(wts)
                 + [_const_spec(e_mat.shape), _const_spec(tri.shape)],
        out_specs=[pl.BlockSpec((n_seq, chunk, width), lambda b, c: (b, c, 0)),
                   pl.BlockSpec((n_seq, heads, HEAD, HEAD), lambda b, c: (b, 0, 0, 0))],
        out_shape=[jax.ShapeDtypeStruct((bsz, t, width), BF16),
                   jax.ShapeDtypeStruct((bsz, heads, HEAD, HEAD), F32)],
        scratch_shapes=[pltpu.VMEM((n_seq, 1, c_shift), F32),
                        pltpu.VMEM((n_seq, width // GROUP_W, GROUP_W, GROUP_W), F32)],
        compiler_params=pltpu.CompilerParams(dimension_semantics=("parallel", "arbitrary"),
                                             vmem_limit_bytes=VMEM_LIMIT),
        name="rwkv_chunk",
    )(pr3d, shift0[:, None, :], wkv0, *_rwkv_weight_inputs(wts), e_mat, tri)


def _rwkv_step_kernel(pr_ref, shift0_ref, s_ref, *rest):
    n_w = len(_RWKV_W_NAMES)
    w = dict(zip(_RWKV_W_NAMES, rest[:n_w]))
    e_ref, oa_ref, s1_ref = rest[n_w], rest[n_w + 1], rest[n_w + 2]
    n_seq, heads = s_ref.shape[0], s_ref.shape[1]
    pr = pr_ref[...]
    xr = pr + (shift0_ref[...] - pr) * w["mu"][...]
    r, kmod, v, kk, bvec, lw, g = _rwkv_token_prep(xr, w, e_ref)
    dec = jnp.exp(lw)
    head = lambda a, h: a[:, h * HEAD:(h + 1) * HEAD]

    left = jnp.concatenate([kk, r * dec], axis=0).astype(BF16)
    seq_of_row = lax.broadcasted_iota(jnp.int32, (2 * n_seq, HEAD), 0) % n_seq
    reads = []
    for h in range(heads):
        lh = head(left, h)
        acc = jnp.zeros((2 * n_seq, HEAD), F32)
        for q in range(n_seq):
            acc = jnp.where(seq_of_row == q, _dot(lh, s_ref[q, h].astype(BF16), NT), acc)
        reads.append(acc)
    reads = jnp.concatenate(reads, axis=1)
    u, s_r = reads[:n_seq], reads[n_seq:]
    o = s_r + _head_sum(r * kmod, e_ref) * v - _head_sum(r * bvec, e_ref) * u
    oa_ref[...] = _rwkv_post(o, r, kmod, v, g, w, e_ref).astype(oa_ref.dtype)

    lt_hi, lt_lo = _split2(jnp.concatenate([v, u], axis=0).T)
    rt_hi, rt_lo = _split2(jnp.concatenate([kmod, -bvec], axis=0))
    lt3 = jnp.concatenate([lt_hi, lt_hi, lt_lo], axis=1)
    rt3 = jnp.concatenate([rt_hi, rt_lo, rt_hi], axis=0)
    seq_of_row3 = lax.broadcasted_iota(jnp.int32, (6 * n_seq, HEAD), 0) % n_seq
    for h in range(heads):
        lt_h = lt3[h * HEAD:(h + 1) * HEAD, :]
        rt_h, dec_h = head(rt3, h), head(dec, h)
        for q in range(n_seq):
            upd = _dot(lt_h, jnp.where(seq_of_row3 == q, rt_h, jnp.zeros_like(rt_h)))
            s1_ref[q, h] = s_ref[q, h] * dec_h[q:q + 1, :] + upd


def _rwkv_step(pr2d, shift0, wkv0, wts, e_mat, n_seq):
    bsz, c_shift = pr2d.shape
    heads = wkv0.shape[1]
    width = heads * HEAD
    rows = lambda wd: pl.BlockSpec((n_seq, wd), lambda i: (i, 0))
    st = pl.BlockSpec((n_seq, heads, HEAD, HEAD), lambda i: (i, 0, 0, 0))
    return pl.pallas_call(
        _rwkv_step_kernel,
        grid=(bsz // n_seq,),
        in_specs=[rows(c_shift), rows(c_shift), st] + _rwkv_weight_specs(wts) + [_const_spec(e_mat.shape)],
        out_specs=[rows(width), st],
        out_shape=[jax.ShapeDtypeStruct((bsz, width), BF16), jax.ShapeDtypeStruct(wkv0.shape, F32)],
        compiler_params=pltpu.CompilerParams(dimension_semantics=("parallel",), vmem_limit_bytes=VMEM_LIMIT),
        name="rwkv_step",
    )(pr2d, shift0, wkv0, *_rwkv_weight_inputs(wts), e_mat)


def _s5_disc_kernel(lre_ref, lim_ref, ldt_ref, bre_ref, bim_ref, lbr_ref, lbi_ref, bbr_ref, bbi_ref):
    lam_re, lam_im = lre_ref[...], lim_ref[...]
    dt = jnp.exp(ldt_ref[...])
    mag = jnp.exp(lam_re * dt)
    ang = lam_im * dt
    lb_re, lb_im = mag * jnp.cos(ang), mag * jnp.sin(ang)
    nr, ni = lb_re - 1.0, lb_im
    den = lam_re * lam_re + lam_im * lam_im
    f_re = (nr * lam_re + ni * lam_im) / den
    f_im = (ni * lam_re - nr * lam_im) / den
    b_re, b_im = bre_ref[...], bim_ref[...]
    lbr_ref[...] = lb_re
    lbi_ref[...] = lb_im
    bbr_ref[...] = f_re * b_re - f_im * b_im
    bbi_ref[...] = f_re * b_im + f_im * b_re


def _s5_discretise(lam_re, lam_im, log_dt, b_re, b_im):
    g, p = lam_re.shape
    full = lambda a: _const_spec(a.shape)
    args = (lam_re[..., None], lam_im[..., None], log_dt[:, None, None], b_re, b_im)
    return pl.pallas_call(
        _s5_disc_kernel,
        grid=(1,),
        in_specs=[full(a) for a in args],
        out_specs=[_const_spec((g, p, 1))] * 2 + [full(b_re)] * 2,
        out_shape=[jax.ShapeDtypeStruct((g, p, 1), F32)] * 2 + [jax.ShapeDtypeStruct(b_re.shape, F32)] * 2,
        name="s5_discretise",
    )(*args)


def _s5_kernel(u_ref, re0_ref, im0_ref, lbr_ref, lbi_ref, wb_ref, wc_ref, d_ref,
               h_ref, re1_ref, im1_ref, bu_ref, xs_ref, *, tt, n_slabs):
    t_blk = pl.program_id(1)
    rows = SUBLANES * tt
    s_w = SLAB_GROUPS * S5_STATE
    u_w = SLAB_GROUPS * S5_GROUP

    @pl.when(t_blk == 0)
    def _():
        re1_ref[...] = re0_ref[...]
        im1_ref[...] = im0_ref[...]

    u = jnp.swapaxes(u_ref[...], 0, 1).reshape(rows, n_slabs * u_w)
    ys = []
    for s in range(n_slabs):
        us = u[:, s * u_w:(s + 1) * u_w]
        bu_ref[...] = _mm1(us, wb_ref[s])
        st = slice(s * s_w, (s + 1) * s_w)
        lbr = jnp.broadcast_to(lbr_ref[:, st], (SUBLANES, s_w))
        lbi = jnp.broadcast_to(lbi_ref[:, st], (SUBLANES, s_w))

        def step(t, carry):
            xr, xi = carry
            at_t = pl.ds(pl.multiple_of(t * SUBLANES, SUBLANES), SUBLANES)
            nr = lbr * xr - lbi * xi + bu_ref[at_t, :s_w]
            ni = lbr * xi + lbi * xr + bu_ref[at_t, s_w:]
            xs_ref[at_t, :s_w] = nr
            xs_ref[at_t, s_w:] = ni
            return nr, ni

        xr, xi = lax.fori_loop(0, tt, step, (re1_ref[:, st], im1_ref[:, st]), unroll=8)
        re1_ref[:, st] = xr
        im1_ref[:, st] = xi
        y = _mm1(xs_ref[...], wc_ref[s]) + d_ref[:, s * u_w:(s + 1) * u_w] * us
        ys.append(_gelu_tanh(y))
    h = jnp.concatenate(ys, axis=1).reshape(tt, SUBLANES, n_slabs * u_w)
    h_ref[...] = jnp.swapaxes(h, 0, 1).astype(h_ref.dtype)


def _s5(u_blocks, re0, im0, lb_re, lb_im, wb, wc, d_skip, tt):
    n_slabs = wb.shape[0]
    n_state = re0.shape[1]
    blk = (SUBLANES, tt, u_blocks.shape[-1])
    grid = (u_blocks.shape[0] // SUBLANES, u_blocks.shape[1] // tt)
    rows = SUBLANES * tt
    u_spec = pl.BlockSpec(blk, lambda i, j: (i, j, 0))
    st_spec = pl.BlockSpec((SUBLANES, n_state), lambda i, j: (i, 0))
    full = lambda a: _const_spec(a.shape)
    return pl.pallas_call(
        functools.partial(_s5_kernel, tt=tt, n_slabs=n_slabs),
        grid=grid,
        in_specs=[u_spec, st_spec, st_spec, full(lb_re), full(lb_im), full(wb), full(wc), full(d_skip)],
        out_specs=[u_spec, st_spec, st_spec],
        out_shape=[jax.ShapeDtypeStruct(u_blocks.shape, BF16),
                   jax.ShapeDtypeStruct(re0.shape, F32), jax.ShapeDtypeStruct(im0.shape, F32)],
        scratch_shapes=[pltpu.VMEM((rows, 2 * SLAB_GROUPS * S5_STATE), F32)] * 2,
        compiler_params=pltpu.CompilerParams(dimension_semantics=("parallel", "arbitrary"),
                                             vmem_limit_bytes=VMEM_LIMIT),
        name="s5_scan",
    )(u_blocks, re0, im0, lb_re, lb_im, wb, wc, d_skip)


def _s5_step_kernel(u_ref, re0_ref, im0_ref, lbr_ref, lbi_ref, wb_ref, wc_ref, d_ref,
                    h_ref, re1_ref, im1_ref, *, n_slabs):
    s_w = SLAB_GROUPS * S5_STATE
    u_w = SLAB_GROUPS * S5_GROUP
    u = u_ref[...]
    ys = []
    for s in range(n_slabs):
        us = u[:, s * u_w:(s + 1) * u_w]
        bu = _mm1(us, wb_ref[s])
        st = slice(s * s_w, (s + 1) * s_w)
        lbr, lbi = lbr_ref[:, st], lbi_ref[:, st]
        xr, xi = re0_ref[:, st], im0_ref[:, st]
        nr = lbr * xr - lbi * xi + bu[:, :s_w]
        ni = lbr * xi + lbi * xr + bu[:, s_w:]
        re1_ref[:, st] = nr
        im1_ref[:, st] = ni
        y = _mm1(jnp.concatenate([nr, ni], axis=1), wc_ref[s]) + d_ref[:, s * u_w:(s + 1) * u_w] * us
        ys.append(_gelu_tanh(y))
    h_ref[...] = jnp.concatenate(ys, axis=1).astype(h_ref.dtype)


def _s5_step(u2d, re0, im0, lb_re, lb_im, wb, wc, d_skip):
    full = lambda a: _const_spec(a.shape)
    args = (u2d, re0, im0, lb_re, lb_im, wb, wc, d_skip)
    return pl.pallas_call(
        functools.partial(_s5_step_kernel, n_slabs=wb.shape[0]),
        grid=(1,),
        in_specs=[full(a) for a in args],
        out_specs=[full(u2d), full(re0), full(im0)],
        out_shape=[jax.ShapeDtypeStruct(u2d.shape, BF16),
                   jax.ShapeDtypeStruct(re0.shape, F32), jax.ShapeDtypeStruct(im0.shape, F32)],
        compiler_params=pltpu.CompilerParams(vmem_limit_bytes=VMEM_LIMIT),
        name="s5_step",
    )(*args)


def _block_diag_slabs(m):
    g, a, b = m.shape
    eye = jnp.eye(SLAB_GROUPS, dtype=m.dtype)
    m4 = m.reshape(g // SLAB_GROUPS, SLAB_GROUPS, a, b)
    return jnp.einsum("sgab,gh->sgahb", m4, eye).reshape(g // SLAB_GROUPS, SLAB_GROUPS * a, SLAB_GROUPS * b)


def _tail_kernel(x_ref, oa_ref, hg_ref, gt_ref, wro_ref, w1_ref, b1_ref, w2_ref, b2_ref, wmo_ref,
                 npm_ref, nf_ref, npf_ref, wg_ref, wu_ref, wd_ref, y_ref):
    d = x_ref.shape[-1]
    tm = x_ref.shape[0]
    n_sub = 2 if tm % 32 == 0 else 1
    subs = [slice(i * (tm // n_sub), (i + 1) * (tm // n_sub)) for i in range(n_sub)]
    hg = [hg_ref[s, :] for s in subs]
    a_out = [_dot(oa_ref[s, :], wro_ref[...]) for s in subs]
    b_lin = [_dot(h, w1_ref[...]) + b1_ref[...] for h in hg]
    b_gate = [_dot(h, w2_ref[...]) + b2_ref[...] for h in hg]
    merged = [(_sigmoid(gt_ref[s, :d]) * a + _sigmoid(gt_ref[s, d:]) * (bl * _sigmoid(bg))).astype(BF16)
              for s, a, bl, bg in zip(subs, a_out, b_lin, b_gate)]
    mix = [_dot(m, wmo_ref[...]) for m in merged]
    x1 = [x_ref[s, :] + _rms(m, npm_ref[...]) for s, m in zip(subs, mix)]
    hb = [_rms(x, nf_ref[...]).astype(BF16) for x in x1]
    gate = [_dot(h, wg_ref[...]) for h in hb]
    up = [_dot(h, wu_ref[...]) for h in hb]
    act = [(g * _sigmoid(g) * u).astype(BF16) for g, u in zip(gate, up)]
    f = [_dot(a, wd_ref[...]) for a in act]
    for s, x, ff in zip(subs, x1, f):
        y_ref[s, :] = x + _rms(ff, npf_ref[...])


def _tail(x2d, oa, hg, gates, tw, tm):
    rows, d = x2d.shape
    row = lambda a: pl.BlockSpec((tm, a.shape[1]), lambda i: (i, 0))
    wnames = ("wro", "w1", "b1", "w2", "b2", "wmo", "npm", "nf", "npf", "wg", "wu", "wd")
    wargs = [tw[n] for n in wnames]
    return pl.pallas_call(
        _tail_kernel,
        grid=(rows // tm,),
        in_specs=[row(x2d), row(oa), row(hg), row(gates)] + [_const_spec(a.shape, True) for a in wargs],
        out_specs=pl.BlockSpec((tm, d), lambda i: (i, 0)),
        out_shape=jax.ShapeDtypeStruct((rows, d), F32),
        compiler_params=pltpu.CompilerParams(dimension_semantics=("parallel",),
                                             vmem_limit_bytes=VMEM_LIMIT),
        name="tail",
    )(x2d, oa, hg, gates, *wargs)


def _pick_tile(rows, target):
    t = min(rows, target)
    assert rows % t == 0
    return t


def _layer(x, shift0, wkv0, re0, im0, lw, *, chunk, n_seq, s5_tt, row_tile):
    bsz, t, d = x.shape
    rows = bsz * t
    c_shift = shift0.shape[-1]
    c_u = lw["d_skip"].shape[-1]
    tm = _pick_tile(rows, row_tile)
    x2d = x.reshape(rows, d)
    pr, u, gates = _proj(x2d, lw["norm_pre_mix"], lw["w_in"], c_shift, c_u, tm)

    n_state = re0.shape[1] * re0.shape[2]
    s5_args = (re0.reshape(bsz, n_state), im0.reshape(bsz, n_state),
               lw["lb_re"], lw["lb_im"], lw["wb"], lw["wc"], lw["d_skip"])
    if t == 1:
        oa, wkv1 = _rwkv_step(pr, shift0, wkv0, lw["rwkv"], lw["e_mat"], n_seq=STEP_SEQS)
        pr_last = pr
        hg, re1, im1 = _s5_step(u, *s5_args)
    else:
        oa, wkv1 = _rwkv_chunked(pr.reshape(bsz, t, c_shift), shift0, wkv0, lw["rwkv"], lw["e_mat"], chunk, n_seq)
        oa = oa.reshape(rows, -1)
        pr_last = pr.reshape(bsz, t, c_shift)[:, -1]
        hg, re1, im1 = _s5(u.reshape(bsz, t, c_u), *s5_args, tt=s5_tt)
    y = _tail(x2d, oa, hg.reshape(rows, c_u), gates, lw["tail"], tm)
    return y.reshape(x.shape), pr_last, wkv1, re1.reshape(re0.shape), im1.reshape(im0.shape)


def _prepare_layer_weights(l, p):
    row = lambda a: a[l][None, :].astype(F32)
    width = p["w0"].shape[-1]
    n_dec, n_aaa, n_gate = p["w_decay_up"].shape[1], p["w_aaa_up"].shape[1], p["w_gate_up"].shape[1]
    assert n_dec + n_aaa + n_gate == LORA_PAD

    def lora_pad(wup, start):
        return jnp.zeros((LORA_PAD, width), F32).at[start:start + wup.shape[0]].set(wup).astype(BF16)

    rwkv = {
        "mu": row(p["mu_shift"]), "w0": row(p["w0"]), "a0": row(p["a0"]), "k_k": row(p["k_k"]),
        "k_a": row(p["k_a"]), "r_k": row(p["r_k"]), "lnx_g": row(p["lnx_g"]), "lnx_b": row(p["lnx_b"]),
        "wd": lora_pad(p["w_decay_up"][l], 0),
        "wa": lora_pad(p["w_aaa_up"][l], n_dec),
        "wg": lora_pad(p["w_gate_up"][l], n_dec + n_aaa),
    }
    head_id = jnp.arange(LANES) // HEAD
    e_mat = (head_id[:, None] == head_id[None, :]).astype(BF16)

    lb_re, lb_im, bb_re, bb_im = _s5_discretise(p["s5_lam_re"][l], p["s5_lam_im"][l], p["s5_log_dt"][l],
                                                p["s5_b_re"][l], p["s5_b_im"][l])
    n_state = lb_re.shape[0] * lb_re.shape[1]
    to_in = lambda bb: _block_diag_slabs(jnp.swapaxes(bb, 1, 2))
    to_out = lambda cc: _block_diag_slabs(jnp.swapaxes(cc, 1, 2))
    wb = jnp.concatenate([to_in(bb_re), to_in(bb_im)], axis=-1).astype(BF16)
    wc = jnp.concatenate([to_out(p["s5_c_re"][l]), -to_out(p["s5_c_im"][l])], axis=1).astype(BF16)

    bf = lambda a: a[l].astype(BF16)
    tail = {
        "wro": bf(p["w_rwkv_out"]), "w1": bf(p["glu_w1"]), "b1": row(p["glu_b1"]), "w2": bf(p["glu_w2"]),
        "b2": row(p["glu_b2"]), "wmo": bf(p["w_merge_out"]), "npm": row(p["norm_post_mix"]),
        "nf": row(p["norm_pre_ffn"]), "npf": row(p["norm_post_ffn"]),
        "wg": bf(p["w_ffn_gate"]), "wu": bf(p["w_ffn_up"]), "wd": bf(p["w_ffn_down"]),
    }
    return {
        "norm_pre_mix": row(p["norm_pre_mix"]), "w_in": bf(p["w_in"]), "rwkv": rwkv, "e_mat": e_mat,
        "lb_re": lb_re.reshape(1, n_state), "lb_im": lb_im.reshape(1, n_state), "wb": wb, "wc": wc,
        "d_skip": row(p["s5_d"]), "tail": tail,
    }


_PARAM_NAMES = ("norm_pre_mix", "norm_post_mix", "norm_pre_ffn", "norm_post_ffn", "w_in", "mu_shift",
                "w0", "w_decay_up", "a0", "w_aaa_up", "w_gate_up", "k_k", "k_a", "r_k", "lnx_g", "lnx_b",
                "w_rwkv_out", "s5_lam_re", "s5_lam_im", "s5_log_dt", "s5_b_re", "s5_b_im", "s5_c_re",
                "s5_c_im", "s5_d", "glu_w1", "glu_b1", "glu_w2", "glu_b2", "w_merge_out",
                "w_ffn_gate", "w_ffn_up", "w_ffn_down")


def _forward(x_prompt, x_sample, state_shift, state_wkv, state_s5_re, state_s5_im, params,
             *, chunk=64, n_seq=4, s5_tt=128, row_tile=256):
    depth = params["w_in"].shape[0]
    heads = state_wkv.shape[2]
    bp = x_prompt.shape[0]
    yp, ys = x_prompt, x_sample
    outs_p, outs_s = [], []
    for l in range(depth):
        lw = _prepare_layer_weights(l, params)
        zp_shift = jnp.zeros((bp, state_shift.shape[-1]), F32)
        zp_wkv = jnp.zeros((bp, heads, HEAD, HEAD), F32)
        zp_s5 = jnp.zeros((bp,) + state_s5_re.shape[2:], F32)
        yp, *st_p = _layer(yp, zp_shift, zp_wkv, zp_s5, zp_s5, lw, chunk=chunk, n_seq=n_seq, s5_tt=s5_tt, row_tile=row_tile)
        ys, *st_s = _layer(ys, state_shift[l], state_wkv[l], state_s5_re[l], state_s5_im[l], lw,
                           chunk=chunk, n_seq=n_seq, s5_tt=s5_tt, row_tile=row_tile)
        outs_p.append(st_p)
        outs_s.append(st_s)
    stack = lambda outs, i, dt: jnp.stack([o[i] for o in outs]).astype(dt)
    dt_p, dt_s = x_prompt.dtype, x_sample.dtype
    return (yp, ys,
            stack(outs_p, 0, dt_p), stack(outs_p, 1, dt_p), stack(outs_p, 2, dt_p), stack(outs_p, 3, dt_p),
            stack(outs_s, 0, dt_s), stack(outs_s, 1, dt_s), stack(outs_s, 2, dt_s), stack(outs_s, 3, dt_s))


def kernel(x_prompt, x_sample, state_shift, state_wkv, state_s5_re, state_s5_im, norm_pre_mix, norm_post_mix, norm_pre_ffn, norm_post_ffn, w_in, mu_shift, w0, w_decay_up, a0, w_aaa_up, w_gate_up, k_k, k_a, r_k, lnx_g, lnx_b, w_rwkv_out, s5_lam_re, s5_lam_im, s5_log_dt, s5_b_re, s5_b_im, s5_c_re, s5_c_im, s5_d, glu_w1, glu_b1, glu_w2, glu_b2, w_merge_out, w_ffn_gate, w_ffn_up, w_ffn_down):
    params = dict(zip(_PARAM_NAMES, (norm_pre_mix, norm_post_mix, norm_pre_ffn, norm_post_ffn, w_in, mu_shift,
                                     w0, w_decay_up, a0, w_aaa_up, w_gate_up, k_k, k_a, r_k, lnx_g, lnx_b,
                                     w_rwkv_out, s5_lam_re, s5_lam_im, s5_log_dt, s5_b_re, s5_b_im, s5_c_re,
                                     s5_c_im, s5_d, glu_w1, glu_b1, glu_w2, glu_b2, w_merge_out,
                                     w_ffn_gate, w_ffn_up, w_ffn_down)))
    return _forward(x_prompt, x_sample, state_shift, state_wkv, state_s5_re, state_s5_im, params)
```

```python
import functools
import math

import jax
import jax.numpy as jnp
from jax import lax
from jax.experimental import pallas as pl
from jax.experimental.pallas import tpu as pltpu

F32 = jnp.float32
BF16 = jnp.bfloat16

NORM_EPS = 1e-6
LNX_EPS = 64e-5
HEAD = 64
GROUP_HEADS = 4
GROUP_W = GROUP_HEADS * HEAD
LORA_PAD = 128
S5_GROUP = 16
S5_STATE = 64
SLAB_GROUPS = 8
SUBLANES = 8
LANES = 128
MXU_DIM = 256
STEP_SEQS = 16
VMEM_LIMIT = 56 * 1024 * 1024

NN = (((1,), (0,)), ((), ()))
NT = (((1,), (1,)), ((), ()))


def _dot(a, b, dims=NN):
    return lax.dot_general(a, b, dims, preferred_element_type=F32)


def _split2(x):
    hi = x.astype(BF16)
    lo = (x - hi.astype(F32)).astype(BF16)
    return hi, lo


def _split3(x):
    hi = x.astype(BF16)
    r1 = x - hi.astype(F32)
    mid = r1.astype(BF16)
    lo = (r1 - mid.astype(F32)).astype(BF16)
    return hi, mid, lo


def _mm1(a, b, dims=NN):
    return _dot(a.astype(BF16), b.astype(BF16), dims)


def _mm_exact_lhs(a_bf16, b):
    h, m, l = _split3(b)
    return _dot(a_bf16, h) + (_dot(a_bf16, m) + _dot(a_bf16, l))


def _rms(x, g):
    return x * lax.rsqrt(jnp.mean(x * x, axis=-1, keepdims=True) + NORM_EPS) * g


def _sigmoid(x):
    return 1.0 / (1.0 + jnp.exp(-x))


def _gelu_tanh(x):
    c = math.sqrt(2.0 / math.pi)
    return 0.5 * x * (1.0 + jnp.tanh(c * (x + 0.044715 * (x * x * x))))


def _const_spec(shape, single_buffer=False):
    idx = lambda *_: (0,) * len(shape)
    if single_buffer:
        return pl.BlockSpec(shape, idx, pipeline_mode=pl.Buffered(1))
    return pl.BlockSpec(shape, idx)


def _proj_kernel(x_ref, g_ref, w_ref, pr_ref, u_ref, gt_ref, *, c_shift, c_u):
    hb = _rms(x_ref[...], g_ref[...]).astype(BF16)
    pr_ref[...] = _dot(hb, w_ref[:, :c_shift])
    u_ref[...] = _dot(hb, w_ref[:, c_shift:c_shift + c_u])
    gt_ref[...] = _dot(hb, w_ref[:, c_shift + c_u:])


def _proj(x2d, g, w_in_bf16, c_shift, c_u, tm):
    rows, d = x2d.shape
    cols = w_in_bf16.shape[1]
    c_g = cols - c_shift - c_u
    row = lambda w: pl.BlockSpec((tm, w), lambda i: (i, 0))
    return pl.pallas_call(
        functools.partial(_proj_kernel, c_shift=c_shift, c_u=c_u),
        grid=(rows // tm,),
        in_specs=[row(d), _const_spec((1, d)), _const_spec((d, cols), True)],
        out_specs=[row(c_shift), row(c_u), row(c_g)],
        out_shape=[jax.ShapeDtypeStruct((rows, c_shift), F32),
                   jax.ShapeDtypeStruct((rows, c_u), F32),
                   jax.ShapeDtypeStruct((rows, c_g), F32)],
        compiler_params=pltpu.CompilerParams(dimension_semantics=("parallel",),
                                             vmem_limit_bytes=VMEM_LIMIT),
        name="proj",
    )(x2d, g, w_in_bf16)


def _head_sum(x, e_ref):
    rows, width = x.shape
    gw = e_ref.shape[0]
    n_lg = width // gw
    hi, lo = _split2(x)
    stacked = jnp.concatenate([part[:, j * gw:(j + 1) * gw] for part in (hi, lo) for j in range(n_lg)], axis=0)
    sums = _dot(stacked, e_ref[...])
    return jnp.concatenate([sums[j * rows:(j + 1) * rows] + sums[(n_lg + j) * rows:(n_lg + j + 1) * rows]
                            for j in range(n_lg)], axis=1)


def _rwkv_token_prep(xr, w, e_ref):
    width = w["w0"].shape[-1]
    r = xr[:, :width]
    k = xr[:, width:2 * width]
    v = xr[:, 2 * width:3 * width]
    lo = xr[:, 3 * width:3 * width + LORA_PAD]
    wl = w["w0"][...] + _mm1(jnp.tanh(lo), w["wd"][...])
    lw = -math.exp(-0.5) * _sigmoid(wl)
    a = _sigmoid(w["a0"][...] + _mm1(lo, w["wa"][...]))
    g = _mm1(_sigmoid(lo), w["wg"][...])
    kk = k * w["k_k"][...]
    kk = kk * lax.rsqrt(jnp.maximum(_head_sum(kk * kk, e_ref), 1e-24))
    kmod = k * (1.0 + (a - 1.0) * w["k_a"][...])
    return r, kmod, v, kk, kk * a, lw, g


def _rwkv_bonus(r, kmod, v, w, e_ref):
    return _head_sum(r * kmod * w["r_k"][...], e_ref) * v


def _rwkv_post(o, bonus, g, w, e_ref):
    inv_n = 1.0 / HEAD
    mu = _head_sum(o, e_ref) * inv_n
    oc = o - mu
    var = _head_sum(oc * oc, e_ref) * inv_n
    on = oc * lax.rsqrt(var + LNX_EPS) * w["lnx_g"][...] + w["lnx_b"][...]
    return (on + bonus) * g


_RWKV_W_NAMES = ("mu", "w0", "a0", "k_k", "k_a", "r_k", "lnx_g", "lnx_b", "wd", "wa", "wg")


def _rwkv_chunk_kernel(pr_ref, shift0_ref, s0_ref, *rest, chunk, width, n_seq):
    n_w = len(_RWKV_W_NAMES)
    w = dict(zip(_RWKV_W_NAMES, rest[:n_w]))
    e_ref, tri_ref = rest[n_w], rest[n_w + 1]
    oa_ref, s1_ref = rest[n_w + 2], rest[n_w + 3]
    carry_ref, z_ref, ops_ref, aux_ref, gall_ref = rest[n_w + 4:n_w + 9]
    step = pl.program_id(1)
    n_chunks = pl.num_programs(1) - 1
    w_slot = step % 2
    r_slot = 1 - w_slot
    n_groups = width // GROUP_W
    C = chunk
    GC = GROUP_HEADS * C

    def lane_block_masks(n_lanes, block):
        lane_block = lax.broadcasted_iota(jnp.int32, (1, n_lanes), 1) // block
        return [lane_block == h for h in range(GROUP_HEADS)]

    vec_masks = lane_block_masks(GROUP_W, HEAD)
    mat_masks = lane_block_masks(GC, C)

    def stack(x, masks):
        xb = x.astype(BF16)
        zero = jnp.zeros_like(xb)
        return jnp.concatenate([jnp.where(m, xb, zero) for m in masks], axis=0)

    @pl.when(step == 0)
    def _():
        carry_ref[...] = shift0_ref[...]
        ops_ref[1] = jnp.zeros(ops_ref.shape[1:], ops_ref.dtype)
        aux_ref[1] = jnp.zeros(aux_ref.shape[1:], aux_ref.dtype)
        gall_ref[1] = jnp.zeros(gall_ref.shape[1:], gall_ref.dtype)
        z_ref[...] = jnp.zeros(z_ref.shape, z_ref.dtype)

    @pl.when(step == 1)
    def _():
        for q in range(n_seq):
            for gi in range(n_groups):
                z_ref[q, gi] = jnp.zeros((GROUP_W, GROUP_W), F32)
                for h in range(GROUP_HEADS):
                    z_ref[q, gi, h * HEAD:(h + 1) * HEAD, h * HEAD:(h + 1) * HEAD] = s0_ref[q, gi * GROUP_HEADS + h]

    tok = lax.broadcasted_iota(jnp.int32, (C, GC), 0)
    col = lax.broadcasted_iota(jnp.int32, (C, GC), 1) % C
    strict = tok > col
    incl = tok >= col
    eye = jnp.where(tok == col, 1.0, 0.0).astype(F32)
    zi = lax.broadcasted_iota(jnp.int32, (GROUP_W, GROUP_W), 0) // HEAD
    zj = lax.broadcasted_iota(jnp.int32, (GROUP_W, GROUP_W), 1) // HEAD
    block_diag = zi == zj
    rows2 = lambda top, bottom: jnp.concatenate([top, bottom], axis=0).astype(BF16)

    def recurrence():
        chains = [(q, gi) for q in range(n_seq) for gi in range(n_groups)]
        op = lambda i, q, gi: ops_ref[r_slot, i, q * C:(q + 1) * C, gi * GROUP_W:(gi + 1) * GROUP_W]
        kap_t, r_t, v = ([op(i, q, gi) for q, gi in chains] for i in (0, 1, 4))
        kap_s = [stack(x, vec_masks) for x in kap_t]
        v_s = [stack(x, vec_masks) for x in v]
        grams = []
        for i, (q, gi) in enumerate(chains):
            right = jnp.concatenate([stack(op(2, q, gi), vec_masks), stack(op(3, q, gi), vec_masks)], axis=0)
            grams.append(_dot(rows2(kap_t[i], r_t[i]), right, NT))
        yield
        a_k = [jnp.where(strict, gm[:C, :GC], 0.0) for gm in grams]
        a_b = [jnp.where(strict, gm[:C, GC:], 0.0) for gm in grams]
        a_rk = [jnp.where(incl, gm[C:, :GC], 0.0) for gm in grams]
        a_rb = [jnp.where(incl, gm[C:, GC:], 0.0) for gm in grams]

        ps = [-a for a in a_b]
        ts = [eye + p for p in ps]
        ps = [_dot(p.astype(BF16), stack(p, mat_masks)) for p in ps]
        yield
        covered = 2
        while covered < C:
            powers = [stack(p, mat_masks) for p in ps]
            if 2 * covered < C:
                both = [_dot(rows2(t, p), pw) for t, p, pw in zip(ts, ps, powers)]
                ts = [t + x[:C] for t, x in zip(ts, both)]
                ps = [x[C:] for x in both]
            else:
                ts = [t + _dot(t.astype(BF16), pw) for t, pw in zip(ts, powers)]
            covered *= 2
            yield

        av = [_dot(rows2(a, ar), vs) for a, ar, vs in zip(a_k, a_rk, v_s)]
        yield
        tw = [_dot(t.astype(BF16), jnp.concatenate([ks, stack(x[:C], vec_masks)], axis=1))
              for t, ks, x in zip(ts, kap_s, av)]
        yield
        zs = [z_ref[q, gi] for q, gi in chains]
        pz = [_dot(rows2(x[:, :GROUP_W], rt), z.astype(BF16), NT)
              for x, rt, z in zip(tw, r_t, zs)]
        us = [p[:C] + x[:, GROUP_W:] for p, x in zip(pz, tw)]
        yield
        arbu = [_dot(a.astype(BF16), stack(u, vec_masks)) for a, u in zip(a_rb, us)]
        o_blk = [p[C:] + x[C:] - y for p, x, y in zip(pz, av, arbu)]
        for i, (q, gi) in enumerate(chains):
            vu_t = jnp.concatenate([v[i].astype(F32), us[i]], axis=0).T.astype(BF16)
            kb = jnp.concatenate([op(5, q, gi), op(6, q, gi)], axis=0)
            g_all = gall_ref[r_slot, q][:, gi * GROUP_W:(gi + 1) * GROUP_W]
            z_ref[q, gi] = zs[i] * g_all + jnp.where(block_diag, _dot(vu_t, kb), 0.0)
        yield
        o = jnp.concatenate([jnp.concatenate(o_blk[q * n_groups:(q + 1) * n_groups], axis=1)
                             for q in range(n_seq)], axis=0)
        oa_ref[...] = _rwkv_post(o, aux_ref[r_slot, 0], aux_ref[r_slot, 1], w, e_ref
                                 ).astype(oa_ref.dtype).reshape(oa_ref.shape)

    def prepare(q):
        rows = slice(q * C, (q + 1) * C)
        pr = pr_ref[q]
        row_id = lax.broadcasted_iota(jnp.int32, pr.shape, 0)
        pr_prev = jnp.where(row_id == 0, carry_ref[q], pltpu.roll(pr, 1, axis=0))
        carry_ref[q] = pr[C - 1:C, :]
        xr = pr + (pr_prev - pr) * w["mu"][...]
        r, kmod, vv, kk, bvec, lw, g = _rwkv_token_prep(xr, w, e_ref)
        aux_ref[w_slot, 0, rows] = _rwkv_bonus(r, kmod, vv, w, e_ref)
        aux_ref[w_slot, 1, rows] = g
        yield
        cum = _mm_exact_lhs(tri_ref[...], lw)
        cum_last = cum[C - 1:C, :]
        g_neg = jnp.exp(-cum)
        g_end = jnp.exp(cum_last - cum)
        prepared = (kk * jnp.exp(cum - lw),
                    r * jnp.exp(cum),
                    kmod * g_neg, bvec * g_neg,
                    vv, kmod * g_end, -(bvec * g_end))
        for i, x in enumerate(prepared):
            ops_ref[w_slot, i, rows] = x.astype(ops_ref.dtype)
        gall_ref[w_slot, q] = jnp.exp(cum_last)

    pending = [gen for gen in [prepare(q) for q in range(n_seq)] for _ in range(2)]
    for _ in recurrence():
        if pending:
            next(pending.pop(0), None)
    for gen in pending:
        next(gen, None)

    @pl.when(step == n_chunks)
    def _():
        for q in range(n_seq):
            for gi in range(n_groups):
                for h in range(GROUP_HEADS):
                    s1_ref[q, gi * GROUP_HEADS + h] = z_ref[q, gi, h * HEAD:(h + 1) * HEAD, h * HEAD:(h + 1) * HEAD]


_N_OPS = 7


def _rwkv_weight_inputs(wts):
    return [wts[n] for n in _RWKV_W_NAMES]


def _rwkv_weight_specs(wts):
    return [_const_spec(wts[n].shape) for n in _RWKV_W_NAMES]


def _rwkv_chunked(pr3d, shift0, wkv0, wts, e_mat, chunk, n_seq):
    bsz, t, c_shift = pr3d.shape
    heads = wkv0.shape[1]
    width = heads * HEAD
    tri = jnp.tril(jnp.ones((chunk, chunk), F32)).astype(BF16)
    kern = functools.partial(_rwkv_chunk_kernel, chunk=chunk, width=width, n_seq=n_seq)
    n_chunks = t // chunk
    rows = n_seq * chunk
    return pl.pallas_call(
        kern,
        grid=(bsz // n_seq, n_chunks + 1),
        in_specs=[pl.BlockSpec((n_seq, chunk, c_shift), lambda b, s: (b, jnp.minimum(s, n_chunks - 1), 0)),
                  pl.BlockSpec((n_seq, 1, c_shift), lambda b, s: (b, 0, 0)),
                  pl.BlockSpec((n_seq, heads, HEAD, HEAD), lambda b, s: (b, 0, 0, 0))]
                 + _rwkv_weight_specs(wts)
                 + [_const_spec(e_mat.shape), _const_spec(tri.shape)],
        out_specs=[pl.BlockSpec((n_seq, chunk, width), lambda b, s: (b, jnp.maximum(s - 1, 0), 0)),
                   pl.BlockSpec((n_seq, heads, HEAD, HEAD), lambda b, s: (b, 0, 0, 0))],
        out_shape=[jax.ShapeDtypeStruct((bsz, t, width), BF16),
                   jax.ShapeDtypeStruct((bsz, heads, HEAD, HEAD), F32)],
        scratch_shapes=[pltpu.VMEM((n_seq, 1, c_shift), F32),
                        pltpu.VMEM((n_seq, width // GROUP_W, GROUP_W, GROUP_W), F32),
                        pltpu.VMEM((2, _N_OPS, rows, width), BF16),
                        pltpu.VMEM((2, 2, rows, width), F32),
                        pltpu.VMEM((2, n_seq, 1, width), F32)],
        compiler_params=pltpu.CompilerParams(dimension_semantics=("parallel", "arbitrary"),
                                             vmem_limit_bytes=VMEM_LIMIT),
        name="rwkv_chunk",
    )(pr3d, shift0[:, None, :], wkv0, *_rwkv_weight_inputs(wts), e_mat, tri)


def _rwkv_step_kernel(pr_ref, shift0_ref, s_ref, *rest):
    n_w = len(_RWKV_W_NAMES)
    w = dict(zip(_RWKV_W_NAMES, rest[:n_w]))
    e_ref, oa_ref, s1_ref = rest[n_w], rest[n_w + 1], rest[n_w + 2]
    n_seq, heads = s_ref.shape[0], s_ref.shape[1]
    pr = pr_ref[...]
    xr = pr + (shift0_ref[...] - pr) * w["mu"][...]
    r, kmod, v, kk, bvec, lw, g = _rwkv_token_prep(xr, w, e_ref)
    dec = jnp.exp(lw)
    head = lambda a, h: a[:, h * HEAD:(h + 1) * HEAD]

    left = jnp.concatenate([kk, r * dec], axis=0).astype(BF16)
    seq_of_row = lax.broadcasted_iota(jnp.int32, (2 * n_seq, HEAD), 0) % n_seq
    reads = []
    for h in range(heads):
        lh = head(left, h)
        acc = jnp.zeros((2 * n_seq, HEAD), F32)
        for q in range(n_seq):
            acc = jnp.where(seq_of_row == q, _dot(lh, s_ref[q, h].astype(BF16), NT), acc)
        reads.append(acc)
    reads = jnp.concatenate(reads, axis=1)
    u, s_r = reads[:n_seq], reads[n_seq:]
    o = s_r + _head_sum(r * kmod, e_ref) * v - _head_sum(r * bvec, e_ref) * u
    oa_ref[...] = _rwkv_post(o, _rwkv_bonus(r, kmod, v, w, e_ref), g, w, e_ref).astype(oa_ref.dtype)

    lt_hi, lt_lo = _split2(jnp.concatenate([v, u], axis=0).T)
    rt_hi, rt_lo = _split2(jnp.concatenate([kmod, -bvec], axis=0))
    lt3 = jnp.concatenate([lt_hi, lt_hi, lt_lo], axis=1)
    rt3 = jnp.concatenate([rt_hi, rt_lo, rt_hi], axis=0)
    seq_of_row3 = lax.broadcasted_iota(jnp.int32, (6 * n_seq, HEAD), 0) % n_seq
    for h in range(heads):
        lt_h = lt3[h * HEAD:(h + 1) * HEAD, :]
        rt_h, dec_h = head(rt3, h), head(dec, h)
        for q in range(n_seq):
            upd = _dot(lt_h, jnp.where(seq_of_row3 == q, rt_h, jnp.zeros_like(rt_h)))
            s1_ref[q, h] = s_ref[q, h] * dec_h[q:q + 1, :] + upd


def _rwkv_step(pr2d, shift0, wkv0, wts, e_mat, n_seq):
    bsz, c_shift = pr2d.shape
    heads = wkv0.shape[1]
    width = heads * HEAD
    rows = lambda wd: pl.BlockSpec((n_seq, wd), lambda i: (i, 0))
    st = pl.BlockSpec((n_seq, heads, HEAD, HEAD), lambda i: (i, 0, 0, 0))
    return pl.pallas_call(
        _rwkv_step_kernel,
        grid=(bsz // n_seq,),
        in_specs=[rows(c_shift), rows(c_shift), st] + _rwkv_weight_specs(wts) + [_const_spec(e_mat.shape)],
        out_specs=[rows(width), st],
        out_shape=[jax.ShapeDtypeStruct((bsz, width), BF16), jax.ShapeDtypeStruct(wkv0.shape, F32)],
        compiler_params=pltpu.CompilerParams(dimension_semantics=("parallel",), vmem_limit_bytes=VMEM_LIMIT),
        name="rwkv_step",
    )(pr2d, shift0, wkv0, *_rwkv_weight_inputs(wts), e_mat)


def _s5_disc_kernel(lre_ref, lim_ref, ldt_ref, bre_ref, bim_ref, lbr_ref, lbi_ref, bbr_ref, bbi_ref):
    lam_re, lam_im = lre_ref[...], lim_ref[...]
    dt = jnp.exp(ldt_ref[...])
    mag = jnp.exp(lam_re * dt)
    ang = lam_im * dt
    lb_re, lb_im = mag * jnp.cos(ang), mag * jnp.sin(ang)
    nr, ni = lb_re - 1.0, lb_im
    den = lam_re * lam_re + lam_im * lam_im
    f_re = (nr * lam_re + ni * lam_im) / den
    f_im = (ni * lam_re - nr * lam_im) / den
    b_re, b_im = bre_ref[...], bim_ref[...]
    lbr_ref[...] = lb_re
    lbi_ref[...] = lb_im
    bbr_ref[...] = f_re * b_re - f_im * b_im
    bbi_ref[...] = f_re * b_im + f_im * b_re


def _s5_discretise(lam_re, lam_im, log_dt, b_re, b_im):
    g, p = lam_re.shape
    full = lambda a: _const_spec(a.shape)
    args = (lam_re[..., None], lam_im[..., None], log_dt[:, None, None], b_re, b_im)
    return pl.pallas_call(
        _s5_disc_kernel,
        grid=(1,),
        in_specs=[full(a) for a in args],
        out_specs=[_const_spec((g, p, 1))] * 2 + [full(b_re)] * 2,
        out_shape=[jax.ShapeDtypeStruct((g, p, 1), F32)] * 2 + [jax.ShapeDtypeStruct(b_re.shape, F32)] * 2,
        name="s5_discretise",
    )(*args)


def _s5_kernel(u_ref, re0_ref, im0_ref, lbr_ref, lbi_ref, wb_ref, wc_ref, d_ref,
               h_ref, re1_ref, im1_ref, bu_ref, xs_ref, *, tt, n_slabs):
    t_blk = pl.program_id(1)
    rows = SUBLANES * tt
    s_w = SLAB_GROUPS * S5_STATE
    u_w = SLAB_GROUPS * S5_GROUP

    @pl.when(t_blk == 0)
    def _():
        re1_ref[...] = re0_ref[...]
        im1_ref[...] = im0_ref[...]

    u = jnp.swapaxes(u_ref[...], 0, 1).reshape(rows, n_slabs * u_w)
    ys = []
    for s in range(n_slabs):
        us = u[:, s * u_w:(s + 1) * u_w]
        bu_ref[...] = _mm1(us, wb_ref[s])
        st = slice(s * s_w, (s + 1) * s_w)
        lbr = jnp.broadcast_to(lbr_ref[:, st], (SUBLANES, s_w))
        lbi = jnp.broadcast_to(lbi_ref[:, st], (SUBLANES, s_w))

        def step(t, carry):
            xr, xi = carry
            at_t = pl.ds(pl.multiple_of(t * SUBLANES, SUBLANES), SUBLANES)
            nr = lbr * xr - lbi * xi + bu_ref[at_t, :s_w]
            ni = lbr * xi + lbi * xr + bu_ref[at_t, s_w:]
            xs_ref[at_t, :s_w] = nr
            xs_ref[at_t, s_w:] = ni
            return nr, ni

        xr, xi = lax.fori_loop(0, tt, step, (re1_ref[:, st], im1_ref[:, st]), unroll=8)
        re1_ref[:, st] = xr
        im1_ref[:, st] = xi
        y = _mm1(xs_ref[...], wc_ref[s]) + d_ref[:, s * u_w:(s + 1) * u_w] * us
        ys.append(_gelu_tanh(y))
    h = jnp.concatenate(ys, axis=1).reshape(tt, SUBLANES, n_slabs * u_w)
    h_ref[...] = jnp.swapaxes(h, 0, 1).astype(h_ref.dtype)


def _s5(u_blocks, re0, im0, lb_re, lb_im, wb, wc, d_skip, tt):
    n_slabs = wb.shape[0]
    n_state = re0.shape[1]
    blk = (SUBLANES, tt, u_blocks.shape[-1])
    grid = (u_blocks.shape[0] // SUBLANES, u_blocks.shape[1] // tt)
    rows = SUBLANES * tt
    u_spec = pl.BlockSpec(blk, lambda i, j: (i, j, 0))
    st_spec = pl.BlockSpec((SUBLANES, n_state), lambda i, j: (i, 0))
    full = lambda a: _const_spec(a.shape)
    return pl.pallas_call(
        functools.partial(_s5_kernel, tt=tt, n_slabs=n_slabs),
        grid=grid,
        in_specs=[u_spec, st_spec, st_spec, full(lb_re), full(lb_im), full(wb), full(wc), full(d_skip)],
        out_specs=[u_spec, st_spec, st_spec],
        out_shape=[jax.ShapeDtypeStruct(u_blocks.shape, BF16),
                   jax.ShapeDtypeStruct(re0.shape, F32), jax.ShapeDtypeStruct(im0.shape, F32)],
        scratch_shapes=[pltpu.VMEM((rows, 2 * SLAB_GROUPS * S5_STATE), F32)] * 2,
        compiler_params=pltpu.CompilerParams(dimension_semantics=("parallel", "arbitrary"),
                                             vmem_limit_bytes=VMEM_LIMIT),
        name="s5_scan",
    )(u_blocks, re0, im0, lb_re, lb_im, wb, wc, d_skip)


def _s5_step_kernel(u_ref, re0_ref, im0_ref, lbr_ref, lbi_ref, wb_ref, wc_ref, d_ref,
                    h_ref, re1_ref, im1_ref, *, n_slabs):
    s_w = SLAB_GROUPS * S5_STATE
    u_w = SLAB_GROUPS * S5_GROUP
    u = u_ref[...]
    ys = []
    for s in range(n_slabs):
        us = u[:, s * u_w:(s + 1) * u_w]
        bu = _mm1(us, wb_ref[s])
        st = slice(s * s_w, (s + 1) * s_w)
        lbr, lbi = lbr_ref[:, st], lbi_ref[:, st]
        xr, xi = re0_ref[:, st], im0_ref[:, st]
        nr = lbr * xr - lbi * xi + bu[:, :s_w]
        ni = lbr * xi + lbi * xr + bu[:, s_w:]
        re1_ref[:, st] = nr
        im1_ref[:, st] = ni
        y = _mm1(jnp.concatenate([nr, ni], axis=1), wc_ref[s]) + d_ref[:, s * u_w:(s + 1) * u_w] * us
        ys.append(_gelu_tanh(y))
    h_ref[...] = jnp.concatenate(ys, axis=1).astype(h_ref.dtype)


def _s5_step(u2d, re0, im0, lb_re, lb_im, wb, wc, d_skip):
    full = lambda a: _const_spec(a.shape)
    args = (u2d, re0, im0, lb_re, lb_im, wb, wc, d_skip)
    return pl.pallas_call(
        functools.partial(_s5_step_kernel, n_slabs=wb.shape[0]),
        grid=(1,),
        in_specs=[full(a) for a in args],
        out_specs=[full(u2d), full(re0), full(im0)],
        out_shape=[jax.ShapeDtypeStruct(u2d.shape, BF16),
                   jax.ShapeDtypeStruct(re0.shape, F32), jax.ShapeDtypeStruct(im0.shape, F32)],
        compiler_params=pltpu.CompilerParams(vmem_limit_bytes=VMEM_LIMIT),
        name="s5_step",
    )(*args)


def _block_diag_slabs(m):
    g, a, b = m.shape
    eye = jnp.eye(SLAB_GROUPS, dtype=m.dtype)
    m4 = m.reshape(g // SLAB_GROUPS, SLAB_GROUPS, a, b)
    return jnp.einsum("sgab,gh->sgahb", m4, eye).reshape(g // SLAB_GROUPS, SLAB_GROUPS * a, SLAB_GROUPS * b)


def _tail_kernel(x_ref, oa_ref, hg_ref, gt_ref, wro_ref, w1_ref, b1_ref, w2_ref, b2_ref, wmo_ref,
                 npm_ref, nf_ref, npf_ref, wg_ref, wu_ref, wd_ref, y_ref):
    d = x_ref.shape[-1]
    tm = x_ref.shape[0]
    n_sub = 2 if tm % 32 == 0 else 1
    subs = [slice(i * (tm // n_sub), (i + 1) * (tm // n_sub)) for i in range(n_sub)]
    hg = [hg_ref[s, :] for s in subs]
    a_out = [_dot(oa_ref[s, :], wro_ref[...]) for s in subs]
    b_lin = [_dot(h, w1_ref[...]) + b1_ref[...] for h in hg]
    b_gate = [_dot(h, w2_ref[...]) + b2_ref[...] for h in hg]
    merged = [(_sigmoid(gt_ref[s, :d]) * a + _sigmoid(gt_ref[s, d:]) * (bl * _sigmoid(bg))).astype(BF16)
              for s, a, bl, bg in zip(subs, a_out, b_lin, b_gate)]
    mix = [_dot(m, wmo_ref[...]) for m in merged]
    x1 = [x_ref[s, :] + _rms(m, npm_ref[...]) for s, m in zip(subs, mix)]
    hb = [_rms(x, nf_ref[...]).astype(BF16) for x in x1]
    gate = [_dot(h, wg_ref[...]) for h in hb]
    up = [_dot(h, wu_ref[...]) for h in hb]
    act = [(g * _sigmoid(g) * u).astype(BF16) for g, u in zip(gate, up)]
    f = [_dot(a, wd_ref[...]) for a in act]
    for s, x, ff in zip(subs, x1, f):
        y_ref[s, :] = x + _rms(ff, npf_ref[...])


def _tail(x2d, oa, hg, gates, tw, tm):
    rows, d = x2d.shape
    row = lambda a: pl.BlockSpec((tm, a.shape[1]), lambda i: (i, 0))
    wnames = ("wro", "w1", "b1", "w2", "b2", "wmo", "npm", "nf", "npf", "wg", "wu", "wd")
    wargs = [tw[n] for n in wnames]
    return pl.pallas_call(
        _tail_kernel,
        grid=(rows // tm,),
        in_specs=[row(x2d), row(oa), row(hg), row(gates)] + [_const_spec(a.shape, True) for a in wargs],
        out_specs=pl.BlockSpec((tm, d), lambda i: (i, 0)),
        out_shape=jax.ShapeDtypeStruct((rows, d), F32),
        compiler_params=pltpu.CompilerParams(dimension_semantics=("parallel",),
                                             vmem_limit_bytes=VMEM_LIMIT),
        name="tail",
    )(x2d, oa, hg, gates, *wargs)


def _pick_tile(rows, target):
    t = min(rows, target)
    assert rows % t == 0
    return t


def _layer(x, shift0, wkv0, re0, im0, lw, *, chunk, n_seq, s5_tt, row_tile):
    bsz, t, d = x.shape
    rows = bsz * t
    c_shift = shift0.shape[-1]
    c_u = lw["d_skip"].shape[-1]
    tm = _pick_tile(rows, row_tile)
    x2d = x.reshape(rows, d)
    pr, u, gates = _proj(x2d, lw["norm_pre_mix"], lw["w_in"], c_shift, c_u, tm)

    n_state = re0.shape[1] * re0.shape[2]
    s5_args = (re0.reshape(bsz, n_state), im0.reshape(bsz, n_state),
               lw["lb_re"], lw["lb_im"], lw["wb"], lw["wc"], lw["d_skip"])
    if t == 1:
        oa, wkv1 = _rwkv_step(pr, shift0, wkv0, lw["rwkv"], lw["e_mat"], n_seq=STEP_SEQS)
        pr_last = pr
        hg, re1, im1 = _s5_step(u, *s5_args)
    else:
        oa, wkv1 = _rwkv_chunked(pr.reshape(bsz, t, c_shift), shift0, wkv0, lw["rwkv"], lw["e_mat"], chunk, n_seq)
        oa = oa.reshape(rows, -1)
        pr_last = pr.reshape(bsz, t, c_shift)[:, -1]
        hg, re1, im1 = _s5(u.reshape(bsz, t, c_u), *s5_args, tt=s5_tt)
    y = _tail(x2d, oa, hg.reshape(rows, c_u), gates, lw["tail"], tm)
    return y.reshape(x.shape), pr_last, wkv1, re1.reshape(re0.shape), im1.reshape(im0.shape)


def _prepare_layer_weights(l, p):
    row = lambda a: a[l][None, :].astype(F32)
    width = p["w0"].shape[-1]
    n_dec, n_aaa, n_gate = p["w_decay_up"].shape[1], p["w_aaa_up"].shape[1], p["w_gate_up"].shape[1]
    assert n_dec + n_aaa + n_gate == LORA_PAD

    def lora_pad(wup, start):
        return jnp.zeros((LORA_PAD, width), F32).at[start:start + wup.shape[0]].set(wup).astype(BF16)

    rwkv = {
        "mu": row(p["mu_shift"]), "w0": row(p["w0"]), "a0": row(p["a0"]), "k_k": row(p["k_k"]),
        "k_a": row(p["k_a"]), "r_k": row(p["r_k"]), "lnx_g": row(p["lnx_g"]), "lnx_b": row(p["lnx_b"]),
        "wd": lora_pad(p["w_decay_up"][l], 0),
        "wa": lora_pad(p["w_aaa_up"][l], n_dec),
        "wg": lora_pad(p["w_gate_up"][l], n_dec + n_aaa),
    }
    head_id = jnp.arange(MXU_DIM) // HEAD
    e_mat = (head_id[:, None] == head_id[None, :]).astype(BF16)

    lb_re, lb_im, bb_re, bb_im = _s5_discretise(p["s5_lam_re"][l], p["s5_lam_im"][l], p["s5_log_dt"][l],
                                                p["s5_b_re"][l], p["s5_b_im"][l])
    n_state = lb_re.shape[0] * lb_re.shape[1]
    to_in = lambda bb: _block_diag_slabs(jnp.swapaxes(bb, 1, 2))
    to_out = lambda cc: _block_diag_slabs(jnp.swapaxes(cc, 1, 2))
    wb = jnp.concatenate([to_in(bb_re), to_in(bb_im)], axis=-1).astype(BF16)
    wc = jnp.concatenate([to_out(p["s5_c_re"][l]), -to_out(p["s5_c_im"][l])], axis=1).astype(BF16)

    bf = lambda a: a[l].astype(BF16)
    tail = {
        "wro": bf(p["w_rwkv_out"]), "w1": bf(p["glu_w1"]), "b1": row(p["glu_b1"]), "w2": bf(p["glu_w2"]),
        "b2": row(p["glu_b2"]), "wmo": bf(p["w_merge_out"]), "npm": row(p["norm_post_mix"]),
        "nf": row(p["norm_pre_ffn"]), "npf": row(p["norm_post_ffn"]),
        "wg": bf(p["w_ffn_gate"]), "wu": bf(p["w_ffn_up"]), "wd": bf(p["w_ffn_down"]),
    }
    return {
        "norm_pre_mix": row(p["norm_pre_mix"]), "w_in": bf(p["w_in"]), "rwkv": rwkv, "e_mat": e_mat,
        "lb_re": lb_re.reshape(1, n_state), "lb_im": lb_im.reshape(1, n_state), "wb": wb, "wc": wc,
        "d_skip": row(p["s5_d"]), "tail": tail,
    }


_PARAM_NAMES = ("norm_pre_mix", "norm_post_mix", "norm_pre_ffn", "norm_post_ffn", "w_in", "mu_shift",
                "w0", "w_decay_up", "a0", "w_aaa_up", "w_gate_up", "k_k", "k_a", "r_k", "lnx_g", "lnx_b",
                "w_rwkv_out", "s5_lam_re", "s5_lam_im", "s5_log_dt", "s5_b_re", "s5_b_im", "s5_c_re",
                "s5_c_im", "s5_d", "glu_w1", "glu_b1", "glu_w2", "glu_b2", "w_merge_out",
                "w_ffn_gate", "w_ffn_up", "w_ffn_down")


def _forward(x_prompt, x_sample, state_shift, state_wkv, state_s5_re, state_s5_im, params,
             *, chunk=64, n_seq=4, s5_tt=128, row_tile=256):
    depth = params["w_in"].shape[0]
    heads = state_wkv.shape[2]
    bp = x_prompt.shape[0]
    yp, ys = x_prompt, x_sample
    outs_p, outs_s = [], []
    for l in range(depth):
        lw = _prepare_layer_weights(l, params)
        zp_shift = jnp.zeros((bp, state_shift.shape[-1]), F32)
        zp_wkv = jnp.zeros((bp, heads, HEAD, HEAD), F32)
        zp_s5 = jnp.zeros((bp,) + state_s5_re.shape[2:], F32)
        yp, *st_p = _layer(yp, zp_shift, zp_wkv, zp_s5, zp_s5, lw, chunk=chunk, n_seq=n_seq, s5_tt=s5_tt, row_tile=row_tile)
        ys, *st_s = _layer(ys, state_shift[l], state_wkv[l], state_s5_re[l], state_s5_im[l], lw,
                           chunk=chunk, n_seq=n_seq, s5_tt=s5_tt, row_tile=row_tile)
        outs_p.append(st_p)
        outs_s.append(st_s)
    stack = lambda outs, i, dt: jnp.stack([o[i] for o in outs]).astype(dt)
    dt_p, dt_s = x_prompt.dtype, x_sample.dtype
    return (yp, ys,
            stack(outs_p, 0, dt_p), stack(outs_p, 1, dt_p), stack(outs_p, 2, dt_p), stack(outs_p, 3, dt_p),
            stack(outs_s, 0, dt_s), stack(outs_s, 1, dt_s), stack(outs_s, 2, dt_s), stack(outs_s, 3, dt_s))


def kernel(x_prompt, x_sample, state_shift, state_wkv, state_s5_re, state_s5_im, norm_pre_mix, norm_post_mix, norm_pre_ffn, norm_post_ffn, w_in, mu_shift, w0, w_decay_up, a0, w_aaa_up, w_gate_up, k_k, k_a, r_k, lnx_g, lnx_b, w_rwkv_out, s5_lam_re, s5_lam_im, s5_log_dt, s5_b_re, s5_b_im, s5_c_re, s5_c_im, s5_d, glu_w1, glu_b1, glu_w2, glu_b2, w_merge_out, w_ffn_gate, w_ffn_up, w_ffn_down):
    params = dict(zip(_PARAM_NAMES, (norm_pre_mix, norm_post_mix, norm_pre_ffn, norm_post_ffn, w_in, mu_shift,
                                     w0, w_decay_up, a0, w_aaa_up, w_gate_up, k_k, k_a, r_k, lnx_g, lnx_b,
                                     w_rwkv_out, s5_lam_re, s5_lam_im, s5_log_dt, s5_b_re, s5_b_im, s5_c_re,
                                     s5_c_im, s5_d, glu_w1, glu_b1, glu_w2, glu_b2, w_merge_out,
                                     w_ffn_gate, w_ffn_up, w_ffn_down)))
    return _forward(x_prompt, x_sample, state_shift, state_wkv, state_s5_re, state_s5_im, params)
```

```python
import functools
import math

import jax
import jax.numpy as jnp
from jax import lax
from jax.experimental import pallas as pl
from jax.experimental.pallas import tpu as pltpu

F32 = jnp.float32
BF16 = jnp.bfloat16

NORM_EPS = 1e-6
LNX_EPS = 64e-5
HEAD = 64
GROUP_HEADS = 4
GROUP_W = GROUP_HEADS * HEAD
LORA_PAD = 128
S5_GROUP = 16
S5_STATE = 64
SLAB_GROUPS = 8
SUBLANES = 8
LANES = 128
MXU_DIM = 256
S5_ROW_BLOCK = 256
STEP_SEQS = 16
VMEM_LIMIT = 56 * 1024 * 1024

NN = (((1,), (0,)), ((), ()))
NT = (((1,), (1,)), ((), ()))


def _dot(a, b, dims=NN):
    return lax.dot_general(a, b, dims, preferred_element_type=F32)


def _split2(x):
    hi = x.astype(BF16)
    lo = (x - hi.astype(F32)).astype(BF16)
    return hi, lo


def _split3(x):
    hi = x.astype(BF16)
    r1 = x - hi.astype(F32)
    mid = r1.astype(BF16)
    lo = (r1 - mid.astype(F32)).astype(BF16)
    return hi, mid, lo


def _mm1(a, b, dims=NN):
    return _dot(a.astype(BF16), b.astype(BF16), dims)


def _mm_exact_lhs(a_bf16, b):
    h, m, l = _split3(b)
    return _dot(a_bf16, h) + (_dot(a_bf16, m) + _dot(a_bf16, l))


def _rms(x, g):
    return x * lax.rsqrt(jnp.mean(x * x, axis=-1, keepdims=True) + NORM_EPS) * g


def _sigmoid(x):
    return 1.0 / (1.0 + jnp.exp(-x))


def _gelu_tanh(x):
    c = math.sqrt(2.0 / math.pi)
    return 0.5 * x * (1.0 + jnp.tanh(c * (x + 0.044715 * (x * x * x))))


def _const_spec(shape, single_buffer=False):
    idx = lambda *_: (0,) * len(shape)
    if single_buffer:
        return pl.BlockSpec(shape, idx, pipeline_mode=pl.Buffered(1))
    return pl.BlockSpec(shape, idx)


def _proj_kernel(x_ref, g_ref, w_ref, pr_ref, u_ref, gt_ref, *, c_shift, c_u):
    hb = _rms(x_ref[...], g_ref[...]).astype(BF16)
    pr_ref[...] = _dot(hb, w_ref[:, :c_shift])
    u_ref[...] = _dot(hb, w_ref[:, c_shift:c_shift + c_u])
    gt_ref[...] = _dot(hb, w_ref[:, c_shift + c_u:])


def _proj(x2d, g, w_in_bf16, c_shift, c_u, tm):
    rows, d = x2d.shape
    cols = w_in_bf16.shape[1]
    c_g = cols - c_shift - c_u
    row = lambda w: pl.BlockSpec((tm, w), lambda i: (i, 0))
    return pl.pallas_call(
        functools.partial(_proj_kernel, c_shift=c_shift, c_u=c_u),
        grid=(rows // tm,),
        in_specs=[row(d), _const_spec((1, d)), _const_spec((d, cols), True)],
        out_specs=[row(c_shift), row(c_u), row(c_g)],
        out_shape=[jax.ShapeDtypeStruct((rows, c_shift), F32),
                   jax.ShapeDtypeStruct((rows, c_u), F32),
                   jax.ShapeDtypeStruct((rows, c_g), F32)],
        compiler_params=pltpu.CompilerParams(dimension_semantics=("parallel",),
                                             vmem_limit_bytes=VMEM_LIMIT),
        name="proj",
    )(x2d, g, w_in_bf16)


def _head_sum(x, e_ref):
    rows, width = x.shape
    gw = e_ref.shape[0]
    n_lg = width // gw
    hi, lo = _split2(x)
    stacked = jnp.concatenate([part[:, j * gw:(j + 1) * gw] for part in (hi, lo) for j in range(n_lg)], axis=0)
    sums = _dot(stacked, e_ref[...])
    return jnp.concatenate([sums[j * rows:(j + 1) * rows] + sums[(n_lg + j) * rows:(n_lg + j + 1) * rows]
                            for j in range(n_lg)], axis=1)


def _rwkv_token_prep(xr, w, e_ref):
    width = w["w0"].shape[-1]
    r = xr[:, :width]
    k = xr[:, width:2 * width]
    v = xr[:, 2 * width:3 * width]
    lo = xr[:, 3 * width:3 * width + LORA_PAD]
    wl = w["w0"][...] + _mm1(jnp.tanh(lo), w["wd"][...])
    lw = -math.exp(-0.5) * _sigmoid(wl)
    a = _sigmoid(w["a0"][...] + _mm1(lo, w["wa"][...]))
    g = _mm1(_sigmoid(lo), w["wg"][...])
    kk = k * w["k_k"][...]
    kk = kk * lax.rsqrt(jnp.maximum(_head_sum(kk * kk, e_ref), 1e-24))
    kmod = k * (1.0 + (a - 1.0) * w["k_a"][...])
    return r, kmod, v, kk, kk * a, lw, g


def _rwkv_bonus(r, kmod, v, w, e_ref):
    return _head_sum(r * kmod * w["r_k"][...], e_ref) * v


def _rwkv_post(o, bonus, g, w, e_ref):
    inv_n = 1.0 / HEAD
    mu = _head_sum(o, e_ref) * inv_n
    oc = o - mu
    var = _head_sum(oc * oc, e_ref) * inv_n
    on = oc * lax.rsqrt(var + LNX_EPS) * w["lnx_g"][...] + w["lnx_b"][...]
    return (on + bonus) * g


_RWKV_W_NAMES = ("mu", "w0", "a0", "k_k", "k_a", "r_k", "lnx_g", "lnx_b", "wd", "wa", "wg")


def _rwkv_chunk_kernel(pr_ref, shift0_ref, s0_ref, *rest, chunk, width, n_seq):
    n_w = len(_RWKV_W_NAMES)
    w = dict(zip(_RWKV_W_NAMES, rest[:n_w]))
    e_ref, tri_ref = rest[n_w], rest[n_w + 1]
    oa_ref, s1_ref = rest[n_w + 2], rest[n_w + 3]
    carry_ref, z_ref, ops_ref, aux_ref, gall_ref = rest[n_w + 4:n_w + 9]
    step = pl.program_id(1)
    n_chunks = pl.num_programs(1) - 1
    w_slot = step % 2
    r_slot = 1 - w_slot
    n_groups = width // GROUP_W
    C = chunk
    GC = GROUP_HEADS * C

    def lane_block_masks(n_lanes, block):
        lane_block = lax.broadcasted_iota(jnp.int32, (1, n_lanes), 1) // block
        return [lane_block == h for h in range(GROUP_HEADS)]

    vec_masks = lane_block_masks(GROUP_W, HEAD)
    mat_masks = lane_block_masks(GC, C)

    def stack(x, masks):
        xb = x.astype(BF16)
        zero = jnp.zeros_like(xb)
        return jnp.concatenate([jnp.where(m, xb, zero) for m in masks], axis=0)

    @pl.when(step == 0)
    def _():
        carry_ref[...] = shift0_ref[...]
        ops_ref[1] = jnp.zeros(ops_ref.shape[1:], ops_ref.dtype)
        aux_ref[1] = jnp.zeros(aux_ref.shape[1:], aux_ref.dtype)
        gall_ref[1] = jnp.zeros(gall_ref.shape[1:], gall_ref.dtype)
        z_ref[...] = jnp.zeros(z_ref.shape, z_ref.dtype)

    @pl.when(step == 1)
    def _():
        for q in range(n_seq):
            for gi in range(n_groups):
                z_ref[q, gi] = jnp.zeros((GROUP_W, GROUP_W), F32)
                for h in range(GROUP_HEADS):
                    z_ref[q, gi, h * HEAD:(h + 1) * HEAD, h * HEAD:(h + 1) * HEAD] = s0_ref[q, gi * GROUP_HEADS + h]

    tok = lax.broadcasted_iota(jnp.int32, (C, GC), 0)
    col = lax.broadcasted_iota(jnp.int32, (C, GC), 1) % C
    strict = tok > col
    incl = tok >= col
    eye = jnp.where(tok == col, 1.0, 0.0).astype(F32)
    zi = lax.broadcasted_iota(jnp.int32, (GROUP_W, GROUP_W), 0) // HEAD
    zj = lax.broadcasted_iota(jnp.int32, (GROUP_W, GROUP_W), 1) // HEAD
    block_diag = zi == zj
    rows2 = lambda top, bottom: jnp.concatenate([top, bottom], axis=0).astype(BF16)

    def recurrence():
        chains = [(q, gi) for q in range(n_seq) for gi in range(n_groups)]
        op = lambda i, q, gi: ops_ref[r_slot, i, q * C:(q + 1) * C, gi * GROUP_W:(gi + 1) * GROUP_W]
        kap_t, r_t, v = ([op(i, q, gi) for q, gi in chains] for i in (0, 1, 4))
        kap_s = [stack(x, vec_masks) for x in kap_t]
        v_s = [stack(x, vec_masks) for x in v]
        grams = []
        for i, (q, gi) in enumerate(chains):
            right = jnp.concatenate([stack(op(2, q, gi), vec_masks), stack(op(3, q, gi), vec_masks)], axis=0)
            grams.append(_dot(rows2(kap_t[i], r_t[i]), right, NT))
        yield
        a_k = [jnp.where(strict, gm[:C, :GC], 0.0) for gm in grams]
        a_b = [jnp.where(strict, gm[:C, GC:], 0.0) for gm in grams]
        a_rk = [jnp.where(incl, gm[C:, :GC], 0.0) for gm in grams]
        a_rb = [jnp.where(incl, gm[C:, GC:], 0.0) for gm in grams]

        ps = [-a for a in a_b]
        ts = [eye + p for p in ps]
        ps = [_dot(p.astype(BF16), stack(p, mat_masks)) for p in ps]
        yield
        covered = 2
        while covered < C:
            powers = [stack(p, mat_masks) for p in ps]
            if 2 * covered < C:
                both = [_dot(rows2(t, p), pw) for t, p, pw in zip(ts, ps, powers)]
                ts = [t + x[:C] for t, x in zip(ts, both)]
                ps = [x[C:] for x in both]
            else:
                ts = [t + _dot(t.astype(BF16), pw) for t, pw in zip(ts, powers)]
            covered *= 2
            yield

        av = [_dot(rows2(a, ar), vs) for a, ar, vs in zip(a_k, a_rk, v_s)]
        yield
        tw = [_dot(t.astype(BF16), jnp.concatenate([ks, stack(x[:C], vec_masks)], axis=1))
              for t, ks, x in zip(ts, kap_s, av)]
        yield
        zs = [z_ref[q, gi] for q, gi in chains]
        pz = [_dot(rows2(x[:, :GROUP_W], rt), z.astype(BF16), NT)
              for x, rt, z in zip(tw, r_t, zs)]
        us = [p[:C] + x[:, GROUP_W:] for p, x in zip(pz, tw)]
        yield
        arbu = [_dot(a.astype(BF16), stack(u, vec_masks)) for a, u in zip(a_rb, us)]
        o_blk = [p[C:] + x[C:] - y for p, x, y in zip(pz, av, arbu)]
        for i, (q, gi) in enumerate(chains):
            vu_t = jnp.concatenate([v[i].astype(F32), us[i]], axis=0).T.astype(BF16)
            kb = jnp.concatenate([op(5, q, gi), op(6, q, gi)], axis=0)
            g_all = gall_ref[r_slot, q][:, gi * GROUP_W:(gi + 1) * GROUP_W]
            z_ref[q, gi] = zs[i] * g_all + jnp.where(block_diag, _dot(vu_t, kb), 0.0)
        yield
        o = jnp.concatenate([jnp.concatenate(o_blk[q * n_groups:(q + 1) * n_groups], axis=1)
                             for q in range(n_seq)], axis=0)
        oa_ref[...] = _rwkv_post(o, aux_ref[r_slot, 0], aux_ref[r_slot, 1], w, e_ref
                                 ).astype(oa_ref.dtype).reshape(oa_ref.shape)

    def prepare(q):
        rows = slice(q * C, (q + 1) * C)
        pr = pr_ref[q]
        row_id = lax.broadcasted_iota(jnp.int32, pr.shape, 0)
        pr_prev = jnp.where(row_id == 0, carry_ref[q], pltpu.roll(pr, 1, axis=0))
        carry_ref[q] = pr[C - 1:C, :]
        xr = pr + (pr_prev - pr) * w["mu"][...]
        r, kmod, vv, kk, bvec, lw, g = _rwkv_token_prep(xr, w, e_ref)
        aux_ref[w_slot, 0, rows] = _rwkv_bonus(r, kmod, vv, w, e_ref)
        aux_ref[w_slot, 1, rows] = g
        yield
        cum = _mm_exact_lhs(tri_ref[...], lw)
        cum_last = cum[C - 1:C, :]
        g_neg = jnp.exp(-cum)
        g_end = jnp.exp(cum_last - cum)
        prepared = (kk * jnp.exp(cum - lw),
                    r * jnp.exp(cum),
                    kmod * g_neg, bvec * g_neg,
                    vv, kmod * g_end, -(bvec * g_end))
        for i, x in enumerate(prepared):
            ops_ref[w_slot, i, rows] = x.astype(ops_ref.dtype)
        gall_ref[w_slot, q] = jnp.exp(cum_last)

    pending = [gen for gen in [prepare(q) for q in range(n_seq)] for _ in range(2)]
    for _ in recurrence():
        if pending:
            next(pending.pop(0), None)
    for gen in pending:
        next(gen, None)

    @pl.when(step == n_chunks)
    def _():
        for q in range(n_seq):
            for gi in range(n_groups):
                for h in range(GROUP_HEADS):
                    s1_ref[q, gi * GROUP_HEADS + h] = z_ref[q, gi, h * HEAD:(h + 1) * HEAD, h * HEAD:(h + 1) * HEAD]


_N_OPS = 7


def _rwkv_weight_inputs(wts):
    return [wts[n] for n in _RWKV_W_NAMES]


def _rwkv_weight_specs(wts):
    return [_const_spec(wts[n].shape) for n in _RWKV_W_NAMES]


def _rwkv_chunked(pr3d, shift0, wkv0, wts, e_mat, chunk, n_seq):
    bsz, t, c_shift = pr3d.shape
    heads = wkv0.shape[1]
    width = heads * HEAD
    tri = jnp.tril(jnp.ones((chunk, chunk), F32)).astype(BF16)
    kern = functools.partial(_rwkv_chunk_kernel, chunk=chunk, width=width, n_seq=n_seq)
    n_chunks = t // chunk
    rows = n_seq * chunk
    return pl.pallas_call(
        kern,
        grid=(bsz // n_seq, n_chunks + 1),
        in_specs=[pl.BlockSpec((n_seq, chunk, c_shift), lambda b, s: (b, jnp.minimum(s, n_chunks - 1), 0)),
                  pl.BlockSpec((n_seq, 1, c_shift), lambda b, s: (b, 0, 0)),
                  pl.BlockSpec((n_seq, heads, HEAD, HEAD), lambda b, s: (b, 0, 0, 0))]
                 + _rwkv_weight_specs(wts)
                 + [_const_spec(e_mat.shape), _const_spec(tri.shape)],
        out_specs=[pl.BlockSpec((n_seq, chunk, width), lambda b, s: (b, jnp.maximum(s - 1, 0), 0)),
                   pl.BlockSpec((n_seq, heads, HEAD, HEAD), lambda b, s: (b, 0, 0, 0))],
        out_shape=[jax.ShapeDtypeStruct((bsz, t, width), BF16),
                   jax.ShapeDtypeStruct((bsz, heads, HEAD, HEAD), F32)],
        scratch_shapes=[pltpu.VMEM((n_seq, 1, c_shift), F32),
                        pltpu.VMEM((n_seq, width // GROUP_W, GROUP_W, GROUP_W), F32),
                        pltpu.VMEM((2, _N_OPS, rows, width), BF16),
                        pltpu.VMEM((2, 2, rows, width), F32),
                        pltpu.VMEM((2, n_seq, 1, width), F32)],
        compiler_params=pltpu.CompilerParams(dimension_semantics=("parallel", "arbitrary"),
                                             vmem_limit_bytes=VMEM_LIMIT),
        name="rwkv_chunk",
    )(pr3d, shift0[:, None, :], wkv0, *_rwkv_weight_inputs(wts), e_mat, tri)


def _rwkv_step_kernel(pr_ref, shift0_ref, s_ref, *rest):
    n_w = len(_RWKV_W_NAMES)
    w = dict(zip(_RWKV_W_NAMES, rest[:n_w]))
    e_ref, oa_ref, s1_ref = rest[n_w], rest[n_w + 1], rest[n_w + 2]
    n_seq, heads = s_ref.shape[0], s_ref.shape[1]
    pr = pr_ref[...]
    xr = pr + (shift0_ref[...] - pr) * w["mu"][...]
    r, kmod, v, kk, bvec, lw, g = _rwkv_token_prep(xr, w, e_ref)
    dec = jnp.exp(lw)
    head = lambda a, h: a[:, h * HEAD:(h + 1) * HEAD]

    left = jnp.concatenate([kk, r * dec], axis=0).astype(BF16)
    seq_of_row = lax.broadcasted_iota(jnp.int32, (2 * n_seq, HEAD), 0) % n_seq
    reads = []
    for h in range(heads):
        lh = head(left, h)
        acc = jnp.zeros((2 * n_seq, HEAD), F32)
        for q in range(n_seq):
            acc = jnp.where(seq_of_row == q, _dot(lh, s_ref[q, h].astype(BF16), NT), acc)
        reads.append(acc)
    reads = jnp.concatenate(reads, axis=1)
    u, s_r = reads[:n_seq], reads[n_seq:]
    o = s_r + _head_sum(r * kmod, e_ref) * v - _head_sum(r * bvec, e_ref) * u
    oa_ref[...] = _rwkv_post(o, _rwkv_bonus(r, kmod, v, w, e_ref), g, w, e_ref).astype(oa_ref.dtype)

    lt_hi, lt_lo = _split2(jnp.concatenate([v, u], axis=0).T)
    rt_hi, rt_lo = _split2(jnp.concatenate([kmod, -bvec], axis=0))
    lt3 = jnp.concatenate([lt_hi, lt_hi, lt_lo], axis=1)
    rt3 = jnp.concatenate([rt_hi, rt_lo, rt_hi], axis=0)
    seq_of_row3 = lax.broadcasted_iota(jnp.int32, (6 * n_seq, HEAD), 0) % n_seq
    for h in range(heads):
        lt_h = lt3[h * HEAD:(h + 1) * HEAD, :]
        rt_h, dec_h = head(rt3, h), head(dec, h)
        for q in range(n_seq):
            upd = _dot(lt_h, jnp.where(seq_of_row3 == q, rt_h, jnp.zeros_like(rt_h)))
            s1_ref[q, h] = s_ref[q, h] * dec_h[q:q + 1, :] + upd


def _rwkv_step(pr2d, shift0, wkv0, wts, e_mat, n_seq):
    bsz, c_shift = pr2d.shape
    heads = wkv0.shape[1]
    width = heads * HEAD
    rows = lambda wd: pl.BlockSpec((n_seq, wd), lambda i: (i, 0))
    st = pl.BlockSpec((n_seq, heads, HEAD, HEAD), lambda i: (i, 0, 0, 0))
    return pl.pallas_call(
        _rwkv_step_kernel,
        grid=(bsz // n_seq,),
        in_specs=[rows(c_shift), rows(c_shift), st] + _rwkv_weight_specs(wts) + [_const_spec(e_mat.shape)],
        out_specs=[rows(width), st],
        out_shape=[jax.ShapeDtypeStruct((bsz, width), BF16), jax.ShapeDtypeStruct(wkv0.shape, F32)],
        compiler_params=pltpu.CompilerParams(dimension_semantics=("parallel",), vmem_limit_bytes=VMEM_LIMIT),
        name="rwkv_step",
    )(pr2d, shift0, wkv0, *_rwkv_weight_inputs(wts), e_mat)


def _s5_disc_kernel(lre_ref, lim_ref, ldt_ref, bre_ref, bim_ref, lbr_ref, lbi_ref, bbr_ref, bbi_ref):
    lam_re, lam_im = lre_ref[...], lim_ref[...]
    dt = jnp.exp(ldt_ref[...])
    mag = jnp.exp(lam_re * dt)
    ang = lam_im * dt
    lb_re, lb_im = mag * jnp.cos(ang), mag * jnp.sin(ang)
    nr, ni = lb_re - 1.0, lb_im
    den = lam_re * lam_re + lam_im * lam_im
    f_re = (nr * lam_re + ni * lam_im) / den
    f_im = (ni * lam_re - nr * lam_im) / den
    b_re, b_im = bre_ref[...], bim_ref[...]
    lbr_ref[...] = lb_re
    lbi_ref[...] = lb_im
    bbr_ref[...] = f_re * b_re - f_im * b_im
    bbi_ref[...] = f_re * b_im + f_im * b_re


def _s5_discretise(lam_re, lam_im, log_dt, b_re_t, b_im_t):
    g, p = lam_re.shape
    full = lambda a: _const_spec(a.shape)
    args = (lam_re[:, None, :], lam_im[:, None, :], log_dt[:, None, None], b_re_t, b_im_t)
    return pl.pallas_call(
        _s5_disc_kernel,
        grid=(1,),
        in_specs=[full(a) for a in args],
        out_specs=[_const_spec((g, 1, p))] * 2 + [full(b_re_t)] * 2,
        out_shape=[jax.ShapeDtypeStruct((g, 1, p), F32)] * 2 + [jax.ShapeDtypeStruct(b_re_t.shape, F32)] * 2,
        name="s5_discretise",
    )(*args)


def _s5_kernel(u_ref, re0_ref, im0_ref, lbr_ref, lbi_ref, wb_ref, wc_ref, d_ref,
               h_ref, re1_ref, im1_ref, u_tm, h_tm, bu0, bu1, xs0, xs1, *, tt, n_slabs):
    t_blk = pl.program_id(1)
    rows = SUBLANES * tt
    s_w = SLAB_GROUPS * S5_STATE
    u_w = SLAB_GROUPS * S5_GROUP
    bu, xs = (bu0, bu1), (xs0, xs1)
    rb = min(rows, S5_ROW_BLOCK)
    steps_per_rb = rb // SUBLANES

    @pl.when(t_blk == 0)
    def _():
        re1_ref[...] = re0_ref[...]
        im1_ref[...] = im0_ref[...]

    u_tm[...] = jnp.swapaxes(u_ref[...], 0, 1).reshape(rows, n_slabs * u_w)

    def project_in(s, j):
        r0 = j * rb
        bu[s % 2][r0:r0 + rb, :] = _mm1(u_tm[r0:r0 + rb, s * u_w:(s + 1) * u_w], wb_ref[s])

    def project_out(s, j):
        r0 = j * rb
        lanes = slice(s * u_w, (s + 1) * u_w)
        y = _mm1(xs[s % 2][r0:r0 + rb, :], wc_ref[s]) + d_ref[:, lanes] * u_tm[r0:r0 + rb, lanes]
        h_tm[r0:r0 + rb, lanes] = _gelu_tanh(y)

    for j in range(rows // rb):
        project_in(0, j)
    for p in range(n_slabs + 1):
        if p < n_slabs:
            st = slice(p * s_w, (p + 1) * s_w)
            lbr = jnp.broadcast_to(lbr_ref[:, st], (SUBLANES, s_w))
            lbi = jnp.broadcast_to(lbi_ref[:, st], (SUBLANES, s_w))
            xr, xi = re1_ref[:, st], im1_ref[:, st]
        for j in range(rows // rb):
            if p < n_slabs:
                for t in range(j * steps_per_rb, (j + 1) * steps_per_rb):
                    at_t = slice(t * SUBLANES, (t + 1) * SUBLANES)
                    xr, xi = (lbr * xr - lbi * xi + bu[p % 2][at_t, :s_w],
                              lbr * xi + lbi * xr + bu[p % 2][at_t, s_w:])
                    xs[p % 2][at_t, :s_w] = xr
                    xs[p % 2][at_t, s_w:] = xi
            if p + 1 < n_slabs:
                project_in(p + 1, j)
            if p >= 1:
                project_out(p - 1, j)
        if p < n_slabs:
            re1_ref[:, st] = xr
            im1_ref[:, st] = xi
    h = h_tm[...].reshape(tt, SUBLANES, n_slabs * u_w)
    h_ref[...] = jnp.swapaxes(h, 0, 1).astype(h_ref.dtype)


def _s5(u_blocks, re0, im0, lb_re, lb_im, wb, wc, d_skip, tt):
    n_slabs = wb.shape[0]
    n_state = re0.shape[1]
    blk = (SUBLANES, tt, u_blocks.shape[-1])
    grid = (u_blocks.shape[0] // SUBLANES, u_blocks.shape[1] // tt)
    rows = SUBLANES * tt
    u_spec = pl.BlockSpec(blk, lambda i, j: (i, j, 0))
    st_spec = pl.BlockSpec((SUBLANES, n_state), lambda i, j: (i, 0))
    full = lambda a: _const_spec(a.shape)
    return pl.pallas_call(
        functools.partial(_s5_kernel, tt=tt, n_slabs=n_slabs),
        grid=grid,
        in_specs=[u_spec, st_spec, st_spec, full(lb_re), full(lb_im), full(wb), full(wc), full(d_skip)],
        out_specs=[u_spec, st_spec, st_spec],
        out_shape=[jax.ShapeDtypeStruct(u_blocks.shape, BF16),
                   jax.ShapeDtypeStruct(re0.shape, F32), jax.ShapeDtypeStruct(im0.shape, F32)],
        scratch_shapes=[pltpu.VMEM((rows, u_blocks.shape[-1]), F32)] * 2
                       + [pltpu.VMEM((rows, 2 * SLAB_GROUPS * S5_STATE), F32)] * 4,
        compiler_params=pltpu.CompilerParams(dimension_semantics=("parallel", "arbitrary"),
                                             vmem_limit_bytes=VMEM_LIMIT),
        name="s5_scan",
    )(u_blocks, re0, im0, lb_re, lb_im, wb, wc, d_skip)


def _s5_step_kernel(u_ref, re0_ref, im0_ref, lbr_ref, lbi_ref, wb_ref, wc_ref, d_ref,
                    h_ref, re1_ref, im1_ref, *, n_slabs):
    s_w = SLAB_GROUPS * S5_STATE
    u_w = SLAB_GROUPS * S5_GROUP
    u = u_ref[...]
    ys = []
    for s in range(n_slabs):
        us = u[:, s * u_w:(s + 1) * u_w]
        bu = _mm1(us, wb_ref[s])
        st = slice(s * s_w, (s + 1) * s_w)
        lbr, lbi = lbr_ref[:, st], lbi_ref[:, st]
        xr, xi = re0_ref[:, st], im0_ref[:, st]
        nr = lbr * xr - lbi * xi + bu[:, :s_w]
        ni = lbr * xi + lbi * xr + bu[:, s_w:]
        re1_ref[:, st] = nr
        im1_ref[:, st] = ni
        y = _mm1(jnp.concatenate([nr, ni], axis=1), wc_ref[s]) + d_ref[:, s * u_w:(s + 1) * u_w] * us
        ys.append(_gelu_tanh(y))
    h_ref[...] = jnp.concatenate(ys, axis=1).astype(h_ref.dtype)


def _s5_step(u2d, re0, im0, lb_re, lb_im, wb, wc, d_skip):
    full = lambda a: _const_spec(a.shape)
    args = (u2d, re0, im0, lb_re, lb_im, wb, wc, d_skip)
    return pl.pallas_call(
        functools.partial(_s5_step_kernel, n_slabs=wb.shape[0]),
        grid=(1,),
        in_specs=[full(a) for a in args],
        out_specs=[full(u2d), full(re0), full(im0)],
        out_shape=[jax.ShapeDtypeStruct(u2d.shape, BF16),
                   jax.ShapeDtypeStruct(re0.shape, F32), jax.ShapeDtypeStruct(im0.shape, F32)],
        compiler_params=pltpu.CompilerParams(vmem_limit_bytes=VMEM_LIMIT),
        name="s5_step",
    )(*args)


def _block_diag_slabs(m):
    g, a, b = m.shape
    eye = jnp.eye(SLAB_GROUPS, dtype=m.dtype)
    m4 = m.reshape(g // SLAB_GROUPS, SLAB_GROUPS, a, b)
    return jnp.einsum("sgab,gh->sgahb", m4, eye).reshape(g // SLAB_GROUPS, SLAB_GROUPS * a, SLAB_GROUPS * b)


def _tail_kernel(x_ref, oa_ref, hg_ref, gt_ref, wro_ref, w1_ref, b1_ref, w2_ref, b2_ref, wmo_ref,
                 npm_ref, nf_ref, npf_ref, wg_ref, wu_ref, wd_ref, y_ref):
    d = x_ref.shape[-1]
    tm = x_ref.shape[0]
    n_sub = 2 if tm % 32 == 0 else 1
    subs = [slice(i * (tm // n_sub), (i + 1) * (tm // n_sub)) for i in range(n_sub)]
    hg = [hg_ref[s, :] for s in subs]
    a_out = [_dot(oa_ref[s, :], wro_ref[...]) for s in subs]
    b_lin = [_dot(h, w1_ref[...]) + b1_ref[...] for h in hg]
    b_gate = [_dot(h, w2_ref[...]) + b2_ref[...] for h in hg]
    merged = [(_sigmoid(gt_ref[s, :d]) * a + _sigmoid(gt_ref[s, d:]) * (bl * _sigmoid(bg))).astype(BF16)
              for s, a, bl, bg in zip(subs, a_out, b_lin, b_gate)]
    mix = [_dot(m, wmo_ref[...]) for m in merged]
    x1 = [x_ref[s, :] + _rms(m, npm_ref[...]) for s, m in zip(subs, mix)]
    hb = [_rms(x, nf_ref[...]).astype(BF16) for x in x1]
    gate = [_dot(h, wg_ref[...]) for h in hb]
    up = [_dot(h, wu_ref[...]) for h in hb]
    act = [(g * _sigmoid(g) * u).astype(BF16) for g, u in zip(gate, up)]
    f = [_dot(a, wd_ref[...]) for a in act]
    for s, x, ff in zip(subs, x1, f):
        y_ref[s, :] = x + _rms(ff, npf_ref[...])


def _tail(x2d, oa, hg, gates, tw, tm):
    rows, d = x2d.shape
    row = lambda a: pl.BlockSpec((tm, a.shape[1]), lambda i: (i, 0))
    wnames = ("wro", "w1", "b1", "w2", "b2", "wmo", "npm", "nf", "npf", "wg", "wu", "wd")
    wargs = [tw[n] for n in wnames]
    return pl.pallas_call(
        _tail_kernel,
        grid=(rows // tm,),
        in_specs=[row(x2d), row(oa), row(hg), row(gates)] + [_const_spec(a.shape, True) for a in wargs],
        out_specs=pl.BlockSpec((tm, d), lambda i: (i, 0)),
        out_shape=jax.ShapeDtypeStruct((rows, d), F32),
        compiler_params=pltpu.CompilerParams(dimension_semantics=("parallel",),
                                             vmem_limit_bytes=VMEM_LIMIT),
        name="tail",
    )(x2d, oa, hg, gates, *wargs)


def _pick_tile(rows, target):
    t = min(rows, target)
    assert rows % t == 0
    return t


def _layer(x, shift0, wkv0, re0, im0, lw, *, chunk, n_seq, s5_tt, row_tile):
    bsz, t, d = x.shape
    rows = bsz * t
    c_shift = shift0.shape[-1]
    c_u = lw["d_skip"].shape[-1]
    tm = _pick_tile(rows, row_tile)
    x2d = x.reshape(rows, d)
    pr, u, gates = _proj(x2d, lw["norm_pre_mix"], lw["w_in"], c_shift, c_u, tm)

    n_state = re0.shape[1] * re0.shape[2]
    s5_args = (re0.reshape(bsz, n_state), im0.reshape(bsz, n_state),
               lw["lb_re"], lw["lb_im"], lw["wb"], lw["wc"], lw["d_skip"])
    if t == 1:
        oa, wkv1 = _rwkv_step(pr, shift0, wkv0, lw["rwkv"], lw["e_mat"], n_seq=STEP_SEQS)
        pr_last = pr
        hg, re1, im1 = _s5_step(u, *s5_args)
    else:
        oa, wkv1 = _rwkv_chunked(pr.reshape(bsz, t, c_shift), shift0, wkv0, lw["rwkv"], lw["e_mat"], chunk, n_seq)
        oa = oa.reshape(rows, -1)
        pr_last = pr.reshape(bsz, t, c_shift)[:, -1]
        hg, re1, im1 = _s5(u.reshape(bsz, t, c_u), *s5_args, tt=s5_tt)
    y = _tail(x2d, oa, hg.reshape(rows, c_u), gates, lw["tail"], tm)
    return y.reshape(x.shape), pr_last, wkv1, re1.reshape(re0.shape), im1.reshape(im0.shape)


def _prepare_layer_weights(l, p):
    row = lambda a: a[l][None, :].astype(F32)
    width = p["w0"].shape[-1]
    n_dec, n_aaa, n_gate = p["w_decay_up"].shape[1], p["w_aaa_up"].shape[1], p["w_gate_up"].shape[1]
    assert n_dec + n_aaa + n_gate == LORA_PAD

    def lora_pad(wup, start):
        return jnp.zeros((LORA_PAD, width), F32).at[start:start + wup.shape[0]].set(wup).astype(BF16)

    rwkv = {
        "mu": row(p["mu_shift"]), "w0": row(p["w0"]), "a0": row(p["a0"]), "k_k": row(p["k_k"]),
        "k_a": row(p["k_a"]), "r_k": row(p["r_k"]), "lnx_g": row(p["lnx_g"]), "lnx_b": row(p["lnx_b"]),
        "wd": lora_pad(p["w_decay_up"][l], 0),
        "wa": lora_pad(p["w_aaa_up"][l], n_dec),
        "wg": lora_pad(p["w_gate_up"][l], n_dec + n_aaa),
    }
    head_id = jnp.arange(MXU_DIM) // HEAD
    e_mat = (head_id[:, None] == head_id[None, :]).astype(BF16)

    lb_re, lb_im, bb_re_t, bb_im_t = _s5_discretise(
        p["s5_lam_re"][l], p["s5_lam_im"][l], p["s5_log_dt"][l],
        jnp.swapaxes(p["s5_b_re"][l], 1, 2), jnp.swapaxes(p["s5_b_im"][l], 1, 2))
    n_state = lb_re.shape[0] * lb_re.shape[2]
    to_out = lambda cc: _block_diag_slabs(jnp.swapaxes(cc, 1, 2))
    wb = jnp.concatenate([_block_diag_slabs(bb_re_t), _block_diag_slabs(bb_im_t)], axis=-1).astype(BF16)
    wc = jnp.concatenate([to_out(p["s5_c_re"][l]), -to_out(p["s5_c_im"][l])], axis=1).astype(BF16)

    bf = lambda a: a[l].astype(BF16)
    tail = {
        "wro": bf(p["w_rwkv_out"]), "w1": bf(p["glu_w1"]), "b1": row(p["glu_b1"]), "w2": bf(p["glu_w2"]),
        "b2": row(p["glu_b2"]), "wmo": bf(p["w_merge_out"]), "npm": row(p["norm_post_mix"]),
        "nf": row(p["norm_pre_ffn"]), "npf": row(p["norm_post_ffn"]),
        "wg": bf(p["w_ffn_gate"]), "wu": bf(p["w_ffn_up"]), "wd": bf(p["w_ffn_down"]),
    }
    return {
        "norm_pre_mix": row(p["norm_pre_mix"]), "w_in": bf(p["w_in"]), "rwkv": rwkv, "e_mat": e_mat,
        "lb_re": lb_re.reshape(1, n_state), "lb_im": lb_im.reshape(1, n_state), "wb": wb, "wc": wc,
        "d_skip": row(p["s5_d"]), "tail": tail,
    }


_PARAM_NAMES = ("norm_pre_mix", "norm_post_mix", "norm_pre_ffn", "norm_post_ffn", "w_in", "mu_shift",
                "w0", "w_decay_up", "a0", "w_aaa_up", "w_gate_up", "k_k", "k_a", "r_k", "lnx_g", "lnx_b",
                "w_rwkv_out", "s5_lam_re", "s5_lam_im", "s5_log_dt", "s5_b_re", "s5_b_im", "s5_c_re",
                "s5_c_im", "s5_d", "glu_w1", "glu_b1", "glu_w2", "glu_b2", "w_merge_out",
                "w_ffn_gate", "w_ffn_up", "w_ffn_down")


def _forward(x_prompt, x_sample, state_shift, state_wkv, state_s5_re, state_s5_im, params,
             *, chunk=64, n_seq=4, s5_tt=128, row_tile=256):
    depth = params["w_in"].shape[0]
    heads = state_wkv.shape[2]
    bp = x_prompt.shape[0]
    yp, ys = x_prompt, x_sample
    outs_p, outs_s = [], []
    for l in range(depth):
        lw = _prepare_layer_weights(l, params)
        zp_shift = jnp.zeros((bp, state_shift.shape[-1]), F32)
        zp_wkv = jnp.zeros((bp, heads, HEAD, HEAD), F32)
        zp_s5 = jnp.zeros((bp,) + state_s5_re.shape[2:], F32)
        yp, *st_p = _layer(yp, zp_shift, zp_wkv, zp_s5, zp_s5, lw, chunk=chunk, n_seq=n_seq, s5_tt=s5_tt, row_tile=row_tile)
        ys, *st_s = _layer(ys, state_shift[l], state_wkv[l], state_s5_re[l], state_s5_im[l], lw,
                           chunk=chunk, n_seq=n_seq, s5_tt=s5_tt, row_tile=row_tile)
        outs_p.append(st_p)
        outs_s.append(st_s)
    stack = lambda outs, i, dt: jnp.stack([o[i] for o in outs]).astype(dt)
    dt_p, dt_s = x_prompt.dtype, x_sample.dtype
    return (yp, ys,
            stack(outs_p, 0, dt_p), stack(outs_p, 1, dt_p), stack(outs_p, 2, dt_p), stack(outs_p, 3, dt_p),
            stack(outs_s, 0, dt_s), stack(outs_s, 1, dt_s), stack(outs_s, 2, dt_s), stack(outs_s, 3, dt_s))


def kernel(x_prompt, x_sample, state_shift, state_wkv, state_s5_re, state_s5_im, norm_pre_mix, norm_post_mix, norm_pre_ffn, norm_post_ffn, w_in, mu_shift, w0, w_decay_up, a0, w_aaa_up, w_gate_up, k_k, k_a, r_k, lnx_g, lnx_b, w_rwkv_out, s5_lam_re, s5_lam_im, s5_log_dt, s5_b_re, s5_b_im, s5_c_re, s5_c_im, s5_d, glu_w1, glu_b1, glu_w2, glu_b2, w_merge_out, w_ffn_gate, w_ffn_up, w_ffn_down):
    params = dict(zip(_PARAM_NAMES, (norm_pre_mix, norm_post_mix, norm_pre_ffn, norm_post_ffn, w_in, mu_shift,
                                     w0, w_decay_up, a0, w_aaa_up, w_gate_up, k_k, k_a, r_k, lnx_g, lnx_b,
                                     w_rwkv_out, s5_lam_re, s5_lam_im, s5_log_dt, s5_b_re, s5_b_im, s5_c_re,
                                     s5_c_im, s5_d, glu_w1, glu_b1, glu_w2, glu_b2, w_merge_out,
                                     w_ffn_gate, w_ffn_up, w_ffn_down)))
    return _forward(x_prompt, x_sample, state_shift, state_wkv, state_s5_re, state_s5_im, params)
```

```python
import functools
import math

import jax
import jax.numpy as jnp
from jax import lax
from jax.experimental import pallas as pl
from jax.experimental.pallas import tpu as pltpu

F32 = jnp.float32
BF16 = jnp.bfloat16

NORM_EPS = 1e-6
LNX_EPS = 64e-5
HEAD = 64
GROUP_HEADS = 4
GROUP_W = GROUP_HEADS * HEAD
LORA_PAD = 128
S5_GROUP = 16
S5_STATE = 64
SLAB_GROUPS = 8
SUBLANES = 8
LANES = 128
MXU_DIM = 256
S5_ROW_BLOCK = 256
VMEM_LIMIT = 56 * 1024 * 1024

NN = (((1,), (0,)), ((), ()))
NT = (((1,), (1,)), ((), ()))


def _dot(a, b, dims=NN):
    return lax.dot_general(a, b, dims, preferred_element_type=F32)


def _split2(x):
    hi = x.astype(BF16)
    lo = (x - hi.astype(F32)).astype(BF16)
    return hi, lo


def _split3(x):
    hi = x.astype(BF16)
    r1 = x - hi.astype(F32)
    mid = r1.astype(BF16)
    lo = (r1 - mid.astype(F32)).astype(BF16)
    return hi, mid, lo


def _mm1(a, b, dims=NN):
    return _dot(a.astype(BF16), b.astype(BF16), dims)


def _mm_exact_lhs(a_bf16, b):
    h, m, l = _split3(b)
    return _dot(a_bf16, h) + (_dot(a_bf16, m) + _dot(a_bf16, l))


def _rms(x, g):
    return x * lax.rsqrt(jnp.mean(x * x, axis=-1, keepdims=True) + NORM_EPS) * g


def _sigmoid(x):
    return 1.0 / (1.0 + jnp.exp(-x))


def _gelu_tanh(x):
    c = math.sqrt(2.0 / math.pi)
    return 0.5 * x * (1.0 + jnp.tanh(c * (x + 0.044715 * (x * x * x))))


def _const_spec(shape, single_buffer=False):
    idx = lambda *_: (0,) * len(shape)
    if single_buffer:
        return pl.BlockSpec(shape, idx, pipeline_mode=pl.Buffered(1))
    return pl.BlockSpec(shape, idx)


def _proj_kernel(x_ref, g_ref, w_ref, pr_ref, u_ref, gt_ref, *, c_shift, c_u):
    hb = _rms(x_ref[...], g_ref[...]).astype(BF16)
    pr_ref[...] = _dot(hb, w_ref[:, :c_shift])
    u_ref[...] = _dot(hb, w_ref[:, c_shift:c_shift + c_u])
    gt_ref[...] = _dot(hb, w_ref[:, c_shift + c_u:])


def _proj(x2d, g, w_in_bf16, c_shift, c_u, tm):
    rows, d = x2d.shape
    cols = w_in_bf16.shape[1]
    c_g = cols - c_shift - c_u
    row = lambda w: pl.BlockSpec((tm, w), lambda i: (i, 0))
    return pl.pallas_call(
        functools.partial(_proj_kernel, c_shift=c_shift, c_u=c_u),
        grid=(rows // tm,),
        in_specs=[row(d), _const_spec((1, d)), _const_spec((d, cols), True)],
        out_specs=[row(c_shift), row(c_u), row(c_g)],
        out_shape=[jax.ShapeDtypeStruct((rows, c_shift), F32),
                   jax.ShapeDtypeStruct((rows, c_u), F32),
                   jax.ShapeDtypeStruct((rows, c_g), F32)],
        compiler_params=pltpu.CompilerParams(dimension_semantics=("parallel",),
                                             vmem_limit_bytes=VMEM_LIMIT),
        name="proj",
    )(x2d, g, w_in_bf16)


def _head_sum(x, e_ref):
    rows, width = x.shape
    gw = e_ref.shape[0]
    n_lg = width // gw
    hi, lo = _split2(x)
    stacked = jnp.concatenate([part[:, j * gw:(j + 1) * gw] for part in (hi, lo) for j in range(n_lg)], axis=0)
    sums = _dot(stacked, e_ref[...])
    return jnp.concatenate([sums[j * rows:(j + 1) * rows] + sums[(n_lg + j) * rows:(n_lg + j + 1) * rows]
                            for j in range(n_lg)], axis=1)


def _rwkv_token_prep(xr, w, e_ref):
    width = w["w0"].shape[-1]
    r = xr[:, :width]
    k = xr[:, width:2 * width]
    v = xr[:, 2 * width:3 * width]
    lo = xr[:, 3 * width:3 * width + LORA_PAD]
    wl = w["w0"][...] + _mm1(jnp.tanh(lo), w["wd"][...])
    lw = -math.exp(-0.5) * _sigmoid(wl)
    a = _sigmoid(w["a0"][...] + _mm1(lo, w["wa"][...]))
    g = _mm1(_sigmoid(lo), w["wg"][...])
    kk = k * w["k_k"][...]
    kk = kk * lax.rsqrt(jnp.maximum(_head_sum(kk * kk, e_ref), 1e-24))
    kmod = k * (1.0 + (a - 1.0) * w["k_a"][...])
    return r, kmod, v, kk, kk * a, lw, g


def _rwkv_bonus(r, kmod, v, w, e_ref):
    return _head_sum(r * kmod * w["r_k"][...], e_ref) * v


def _rwkv_post(o, bonus, g, w, e_ref):
    inv_n = 1.0 / HEAD
    mu = _head_sum(o, e_ref) * inv_n
    oc = o - mu
    var = _head_sum(oc * oc, e_ref) * inv_n
    on = oc * lax.rsqrt(var + LNX_EPS) * w["lnx_g"][...] + w["lnx_b"][...]
    return (on + bonus) * g


_RWKV_W_NAMES = ("mu", "w0", "a0", "k_k", "k_a", "r_k", "lnx_g", "lnx_b", "wd", "wa", "wg")


def _rwkv_chunk_kernel(pr_ref, shift0_ref, s0_ref, *rest, chunk, width, n_seq):
    n_w = len(_RWKV_W_NAMES)
    w = dict(zip(_RWKV_W_NAMES, rest[:n_w]))
    e_ref, tri_ref = rest[n_w], rest[n_w + 1]
    oa_ref, s1_ref = rest[n_w + 2], rest[n_w + 3]
    carry_ref, z_ref, ops_ref, aux_ref, gall_ref = rest[n_w + 4:n_w + 9]
    step = pl.program_id(1)
    n_chunks = pl.num_programs(1) - 1
    w_slot = step % 2
    r_slot = 1 - w_slot
    n_groups = width // GROUP_W
    C = chunk
    GC = GROUP_HEADS * C

    def lane_block_masks(n_lanes, block):
        lane_block = lax.broadcasted_iota(jnp.int32, (1, n_lanes), 1) // block
        return [lane_block == h for h in range(GROUP_HEADS)]

    vec_masks = lane_block_masks(GROUP_W, HEAD)
    mat_masks = lane_block_masks(GC, C)

    def stack(x, masks):
        xb = x.astype(BF16)
        zero = jnp.zeros_like(xb)
        return jnp.concatenate([jnp.where(m, xb, zero) for m in masks], axis=0)

    @pl.when(step == 0)
    def _():
        carry_ref[...] = shift0_ref[...]
        ops_ref[1] = jnp.zeros(ops_ref.shape[1:], ops_ref.dtype)
        aux_ref[1] = jnp.zeros(aux_ref.shape[1:], aux_ref.dtype)
        gall_ref[1] = jnp.zeros(gall_ref.shape[1:], gall_ref.dtype)
        z_ref[...] = jnp.zeros(z_ref.shape, z_ref.dtype)

    @pl.when(step == 1)
    def _():
        for q in range(n_seq):
            for gi in range(n_groups):
                z_ref[q, gi] = jnp.zeros((GROUP_W, GROUP_W), F32)
                for h in range(GROUP_HEADS):
                    z_ref[q, gi, h * HEAD:(h + 1) * HEAD, h * HEAD:(h + 1) * HEAD] = s0_ref[q, gi * GROUP_HEADS + h]

    tok = lax.broadcasted_iota(jnp.int32, (C, GC), 0)
    col = lax.broadcasted_iota(jnp.int32, (C, GC), 1) % C
    strict = tok > col
    incl = tok >= col
    eye = jnp.where(tok == col, 1.0, 0.0).astype(F32)
    zi = lax.broadcasted_iota(jnp.int32, (GROUP_W, GROUP_W), 0) // HEAD
    zj = lax.broadcasted_iota(jnp.int32, (GROUP_W, GROUP_W), 1) // HEAD
    block_diag = zi == zj
    rows2 = lambda top, bottom: jnp.concatenate([top, bottom], axis=0).astype(BF16)

    def recurrence():
        chains = [(q, gi) for q in range(n_seq) for gi in range(n_groups)]
        op = lambda i, q, gi: ops_ref[r_slot, i, q * C:(q + 1) * C, gi * GROUP_W:(gi + 1) * GROUP_W]
        kap_t, r_t, v = ([op(i, q, gi) for q, gi in chains] for i in (0, 1, 4))
        kap_s = [stack(x, vec_masks) for x in kap_t]
        v_s = [stack(x, vec_masks) for x in v]
        grams = []
        for i, (q, gi) in enumerate(chains):
            right = jnp.concatenate([stack(op(2, q, gi), vec_masks), stack(op(3, q, gi), vec_masks)], axis=0)
            grams.append(_dot(rows2(kap_t[i], r_t[i]), right, NT))
        yield
        a_k = [jnp.where(strict, gm[:C, :GC], 0.0) for gm in grams]
        a_b = [jnp.where(strict, gm[:C, GC:], 0.0) for gm in grams]
        a_rk = [jnp.where(incl, gm[C:, :GC], 0.0) for gm in grams]
        a_rb = [jnp.where(incl, gm[C:, GC:], 0.0) for gm in grams]

        ps = [-a for a in a_b]
        ts = [eye + p for p in ps]
        ps = [_dot(p.astype(BF16), stack(p, mat_masks)) for p in ps]
        yield
        covered = 2
        while covered < C:
            powers = [stack(p, mat_masks) for p in ps]
            if 2 * covered < C:
                both = [_dot(rows2(t, p), pw) for t, p, pw in zip(ts, ps, powers)]
                ts = [t + x[:C] for t, x in zip(ts, both)]
                ps = [x[C:] for x in both]
            else:
                ts = [t + _dot(t.astype(BF16), pw) for t, pw in zip(ts, powers)]
            covered *= 2
            yield

        av = [_dot(rows2(a, ar), vs) for a, ar, vs in zip(a_k, a_rk, v_s)]
        yield
        tw = [_dot(t.astype(BF16), jnp.concatenate([ks, stack(x[:C], vec_masks)], axis=1))
              for t, ks, x in zip(ts, kap_s, av)]
        yield
        zs = [z_ref[q, gi] for q, gi in chains]
        pz = [_dot(rows2(x[:, :GROUP_W], rt), z.astype(BF16), NT)
              for x, rt, z in zip(tw, r_t, zs)]
        us = [p[:C] + x[:, GROUP_W:] for p, x in zip(pz, tw)]
        yield
        arbu = [_dot(a.astype(BF16), stack(u, vec_masks)) for a, u in zip(a_rb, us)]
        o_blk = [p[C:] + x[C:] - y for p, x, y in zip(pz, av, arbu)]
        for i, (q, gi) in enumerate(chains):
            vu_t = jnp.concatenate([v[i].astype(F32), us[i]], axis=0).T.astype(BF16)
            kb = jnp.concatenate([op(5, q, gi), op(6, q, gi)], axis=0)
            g_all = gall_ref[r_slot, q][:, gi * GROUP_W:(gi + 1) * GROUP_W]
            z_ref[q, gi] = zs[i] * g_all + jnp.where(block_diag, _dot(vu_t, kb), 0.0)
        yield
        o = jnp.concatenate([jnp.concatenate(o_blk[q * n_groups:(q + 1) * n_groups], axis=1)
                             for q in range(n_seq)], axis=0)
        oa_ref[...] = _rwkv_post(o, aux_ref[r_slot, 0], aux_ref[r_slot, 1], w, e_ref
                                 ).astype(oa_ref.dtype).reshape(oa_ref.shape)

    def prepare(q):
        rows = slice(q * C, (q + 1) * C)
        pr = pr_ref[q]
        row_id = lax.broadcasted_iota(jnp.int32, pr.shape, 0)
        pr_prev = jnp.where(row_id == 0, carry_ref[q], pltpu.roll(pr, 1, axis=0))
        carry_ref[q] = pr[C - 1:C, :]
        xr = pr + (pr_prev - pr) * w["mu"][...]
        r, kmod, vv, kk, bvec, lw, g = _rwkv_token_prep(xr, w, e_ref)
        aux_ref[w_slot, 0, rows] = _rwkv_bonus(r, kmod, vv, w, e_ref)
        aux_ref[w_slot, 1, rows] = g
        yield
        cum = _mm_exact_lhs(tri_ref[...], lw)
        cum_last = cum[C - 1:C, :]
        g_neg = jnp.exp(-cum)
        g_end = jnp.exp(cum_last - cum)
        prepared = (kk * jnp.exp(cum - lw),
                    r * jnp.exp(cum),
                    kmod * g_neg, bvec * g_neg,
                    vv, kmod * g_end, -(bvec * g_end))
        for i, x in enumerate(prepared):
            ops_ref[w_slot, i, rows] = x.astype(ops_ref.dtype)
        gall_ref[w_slot, q] = jnp.exp(cum_last)

    pending = [gen for gen in [prepare(q) for q in range(n_seq)] for _ in range(2)]
    for _ in recurrence():
        if pending:
            next(pending.pop(0), None)
    for gen in pending:
        next(gen, None)

    @pl.when(step == n_chunks)
    def _():
        for q in range(n_seq):
            for gi in range(n_groups):
                for h in range(GROUP_HEADS):
                    s1_ref[q, gi * GROUP_HEADS + h] = z_ref[q, gi, h * HEAD:(h + 1) * HEAD, h * HEAD:(h + 1) * HEAD]


_N_OPS = 7


def _rwkv_weight_inputs(wts):
    return [wts[n] for n in _RWKV_W_NAMES]


def _rwkv_weight_specs(wts):
    return [_const_spec(wts[n].shape) for n in _RWKV_W_NAMES]


def _rwkv_chunked(pr3d, shift0, wkv0, wts, e_mat, chunk, n_seq):
    bsz, t, c_shift = pr3d.shape
    heads = wkv0.shape[1]
    width = heads * HEAD
    tri = jnp.tril(jnp.ones((chunk, chunk), F32)).astype(BF16)
    kern = functools.partial(_rwkv_chunk_kernel, chunk=chunk, width=width, n_seq=n_seq)
    n_chunks = t // chunk
    rows = n_seq * chunk
    return pl.pallas_call(
        kern,
        grid=(bsz // n_seq, n_chunks + 1),
        in_specs=[pl.BlockSpec((n_seq, chunk, c_shift), lambda b, s: (b, jnp.minimum(s, n_chunks - 1), 0)),
                  pl.BlockSpec((n_seq, 1, c_shift), lambda b, s: (b, 0, 0)),
                  pl.BlockSpec((n_seq, heads, HEAD, HEAD), lambda b, s: (b, 0, 0, 0))]
                 + _rwkv_weight_specs(wts)
                 + [_const_spec(e_mat.shape), _const_spec(tri.shape)],
        out_specs=[pl.BlockSpec((n_seq, chunk, width), lambda b, s: (b, jnp.maximum(s - 1, 0), 0)),
                   pl.BlockSpec((n_seq, heads, HEAD, HEAD), lambda b, s: (b, 0, 0, 0))],
        out_shape=[jax.ShapeDtypeStruct((bsz, t, width), BF16),
                   jax.ShapeDtypeStruct((bsz, heads, HEAD, HEAD), F32)],
        scratch_shapes=[pltpu.VMEM((n_seq, 1, c_shift), F32),
                        pltpu.VMEM((n_seq, width // GROUP_W, GROUP_W, GROUP_W), F32),
                        pltpu.VMEM((2, _N_OPS, rows, width), BF16),
                        pltpu.VMEM((2, 2, rows, width), F32),
                        pltpu.VMEM((2, n_seq, 1, width), F32)],
        compiler_params=pltpu.CompilerParams(dimension_semantics=("parallel", "arbitrary"),
                                             vmem_limit_bytes=VMEM_LIMIT),
        name="rwkv_chunk",
    )(pr3d, shift0[:, None, :], wkv0, *_rwkv_weight_inputs(wts), e_mat, tri)


_STEP_VECS = 6


def _rwkv_step_kernel(pr_ref, shift0_ref, s_ref, *rest):
    n_w = len(_RWKV_W_NAMES)
    w = dict(zip(_RWKV_W_NAMES, rest[:n_w]))
    e_ref, oa_ref, s1_ref, vec_ref, post_ref, o_ref = rest[n_w:n_w + 6]
    h = pl.program_id(0)

    @pl.when(h == 0)
    def _():
        pr = pr_ref[...]
        xr = pr + (shift0_ref[...] - pr) * w["mu"][...]
        r, kmod, v, kk, bvec, lw, g = _rwkv_token_prep(xr, w, e_ref)
        for i, x in enumerate((kk, bvec, jnp.exp(lw), kmod, r, v)):
            vec_ref[i] = x.T
        post_ref[0] = _rwkv_bonus(r, kmod, v, w, e_ref)
        post_ref[1] = g

    base = pl.multiple_of(h * HEAD, HEAD)
    kk_h, b_h, dec_h, k_h, r_h = [vec_ref[i, pl.ds(base, HEAD), :] for i in range(5)]

    def value_row(vi, carry):
        tile = s_ref[0, vi]
        s_kappa = jnp.sum(tile * kk_h, axis=0, keepdims=True)
        new = tile * dec_h - s_kappa * b_h + vec_ref[5, pl.ds(base + vi, 1), :] * k_h
        s1_ref[0, vi] = new
        o_ref[pl.ds(base + vi, 1), :] = jnp.sum(new * r_h, axis=0, keepdims=True)
        return carry

    lax.fori_loop(0, HEAD, value_row, 0, unroll=4)

    @pl.when(h == pl.num_programs(0) - 1)
    def _():
        oa_ref[...] = _rwkv_post(o_ref[...].T, post_ref[0], post_ref[1], w, e_ref).astype(oa_ref.dtype)


def _rwkv_step(pr2d, shift0, wkv0, wts, e_mat):
    bsz, c_shift = pr2d.shape
    heads = wkv0.shape[1]
    width = heads * HEAD
    full = lambda a: _const_spec(a.shape)
    st = pl.BlockSpec((1, HEAD, HEAD, bsz), lambda hh: (hh, 0, 0, 0))
    oa, s1_t = pl.pallas_call(
        _rwkv_step_kernel,
        grid=(heads,),
        in_specs=[full(pr2d), full(shift0), st] + _rwkv_weight_specs(wts) + [full(e_mat)],
        out_specs=[_const_spec((bsz, width)), st],
        out_shape=[jax.ShapeDtypeStruct((bsz, width), BF16),
                   jax.ShapeDtypeStruct((heads, HEAD, HEAD, bsz), F32)],
        scratch_shapes=[pltpu.VMEM((_STEP_VECS, width, bsz), F32),
                        pltpu.VMEM((2, bsz, width), F32),
                        pltpu.VMEM((width, bsz), F32)],
        compiler_params=pltpu.CompilerParams(dimension_semantics=("arbitrary",), vmem_limit_bytes=VMEM_LIMIT),
        name="rwkv_step",
    )(pr2d, shift0, jnp.transpose(wkv0, (1, 2, 3, 0)), *_rwkv_weight_inputs(wts), e_mat)
    return oa, jnp.transpose(s1_t, (3, 0, 1, 2))


def _s5_disc_kernel(lre_ref, lim_ref, ldt_ref, bre_ref, bim_ref, lbr_ref, lbi_ref, bbr_ref, bbi_ref):
    lam_re, lam_im = lre_ref[...], lim_ref[...]
    dt = jnp.exp(ldt_ref[...])
    mag = jnp.exp(lam_re * dt)
    ang = lam_im * dt
    lb_re, lb_im = mag * jnp.cos(ang), mag * jnp.sin(ang)
    nr, ni = lb_re - 1.0, lb_im
    den = lam_re * lam_re + lam_im * lam_im
    f_re = (nr * lam_re + ni * lam_im) / den
    f_im = (ni * lam_re - nr * lam_im) / den
    b_re, b_im = bre_ref[...], bim_ref[...]
    lbr_ref[...] = lb_re
    lbi_ref[...] = lb_im
    bbr_ref[...] = f_re * b_re - f_im * b_im
    bbi_ref[...] = f_re * b_im + f_im * b_re


def _s5_discretise(lam_re, lam_im, log_dt, b_re_t, b_im_t):
    g, p = lam_re.shape
    full = lambda a: _const_spec(a.shape)
    args = (lam_re[:, None, :], lam_im[:, None, :], log_dt[:, None, None], b_re_t, b_im_t)
    return pl.pallas_call(
        _s5_disc_kernel,
        grid=(1,),
        in_specs=[full(a) for a in args],
        out_specs=[_const_spec((g, 1, p))] * 2 + [full(b_re_t)] * 2,
        out_shape=[jax.ShapeDtypeStruct((g, 1, p), F32)] * 2 + [jax.ShapeDtypeStruct(b_re_t.shape, F32)] * 2,
        name="s5_discretise",
    )(*args)


def _s5_kernel(u_ref, re0_ref, im0_ref, lbr_ref, lbi_ref, wb_ref, wc_ref, d_ref,
               h_ref, re1_ref, im1_ref, u_tm, h_tm, bu0, bu1, xs0, xs1, *, tt, n_slabs):
    t_blk = pl.program_id(1)
    rows = SUBLANES * tt
    s_w = SLAB_GROUPS * S5_STATE
    u_w = SLAB_GROUPS * S5_GROUP
    bu, xs = (bu0, bu1), (xs0, xs1)
    rb = min(rows, S5_ROW_BLOCK)
    steps_per_rb = rb // SUBLANES

    @pl.when(t_blk == 0)
    def _():
        re1_ref[...] = re0_ref[...]
        im1_ref[...] = im0_ref[...]

    u_tm[...] = jnp.swapaxes(u_ref[...], 0, 1).reshape(rows, n_slabs * u_w)

    def project_in(s, j):
        r0 = j * rb
        bu[s % 2][r0:r0 + rb, :] = _mm1(u_tm[r0:r0 + rb, s * u_w:(s + 1) * u_w], wb_ref[s])

    def project_out(s, j):
        r0 = j * rb
        lanes = slice(s * u_w, (s + 1) * u_w)
        y = _mm1(xs[s % 2][r0:r0 + rb, :], wc_ref[s]) + d_ref[:, lanes] * u_tm[r0:r0 + rb, lanes]
        h_tm[r0:r0 + rb, lanes] = _gelu_tanh(y)

    for j in range(rows // rb):
        project_in(0, j)
    for p in range(n_slabs + 1):
        if p < n_slabs:
            st = slice(p * s_w, (p + 1) * s_w)
            lbr = jnp.broadcast_to(lbr_ref[:, st], (SUBLANES, s_w))
            lbi = jnp.broadcast_to(lbi_ref[:, st], (SUBLANES, s_w))
            xr, xi = re1_ref[:, st], im1_ref[:, st]
        for j in range(rows // rb):
            if p < n_slabs:
                for t in range(j * steps_per_rb, (j + 1) * steps_per_rb):
                    at_t = slice(t * SUBLANES, (t + 1) * SUBLANES)
                    xr, xi = (lbr * xr - lbi * xi + bu[p % 2][at_t, :s_w],
                              lbr * xi + lbi * xr + bu[p % 2][at_t, s_w:])
                    xs[p % 2][at_t, :s_w] = xr
                    xs[p % 2][at_t, s_w:] = xi
            if p + 1 < n_slabs:
                project_in(p + 1, j)
            if p >= 1:
                project_out(p - 1, j)
        if p < n_slabs:
            re1_ref[:, st] = xr
            im1_ref[:, st] = xi
    h = h_tm[...].reshape(tt, SUBLANES, n_slabs * u_w)
    h_ref[...] = jnp.swapaxes(h, 0, 1).astype(h_ref.dtype)


def _s5(u_blocks, re0, im0, lb_re, lb_im, wb, wc, d_skip, tt):
    n_slabs = wb.shape[0]
    n_state = re0.shape[1]
    blk = (SUBLANES, tt, u_blocks.shape[-1])
    grid = (u_blocks.shape[0] // SUBLANES, u_blocks.shape[1] // tt)
    rows = SUBLANES * tt
    u_spec = pl.BlockSpec(blk, lambda i, j: (i, j, 0))
    st_spec = pl.BlockSpec((SUBLANES, n_state), lambda i, j: (i, 0))
    full = lambda a: _const_spec(a.shape)
    return pl.pallas_call(
        functools.partial(_s5_kernel, tt=tt, n_slabs=n_slabs),
        grid=grid,
        in_specs=[u_spec, st_spec, st_spec, full(lb_re), full(lb_im), full(wb), full(wc), full(d_skip)],
        out_specs=[u_spec, st_spec, st_spec],
        out_shape=[jax.ShapeDtypeStruct(u_blocks.shape, BF16),
                   jax.ShapeDtypeStruct(re0.shape, F32), jax.ShapeDtypeStruct(im0.shape, F32)],
        scratch_shapes=[pltpu.VMEM((rows, u_blocks.shape[-1]), F32)] * 2
                       + [pltpu.VMEM((rows, 2 * SLAB_GROUPS * S5_STATE), F32)] * 4,
        compiler_params=pltpu.CompilerParams(dimension_semantics=("parallel", "arbitrary"),
                                             vmem_limit_bytes=VMEM_LIMIT),
        name="s5_scan",
    )(u_blocks, re0, im0, lb_re, lb_im, wb, wc, d_skip)


def _s5_step_kernel(u_ref, re0_ref, im0_ref, lbr_ref, lbi_ref, wb_ref, wc_ref, d_ref,
                    h_ref, re1_ref, im1_ref, *, n_slabs):
    s_w = SLAB_GROUPS * S5_STATE
    u_w = SLAB_GROUPS * S5_GROUP
    u = u_ref[...]
    ys = []
    for s in range(n_slabs):
        us = u[:, s * u_w:(s + 1) * u_w]
        bu = _mm1(us, wb_ref[s])
        st = slice(s * s_w, (s + 1) * s_w)
        lbr, lbi = lbr_ref[:, st], lbi_ref[:, st]
        xr, xi = re0_ref[:, st], im0_ref[:, st]
        nr = lbr * xr - lbi * xi + bu[:, :s_w]
        ni = lbr * xi + lbi * xr + bu[:, s_w:]
        re1_ref[:, st] = nr
        im1_ref[:, st] = ni
        y = _mm1(jnp.concatenate([nr, ni], axis=1), wc_ref[s]) + d_ref[:, s * u_w:(s + 1) * u_w] * us
        ys.append(_gelu_tanh(y))
    h_ref[...] = jnp.concatenate(ys, axis=1).astype(h_ref.dtype)


def _s5_step(u2d, re0, im0, lb_re, lb_im, wb, wc, d_skip):
    full = lambda a: _const_spec(a.shape)
    args = (u2d, re0, im0, lb_re, lb_im, wb, wc, d_skip)
    return pl.pallas_call(
        functools.partial(_s5_step_kernel, n_slabs=wb.shape[0]),
        grid=(1,),
        in_specs=[full(a) for a in args],
        out_specs=[full(u2d), full(re0), full(im0)],
        out_shape=[jax.ShapeDtypeStruct(u2d.shape, BF16),
                   jax.ShapeDtypeStruct(re0.shape, F32), jax.ShapeDtypeStruct(im0.shape, F32)],
        compiler_params=pltpu.CompilerParams(vmem_limit_bytes=VMEM_LIMIT),
        name="s5_step",
    )(*args)


def _block_diag_slabs(m):
    g, a, b = m.shape
    eye = jnp.eye(SLAB_GROUPS, dtype=m.dtype)
    m4 = m.reshape(g // SLAB_GROUPS, SLAB_GROUPS, a, b)
    return jnp.einsum("sgab,gh->sgahb", m4, eye).reshape(g // SLAB_GROUPS, SLAB_GROUPS * a, SLAB_GROUPS * b)


def _tail_kernel(x_ref, oa_ref, hg_ref, gt_ref, wro_ref, w1_ref, b1_ref, w2_ref, b2_ref, wmo_ref,
                 npm_ref, nf_ref, npf_ref, wg_ref, wu_ref, wd_ref, y_ref):
    d = x_ref.shape[-1]
    tm = x_ref.shape[0]
    n_sub = 2 if tm % 32 == 0 else 1
    subs = [slice(i * (tm // n_sub), (i + 1) * (tm // n_sub)) for i in range(n_sub)]
    hg = [hg_ref[s, :] for s in subs]
    a_out = [_dot(oa_ref[s, :], wro_ref[...]) for s in subs]
    b_lin = [_dot(h, w1_ref[...]) + b1_ref[...] for h in hg]
    b_gate = [_dot(h, w2_ref[...]) + b2_ref[...] for h in hg]
    merged = [(_sigmoid(gt_ref[s, :d]) * a + _sigmoid(gt_ref[s, d:]) * (bl * _sigmoid(bg))).astype(BF16)
              for s, a, bl, bg in zip(subs, a_out, b_lin, b_gate)]
    mix = [_dot(m, wmo_ref[...]) for m in merged]
    x1 = [x_ref[s, :] + _rms(m, npm_ref[...]) for s, m in zip(subs, mix)]
    hb = [_rms(x, nf_ref[...]).astype(BF16) for x in x1]
    gate = [_dot(h, wg_ref[...]) for h in hb]
    up = [_dot(h, wu_ref[...]) for h in hb]
    act = [(g * _sigmoid(g) * u).astype(BF16) for g, u in zip(gate, up)]
    f = [_dot(a, wd_ref[...]) for a in act]
    for s, x, ff in zip(subs, x1, f):
        y_ref[s, :] = x + _rms(ff, npf_ref[...])


def _tail(x2d, oa, hg, gates, tw, tm):
    rows, d = x2d.shape
    row = lambda a: pl.BlockSpec((tm, a.shape[1]), lambda i: (i, 0))
    wnames = ("wro", "w1", "b1", "w2", "b2", "wmo", "npm", "nf", "npf", "wg", "wu", "wd")
    wargs = [tw[n] for n in wnames]
    return pl.pallas_call(
        _tail_kernel,
        grid=(rows // tm,),
        in_specs=[row(x2d), row(oa), row(hg), row(gates)] + [_const_spec(a.shape, True) for a in wargs],
        out_specs=pl.BlockSpec((tm, d), lambda i: (i, 0)),
        out_shape=jax.ShapeDtypeStruct((rows, d), F32),
        compiler_params=pltpu.CompilerParams(dimension_semantics=("parallel",),
                                             vmem_limit_bytes=VMEM_LIMIT),
        name="tail",
    )(x2d, oa, hg, gates, *wargs)


def _pick_tile(rows, target):
    t = min(rows, target)
    assert rows % t == 0
    return t


def _layer(x, shift0, wkv0, re0, im0, lw, *, chunk, n_seq, s5_tt, row_tile):
    bsz, t, d = x.shape
    rows = bsz * t
    c_shift = shift0.shape[-1]
    c_u = lw["d_skip"].shape[-1]
    tm = _pick_tile(rows, row_tile)
    x2d = x.reshape(rows, d)
    pr, u, gates = _proj(x2d, lw["norm_pre_mix"], lw["w_in"], c_shift, c_u, tm)

    n_state = re0.shape[1] * re0.shape[2]
    s5_args = (re0.reshape(bsz, n_state), im0.reshape(bsz, n_state),
               lw["lb_re"], lw["lb_im"], lw["wb"], lw["wc"], lw["d_skip"])
    if t == 1:
        oa, wkv1 = _rwkv_step(pr, shift0, wkv0, lw["rwkv"], lw["e_mat"])
        pr_last = pr
        hg, re1, im1 = _s5_step(u, *s5_args)
    else:
        oa, wkv1 = _rwkv_chunked(pr.reshape(bsz, t, c_shift), shift0, wkv0, lw["rwkv"], lw["e_mat"], chunk, n_seq)
        oa = oa.reshape(rows, -1)
        pr_last = pr.reshape(bsz, t, c_shift)[:, -1]
        hg, re1, im1 = _s5(u.reshape(bsz, t, c_u), *s5_args, tt=s5_tt)
    y = _tail(x2d, oa, hg.reshape(rows, c_u), gates, lw["tail"], tm)
    return y.reshape(x.shape), pr_last, wkv1, re1.reshape(re0.shape), im1.reshape(im0.shape)


def _prepare_layer_weights(l, p):
    row = lambda a: a[l][None, :].astype(F32)
    width = p["w0"].shape[-1]
    n_dec, n_aaa, n_gate = p["w_decay_up"].shape[1], p["w_aaa_up"].shape[1], p["w_gate_up"].shape[1]
    assert n_dec + n_aaa + n_gate == LORA_PAD

    def lora_pad(wup, start):
        return jnp.zeros((LORA_PAD, width), F32).at[start:start + wup.shape[0]].set(wup).astype(BF16)

    rwkv = {
        "mu": row(p["mu_shift"]), "w0": row(p["w0"]), "a0": row(p["a0"]), "k_k": row(p["k_k"]),
        "k_a": row(p["k_a"]), "r_k": row(p["r_k"]), "lnx_g": row(p["lnx_g"]), "lnx_b": row(p["lnx_b"]),
        "wd": lora_pad(p["w_decay_up"][l], 0),
        "wa": lora_pad(p["w_aaa_up"][l], n_dec),
        "wg": lora_pad(p["w_gate_up"][l], n_dec + n_aaa),
    }
    head_id = jnp.arange(MXU_DIM) // HEAD
    e_mat = (head_id[:, None] == head_id[None, :]).astype(BF16)

    lb_re, lb_im, bb_re_t, bb_im_t = _s5_discretise(
        p["s5_lam_re"][l], p["s5_lam_im"][l], p["s5_log_dt"][l],
        jnp.swapaxes(p["s5_b_re"][l], 1, 2), jnp.swapaxes(p["s5_b_im"][l], 1, 2))
    n_state = lb_re.shape[0] * lb_re.shape[2]
    to_out = lambda cc: _block_diag_slabs(jnp.swapaxes(cc, 1, 2))
    wb = jnp.concatenate([_block_diag_slabs(bb_re_t), _block_diag_slabs(bb_im_t)], axis=-1).astype(BF16)
    wc = jnp.concatenate([to_out(p["s5_c_re"][l]), -to_out(p["s5_c_im"][l])], axis=1).astype(BF16)

    bf = lambda a: a[l].astype(BF16)
    tail = {
        "wro": bf(p["w_rwkv_out"]), "w1": bf(p["glu_w1"]), "b1": row(p["glu_b1"]), "w2": bf(p["glu_w2"]),
        "b2": row(p["glu_b2"]), "wmo": bf(p["w_merge_out"]), "npm": row(p["norm_post_mix"]),
        "nf": row(p["norm_pre_ffn"]), "npf": row(p["norm_post_ffn"]),
        "wg": bf(p["w_ffn_gate"]), "wu": bf(p["w_ffn_up"]), "wd": bf(p["w_ffn_down"]),
    }
    return {
        "norm_pre_mix": row(p["norm_pre_mix"]), "w_in": bf(p["w_in"]), "rwkv": rwkv, "e_mat": e_mat,
        "lb_re": lb_re.reshape(1, n_state), "lb_im": lb_im.reshape(1, n_state), "wb": wb, "wc": wc,
        "d_skip": row(p["s5_d"]), "tail": tail,
    }


_PARAM_NAMES = ("norm_pre_mix", "norm_post_mix", "norm_pre_ffn", "norm_post_ffn", "w_in", "mu_shift",
                "w0", "w_decay_up", "a0", "w_aaa_up", "w_gate_up", "k_k", "k_a", "r_k", "lnx_g", "lnx_b",
                "w_rwkv_out", "s5_lam_re", "s5_lam_im", "s5_log_dt", "s5_b_re", "s5_b_im", "s5_c_re",
                "s5_c_im", "s5_d", "glu_w1", "glu_b1", "glu_w2", "glu_b2", "w_merge_out",
                "w_ffn_gate", "w_ffn_up", "w_ffn_down")


def _forward(x_prompt, x_sample, state_shift, state_wkv, state_s5_re, state_s5_im, params,
             *, chunk=64, n_seq=4, s5_tt=128, row_tile=256):
    depth = params["w_in"].shape[0]
    heads = state_wkv.shape[2]
    bp = x_prompt.shape[0]
    yp, ys = x_prompt, x_sample
    outs_p, outs_s = [], []
    for l in range(depth):
        lw = _prepare_layer_weights(l, params)
        zp_shift = jnp.zeros((bp, state_shift.shape[-1]), F32)
        zp_wkv = jnp.zeros((bp, heads, HEAD, HEAD), F32)
        zp_s5 = jnp.zeros((bp,) + state_s5_re.shape[2:], F32)
        yp, *st_p = _layer(yp, zp_shift, zp_wkv, zp_s5, zp_s5, lw, chunk=chunk, n_seq=n_seq, s5_tt=s5_tt, row_tile=row_tile)
        ys, *st_s = _layer(ys, state_shift[l], state_wkv[l], state_s5_re[l], state_s5_im[l], lw,
                           chunk=chunk, n_seq=n_seq, s5_tt=s5_tt, row_tile=row_tile)
        outs_p.append(st_p)
        outs_s.append(st_s)
    stack = lambda outs, i, dt: jnp.stack([o[i] for o in outs]).astype(dt)
    dt_p, dt_s = x_prompt.dtype, x_sample.dtype
    return (yp, ys,
            stack(outs_p, 0, dt_p), stack(outs_p, 1, dt_p), stack(outs_p, 2, dt_p), stack(outs_p, 3, dt_p),
            stack(outs_s, 0, dt_s), stack(outs_s, 1, dt_s), stack(outs_s, 2, dt_s), stack(outs_s, 3, dt_s))


def kernel(x_prompt, x_sample, state_shift, state_wkv, state_s5_re, state_s5_im, norm_pre_mix, norm_post_mix, norm_pre_ffn, norm_post_ffn, w_in, mu_shift, w0, w_decay_up, a0, w_aaa_up, w_gate_up, k_k, k_a, r_k, lnx_g, lnx_b, w_rwkv_out, s5_lam_re, s5_lam_im, s5_log_dt, s5_b_re, s5_b_im, s5_c_re, s5_c_im, s5_d, glu_w1, glu_b1, glu_w2, glu_b2, w_merge_out, w_ffn_gate, w_ffn_up, w_ffn_down):
    params = dict(zip(_PARAM_NAMES, (norm_pre_mix, norm_post_mix, norm_pre_ffn, norm_post_ffn, w_in, mu_shift,
                                     w0, w_decay_up, a0, w_aaa_up, w_gate_up, k_k, k_a, r_k, lnx_g, lnx_b,
                                     w_rwkv_out, s5_lam_re, s5_lam_im, s5_log_dt, s5_b_re, s5_b_im, s5_c_re,
                                     s5_c_im, s5_d, glu_w1, glu_b1, glu_w2, glu_b2, w_merge_out,
                                     w_ffn_gate, w_ffn_up, w_ffn_down)))
    return _forward(x_prompt, x_sample, state_shift, state_wkv, state_s5_re, state_s5_im, params)
```

```python
import functools
import math

import jax
import jax.numpy as jnp
from jax import lax
from jax.experimental import pallas as pl
from jax.experimental.pallas import tpu as pltpu

F32 = jnp.float32
BF16 = jnp.bfloat16

NORM_EPS = 1e-6
LNX_EPS = 64e-5
HEAD = 64
GROUP_HEADS = 4
GROUP_W = GROUP_HEADS * HEAD
LORA_PAD = 128
S5_GROUP = 16
S5_STATE = 64
SLAB_GROUPS = 8
SUBLANES = 8
LANES = 128
MXU_DIM = 256
S5_ROW_BLOCK = 256
VMEM_LIMIT = 56 * 1024 * 1024

NN = (((1,), (0,)), ((), ()))
NT = (((1,), (1,)), ((), ()))


def _dot(a, b, dims=NN):
    return lax.dot_general(a, b, dims, preferred_element_type=F32)


def _split2(x):
    hi = x.astype(BF16)
    lo = (x - hi.astype(F32)).astype(BF16)
    return hi, lo


def _split3(x):
    hi = x.astype(BF16)
    r1 = x - hi.astype(F32)
    mid = r1.astype(BF16)
    lo = (r1 - mid.astype(F32)).astype(BF16)
    return hi, mid, lo


def _mm1(a, b, dims=NN):
    return _dot(a.astype(BF16), b.astype(BF16), dims)


def _mm_exact_lhs(a_bf16, b):
    h, m, l = _split3(b)
    return _dot(a_bf16, h) + (_dot(a_bf16, m) + _dot(a_bf16, l))


def _rms(x, g):
    return x * lax.rsqrt(jnp.mean(x * x, axis=-1, keepdims=True) + NORM_EPS) * g


def _sigmoid(x):
    return 1.0 / (1.0 + jnp.exp(-x))


def _gelu_tanh(x):
    c = math.sqrt(2.0 / math.pi)
    return 0.5 * x * (1.0 + jnp.tanh(c * (x + 0.044715 * (x * x * x))))


def _const_spec(shape, single_buffer=False):
    idx = lambda *_: (0,) * len(shape)
    if single_buffer:
        return pl.BlockSpec(shape, idx, pipeline_mode=pl.Buffered(1))
    return pl.BlockSpec(shape, idx)


def _proj_kernel(x_ref, g_ref, w_ref, pr_ref, u_ref, gt_ref, *, c_shift, c_u):
    hb = _rms(x_ref[...], g_ref[...]).astype(BF16)
    pr_ref[...] = _dot(hb, w_ref[:, :c_shift])
    u_ref[...] = _dot(hb, w_ref[:, c_shift:c_shift + c_u])
    gt_ref[...] = _dot(hb, w_ref[:, c_shift + c_u:])


def _proj(x2d, g, w_in_bf16, c_shift, c_u, tm):
    rows, d = x2d.shape
    cols = w_in_bf16.shape[1]
    c_g = cols - c_shift - c_u
    row = lambda w: pl.BlockSpec((tm, w), lambda i: (i, 0))
    return pl.pallas_call(
        functools.partial(_proj_kernel, c_shift=c_shift, c_u=c_u),
        grid=(rows // tm,),
        in_specs=[row(d), _const_spec((1, d)), _const_spec((d, cols), True)],
        out_specs=[row(c_shift), row(c_u), row(c_g)],
        out_shape=[jax.ShapeDtypeStruct((rows, c_shift), F32),
                   jax.ShapeDtypeStruct((rows, c_u), F32),
                   jax.ShapeDtypeStruct((rows, c_g), F32)],
        compiler_params=pltpu.CompilerParams(dimension_semantics=("parallel",),
                                             vmem_limit_bytes=VMEM_LIMIT),
        name="proj",
    )(x2d, g, w_in_bf16)


def _head_sum(x, e_ref):
    rows, width = x.shape
    gw = e_ref.shape[0]
    n_lg = width // gw
    hi, lo = _split2(x)
    stacked = jnp.concatenate([part[:, j * gw:(j + 1) * gw] for part in (hi, lo) for j in range(n_lg)], axis=0)
    sums = _dot(stacked, e_ref[...])
    return jnp.concatenate([sums[j * rows:(j + 1) * rows] + sums[(n_lg + j) * rows:(n_lg + j + 1) * rows]
                            for j in range(n_lg)], axis=1)


def _rwkv_token_prep(xr, w, e_ref):
    width = w["w0"].shape[-1]
    r = xr[:, :width]
    k = xr[:, width:2 * width]
    v = xr[:, 2 * width:3 * width]
    lo = xr[:, 3 * width:3 * width + LORA_PAD]
    wl = w["w0"][...] + _mm1(jnp.tanh(lo), w["wd"][...])
    lw = -math.exp(-0.5) * _sigmoid(wl)
    a = _sigmoid(w["a0"][...] + _mm1(lo, w["wa"][...]))
    g = _mm1(_sigmoid(lo), w["wg"][...])
    kk = k * w["k_k"][...]
    kk = kk * lax.rsqrt(jnp.maximum(_head_sum(kk * kk, e_ref), 1e-24))
    kmod = k * (1.0 + (a - 1.0) * w["k_a"][...])
    return r, kmod, v, kk, kk * a, lw, g


def _rwkv_bonus(r, kmod, v, w, e_ref):
    return _head_sum(r * kmod * w["r_k"][...], e_ref) * v


def _rwkv_post(o, bonus, g, w, e_ref):
    inv_n = 1.0 / HEAD
    mu = _head_sum(o, e_ref) * inv_n
    oc = o - mu
    var = _head_sum(oc * oc, e_ref) * inv_n
    on = oc * lax.rsqrt(var + LNX_EPS) * w["lnx_g"][...] + w["lnx_b"][...]
    return (on + bonus) * g


_RWKV_W_NAMES = ("mu", "w0", "a0", "k_k", "k_a", "r_k", "lnx_g", "lnx_b", "wd", "wa", "wg")


def _rwkv_chunk_kernel(pr_ref, shift0_ref, s0_ref, *rest, chunk, width, n_seq):
    n_w = len(_RWKV_W_NAMES)
    w = dict(zip(_RWKV_W_NAMES, rest[:n_w]))
    e_ref, tri_ref = rest[n_w], rest[n_w + 1]
    oa_ref, s1_ref = rest[n_w + 2], rest[n_w + 3]
    carry_ref, z_ref, ops_ref, aux_ref, gall_ref = rest[n_w + 4:n_w + 9]
    step = pl.program_id(1)
    n_chunks = pl.num_programs(1) - 1
    n_groups = width // GROUP_W
    C = chunk
    GC = GROUP_HEADS * C

    def lane_block_masks(n_lanes, block):
        lane_block = lax.broadcasted_iota(jnp.int32, (1, n_lanes), 1) // block
        return [lane_block == h for h in range(GROUP_HEADS)]

    vec_masks = lane_block_masks(GROUP_W, HEAD)
    mat_masks = lane_block_masks(GC, C)

    def stack(x, masks):
        xb = x.astype(BF16)
        zero = jnp.zeros_like(xb)
        return jnp.concatenate([jnp.where(m, xb, zero) for m in masks], axis=0)

    @pl.when(step == 0)
    def _():
        carry_ref[...] = shift0_ref[...]
        ops_ref[1] = jnp.zeros(ops_ref.shape[1:], ops_ref.dtype)
        aux_ref[1] = jnp.zeros(aux_ref.shape[1:], aux_ref.dtype)
        gall_ref[1] = jnp.zeros(gall_ref.shape[1:], gall_ref.dtype)
        z_ref[...] = jnp.zeros(z_ref.shape, z_ref.dtype)

    @pl.when(step == 1)
    def _():
        for q in range(n_seq):
            for gi in range(n_groups):
                z_ref[q, gi] = jnp.zeros((GROUP_W, GROUP_W), F32)
                for h in range(GROUP_HEADS):
                    z_ref[q, gi, h * HEAD:(h + 1) * HEAD, h * HEAD:(h + 1) * HEAD] = s0_ref[q, gi * GROUP_HEADS + h]

    tok = lax.broadcasted_iota(jnp.int32, (C, GC), 0)
    col = lax.broadcasted_iota(jnp.int32, (C, GC), 1) % C
    strict = tok > col
    incl = tok >= col
    eye = jnp.where(tok == col, 1.0, 0.0).astype(F32)
    zi = lax.broadcasted_iota(jnp.int32, (GROUP_W, GROUP_W), 0) // HEAD
    zj = lax.broadcasted_iota(jnp.int32, (GROUP_W, GROUP_W), 1) // HEAD
    block_diag = zi == zj
    rows2 = lambda top, bottom: jnp.concatenate([top, bottom], axis=0).astype(BF16)

    def recurrence(r_slot):
        chains = [(q, gi) for q in range(n_seq) for gi in range(n_groups)]
        op = lambda i, q, gi: ops_ref[r_slot, i, q * C:(q + 1) * C, gi * GROUP_W:(gi + 1) * GROUP_W]
        kap_t, r_t, v = ([op(i, q, gi) for q, gi in chains] for i in (0, 1, 4))
        kap_s = [stack(x, vec_masks) for x in kap_t]
        v_s = [stack(x, vec_masks) for x in v]
        grams = []
        for i, (q, gi) in enumerate(chains):
            right = jnp.concatenate([stack(op(2, q, gi), vec_masks), stack(op(3, q, gi), vec_masks)], axis=0)
            grams.append(_dot(rows2(kap_t[i], r_t[i]), right, NT))
        yield
        a_k = [jnp.where(strict, gm[:C, :GC], 0.0) for gm in grams]
        a_b = [jnp.where(strict, gm[:C, GC:], 0.0) for gm in grams]
        a_rk = [jnp.where(incl, gm[C:, :GC], 0.0) for gm in grams]
        a_rb = [jnp.where(incl, gm[C:, GC:], 0.0) for gm in grams]

        ps = [-a for a in a_b]
        ts = [eye + p for p in ps]
        ps = [_dot(p.astype(BF16), stack(p, mat_masks)) for p in ps]
        yield
        covered = 2
        while covered < C:
            powers = [stack(p, mat_masks) for p in ps]
            if 2 * covered < C:
                both = [_dot(rows2(t, p), pw) for t, p, pw in zip(ts, ps, powers)]
                ts = [t + x[:C] for t, x in zip(ts, both)]
                ps = [x[C:] for x in both]
            else:
                ts = [t + _dot(t.astype(BF16), pw) for t, pw in zip(ts, powers)]
            covered *= 2
            yield

        av = [_dot(rows2(a, ar), vs) for a, ar, vs in zip(a_k, a_rk, v_s)]
        yield
        tw = [_dot(t.astype(BF16), jnp.concatenate([ks, stack(x[:C], vec_masks)], axis=1))
              for t, ks, x in zip(ts, kap_s, av)]
        yield
        zs = [z_ref[q, gi] for q, gi in chains]
        pz = [_dot(rows2(x[:, :GROUP_W], rt), z.astype(BF16), NT)
              for x, rt, z in zip(tw, r_t, zs)]
        us = [p[:C] + x[:, GROUP_W:] for p, x in zip(pz, tw)]
        yield
        arbu = [_dot(a.astype(BF16), stack(u, vec_masks)) for a, u in zip(a_rb, us)]
        o_blk = [p[C:] + x[C:] - y for p, x, y in zip(pz, av, arbu)]
        for i, (q, gi) in enumerate(chains):
            vu_t = jnp.concatenate([v[i].astype(F32), us[i]], axis=0).T.astype(BF16)
            kb = jnp.concatenate([op(5, q, gi), op(6, q, gi)], axis=0)
            g_all = gall_ref[r_slot, q][:, gi * GROUP_W:(gi + 1) * GROUP_W]
            z_ref[q, gi] = zs[i] * g_all + jnp.where(block_diag, _dot(vu_t, kb), 0.0)
        yield
        o = jnp.concatenate([jnp.concatenate(o_blk[q * n_groups:(q + 1) * n_groups], axis=1)
                             for q in range(n_seq)], axis=0)
        oa_ref[...] = _rwkv_post(o, aux_ref[r_slot, 0], aux_ref[r_slot, 1], w, e_ref
                                 ).astype(oa_ref.dtype).reshape(oa_ref.shape)

    def prepare(q, w_slot):
        rows = slice(q * C, (q + 1) * C)
        pr = pr_ref[q]
        row_id = lax.broadcasted_iota(jnp.int32, pr.shape, 0)
        pr_prev = jnp.where(row_id == 0, carry_ref[q], pltpu.roll(pr, 1, axis=0))
        carry_ref[q] = pr[C - 1:C, :]
        xr = pr + (pr_prev - pr) * w["mu"][...]
        r, kmod, vv, kk, bvec, lw, g = _rwkv_token_prep(xr, w, e_ref)
        aux_ref[w_slot, 0, rows] = _rwkv_bonus(r, kmod, vv, w, e_ref)
        aux_ref[w_slot, 1, rows] = g
        yield
        cum = _mm_exact_lhs(tri_ref[...], lw)
        cum_last = cum[C - 1:C, :]
        g_neg = jnp.exp(-cum)
        g_end = jnp.exp(cum_last - cum)
        prepared = (kk * jnp.exp(cum - lw),
                    r * jnp.exp(cum),
                    kmod * g_neg, bvec * g_neg,
                    vv, kmod * g_end, -(bvec * g_end))
        for i, x in enumerate(prepared):
            ops_ref[w_slot, i, rows] = x.astype(ops_ref.dtype)
        gall_ref[w_slot, q] = jnp.exp(cum_last)

    def run(r_slot, w_slot):
        pending = [gen for gen in [prepare(q, w_slot) for q in range(n_seq)] for _ in range(2)]
        for _ in recurrence(r_slot):
            if pending:
                next(pending.pop(0), None)
        for gen in pending:
            next(gen, None)

    for parity in (0, 1):
        pl.when(step % 2 == parity)(functools.partial(run, r_slot=1 - parity, w_slot=parity))

    @pl.when(step == n_chunks)
    def _():
        for q in range(n_seq):
            for gi in range(n_groups):
                for h in range(GROUP_HEADS):
                    s1_ref[q, gi * GROUP_HEADS + h] = z_ref[q, gi, h * HEAD:(h + 1) * HEAD, h * HEAD:(h + 1) * HEAD]


_N_OPS = 7


def _rwkv_weight_inputs(wts):
    return [wts[n] for n in _RWKV_W_NAMES]


def _rwkv_weight_specs(wts):
    return [_const_spec(wts[n].shape) for n in _RWKV_W_NAMES]


def _rwkv_chunked(pr3d, shift0, wkv0, wts, e_mat, chunk, n_seq):
    bsz, t, c_shift = pr3d.shape
    heads = wkv0.shape[1]
    width = heads * HEAD
    tri = jnp.tril(jnp.ones((chunk, chunk), F32)).astype(BF16)
    kern = functools.partial(_rwkv_chunk_kernel, chunk=chunk, width=width, n_seq=n_seq)
    n_chunks = t // chunk
    rows = n_seq * chunk
    return pl.pallas_call(
        kern,
        grid=(bsz // n_seq, n_chunks + 1),
        in_specs=[pl.BlockSpec((n_seq, chunk, c_shift), lambda b, s: (b, jnp.minimum(s, n_chunks - 1), 0)),
                  pl.BlockSpec((n_seq, 1, c_shift), lambda b, s: (b, 0, 0)),
                  pl.BlockSpec((n_seq, heads, HEAD, HEAD), lambda b, s: (b, 0, 0, 0))]
                 + _rwkv_weight_specs(wts)
                 + [_const_spec(e_mat.shape), _const_spec(tri.shape)],
        out_specs=[pl.BlockSpec((n_seq, chunk, width), lambda b, s: (b, jnp.maximum(s - 1, 0), 0)),
                   pl.BlockSpec((n_seq, heads, HEAD, HEAD), lambda b, s: (b, 0, 0, 0))],
        out_shape=[jax.ShapeDtypeStruct((bsz, t, width), BF16),
                   jax.ShapeDtypeStruct((bsz, heads, HEAD, HEAD), F32)],
        scratch_shapes=[pltpu.VMEM((n_seq, 1, c_shift), F32),
                        pltpu.VMEM((n_seq, width // GROUP_W, GROUP_W, GROUP_W), F32),
                        pltpu.VMEM((2, _N_OPS, rows, width), BF16),
                        pltpu.VMEM((2, 2, rows, width), F32),
                        pltpu.VMEM((2, n_seq, 1, width), F32)],
        compiler_params=pltpu.CompilerParams(dimension_semantics=("parallel", "arbitrary"),
                                             vmem_limit_bytes=VMEM_LIMIT),
        name="rwkv_chunk",
    )(pr3d, shift0[:, None, :], wkv0, *_rwkv_weight_inputs(wts), e_mat, tri)


_STEP_VECS = 6


def _rwkv_step_kernel(pr_ref, shift0_ref, s_ref, *rest):
    n_w = len(_RWKV_W_NAMES)
    w = dict(zip(_RWKV_W_NAMES, rest[:n_w]))
    e_ref, oa_ref, s1_ref, vec_ref, post_ref, o_ref = rest[n_w:n_w + 6]
    h = pl.program_id(0)

    @pl.when(h == 0)
    def _():
        pr = pr_ref[...]
        xr = pr + (shift0_ref[...] - pr) * w["mu"][...]
        r, kmod, v, kk, bvec, lw, g = _rwkv_token_prep(xr, w, e_ref)
        for i, x in enumerate((kk, bvec, jnp.exp(lw), kmod, r, v)):
            vec_ref[i] = x.T
        post_ref[0] = _rwkv_bonus(r, kmod, v, w, e_ref)
        post_ref[1] = g

    base = pl.multiple_of(h * HEAD, HEAD)
    kk_h, b_h, dec_h, k_h, r_h = [vec_ref[i, pl.ds(base, HEAD), :] for i in range(5)]

    def value_row(vi, carry):
        tile = s_ref[0, vi]
        s_kappa = jnp.sum(tile * kk_h, axis=0, keepdims=True)
        new = tile * dec_h - s_kappa * b_h + vec_ref[5, pl.ds(base + vi, 1), :] * k_h
        s1_ref[0, vi] = new
        o_ref[pl.ds(base + vi, 1), :] = jnp.sum(new * r_h, axis=0, keepdims=True)
        return carry

    lax.fori_loop(0, HEAD, value_row, 0, unroll=4)

    @pl.when(h == pl.num_programs(0) - 1)
    def _():
        oa_ref[...] = _rwkv_post(o_ref[...].T, post_ref[0], post_ref[1], w, e_ref).astype(oa_ref.dtype)


def _rwkv_step(pr2d, shift0, wkv0, wts, e_mat):
    bsz, c_shift = pr2d.shape
    heads = wkv0.shape[1]
    width = heads * HEAD
    full = lambda a: _const_spec(a.shape)
    st = pl.BlockSpec((1, HEAD, HEAD, bsz), lambda hh: (hh, 0, 0, 0))
    oa, s1_t = pl.pallas_call(
        _rwkv_step_kernel,
        grid=(heads,),
        in_specs=[full(pr2d), full(shift0), st] + _rwkv_weight_specs(wts) + [full(e_mat)],
        out_specs=[_const_spec((bsz, width)), st],
        out_shape=[jax.ShapeDtypeStruct((bsz, width), BF16),
                   jax.ShapeDtypeStruct((heads, HEAD, HEAD, bsz), F32)],
        scratch_shapes=[pltpu.VMEM((_STEP_VECS, width, bsz), F32),
                        pltpu.VMEM((2, bsz, width), F32),
                        pltpu.VMEM((width, bsz), F32)],
        compiler_params=pltpu.CompilerParams(dimension_semantics=("arbitrary",), vmem_limit_bytes=VMEM_LIMIT),
        name="rwkv_step",
    )(pr2d, shift0, jnp.transpose(wkv0, (1, 2, 3, 0)), *_rwkv_weight_inputs(wts), e_mat)
    return oa, jnp.transpose(s1_t, (3, 0, 1, 2))


def _s5_disc_kernel(lre_ref, lim_ref, ldt_ref, bre_ref, bim_ref, lbr_ref, lbi_ref, bbr_ref, bbi_ref):
    lam_re, lam_im = lre_ref[...], lim_ref[...]
    dt = jnp.exp(ldt_ref[...])
    mag = jnp.exp(lam_re * dt)
    ang = lam_im * dt
    lb_re, lb_im = mag * jnp.cos(ang), mag * jnp.sin(ang)
    nr, ni = lb_re - 1.0, lb_im
    den = lam_re * lam_re + lam_im * lam_im
    f_re = (nr * lam_re + ni * lam_im) / den
    f_im = (ni * lam_re - nr * lam_im) / den
    b_re, b_im = bre_ref[...], bim_ref[...]
    lbr_ref[...] = lb_re
    lbi_ref[...] = lb_im
    bbr_ref[...] = f_re * b_re - f_im * b_im
    bbi_ref[...] = f_re * b_im + f_im * b_re


def _s5_discretise(lam_re, lam_im, log_dt, b_re_t, b_im_t):
    g, p = lam_re.shape
    full = lambda a: _const_spec(a.shape)
    args = (lam_re[:, None, :], lam_im[:, None, :], log_dt[:, None, None], b_re_t, b_im_t)
    return pl.pallas_call(
        _s5_disc_kernel,
        grid=(1,),
        in_specs=[full(a) for a in args],
        out_specs=[_const_spec((g, 1, p))] * 2 + [full(b_re_t)] * 2,
        out_shape=[jax.ShapeDtypeStruct((g, 1, p), F32)] * 2 + [jax.ShapeDtypeStruct(b_re_t.shape, F32)] * 2,
        name="s5_discretise",
    )(*args)


def _s5_kernel(u_ref, re0_ref, im0_ref, lbr_ref, lbi_ref, wb_ref, wc_ref, d_ref,
               h_ref, re1_ref, im1_ref, u_tm, h_tm, bu0, bu1, xs0, xs1, *, tt, n_slabs):
    t_blk = pl.program_id(1)
    rows = SUBLANES * tt
    s_w = SLAB_GROUPS * S5_STATE
    u_w = SLAB_GROUPS * S5_GROUP
    bu, xs = (bu0, bu1), (xs0, xs1)
    rb = min(rows, S5_ROW_BLOCK)
    steps_per_rb = rb // SUBLANES

    @pl.when(t_blk == 0)
    def _():
        re1_ref[...] = re0_ref[...]
        im1_ref[...] = im0_ref[...]

    u_tm[...] = jnp.swapaxes(u_ref[...], 0, 1).reshape(rows, n_slabs * u_w)

    def project_in(s, j):
        r0 = j * rb
        bu[s % 2][r0:r0 + rb, :] = _mm1(u_tm[r0:r0 + rb, s * u_w:(s + 1) * u_w], wb_ref[s])

    def project_out(s, j):
        r0 = j * rb
        lanes = slice(s * u_w, (s + 1) * u_w)
        y = _mm1(xs[s % 2][r0:r0 + rb, :], wc_ref[s]) + d_ref[:, lanes] * u_tm[r0:r0 + rb, lanes]
        h_tm[r0:r0 + rb, lanes] = _gelu_tanh(y)

    for j in range(rows // rb):
        project_in(0, j)
    for p in range(n_slabs + 1):
        if p < n_slabs:
            st = slice(p * s_w, (p + 1) * s_w)
            lbr = jnp.broadcast_to(lbr_ref[:, st], (SUBLANES, s_w))
            lbi = jnp.broadcast_to(lbi_ref[:, st], (SUBLANES, s_w))
            xr, xi = re1_ref[:, st], im1_ref[:, st]
        for j in range(rows // rb):
            if p < n_slabs:
                for t in range(j * steps_per_rb, (j + 1) * steps_per_rb):
                    at_t = slice(t * SUBLANES, (t + 1) * SUBLANES)
                    xr, xi = (lbr * xr - lbi * xi + bu[p % 2][at_t, :s_w],
                              lbr * xi + lbi * xr + bu[p % 2][at_t, s_w:])
                    xs[p % 2][at_t, :s_w] = xr
                    xs[p % 2][at_t, s_w:] = xi
            if p + 1 < n_slabs:
                project_in(p + 1, j)
            if p >= 1:
                project_out(p - 1, j)
        if p < n_slabs:
            re1_ref[:, st] = xr
            im1_ref[:, st] = xi
    h = h_tm[...].reshape(tt, SUBLANES, n_slabs * u_w)
    h_ref[...] = jnp.swapaxes(h, 0, 1).astype(h_ref.dtype)


def _s5(u_blocks, re0, im0, lb_re, lb_im, wb, wc, d_skip, tt):
    n_slabs = wb.shape[0]
    n_state = re0.shape[1]
    blk = (SUBLANES, tt, u_blocks.shape[-1])
    grid = (u_blocks.shape[0] // SUBLANES, u_blocks.shape[1] // tt)
    rows = SUBLANES * tt
    u_spec = pl.BlockSpec(blk, lambda i, j: (i, j, 0))
    st_spec = pl.BlockSpec((SUBLANES, n_state), lambda i, j: (i, 0))
    full = lambda a: _const_spec(a.shape)
    return pl.pallas_call(
        functools.partial(_s5_kernel, tt=tt, n_slabs=n_slabs),
        grid=grid,
        in_specs=[u_spec, st_spec, st_spec, full(lb_re), full(lb_im), full(wb), full(wc), full(d_skip)],
        out_specs=[u_spec, st_spec, st_spec],
        out_shape=[jax.ShapeDtypeStruct(u_blocks.shape, BF16),
                   jax.ShapeDtypeStruct(re0.shape, F32), jax.ShapeDtypeStruct(im0.shape, F32)],
        scratch_shapes=[pltpu.VMEM((rows, u_blocks.shape[-1]), F32)] * 2
                       + [pltpu.VMEM((rows, 2 * SLAB_GROUPS * S5_STATE), F32)] * 4,
        compiler_params=pltpu.CompilerParams(dimension_semantics=("parallel", "arbitrary"),
                                             vmem_limit_bytes=VMEM_LIMIT),
        name="s5_scan",
    )(u_blocks, re0, im0, lb_re, lb_im, wb, wc, d_skip)


def _s5_step_kernel(u_ref, re0_ref, im0_ref, lbr_ref, lbi_ref, wb_ref, wc_ref, d_ref,
                    h_ref, re1_ref, im1_ref, *, n_slabs):
    s_w = SLAB_GROUPS * S5_STATE
    u_w = SLAB_GROUPS * S5_GROUP
    u = u_ref[...]
    ys = []
    for s in range(n_slabs):
        us = u[:, s * u_w:(s + 1) * u_w]
        bu = _mm1(us, wb_ref[s])
        st = slice(s * s_w, (s + 1) * s_w)
        lbr, lbi = lbr_ref[:, st], lbi_ref[:, st]
        xr, xi = re0_ref[:, st], im0_ref[:, st]
        nr = lbr * xr - lbi * xi + bu[:, :s_w]
        ni = lbr * xi + lbi * xr + bu[:, s_w:]
        re1_ref[:, st] = nr
        im1_ref[:, st] = ni
        y = _mm1(jnp.concatenate([nr, ni], axis=1), wc_ref[s]) + d_ref[:, s * u_w:(s + 1) * u_w] * us
        ys.append(_gelu_tanh(y))
    h_ref[...] = jnp.concatenate(ys, axis=1).astype(h_ref.dtype)


def _s5_step(u2d, re0, im0, lb_re, lb_im, wb, wc, d_skip):
    full = lambda a: _const_spec(a.shape)
    args = (u2d, re0, im0, lb_re, lb_im, wb, wc, d_skip)
    return pl.pallas_call(
        functools.partial(_s5_step_kernel, n_slabs=wb.shape[0]),
        grid=(1,),
        in_specs=[full(a) for a in args],
        out_specs=[full(u2d), full(re0), full(im0)],
        out_shape=[jax.ShapeDtypeStruct(u2d.shape, BF16),
                   jax.ShapeDtypeStruct(re0.shape, F32), jax.ShapeDtypeStruct(im0.shape, F32)],
        compiler_params=pltpu.CompilerParams(vmem_limit_bytes=VMEM_LIMIT),
        name="s5_step",
    )(*args)


def _block_diag_slabs(m):
    g, a, b = m.shape
    eye = jnp.eye(SLAB_GROUPS, dtype=m.dtype)
    m4 = m.reshape(g // SLAB_GROUPS, SLAB_GROUPS, a, b)
    return jnp.einsum("sgab,gh->sgahb", m4, eye).reshape(g // SLAB_GROUPS, SLAB_GROUPS * a, SLAB_GROUPS * b)


def _tail_kernel(x_ref, oa_ref, hg_ref, gt_ref, wro_ref, w1_ref, b1_ref, w2_ref, b2_ref, wmo_ref,
                 npm_ref, nf_ref, npf_ref, wg_ref, wu_ref, wd_ref, y_ref):
    d = x_ref.shape[-1]
    tm = x_ref.shape[0]
    n_sub = 2 if tm % 32 == 0 else 1
    subs = [slice(i * (tm // n_sub), (i + 1) * (tm // n_sub)) for i in range(n_sub)]
    hg = [hg_ref[s, :] for s in subs]
    a_out = [_dot(oa_ref[s, :], wro_ref[...]) for s in subs]
    b_lin = [_dot(h, w1_ref[...]) + b1_ref[...] for h in hg]
    b_gate = [_dot(h, w2_ref[...]) + b2_ref[...] for h in hg]
    merged = [(_sigmoid(gt_ref[s, :d]) * a + _sigmoid(gt_ref[s, d:]) * (bl * _sigmoid(bg))).astype(BF16)
              for s, a, bl, bg in zip(subs, a_out, b_lin, b_gate)]
    mix = [_dot(m, wmo_ref[...]) for m in merged]
    x1 = [x_ref[s, :] + _rms(m, npm_ref[...]) for s, m in zip(subs, mix)]
    hb = [_rms(x, nf_ref[...]).astype(BF16) for x in x1]
    gate = [_dot(h, wg_ref[...]) for h in hb]
    up = [_dot(h, wu_ref[...]) for h in hb]
    act = [(g * _sigmoid(g) * u).astype(BF16) for g, u in zip(gate, up)]
    f = [_dot(a, wd_ref[...]) for a in act]
    for s, x, ff in zip(subs, x1, f):
        y_ref[s, :] = x + _rms(ff, npf_ref[...])


def _tail(x2d, oa, hg, gates, tw, tm):
    rows, d = x2d.shape
    row = lambda a: pl.BlockSpec((tm, a.shape[1]), lambda i: (i, 0))
    wnames = ("wro", "w1", "b1", "w2", "b2", "wmo", "npm", "nf", "npf", "wg", "wu", "wd")
    wargs = [tw[n] for n in wnames]
    return pl.pallas_call(
        _tail_kernel,
        grid=(rows // tm,),
        in_specs=[row(x2d), row(oa), row(hg), row(gates)] + [_const_spec(a.shape, True) for a in wargs],
        out_specs=pl.BlockSpec((tm, d), lambda i: (i, 0)),
        out_shape=jax.ShapeDtypeStruct((rows, d), F32),
        compiler_params=pltpu.CompilerParams(dimension_semantics=("parallel",),
                                             vmem_limit_bytes=VMEM_LIMIT),
        name="tail",
    )(x2d, oa, hg, gates, *wargs)


def _pick_tile(rows, target):
    t = min(rows, target)
    assert rows % t == 0
    return t


def _layer(x, shift0, wkv0, re0, im0, lw, *, chunk, n_seq, s5_tt, row_tile):
    bsz, t, d = x.shape
    rows = bsz * t
    c_shift = shift0.shape[-1]
    c_u = lw["d_skip"].shape[-1]
    tm = _pick_tile(rows, row_tile)
    x2d = x.reshape(rows, d)
    pr, u, gates = _proj(x2d, lw["norm_pre_mix"], lw["w_in"], c_shift, c_u, tm)

    n_state = re0.shape[1] * re0.shape[2]
    s5_args = (re0.reshape(bsz, n_state), im0.reshape(bsz, n_state),
               lw["lb_re"], lw["lb_im"], lw["wb"], lw["wc"], lw["d_skip"])
    if t == 1:
        oa, wkv1 = _rwkv_step(pr, shift0, wkv0, lw["rwkv"], lw["e_mat"])
        pr_last = pr
        hg, re1, im1 = _s5_step(u, *s5_args)
    else:
        oa, wkv1 = _rwkv_chunked(pr.reshape(bsz, t, c_shift), shift0, wkv0, lw["rwkv"], lw["e_mat"], chunk, n_seq)
        oa = oa.reshape(rows, -1)
        pr_last = pr.reshape(bsz, t, c_shift)[:, -1]
        hg, re1, im1 = _s5(u.reshape(bsz, t, c_u), *s5_args, tt=s5_tt)
    y = _tail(x2d, oa, hg.reshape(rows, c_u), gates, lw["tail"], tm)
    return y.reshape(x.shape), pr_last, wkv1, re1.reshape(re0.shape), im1.reshape(im0.shape)


def _prepare_layer_weights(l, p):
    row = lambda a: a[l][None, :].astype(F32)
    width = p["w0"].shape[-1]
    n_dec, n_aaa, n_gate = p["w_decay_up"].shape[1], p["w_aaa_up"].shape[1], p["w_gate_up"].shape[1]
    assert n_dec + n_aaa + n_gate == LORA_PAD

    def lora_pad(wup, start):
        return jnp.zeros((LORA_PAD, width), F32).at[start:start + wup.shape[0]].set(wup).astype(BF16)

    rwkv = {
        "mu": row(p["mu_shift"]), "w0": row(p["w0"]), "a0": row(p["a0"]), "k_k": row(p["k_k"]),
        "k_a": row(p["k_a"]), "r_k": row(p["r_k"]), "lnx_g": row(p["lnx_g"]), "lnx_b": row(p["lnx_b"]),
        "wd": lora_pad(p["w_decay_up"][l], 0),
        "wa": lora_pad(p["w_aaa_up"][l], n_dec),
        "wg": lora_pad(p["w_gate_up"][l], n_dec + n_aaa),
    }
    head_id = jnp.arange(MXU_DIM) // HEAD
    e_mat = (head_id[:, None] == head_id[None, :]).astype(BF16)

    lb_re, lb_im, bb_re_t, bb_im_t = _s5_discretise(
        p["s5_lam_re"][l], p["s5_lam_im"][l], p["s5_log_dt"][l],
        jnp.swapaxes(p["s5_b_re"][l], 1, 2), jnp.swapaxes(p["s5_b_im"][l], 1, 2))
    n_state = lb_re.shape[0] * lb_re.shape[2]
    to_out = lambda cc: _block_diag_slabs(jnp.swapaxes(cc, 1, 2))
    wb = jnp.concatenate([_block_diag_slabs(bb_re_t), _block_diag_slabs(bb_im_t)], axis=-1).astype(BF16)
    wc = jnp.concatenate([to_out(p["s5_c_re"][l]), -to_out(p["s5_c_im"][l])], axis=1).astype(BF16)

    bf = lambda a: a[l].astype(BF16)
    tail = {
        "wro": bf(p["w_rwkv_out"]), "w1": bf(p["glu_w1"]), "b1": row(p["glu_b1"]), "w2": bf(p["glu_w2"]),
        "b2": row(p["glu_b2"]), "wmo": bf(p["w_merge_out"]), "npm": row(p["norm_post_mix"]),
        "nf": row(p["norm_pre_ffn"]), "npf": row(p["norm_post_ffn"]),
        "wg": bf(p["w_ffn_gate"]), "wu": bf(p["w_ffn_up"]), "wd": bf(p["w_ffn_down"]),
    }
    return {
        "norm_pre_mix": row(p["norm_pre_mix"]), "w_in": bf(p["w_in"]), "rwkv": rwkv, "e_mat": e_mat,
        "lb_re": lb_re.reshape(1, n_state), "lb_im": lb_im.reshape(1, n_state), "wb": wb, "wc": wc,
        "d_skip": row(p["s5_d"]), "tail": tail,
    }


_PARAM_NAMES = ("norm_pre_mix", "norm_post_mix", "norm_pre_ffn", "norm_post_ffn", "w_in", "mu_shift",
                "w0", "w_decay_up", "a0", "w_aaa_up", "w_gate_up", "k_k", "k_a", "r_k", "lnx_g", "lnx_b",
                "w_rwkv_out", "s5_lam_re", "s5_lam_im", "s5_log_dt", "s5_b_re", "s5_b_im", "s5_c_re",
                "s5_c_im", "s5_d", "glu_w1", "glu_b1", "glu_w2", "glu_b2", "w_merge_out",
                "w_ffn_gate", "w_ffn_up", "w_ffn_down")


def _forward(x_prompt, x_sample, state_shift, state_wkv, state_s5_re, state_s5_im, params,
             *, chunk=64, n_seq=4, s5_tt=128, row_tile=256):
    depth = params["w_in"].shape[0]
    heads = state_wkv.shape[2]
    bp = x_prompt.shape[0]
    yp, ys = x_prompt, x_sample
    outs_p, outs_s = [], []
    for l in range(depth):
        lw = _prepare_layer_weights(l, params)
        zp_shift = jnp.zeros((bp, state_shift.shape[-1]), F32)
        zp_wkv = jnp.zeros((bp, heads, HEAD, HEAD), F32)
        zp_s5 = jnp.zeros((bp,) + state_s5_re.shape[2:], F32)
        yp, *st_p = _layer(yp, zp_shift, zp_wkv, zp_s5, zp_s5, lw, chunk=chunk, n_seq=n_seq, s5_tt=s5_tt, row_tile=row_tile)
        ys, *st_s = _layer(ys, state_shift[l], state_wkv[l], state_s5_re[l], state_s5_im[l], lw,
                           chunk=chunk, n_seq=n_seq, s5_tt=s5_tt, row_tile=row_tile)
        outs_p.append(st_p)
        outs_s.append(st_s)
    stack = lambda outs, i, dt: jnp.stack([o[i] for o in outs]).astype(dt)
    dt_p, dt_s = x_prompt.dtype, x_sample.dtype
    return (yp, ys,
            stack(outs_p, 0, dt_p), stack(outs_p, 1, dt_p), stack(outs_p, 2, dt_p), stack(outs_p, 3, dt_p),
            stack(outs_s, 0, dt_s), stack(outs_s, 1, dt_s), stack(outs_s, 2, dt_s), stack(outs_s, 3, dt_s))


def kernel(x_prompt, x_sample, state_shift, state_wkv, state_s5_re, state_s5_im, norm_pre_mix, norm_post_mix, norm_pre_ffn, norm_post_ffn, w_in, mu_shift, w0, w_decay_up, a0, w_aaa_up, w_gate_up, k_k, k_a, r_k, lnx_g, lnx_b, w_rwkv_out, s5_lam_re, s5_lam_im, s5_log_dt, s5_b_re, s5_b_im, s5_c_re, s5_c_im, s5_d, glu_w1, glu_b1, glu_w2, glu_b2, w_merge_out, w_ffn_gate, w_ffn_up, w_ffn_down):
    params = dict(zip(_PARAM_NAMES, (norm_pre_mix, norm_post_mix, norm_pre_ffn, norm_post_ffn, w_in, mu_shift,
                                     w0, w_decay_up, a0, w_aaa_up, w_gate_up, k_k, k_a, r_k, lnx_g, lnx_b,
                                     w_rwkv_out, s5_lam_re, s5_lam_im, s5_log_dt, s5_b_re, s5_b_im, s5_c_re,
                                     s5_c_im, s5_d, glu_w1, glu_b1, glu_w2, glu_b2, w_merge_out,
                                     w_ffn_gate, w_ffn_up, w_ffn_down)))
    return _forward(x_prompt, x_sample, state_shift, state_wkv, state_s5_re, state_s5_im, params)
```

```python
import functools
import math

import jax
import jax.numpy as jnp
from jax import lax
from jax.experimental import pallas as pl
from jax.experimental.pallas import tpu as pltpu

F32 = jnp.float32
BF16 = jnp.bfloat16

NORM_EPS = 1e-6
LNX_EPS = 64e-5
HEAD = 64
GROUP_HEADS = 4
GROUP_W = GROUP_HEADS * HEAD
LORA_PAD = 128
S5_GROUP = 16
S5_STATE = 64
SLAB_GROUPS = 8
SUBLANES = 8
LANES = 128
MXU_DIM = 256
S5_ROW_BLOCK = 256
VMEM_LIMIT = 56 * 1024 * 1024

NN = (((1,), (0,)), ((), ()))
NT = (((1,), (1,)), ((), ()))


def _dot(a, b, dims=NN):
    return lax.dot_general(a, b, dims, preferred_element_type=F32)


def _split2(x):
    hi = x.astype(BF16)
    lo = (x - hi.astype(F32)).astype(BF16)
    return hi, lo


def _split3(x):
    hi = x.astype(BF16)
    r1 = x - hi.astype(F32)
    mid = r1.astype(BF16)
    lo = (r1 - mid.astype(F32)).astype(BF16)
    return hi, mid, lo


def _mm1(a, b, dims=NN):
    return _dot(a.astype(BF16), b.astype(BF16), dims)


def _mm_exact_lhs(a_bf16, b):
    h, m, l = _split3(b)
    return _dot(a_bf16, h) + (_dot(a_bf16, m) + _dot(a_bf16, l))


def _rms(x, g):
    return x * lax.rsqrt(jnp.mean(x * x, axis=-1, keepdims=True) + NORM_EPS) * g


def _sigmoid(x):
    return 1.0 / (1.0 + jnp.exp(-x))


def _gelu_tanh(x):
    c = math.sqrt(2.0 / math.pi)
    return 0.5 * x * (1.0 + jnp.tanh(c * (x + 0.044715 * (x * x * x))))


def _const_spec(shape, single_buffer=False):
    idx = lambda *_: (0,) * len(shape)
    if single_buffer:
        return pl.BlockSpec(shape, idx, pipeline_mode=pl.Buffered(1))
    return pl.BlockSpec(shape, idx)


def _two_group_grid(rows_main, tm):
    n_main = rows_main // tm
    main_spec = lambda width: pl.BlockSpec((tm, width), lambda i: (jnp.minimum(i, n_main - 1), 0))
    return n_main, main_spec


def _on_group(body, main_refs, side_refs):
    i, n_main = pl.program_id(0), pl.num_programs(0) - 1
    pl.when(i < n_main)(functools.partial(body, *main_refs))
    pl.when(i == n_main)(functools.partial(body, *side_refs))


def _proj_kernel(x_ref, xs_ref, g_ref, w_ref, pr_ref, u_ref, gt_ref, prs_ref, us_ref, gts_ref, *, c_shift, c_u):
    def project(x_ref, pr_ref, u_ref, gt_ref):
        hb = _rms(x_ref[...], g_ref[...]).astype(BF16)
        pr_ref[...] = _dot(hb, w_ref[:, :c_shift])
        u_ref[...] = _dot(hb, w_ref[:, c_shift:c_shift + c_u])
        gt_ref[...] = _dot(hb, w_ref[:, c_shift + c_u:])

    _on_group(project, (x_ref, pr_ref, u_ref, gt_ref), (xs_ref, prs_ref, us_ref, gts_ref))


def _proj(x_main, x_side, g, w_in_bf16, c_shift, c_u, tm):
    d = x_main.shape[1]
    cols = w_in_bf16.shape[1]
    widths = (c_shift, c_u, cols - c_shift - c_u)
    n_main, main_spec = _two_group_grid(x_main.shape[0], tm)
    side_spec = lambda width: _const_spec((x_side.shape[0], width))
    outs = pl.pallas_call(
        functools.partial(_proj_kernel, c_shift=c_shift, c_u=c_u),
        grid=(n_main + 1,),
        in_specs=[main_spec(d), side_spec(d), _const_spec((1, d)), _const_spec((d, cols), True)],
        out_specs=[main_spec(wd) for wd in widths] + [side_spec(wd) for wd in widths],
        out_shape=[jax.ShapeDtypeStruct((x.shape[0], wd), F32) for x in (x_main, x_side) for wd in widths],
        compiler_params=pltpu.CompilerParams(dimension_semantics=("arbitrary",),
                                             vmem_limit_bytes=VMEM_LIMIT),
        name="proj",
    )(x_main, x_side, g, w_in_bf16)
    return outs[:3], outs[3:]


def _head_sum(x, e_ref):
    rows, width = x.shape
    gw = e_ref.shape[0]
    n_lg = width // gw
    hi, lo = _split2(x)
    stacked = jnp.concatenate([part[:, j * gw:(j + 1) * gw] for part in (hi, lo) for j in range(n_lg)], axis=0)
    sums = _dot(stacked, e_ref[...])
    return jnp.concatenate([sums[j * rows:(j + 1) * rows] + sums[(n_lg + j) * rows:(n_lg + j + 1) * rows]
                            for j in range(n_lg)], axis=1)


def _rwkv_token_prep(xr, w, e_ref):
    width = w["w0"].shape[-1]
    r = xr[:, :width]
    k = xr[:, width:2 * width]
    v = xr[:, 2 * width:3 * width]
    lo = xr[:, 3 * width:3 * width + LORA_PAD]
    wl = w["w0"][...] + _mm1(jnp.tanh(lo), w["wd"][...])
    lw = -math.exp(-0.5) * _sigmoid(wl)
    a = _sigmoid(w["a0"][...] + _mm1(lo, w["wa"][...]))
    g = _mm1(_sigmoid(lo), w["wg"][...])
    kk = k * w["k_k"][...]
    kk = kk * lax.rsqrt(jnp.maximum(_head_sum(kk * kk, e_ref), 1e-24))
    kmod = k * (1.0 + (a - 1.0) * w["k_a"][...])
    return r, kmod, v, kk, kk * a, lw, g


def _rwkv_bonus(r, kmod, v, w, e_ref):
    return _head_sum(r * kmod * w["r_k"][...], e_ref) * v


def _rwkv_post(o, bonus, g, w, e_ref):
    inv_n = 1.0 / HEAD
    mu = _head_sum(o, e_ref) * inv_n
    oc = o - mu
    var = _head_sum(oc * oc, e_ref) * inv_n
    on = oc * lax.rsqrt(var + LNX_EPS) * w["lnx_g"][...] + w["lnx_b"][...]
    return (on + bonus) * g


_RWKV_W_NAMES = ("mu", "w0", "a0", "k_k", "k_a", "r_k", "lnx_g", "lnx_b", "wd", "wa", "wg")


def _rwkv_chunk_kernel(pr_ref, shift0_ref, s0_ref, *rest, chunk, width, n_seq):
    n_w = len(_RWKV_W_NAMES)
    w = dict(zip(_RWKV_W_NAMES, rest[:n_w]))
    e_ref, tri_ref = rest[n_w], rest[n_w + 1]
    oa_ref, s1_ref = rest[n_w + 2], rest[n_w + 3]
    carry_ref, z_ref, ops_ref, aux_ref, gall_ref = rest[n_w + 4:n_w + 9]
    step = pl.program_id(1)
    n_chunks = pl.num_programs(1) - 1
    n_groups = width // GROUP_W
    C = chunk
    GC = GROUP_HEADS * C

    def lane_block_masks(n_lanes, block):
        lane_block = lax.broadcasted_iota(jnp.int32, (1, n_lanes), 1) // block
        return [jnp.where(lane_block == h, 1.0, 0.0).astype(BF16) for h in range(GROUP_HEADS)]

    vec_masks = lane_block_masks(GROUP_W, HEAD)
    mat_masks = lane_block_masks(GC, C)

    def stack(x, masks):
        xb = x.astype(BF16)
        return jnp.concatenate([xb * m for m in masks], axis=0)

    @pl.when(step == 0)
    def _():
        carry_ref[...] = shift0_ref[...]
        ops_ref[1] = jnp.zeros(ops_ref.shape[1:], ops_ref.dtype)
        aux_ref[1] = jnp.zeros(aux_ref.shape[1:], aux_ref.dtype)
        gall_ref[1] = jnp.zeros(gall_ref.shape[1:], gall_ref.dtype)
        z_ref[...] = jnp.zeros(z_ref.shape, z_ref.dtype)

    @pl.when(step == 1)
    def _():
        for q in range(n_seq):
            for gi in range(n_groups):
                z_ref[q, gi] = jnp.zeros((GROUP_W, GROUP_W), F32)
                for h in range(GROUP_HEADS):
                    z_ref[q, gi, h * HEAD:(h + 1) * HEAD, h * HEAD:(h + 1) * HEAD] = s0_ref[q, gi * GROUP_HEADS + h]

    tok = lax.broadcasted_iota(jnp.int32, (C, GC), 0)
    col = lax.broadcasted_iota(jnp.int32, (C, GC), 1) % C
    strict = tok > col
    incl = tok >= col
    eye = jnp.where(tok == col, 1.0, 0.0).astype(F32)
    zi = lax.broadcasted_iota(jnp.int32, (GROUP_W, GROUP_W), 0) // HEAD
    zj = lax.broadcasted_iota(jnp.int32, (GROUP_W, GROUP_W), 1) // HEAD
    block_diag = zi == zj
    rows2 = lambda top, bottom: jnp.concatenate([top, bottom], axis=0).astype(BF16)

    def recurrence(r_slot):
        chains = [(q, gi) for q in range(n_seq) for gi in range(n_groups)]
        op = lambda i, q, gi: ops_ref[r_slot, i, q * C:(q + 1) * C, gi * GROUP_W:(gi + 1) * GROUP_W]
        kap_t, r_t, v = ([op(i, q, gi) for q, gi in chains] for i in (0, 1, 4))
        kap_s = [stack(x, vec_masks) for x in kap_t]
        v_s = [stack(x, vec_masks) for x in v]
        grams = []
        for i, (q, gi) in enumerate(chains):
            right = jnp.concatenate([stack(op(2, q, gi), vec_masks), stack(op(3, q, gi), vec_masks)], axis=0)
            grams.append(_dot(rows2(kap_t[i], r_t[i]), right, NT))
        yield
        a_k = [jnp.where(strict, gm[:C, :GC], 0.0) for gm in grams]
        a_b = [jnp.where(strict, gm[:C, GC:], 0.0) for gm in grams]
        a_rk = [jnp.where(incl, gm[C:, :GC], 0.0) for gm in grams]
        a_rb = [jnp.where(incl, gm[C:, GC:], 0.0) for gm in grams]

        ps = [-a for a in a_b]
        ts = [eye + p for p in ps]
        ps = [_dot(p.astype(BF16), stack(p, mat_masks)) for p in ps]
        yield
        covered = 2
        while covered < C:
            powers = [stack(p, mat_masks) for p in ps]
            if 2 * covered < C:
                both = [_dot(rows2(t, p), pw) for t, p, pw in zip(ts, ps, powers)]
                ts = [t + x[:C] for t, x in zip(ts, both)]
                ps = [x[C:] for x in both]
            else:
                ts = [t + _dot(t.astype(BF16), pw) for t, pw in zip(ts, powers)]
            covered *= 2
            yield

        av = [_dot(rows2(a, ar), vs) for a, ar, vs in zip(a_k, a_rk, v_s)]
        yield
        tw = [_dot(t.astype(BF16), jnp.concatenate([ks, stack(x[:C], vec_masks)], axis=1))
              for t, ks, x in zip(ts, kap_s, av)]
        yield
        zs = [z_ref[q, gi] for q, gi in chains]
        pz = [_dot(rows2(x[:, :GROUP_W], rt), z.astype(BF16), NT)
              for x, rt, z in zip(tw, r_t, zs)]
        us = [p[:C] + x[:, GROUP_W:] for p, x in zip(pz, tw)]
        yield
        arbu = [_dot(a.astype(BF16), stack(u, vec_masks)) for a, u in zip(a_rb, us)]
        o_blk = [p[C:] + x[C:] - y for p, x, y in zip(pz, av, arbu)]
        for i, (q, gi) in enumerate(chains):
            vu_t = jnp.concatenate([v[i].astype(F32), us[i]], axis=0).T.astype(BF16)
            kb = jnp.concatenate([op(5, q, gi), op(6, q, gi)], axis=0)
            g_all = gall_ref[r_slot, q][:, gi * GROUP_W:(gi + 1) * GROUP_W]
            z_ref[q, gi] = zs[i] * g_all + jnp.where(block_diag, _dot(vu_t, kb), 0.0)
        yield
        o = jnp.concatenate([jnp.concatenate(o_blk[q * n_groups:(q + 1) * n_groups], axis=1)
                             for q in range(n_seq)], axis=0)
        oa_ref[...] = _rwkv_post(o, aux_ref[r_slot, 0], aux_ref[r_slot, 1], w, e_ref
                                 ).astype(oa_ref.dtype).reshape(oa_ref.shape)

    def prepare(q, w_slot):
        rows = slice(q * C, (q + 1) * C)
        pr = pr_ref[q]
        row_id = lax.broadcasted_iota(jnp.int32, pr.shape, 0)
        pr_prev = jnp.where(row_id == 0, carry_ref[q], pltpu.roll(pr, 1, axis=0))
        carry_ref[q] = pr[C - 1:C, :]
        xr = pr + (pr_prev - pr) * w["mu"][...]
        r, kmod, vv, kk, bvec, lw, g = _rwkv_token_prep(xr, w, e_ref)
        aux_ref[w_slot, 0, rows] = _rwkv_bonus(r, kmod, vv, w, e_ref)
        aux_ref[w_slot, 1, rows] = g
        yield
        cum = _mm_exact_lhs(tri_ref[...], lw)
        cum_last = cum[C - 1:C, :]
        g_neg = jnp.exp(-cum)
        g_end = jnp.exp(cum_last - cum)
        prepared = (kk * jnp.exp(cum - lw),
                    r * jnp.exp(cum),
                    kmod * g_neg, bvec * g_neg,
                    vv, kmod * g_end, -(bvec * g_end))
        for i, x in enumerate(prepared):
            ops_ref[w_slot, i, rows] = x.astype(ops_ref.dtype)
        gall_ref[w_slot, q] = jnp.exp(cum_last)

    def run(r_slot, w_slot):
        pending = [gen for gen in [prepare(q, w_slot) for q in range(n_seq)] for _ in range(2)]
        for _ in recurrence(r_slot):
            if pending:
                next(pending.pop(0), None)
        for gen in pending:
            next(gen, None)

    for parity in (0, 1):
        pl.when(step % 2 == parity)(functools.partial(run, r_slot=1 - parity, w_slot=parity))

    @pl.when(step == n_chunks)
    def _():
        for q in range(n_seq):
            for gi in range(n_groups):
                for h in range(GROUP_HEADS):
                    s1_ref[q, gi * GROUP_HEADS + h] = z_ref[q, gi, h * HEAD:(h + 1) * HEAD, h * HEAD:(h + 1) * HEAD]


_N_OPS = 7


def _rwkv_weight_inputs(wts):
    return [wts[n] for n in _RWKV_W_NAMES]


def _rwkv_weight_specs(wts):
    return [_const_spec(wts[n].shape) for n in _RWKV_W_NAMES]


def _rwkv_chunked(pr3d, shift0, wkv0, wts, e_mat, chunk, n_seq):
    bsz, t, c_shift = pr3d.shape
    heads = wkv0.shape[1]
    width = heads * HEAD
    tri = jnp.tril(jnp.ones((chunk, chunk), F32)).astype(BF16)
    kern = functools.partial(_rwkv_chunk_kernel, chunk=chunk, width=width, n_seq=n_seq)
    n_chunks = t // chunk
    rows = n_seq * chunk
    return pl.pallas_call(
        kern,
        grid=(bsz // n_seq, n_chunks + 1),
        in_specs=[pl.BlockSpec((n_seq, chunk, c_shift), lambda b, s: (b, jnp.minimum(s, n_chunks - 1), 0)),
                  pl.BlockSpec((n_seq, 1, c_shift), lambda b, s: (b, 0, 0)),
                  pl.BlockSpec((n_seq, heads, HEAD, HEAD), lambda b, s: (b, 0, 0, 0))]
                 + _rwkv_weight_specs(wts)
                 + [_const_spec(e_mat.shape), _const_spec(tri.shape)],
        out_specs=[pl.BlockSpec((n_seq, chunk, width), lambda b, s: (b, jnp.maximum(s - 1, 0), 0)),
                   pl.BlockSpec((n_seq, heads, HEAD, HEAD), lambda b, s: (b, 0, 0, 0))],
        out_shape=[jax.ShapeDtypeStruct((bsz, t, width), BF16),
                   jax.ShapeDtypeStruct((bsz, heads, HEAD, HEAD), F32)],
        scratch_shapes=[pltpu.VMEM((n_seq, 1, c_shift), F32),
                        pltpu.VMEM((n_seq, width // GROUP_W, GROUP_W, GROUP_W), F32),
                        pltpu.VMEM((2, _N_OPS, rows, width), BF16),
                        pltpu.VMEM((2, 2, rows, width), F32),
                        pltpu.VMEM((2, n_seq, 1, width), F32)],
        compiler_params=pltpu.CompilerParams(dimension_semantics=("parallel", "arbitrary"),
                                             vmem_limit_bytes=VMEM_LIMIT),
        name="rwkv_chunk",
    )(pr3d, shift0[:, None, :], wkv0, *_rwkv_weight_inputs(wts), e_mat, tri)


_STEP_VECS = 6


def _rwkv_step_kernel(pr_ref, shift0_ref, s_ref, *rest):
    n_w = len(_RWKV_W_NAMES)
    w = dict(zip(_RWKV_W_NAMES, rest[:n_w]))
    e_ref, oa_ref, s1_ref, vec_ref, post_ref, o_ref = rest[n_w:n_w + 6]
    h = pl.program_id(0)

    @pl.when(h == 0)
    def _():
        pr = pr_ref[...]
        xr = pr + (shift0_ref[...] - pr) * w["mu"][...]
        r, kmod, v, kk, bvec, lw, g = _rwkv_token_prep(xr, w, e_ref)
        for i, x in enumerate((kk, bvec, jnp.exp(lw), kmod, r, v)):
            vec_ref[i] = x.T
        post_ref[0] = _rwkv_bonus(r, kmod, v, w, e_ref)
        post_ref[1] = g

    base = pl.multiple_of(h * HEAD, HEAD)
    kk_h, b_h, dec_h, k_h, r_h = [vec_ref[i, pl.ds(base, HEAD), :] for i in range(5)]

    def value_row(vi, carry):
        tile = s_ref[0, vi]
        s_kappa = jnp.sum(tile * kk_h, axis=0, keepdims=True)
        new = tile * dec_h - s_kappa * b_h + vec_ref[5, pl.ds(base + vi, 1), :] * k_h
        s1_ref[0, vi] = new
        o_ref[pl.ds(base + vi, 1), :] = jnp.sum(new * r_h, axis=0, keepdims=True)
        return carry

    lax.fori_loop(0, HEAD, value_row, 0, unroll=4)

    @pl.when(h == pl.num_programs(0) - 1)
    def _():
        oa_ref[...] = _rwkv_post(o_ref[...].T, post_ref[0], post_ref[1], w, e_ref).astype(oa_ref.dtype)


def _rwkv_step(pr2d, shift0, wkv0, wts, e_mat):
    bsz, c_shift = pr2d.shape
    heads = wkv0.shape[1]
    width = heads * HEAD
    full = lambda a: _const_spec(a.shape)
    st = pl.BlockSpec((1, HEAD, HEAD, bsz), lambda hh: (hh, 0, 0, 0))
    oa, s1_t = pl.pallas_call(
        _rwkv_step_kernel,
        grid=(heads,),
        in_specs=[full(pr2d), full(shift0), st] + _rwkv_weight_specs(wts) + [full(e_mat)],
        out_specs=[_const_spec((bsz, width)), st],
        out_shape=[jax.ShapeDtypeStruct((bsz, width), BF16),
                   jax.ShapeDtypeStruct((heads, HEAD, HEAD, bsz), F32)],
        scratch_shapes=[pltpu.VMEM((_STEP_VECS, width, bsz), F32),
                        pltpu.VMEM((2, bsz, width), F32),
                        pltpu.VMEM((width, bsz), F32)],
        compiler_params=pltpu.CompilerParams(dimension_semantics=("arbitrary",), vmem_limit_bytes=VMEM_LIMIT),
        name="rwkv_step",
    )(pr2d, shift0, jnp.transpose(wkv0, (1, 2, 3, 0)), *_rwkv_weight_inputs(wts), e_mat)
    return oa, jnp.transpose(s1_t, (3, 0, 1, 2))


def _s5_disc_kernel(lre_ref, lim_ref, ldt_ref, bre_ref, bim_ref, lbr_ref, lbi_ref, bbr_ref, bbi_ref):
    lam_re, lam_im = lre_ref[...], lim_ref[...]
    dt = jnp.exp(ldt_ref[...])
    mag = jnp.exp(lam_re * dt)
    ang = lam_im * dt
    lb_re, lb_im = mag * jnp.cos(ang), mag * jnp.sin(ang)
    nr, ni = lb_re - 1.0, lb_im
    den = lam_re * lam_re + lam_im * lam_im
    f_re = (nr * lam_re + ni * lam_im) / den
    f_im = (ni * lam_re - nr * lam_im) / den
    b_re, b_im = bre_ref[...], bim_ref[...]
    lbr_ref[...] = lb_re
    lbi_ref[...] = lb_im
    bbr_ref[...] = f_re * b_re - f_im * b_im
    bbi_ref[...] = f_re * b_im + f_im * b_re


def _s5_discretise(lam_re, lam_im, log_dt, b_re_t, b_im_t):
    g, p = lam_re.shape
    full = lambda a: _const_spec(a.shape)
    args = (lam_re[:, None, :], lam_im[:, None, :], log_dt[:, None, None], b_re_t, b_im_t)
    return pl.pallas_call(
        _s5_disc_kernel,
        grid=(1,),
        in_specs=[full(a) for a in args],
        out_specs=[_const_spec((g, 1, p))] * 2 + [full(b_re_t)] * 2,
        out_shape=[jax.ShapeDtypeStruct((g, 1, p), F32)] * 2 + [jax.ShapeDtypeStruct(b_re_t.shape, F32)] * 2,
        name="s5_discretise",
    )(*args)


def _s5_kernel(u_ref, re0_ref, im0_ref, lbr_ref, lbi_ref, wb_ref, wc_ref, d_ref,
               h_ref, re1_ref, im1_ref, u_tm, h_tm, bu0, bu1, xs0, xs1, *, tt, n_slabs):
    t_blk = pl.program_id(1)
    rows = SUBLANES * tt
    s_w = SLAB_GROUPS * S5_STATE
    u_w = SLAB_GROUPS * S5_GROUP
    bu, xs = (bu0, bu1), (xs0, xs1)
    rb = min(rows, S5_ROW_BLOCK)
    steps_per_rb = rb // SUBLANES

    @pl.when(t_blk == 0)
    def _():
        re1_ref[...] = re0_ref[...]
        im1_ref[...] = im0_ref[...]

    u_tm[...] = jnp.swapaxes(u_ref[...], 0, 1).reshape(rows, n_slabs * u_w)

    def project_in(s, j):
        r0 = j * rb
        bu[s % 2][r0:r0 + rb, :] = _mm1(u_tm[r0:r0 + rb, s * u_w:(s + 1) * u_w], wb_ref[s])

    def project_out(s, j):
        r0 = j * rb
        lanes = slice(s * u_w, (s + 1) * u_w)
        y = _mm1(xs[s % 2][r0:r0 + rb, :], wc_ref[s]) + d_ref[:, lanes] * u_tm[r0:r0 + rb, lanes]
        h_tm[r0:r0 + rb, lanes] = _gelu_tanh(y)

    for j in range(rows // rb):
        project_in(0, j)
    for p in range(n_slabs + 1):
        if p < n_slabs:
            st = slice(p * s_w, (p + 1) * s_w)
            lbr = jnp.broadcast_to(lbr_ref[:, st], (SUBLANES, s_w))
            lbi = jnp.broadcast_to(lbi_ref[:, st], (SUBLANES, s_w))
            xr, xi = re1_ref[:, st], im1_ref[:, st]
        for j in range(rows // rb):
            if p < n_slabs:
                for t in range(j * steps_per_rb, (j + 1) * steps_per_rb):
                    at_t = slice(t * SUBLANES, (t + 1) * SUBLANES)
                    xr, xi = (lbr * xr - lbi * xi + bu[p % 2][at_t, :s_w],
                              lbr * xi + lbi * xr + bu[p % 2][at_t, s_w:])
                    xs[p % 2][at_t, :s_w] = xr
                    xs[p % 2][at_t, s_w:] = xi
            if p + 1 < n_slabs:
                project_in(p + 1, j)
            if p >= 1:
                project_out(p - 1, j)
        if p < n_slabs:
            re1_ref[:, st] = xr
            im1_ref[:, st] = xi
    h = h_tm[...].reshape(tt, SUBLANES, n_slabs * u_w)
    h_ref[...] = jnp.swapaxes(h, 0, 1).astype(h_ref.dtype)


def _s5(u_blocks, re0, im0, lb_re, lb_im, wb, wc, d_skip, tt):
    n_slabs = wb.shape[0]
    n_state = re0.shape[1]
    blk = (SUBLANES, tt, u_blocks.shape[-1])
    grid = (u_blocks.shape[0] // SUBLANES, u_blocks.shape[1] // tt)
    rows = SUBLANES * tt
    u_spec = pl.BlockSpec(blk, lambda i, j: (i, j, 0))
    st_spec = pl.BlockSpec((SUBLANES, n_state), lambda i, j: (i, 0))
    full = lambda a: _const_spec(a.shape)
    return pl.pallas_call(
        functools.partial(_s5_kernel, tt=tt, n_slabs=n_slabs),
        grid=grid,
        in_specs=[u_spec, st_spec, st_spec, full(lb_re), full(lb_im), full(wb), full(wc), full(d_skip)],
        out_specs=[u_spec, st_spec, st_spec],
        out_shape=[jax.ShapeDtypeStruct(u_blocks.shape, BF16),
                   jax.ShapeDtypeStruct(re0.shape, F32), jax.ShapeDtypeStruct(im0.shape, F32)],
        scratch_shapes=[pltpu.VMEM((rows, u_blocks.shape[-1]), F32)] * 2
                       + [pltpu.VMEM((rows, 2 * SLAB_GROUPS * S5_STATE), F32)] * 4,
        compiler_params=pltpu.CompilerParams(dimension_semantics=("parallel", "arbitrary"),
                                             vmem_limit_bytes=VMEM_LIMIT),
        name="s5_scan",
    )(u_blocks, re0, im0, lb_re, lb_im, wb, wc, d_skip)


def _s5_step_kernel(u_ref, re0_ref, im0_ref, lbr_ref, lbi_ref, wb_ref, wc_ref, d_ref,
                    h_ref, re1_ref, im1_ref, *, n_slabs):
    s_w = SLAB_GROUPS * S5_STATE
    u_w = SLAB_GROUPS * S5_GROUP
    u = u_ref[...]
    ys = []
    for s in range(n_slabs):
        us = u[:, s * u_w:(s + 1) * u_w]
        bu = _mm1(us, wb_ref[s])
        st = slice(s * s_w, (s + 1) * s_w)
        lbr, lbi = lbr_ref[:, st], lbi_ref[:, st]
        xr, xi = re0_ref[:, st], im0_ref[:, st]
        nr = lbr * xr - lbi * xi + bu[:, :s_w]
        ni = lbr * xi + lbi * xr + bu[:, s_w:]
        re1_ref[:, st] = nr
        im1_ref[:, st] = ni
        y = _mm1(jnp.concatenate([nr, ni], axis=1), wc_ref[s]) + d_ref[:, s * u_w:(s + 1) * u_w] * us
        ys.append(_gelu_tanh(y))
    h_ref[...] = jnp.concatenate(ys, axis=1).astype(h_ref.dtype)


def _s5_step(u2d, re0, im0, lb_re, lb_im, wb, wc, d_skip):
    full = lambda a: _const_spec(a.shape)
    args = (u2d, re0, im0, lb_re, lb_im, wb, wc, d_skip)
    return pl.pallas_call(
        functools.partial(_s5_step_kernel, n_slabs=wb.shape[0]),
        grid=(1,),
        in_specs=[full(a) for a in args],
        out_specs=[full(u2d), full(re0), full(im0)],
        out_shape=[jax.ShapeDtypeStruct(u2d.shape, BF16),
                   jax.ShapeDtypeStruct(re0.shape, F32), jax.ShapeDtypeStruct(im0.shape, F32)],
        compiler_params=pltpu.CompilerParams(vmem_limit_bytes=VMEM_LIMIT),
        name="s5_step",
    )(*args)


def _block_diag_slabs(m):
    g, a, b = m.shape
    eye = jnp.eye(SLAB_GROUPS, dtype=m.dtype)
    m4 = m.reshape(g // SLAB_GROUPS, SLAB_GROUPS, a, b)
    return jnp.einsum("sgab,gh->sgahb", m4, eye).reshape(g // SLAB_GROUPS, SLAB_GROUPS * a, SLAB_GROUPS * b)


_TAIL_W_NAMES = ("wro", "w1", "b1", "w2", "b2", "wmo", "npm", "nf", "npf", "wg", "wu", "wd")


def _tail_kernel(*refs):
    n_act = 4
    main_in, side_in = refs[:n_act], refs[n_act:2 * n_act]
    wts = dict(zip(_TAIL_W_NAMES, refs[2 * n_act:2 * n_act + len(_TAIL_W_NAMES)]))
    y_ref, ys_ref = refs[-2:]

    def tail(x_ref, oa_ref, hg_ref, gt_ref, y_ref):
        d = x_ref.shape[-1]
        tm = x_ref.shape[0]
        n_sub = 2 if tm % 32 == 0 else 1
        subs = [slice(i * (tm // n_sub), (i + 1) * (tm // n_sub)) for i in range(n_sub)]
        hg = [hg_ref[s, :] for s in subs]
        a_out = [_dot(oa_ref[s, :], wts["wro"][...]) for s in subs]
        b_lin = [_dot(h, wts["w1"][...]) + wts["b1"][...] for h in hg]
        b_gate = [_dot(h, wts["w2"][...]) + wts["b2"][...] for h in hg]
        merged = [(_sigmoid(gt_ref[s, :d]) * a + _sigmoid(gt_ref[s, d:]) * (bl * _sigmoid(bg))).astype(BF16)
                  for s, a, bl, bg in zip(subs, a_out, b_lin, b_gate)]
        mix = [_dot(m, wts["wmo"][...]) for m in merged]
        x1 = [x_ref[s, :] + _rms(m, wts["npm"][...]) for s, m in zip(subs, mix)]
        hb = [_rms(x, wts["nf"][...]).astype(BF16) for x in x1]
        gate = [_dot(h, wts["wg"][...]) for h in hb]
        up = [_dot(h, wts["wu"][...]) for h in hb]
        act = [(g * _sigmoid(g) * u).astype(BF16) for g, u in zip(gate, up)]
        f = [_dot(a, wts["wd"][...]) for a in act]
        for s, x, ff in zip(subs, x1, f):
            y_ref[s, :] = x + _rms(ff, wts["npf"][...])

    _on_group(tail, (*main_in, y_ref), (*side_in, ys_ref))


def _tail(acts_main, acts_side, tw, tm):
    d = acts_main[0].shape[1]
    n_main, main_spec = _two_group_grid(acts_main[0].shape[0], tm)
    wargs = [tw[n] for n in _TAIL_W_NAMES]
    return pl.pallas_call(
        _tail_kernel,
        grid=(n_main + 1,),
        in_specs=[main_spec(a.shape[1]) for a in acts_main] + [_const_spec(a.shape) for a in acts_side]
                 + [_const_spec(a.shape, True) for a in wargs],
        out_specs=[main_spec(d), _const_spec((acts_side[0].shape[0], d))],
        out_shape=[jax.ShapeDtypeStruct((a[0].shape[0], d), F32) for a in (acts_main, acts_side)],
        compiler_params=pltpu.CompilerParams(dimension_semantics=("arbitrary",),
                                             vmem_limit_bytes=VMEM_LIMIT),
        name="tail",
    )(*acts_main, *acts_side, *wargs)


def _layer(x_p, x_s, shift0, wkv0, re0, im0, lw, *, chunk, n_seq, s5_tt, row_tile):
    bsz, t, d = x_p.shape
    bs = x_s.shape[0]
    assert x_s.shape[1] == 1 and (bsz * t) % row_tile == 0
    heads = wkv0.shape[1]
    c_shift = shift0.shape[-1]
    c_u = lw["d_skip"].shape[-1]
    n_state = re0.shape[1] * re0.shape[2]
    s5_w = (lw["lb_re"], lw["lb_im"], lw["wb"], lw["wc"], lw["d_skip"])
    xp2d, xs2d = x_p.reshape(bsz * t, d), x_s.reshape(bs, d)
    (pr_p, u_p, gates_p), (pr_s, u_s, gates_s) = _proj(xp2d, xs2d, lw["norm_pre_mix"], lw["w_in"],
                                                       c_shift, c_u, row_tile)

    pr_p3 = pr_p.reshape(bsz, t, c_shift)
    oa_p, wkv_p = _rwkv_chunked(pr_p3, jnp.zeros((bsz, c_shift), F32), jnp.zeros((bsz, heads, HEAD, HEAD), F32),
                                lw["rwkv"], lw["e_mat"], chunk, n_seq)
    zeros_state = jnp.zeros((bsz, n_state), F32)
    hg_p, re_p, im_p = _s5(u_p.reshape(bsz, t, c_u), zeros_state, zeros_state, *s5_w, tt=s5_tt)

    oa_s, wkv_s = _rwkv_step(pr_s, shift0, wkv0, lw["rwkv"], lw["e_mat"])
    hg_s, re_s, im_s = _s5_step(u_s, re0.reshape(bs, n_state), im0.reshape(bs, n_state), *s5_w)

    y_p, y_s = _tail((xp2d, oa_p.reshape(bsz * t, -1), hg_p.reshape(bsz * t, c_u), gates_p),
                     (xs2d, oa_s, hg_s, gates_s), lw["tail"], row_tile)
    st_shape = lambda n: (n,) + re0.shape[1:]
    return ((y_p.reshape(x_p.shape), pr_p3[:, -1], wkv_p, re_p.reshape(st_shape(bsz)), im_p.reshape(st_shape(bsz))),
            (y_s.reshape(x_s.shape), pr_s, wkv_s, re_s.reshape(st_shape(bs)), im_s.reshape(st_shape(bs))))


def _prepare_layer_weights(l, p):
    row = lambda a: a[l][None, :].astype(F32)
    width = p["w0"].shape[-1]
    n_dec, n_aaa, n_gate = p["w_decay_up"].shape[1], p["w_aaa_up"].shape[1], p["w_gate_up"].shape[1]
    assert n_dec + n_aaa + n_gate == LORA_PAD

    def lora_pad(wup, start):
        return jnp.zeros((LORA_PAD, width), F32).at[start:start + wup.shape[0]].set(wup).astype(BF16)

    rwkv = {
        "mu": row(p["mu_shift"]), "w0": row(p["w0"]), "a0": row(p["a0"]), "k_k": row(p["k_k"]),
        "k_a": row(p["k_a"]), "r_k": row(p["r_k"]), "lnx_g": row(p["lnx_g"]), "lnx_b": row(p["lnx_b"]),
        "wd": lora_pad(p["w_decay_up"][l], 0),
        "wa": lora_pad(p["w_aaa_up"][l], n_dec),
        "wg": lora_pad(p["w_gate_up"][l], n_dec + n_aaa),
    }
    head_id = jnp.arange(MXU_DIM) // HEAD
    e_mat = (head_id[:, None] == head_id[None, :]).astype(BF16)

    lb_re, lb_im, bb_re_t, bb_im_t = _s5_discretise(
        p["s5_lam_re"][l], p["s5_lam_im"][l], p["s5_log_dt"][l],
        jnp.swapaxes(p["s5_b_re"][l], 1, 2), jnp.swapaxes(p["s5_b_im"][l], 1, 2))
    n_state = lb_re.shape[0] * lb_re.shape[2]
    to_out = lambda cc: _block_diag_slabs(jnp.swapaxes(cc, 1, 2))
    wb = jnp.concatenate([_block_diag_slabs(bb_re_t), _block_diag_slabs(bb_im_t)], axis=-1).astype(BF16)
    wc = jnp.concatenate([to_out(p["s5_c_re"][l]), -to_out(p["s5_c_im"][l])], axis=1).astype(BF16)

    bf = lambda a: a[l].astype(BF16)
    tail = {
        "wro": bf(p["w_rwkv_out"]), "w1": bf(p["glu_w1"]), "b1": row(p["glu_b1"]), "w2": bf(p["glu_w2"]),
        "b2": row(p["glu_b2"]), "wmo": bf(p["w_merge_out"]), "npm": row(p["norm_post_mix"]),
        "nf": row(p["norm_pre_ffn"]), "npf": row(p["norm_post_ffn"]),
        "wg": bf(p["w_ffn_gate"]), "wu": bf(p["w_ffn_up"]), "wd": bf(p["w_ffn_down"]),
    }
    return {
        "norm_pre_mix": row(p["norm_pre_mix"]), "w_in": bf(p["w_in"]), "rwkv": rwkv, "e_mat": e_mat,
        "lb_re": lb_re.reshape(1, n_state), "lb_im": lb_im.reshape(1, n_state), "wb": wb, "wc": wc,
        "d_skip": row(p["s5_d"]), "tail": tail,
    }


_PARAM_NAMES = ("norm_pre_mix", "norm_post_mix", "norm_pre_ffn", "norm_post_ffn", "w_in", "mu_shift",
                "w0", "w_decay_up", "a0", "w_aaa_up", "w_gate_up", "k_k", "k_a", "r_k", "lnx_g", "lnx_b",
                "w_rwkv_out", "s5_lam_re", "s5_lam_im", "s5_log_dt", "s5_b_re", "s5_b_im", "s5_c_re",
                "s5_c_im", "s5_d", "glu_w1", "glu_b1", "glu_w2", "glu_b2", "w_merge_out",
                "w_ffn_gate", "w_ffn_up", "w_ffn_down")


def _forward(x_prompt, x_sample, state_shift, state_wkv, state_s5_re, state_s5_im, params,
             *, chunk=64, n_seq=4, s5_tt=128, row_tile=256):
    depth = params["w_in"].shape[0]
    yp, ys = x_prompt, x_sample
    outs_p, outs_s = [], []
    for l in range(depth):
        lw = _prepare_layer_weights(l, params)
        (yp, *st_p), (ys, *st_s) = _layer(yp, ys, state_shift[l], state_wkv[l], state_s5_re[l], state_s5_im[l], lw,
                                          chunk=chunk, n_seq=n_seq, s5_tt=s5_tt, row_tile=row_tile)
        outs_p.append(st_p)
        outs_s.append(st_s)
    stack = lambda outs, i, dt: jnp.stack([o[i] for o in outs]).astype(dt)
    dt_p, dt_s = x_prompt.dtype, x_sample.dtype
    return (yp, ys,
            stack(outs_p, 0, dt_p), stack(outs_p, 1, dt_p), stack(outs_p, 2, dt_p), stack(outs_p, 3, dt_p),
            stack(outs_s, 0, dt_s), stack(outs_s, 1, dt_s), stack(outs_s, 2, dt_s), stack(outs_s, 3, dt_s))


def kernel(x_prompt, x_sample, state_shift, state_wkv, state_s5_re, state_s5_im, norm_pre_mix, norm_post_mix, norm_pre_ffn, norm_post_ffn, w_in, mu_shift, w0, w_decay_up, a0, w_aaa_up, w_gate_up, k_k, k_a, r_k, lnx_g, lnx_b, w_rwkv_out, s5_lam_re, s5_lam_im, s5_log_dt, s5_b_re, s5_b_im, s5_c_re, s5_c_im, s5_d, glu_w1, glu_b1, glu_w2, glu_b2, w_merge_out, w_ffn_gate, w_ffn_up, w_ffn_down):
    params = dict(zip(_PARAM_NAMES, (norm_pre_mix, norm_post_mix, norm_pre_ffn, norm_post_ffn, w_in, mu_shift,
                                     w0, w_decay_up, a0, w_aaa_up, w_gate_up, k_k, k_a, r_k, lnx_g, lnx_b,
                                     w_rwkv_out, s5_lam_re, s5_lam_im, s5_log_dt, s5_b_re, s5_b_im, s5_c_re,
                                     s5_c_im, s5_d, glu_w1, glu_b1, glu_w2, glu_b2, w_merge_out,
                                     w_ffn_gate, w_ffn_up, w_ffn_down)))
    return _forward(x_prompt, x_sample, state_shift, state_wkv, state_s5_re, state_s5_im, params)
```

```python
import functools
import math

import jax
import jax.numpy as jnp
from jax import lax
from jax.experimental import pallas as pl
from jax.experimental.pallas import tpu as pltpu

F32 = jnp.float32
BF16 = jnp.bfloat16

NORM_EPS = 1e-6
LNX_EPS = 64e-5
HEAD = 64
GROUP_HEADS = 4
GROUP_W = GROUP_HEADS * HEAD
LORA_PAD = 128
S5_GROUP = 16
S5_STATE = 64
SLAB_GROUPS = 8
SUBLANES = 8
LANES = 128
MXU_DIM = 256
S5_ROW_BLOCK = 256
VMEM_LIMIT =56 * 1024 * 1024

NN = (((1,), (0,)), ((), ()))
NT = (((1,), (1,)), ((), ()))


def _dot(a, b, dims=NN):
    return lax.dot_general(a, b, dims, preferred_element_type=F32)


def _split2(x):
    hi = x.astype(BF16)
    lo = (x - hi.astype(F32)).astype(BF16)
    return hi, lo


def _split3(x):
    hi = x.astype(BF16)
    r1 = x - hi.astype(F32)
    mid = r1.astype(BF16)
    lo = (r1 - mid.astype(F32)).astype(BF16)
    return hi, mid, lo


def _mm1(a, b, dims=NN):
    return _dot(a.astype(BF16), b.astype(BF16), dims)


def _mm_exact_lhs(a_bf16, b):
    h, m, l = _split3(b)
    return _dot(a_bf16, h) + (_dot(a_bf16, m) + _dot(a_bf16, l))


def _rms(x, g):
    return x * lax.rsqrt(jnp.mean(x * x, axis=-1, keepdims=True) + NORM_EPS) * g


def _sigmoid(x):
    return 1.0 / (1.0 + jnp.exp(-x))


def _gelu_tanh(x):
    c = math.sqrt(2.0 / math.pi)
    return 0.5 * x * (1.0 + jnp.tanh(c * (x + 0.044715 * (x * x * x))))


def _const_spec(shape, single_buffer=False):
    idx = lambda *_: (0,) * len(shape)
    if single_buffer:
        return pl.BlockSpec(shape, idx, pipeline_mode=pl.Buffered(1))
    return pl.BlockSpec(shape, idx)


def _two_group_grid(rows_main, tm):
    n_main = rows_main // tm
    main_spec = lambda width: pl.BlockSpec((tm, width), lambda i: (jnp.minimum(i, n_main - 1), 0))
    return n_main, main_spec


def _on_group(body, main_refs, side_refs):
    i, n_main = pl.program_id(0), pl.num_programs(0) - 1
    pl.when(i < n_main)(functools.partial(body, *main_refs))
    pl.when(i == n_main)(functools.partial(body, *side_refs))


def _proj_kernel(x_ref, xs_ref, g_ref, w_ref, pr_ref, u_ref, gt_ref, prs_ref, us_ref, gts_ref, *, c_shift, c_u):
    def project(x_ref, pr_ref, u_ref, gt_ref):
        hb = _rms(x_ref[...], g_ref[...]).astype(BF16)
        pr_ref[...] = _dot(hb, w_ref[:, :c_shift])
        u_ref[...] = _dot(hb, w_ref[:, c_shift:c_shift + c_u])
        gt_ref[...] = _sigmoid(_dot(hb, w_ref[:, c_shift + c_u:])).astype(gt_ref.dtype)

    _on_group(project, (x_ref, pr_ref, u_ref, gt_ref), (xs_ref, prs_ref, us_ref, gts_ref))


def _proj(x_main, x_side, g, w_in_bf16, c_shift, c_u, tm):
    d = x_main.shape[1]
    cols = w_in_bf16.shape[1]
    widths = (c_shift, c_u, cols - c_shift - c_u)
    n_main, main_spec = _two_group_grid(x_main.shape[0], tm)
    side_spec = lambda width: _const_spec((x_side.shape[0], width))
    outs = pl.pallas_call(
        functools.partial(_proj_kernel, c_shift=c_shift, c_u=c_u),
        grid=(n_main + 1,),
        in_specs=[main_spec(d), side_spec(d), _const_spec((1, d)), _const_spec((d, cols), True)],
        out_specs=[main_spec(wd) for wd in widths] + [side_spec(wd) for wd in widths],
        out_shape=[jax.ShapeDtypeStruct((x.shape[0], wd), dt) for x in (x_main, x_side)
                   for wd, dt in zip(widths, (F32, F32, BF16))],
        compiler_params=pltpu.CompilerParams(dimension_semantics=("arbitrary",),
                                             vmem_limit_bytes=VMEM_LIMIT),
        name="proj",
    )(x_main, x_side, g, w_in_bf16)
    return outs[:3], outs[3:]


def _head_sum(x, e_ref):
    rows, width = x.shape
    gw = e_ref.shape[0]
    n_lg = width // gw
    hi, lo = _split2(x)
    stacked = jnp.concatenate([part[:, j * gw:(j + 1) * gw] for part in (hi, lo) for j in range(n_lg)], axis=0)
    sums = _dot(stacked, e_ref[...])
    return jnp.concatenate([sums[j * rows:(j + 1) * rows] + sums[(n_lg + j) * rows:(n_lg + j + 1) * rows]
                            for j in range(n_lg)], axis=1)


def _rwkv_token_prep(xr, w, e_ref):
    width = w["w0"].shape[-1]
    r = xr[:, :width]
    k = xr[:, width:2 * width]
    v = xr[:, 2 * width:3 * width]
    lo = xr[:, 3 * width:3 * width + LORA_PAD]
    wl = w["w0"][...] + _mm1(jnp.tanh(lo), w["wd"][...])
    lw = -math.exp(-0.5) * _sigmoid(wl)
    a = _sigmoid(w["a0"][...] + _mm1(lo, w["wa"][...]))
    g = _mm1(_sigmoid(lo), w["wg"][...])
    kk = k * w["k_k"][...]
    kk = kk * lax.rsqrt(jnp.maximum(_head_sum(kk * kk, e_ref), 1e-24))
    kmod = k * (1.0 + (a - 1.0) * w["k_a"][...])
    return r, kmod, v, kk, kk * a, lw, g


def _rwkv_bonus(r, kmod, v, w, e_ref):
    return _head_sum(r * kmod * w["r_k"][...], e_ref) * v


def _rwkv_post(o, bonus, g, w, e_ref):
    inv_n = 1.0 / HEAD
    mu = _head_sum(o, e_ref) * inv_n
    oc = o - mu
    var = _head_sum(oc * oc, e_ref) * inv_n
    on = oc * lax.rsqrt(var + LNX_EPS) * w["lnx_g"][...] + w["lnx_b"][...]
    return (on + bonus) * g


_RWKV_W_NAMES = ("mu", "w0", "a0", "k_k", "k_a", "r_k", "lnx_g", "lnx_b", "wd", "wa", "wg")


def _rwkv_chunk_kernel(pr_ref, shift0_ref, s0_ref, *rest, chunk, width, n_seq):
    n_w = len(_RWKV_W_NAMES)
    w = dict(zip(_RWKV_W_NAMES, rest[:n_w]))
    e_ref, tri_ref = rest[n_w], rest[n_w + 1]
    oa_ref, s1_ref = rest[n_w + 2], rest[n_w + 3]
    carry_ref, z_ref, ops_ref, aux_ref, gall_ref = rest[n_w + 4:n_w + 9]
    step = pl.program_id(1)
    n_chunks = pl.num_programs(1) - 1
    n_groups = width // GROUP_W
    C = chunk
    GC = GROUP_HEADS * C

    def lane_block_masks(n_lanes, block):
        lane_block = lax.broadcasted_iota(jnp.int32, (1, n_lanes), 1) // block
        return [lane_block == h for h in range(GROUP_HEADS)]

    vec_masks = lane_block_masks(GROUP_W, HEAD)
    mat_masks = lane_block_masks(GC, C)

    def stack(x, masks):
        xb = x.astype(BF16)
        zero = jnp.zeros_like(xb)
        return jnp.concatenate([jnp.where(m, xb, zero) for m in masks], axis=0)

    @pl.when(step == 0)
    def _():
        carry_ref[...] = shift0_ref[...]
        ops_ref[1] = jnp.zeros(ops_ref.shape[1:], ops_ref.dtype)
        aux_ref[1] = jnp.zeros(aux_ref.shape[1:], aux_ref.dtype)
        gall_ref[1] = jnp.zeros(gall_ref.shape[1:], gall_ref.dtype)
        z_ref[...] = jnp.zeros(z_ref.shape, z_ref.dtype)

    @pl.when(step == 1)
    def _():
        for q in range(n_seq):
            for gi in range(n_groups):
                z_ref[q, gi] = jnp.zeros((GROUP_W, GROUP_W), F32)
                for h in range(GROUP_HEADS):
                    z_ref[q, gi, h * HEAD:(h + 1) * HEAD, h * HEAD:(h + 1) * HEAD] = s0_ref[q, gi * GROUP_HEADS + h]

    tok = lax.broadcasted_iota(jnp.int32, (C, GC), 0)
    col = lax.broadcasted_iota(jnp.int32, (C, GC), 1) % C
    strict = tok > col
    incl = tok >= col
    eye = jnp.where(tok == col, 1.0, 0.0).astype(F32)
    zi = lax.broadcasted_iota(jnp.int32, (GROUP_W, GROUP_W), 0) // HEAD
    zj = lax.broadcasted_iota(jnp.int32, (GROUP_W, GROUP_W), 1) // HEAD
    block_diag = zi == zj
    rows2 = lambda top, bottom: jnp.concatenate([top, bottom], axis=0).astype(BF16)

    def recurrence(r_slot):
        chains = [(q, gi) for q in range(n_seq) for gi in range(n_groups)]
        op = lambda i, q, gi: ops_ref[r_slot, i, q * C:(q + 1) * C, gi * GROUP_W:(gi + 1) * GROUP_W]
        kap_t, r_t, v = ([op(i, q, gi) for q, gi in chains] for i in (0, 1, 4))
        kap_s = [stack(x, vec_masks) for x in kap_t]
        v_s = [stack(x, vec_masks) for x in v]
        grams = []
        for i, (q, gi) in enumerate(chains):
            right = jnp.concatenate([stack(op(2, q, gi), vec_masks), stack(op(3, q, gi), vec_masks)], axis=0)
            grams.append(_dot(rows2(kap_t[i], r_t[i]), right, NT))
        yield
        a_k = [jnp.where(strict, gm[:C, :GC], 0.0) for gm in grams]
        a_b = [jnp.where(strict, gm[:C, GC:], 0.0) for gm in grams]
        a_rk = [jnp.where(incl, gm[C:, :GC], 0.0) for gm in grams]
        a_rb = [jnp.where(incl, gm[C:, GC:], 0.0) for gm in grams]

        ps = [-a for a in a_b]
        ts = [eye + p for p in ps]
        ps = [_dot(p.astype(BF16), stack(p, mat_masks)) for p in ps]
        yield
        covered = 2
        while covered < C:
            powers = [stack(p, mat_masks) for p in ps]
            if 2 * covered < C:
                both = [_dot(rows2(t, p), pw) for t, p, pw in zip(ts, ps, powers)]
                ts = [t + x[:C] for t, x in zip(ts, both)]
                ps = [x[C:] for x in both]
            else:
                ts = [t + _dot(t.astype(BF16), pw) for t, pw in zip(ts, powers)]
            covered *= 2
            yield

        av = [_dot(rows2(a, ar), vs) for a, ar, vs in zip(a_k, a_rk, v_s)]
        yield
        tw = [_dot(t.astype(BF16), jnp.concatenate([ks, stack(x[:C], vec_masks)], axis=1))
              for t, ks, x in zip(ts, kap_s, av)]
        yield
        zs = [z_ref[q, gi] for q, gi in chains]
        pz = [_dot(rows2(x[:, :GROUP_W], rt), z.astype(BF16), NT)
              for x, rt, z in zip(tw, r_t, zs)]
        us = [p[:C] + x[:, GROUP_W:] for p, x in zip(pz, tw)]
        yield
        arbu = [_dot(a.astype(BF16), stack(u, vec_masks)) for a, u in zip(a_rb, us)]
        o_blk = [p[C:] + x[C:] - y for p, x, y in zip(pz, av, arbu)]
        for i, (q, gi) in enumerate(chains):
            vu_t = jnp.concatenate([v[i].astype(F32), us[i]], axis=0).T.astype(BF16)
            kb = jnp.concatenate([op(5, q, gi), op(6, q, gi)], axis=0)
            g_all = gall_ref[r_slot, q][:, gi * GROUP_W:(gi + 1) * GROUP_W]
            z_ref[q, gi] = zs[i] * g_all + jnp.where(block_diag, _dot(vu_t, kb), 0.0)
        yield
        o = jnp.concatenate([jnp.concatenate(o_blk[q * n_groups:(q + 1) * n_groups], axis=1)
                             for q in range(n_seq)], axis=0)
        oa_ref[...] = _rwkv_post(o, aux_ref[r_slot, 0], aux_ref[r_slot, 1], w, e_ref
                                 ).astype(oa_ref.dtype).reshape(oa_ref.shape)

    def prepare(q, w_slot):
        rows = slice(q * C, (q + 1) * C)
        pr = pr_ref[q]
        row_id = lax.broadcasted_iota(jnp.int32, pr.shape, 0)
        pr_prev = jnp.where(row_id == 0, carry_ref[q], pltpu.roll(pr, 1, axis=0))
        carry_ref[q] = pr[C - 1:C, :]
        xr = pr + (pr_prev - pr) * w["mu"][...]
        r, kmod, vv, kk, bvec, lw, g = _rwkv_token_prep(xr, w, e_ref)
        aux_ref[w_slot, 0, rows] = _rwkv_bonus(r, kmod, vv, w, e_ref)
        aux_ref[w_slot, 1, rows] = g
        yield
        cum = _mm_exact_lhs(tri_ref[...], lw)
        cum_last = cum[C - 1:C, :]
        g_neg = jnp.exp(-cum)
        g_end = jnp.exp(cum_last - cum)
        prepared = (kk * jnp.exp(cum - lw),
                    r * jnp.exp(cum),
                    kmod * g_neg, bvec * g_neg,
                    vv, kmod * g_end, -(bvec * g_end))
        for i, x in enumerate(prepared):
            ops_ref[w_slot, i, rows] = x.astype(ops_ref.dtype)
        gall_ref[w_slot, q] = jnp.exp(cum_last)

    w_slot = step % 2
    pending = [gen for gen in [prepare(q, w_slot) for q in range(n_seq)] for _ in range(2)]
    for _ in recurrence(1 - w_slot):
        if pending:
            next(pending.pop(0), None)
    for gen in pending:
        next(gen, None)

    @pl.when(step == n_chunks)
    def _():
        for q in range(n_seq):
            for gi in range(n_groups):
                for h in range(GROUP_HEADS):
                    s1_ref[q, gi * GROUP_HEADS + h] = z_ref[q, gi, h * HEAD:(h + 1) * HEAD, h * HEAD:(h + 1) * HEAD]


_N_OPS = 7


def _rwkv_weight_inputs(wts):
    return [wts[n] for n in _RWKV_W_NAMES]


def _rwkv_weight_specs(wts):
    return [_const_spec(wts[n].shape) for n in _RWKV_W_NAMES]


def _rwkv_chunked(pr3d, shift0, wkv0, wts, e_mat, chunk, n_seq):
    bsz, t, c_shift = pr3d.shape
    heads = wkv0.shape[1]
    width = heads * HEAD
    tri = jnp.tril(jnp.ones((chunk, chunk), F32)).astype(BF16)
    kern = functools.partial(_rwkv_chunk_kernel, chunk=chunk, width=width, n_seq=n_seq)
    n_chunks = t // chunk
    rows = n_seq * chunk
    return pl.pallas_call(
        kern,
        grid=(bsz // n_seq, n_chunks + 1),
        in_specs=[pl.BlockSpec((n_seq, chunk, c_shift), lambda b, s: (b, jnp.minimum(s, n_chunks - 1), 0)),
                  pl.BlockSpec((n_seq, 1, c_shift), lambda b, s: (b, 0, 0)),
                  pl.BlockSpec((n_seq, heads, HEAD, HEAD), lambda b, s: (b, 0, 0, 0))]
                 + _rwkv_weight_specs(wts)
                 + [_const_spec(e_mat.shape), _const_spec(tri.shape)],
        out_specs=[pl.BlockSpec((n_seq, chunk, width), lambda b, s: (b, jnp.maximum(s - 1, 0), 0)),
                   pl.BlockSpec((n_seq, heads, HEAD, HEAD), lambda b, s: (b, 0, 0, 0))],
        out_shape=[jax.ShapeDtypeStruct((bsz, t, width), BF16),
                   jax.ShapeDtypeStruct((bsz, heads, HEAD, HEAD), F32)],
        scratch_shapes=[pltpu.VMEM((n_seq, 1, c_shift), F32),
                        pltpu.VMEM((n_seq, width // GROUP_W, GROUP_W, GROUP_W), F32),
                        pltpu.VMEM((2, _N_OPS, rows, width), BF16),
                        pltpu.VMEM((2, 2, rows, width), F32),
                        pltpu.VMEM((2, n_seq, 1, width), F32)],
        compiler_params=pltpu.CompilerParams(dimension_semantics=("parallel", "arbitrary"),
                                             vmem_limit_bytes=VMEM_LIMIT),
        name="rwkv_chunk",
    )(pr3d, shift0[:, None, :], wkv0, *_rwkv_weight_inputs(wts), e_mat, tri)


_STEP_VECS = 6


def _rwkv_step_kernel(pr_ref, shift0_ref, s_ref, *rest):
    n_w = len(_RWKV_W_NAMES)
    w = dict(zip(_RWKV_W_NAMES, rest[:n_w]))
    e_ref, oa_ref, s1_ref, vec_ref, post_ref, o_ref = rest[n_w:n_w + 6]
    h = pl.program_id(0)

    @pl.when(h == 0)
    def _():
        pr = pr_ref[...]
        xr = pr + (shift0_ref[...] - pr) * w["mu"][...]
        r, kmod, v, kk, bvec, lw, g = _rwkv_token_prep(xr, w, e_ref)
        for i, x in enumerate((kk, bvec, jnp.exp(lw), kmod, r, v)):
            vec_ref[i] = x.T
        post_ref[0] = _rwkv_bonus(r, kmod, v, w, e_ref)
        post_ref[1] = g

    base = pl.multiple_of(h * HEAD, HEAD)
    kk_h, b_h, dec_h, k_h, r_h = [vec_ref[i, pl.ds(base, HEAD), :] for i in range(5)]

    def value_row(vi, carry):
        tile = s_ref[0, vi]
        s_kappa = jnp.sum(tile * kk_h, axis=0, keepdims=True)
        new = tile * dec_h - s_kappa * b_h + vec_ref[5, pl.ds(base + vi, 1), :] * k_h
        s1_ref[0, vi] = new
        o_ref[pl.ds(base + vi, 1), :] = jnp.sum(new * r_h, axis=0, keepdims=True)
        return carry

    lax.fori_loop(0, HEAD, value_row, 0, unroll=4)

    @pl.when(h == pl.num_programs(0) - 1)
    def _():
        oa_ref[...] = _rwkv_post(o_ref[...].T, post_ref[0], post_ref[1], w, e_ref).astype(oa_ref.dtype)


def _rwkv_step(pr2d, shift0, wkv0, wts, e_mat):
    bsz, c_shift = pr2d.shape
    heads = wkv0.shape[1]
    width = heads * HEAD
    full = lambda a: _const_spec(a.shape)
    st = pl.BlockSpec((1, HEAD, HEAD, bsz), lambda hh: (hh, 0, 0, 0))
    oa, s1_t = pl.pallas_call(
        _rwkv_step_kernel,
        grid=(heads,),
        in_specs=[full(pr2d), full(shift0), st] + _rwkv_weight_specs(wts) + [full(e_mat)],
        out_specs=[_const_spec((bsz, width)), st],
        out_shape=[jax.ShapeDtypeStruct((bsz, width), BF16),
                   jax.ShapeDtypeStruct((heads, HEAD, HEAD, bsz), F32)],
        scratch_shapes=[pltpu.VMEM((_STEP_VECS, width, bsz), F32),
                        pltpu.VMEM((2, bsz, width), F32),
                        pltpu.VMEM((width, bsz), F32)],
        compiler_params=pltpu.CompilerParams(dimension_semantics=("arbitrary",), vmem_limit_bytes=VMEM_LIMIT),
        name="rwkv_step",
    )(pr2d, shift0, jnp.transpose(wkv0, (1, 2, 3, 0)), *_rwkv_weight_inputs(wts), e_mat)
    return oa, jnp.transpose(s1_t, (3, 0, 1, 2))


def _s5_disc_kernel(lre_ref, lim_ref, ldt_ref, bre_ref, bim_ref, lbr_ref, lbi_ref, bbr_ref, bbi_ref):
    lam_re, lam_im = lre_ref[...], lim_ref[...]
    dt = jnp.exp(ldt_ref[...])
    mag = jnp.exp(lam_re * dt)
    ang = lam_im * dt
    lb_re, lb_im = mag * jnp.cos(ang), mag * jnp.sin(ang)
    nr, ni = lb_re - 1.0, lb_im
    den = lam_re * lam_re + lam_im * lam_im
    f_re = (nr * lam_re + ni * lam_im) / den
    f_im = (ni * lam_re - nr * lam_im) / den
    b_re, b_im = bre_ref[...], bim_ref[...]
    lbr_ref[...] = lb_re
    lbi_ref[...] = lb_im
    bbr_ref[...] = f_re * b_re - f_im * b_im
    bbi_ref[...] = f_re * b_im + f_im * b_re


def _s5_discretise(lam_re, lam_im, log_dt, b_re_t, b_im_t):
    g, p = lam_re.shape
    full = lambda a: _const_spec(a.shape)
    args = (lam_re[:, None, :], lam_im[:, None, :], log_dt[:, None, None], b_re_t, b_im_t)
    return pl.pallas_call(
        _s5_disc_kernel,
        grid=(1,),
        in_specs=[full(a) for a in args],
        out_specs=[_const_spec((g, 1, p))] * 2 + [full(b_re_t)] * 2,
        out_shape=[jax.ShapeDtypeStruct((g, 1, p), F32)] * 2 + [jax.ShapeDtypeStruct(b_re_t.shape, F32)] * 2,
        name="s5_discretise",
    )(*args)


def _s5_kernel(u_ref, re0_ref, im0_ref, lbr_ref, lbi_ref, wb_ref, wc_ref, d_ref,
               h_ref, re1_ref, im1_ref, u_tm, h_tm, bu0, bu1, xs0, xs1, *, tt, n_slabs):
    t_blk = pl.program_id(1)
    rows = SUBLANES * tt
    s_w = SLAB_GROUPS * S5_STATE
    u_w = SLAB_GROUPS * S5_GROUP
    bu, xs = (bu0, bu1), (xs0, xs1)
    rb = min(rows, S5_ROW_BLOCK)
    steps_per_rb = rb // SUBLANES

    @pl.when(t_blk == 0)
    def _():
        re1_ref[...] = re0_ref[...]
        im1_ref[...] = im0_ref[...]

    u_tm[...] = jnp.swapaxes(u_ref[...], 0, 1).reshape(rows, n_slabs * u_w)

    def project_in(s, j):
        r0 = j * rb
        bu[s % 2][r0:r0 + rb, :] = _mm1(u_tm[r0:r0 + rb, s * u_w:(s + 1) * u_w], wb_ref[s])

    def project_out(s, j):
        r0 = j * rb
        lanes = slice(s * u_w, (s + 1) * u_w)
        y = _mm1(xs[s % 2][r0:r0 + rb, :], wc_ref[s]) + d_ref[:, lanes] * u_tm[r0:r0 + rb, lanes]
        h_tm[r0:r0 + rb, lanes] = _gelu_tanh(y)

    for j in range(rows // rb):
        project_in(0, j)
    for p in range(n_slabs + 1):
        if p < n_slabs:
            st = slice(p * s_w, (p + 1) * s_w)
            lbr = jnp.broadcast_to(lbr_ref[:, st], (SUBLANES, s_w))
            lbi = jnp.broadcast_to(lbi_ref[:, st], (SUBLANES, s_w))
            xr, xi = re1_ref[:, st], im1_ref[:, st]
        for j in range(rows // rb):
            if p < n_slabs:
                for t in range(j * steps_per_rb, (j + 1) * steps_per_rb):
                    at_t = slice(t * SUBLANES, (t + 1) * SUBLANES)
                    xr, xi = (lbr * xr - lbi * xi + bu[p % 2][at_t, :s_w],
                              lbr * xi + lbi * xr + bu[p % 2][at_t, s_w:])
                    xs[p % 2][at_t, :s_w] = xr
                    xs[p % 2][at_t, s_w:] = xi
            if p + 1 < n_slabs:
                project_in(p + 1, j)
            if p >= 1:
                project_out(p - 1, j)
        if p < n_slabs:
            re1_ref[:, st] = xr
            im1_ref[:, st] = xi
    h = h_tm[...].reshape(tt, SUBLANES, n_slabs * u_w)
    h_ref[...] = jnp.swapaxes(h, 0, 1).astype(h_ref.dtype)


def _s5(u_blocks, re0, im0, lb_re, lb_im, wb, wc, d_skip, tt):
    n_slabs = wb.shape[0]
    n_state = re0.shape[1]
    blk = (SUBLANES, tt, u_blocks.shape[-1])
    grid = (u_blocks.shape[0] // SUBLANES, u_blocks.shape[1] // tt)
    rows = SUBLANES * tt
    u_spec = pl.BlockSpec(blk, lambda i, j: (i, j, 0))
    st_spec = pl.BlockSpec((SUBLANES, n_state), lambda i, j: (i, 0))
    full = lambda a: _const_spec(a.shape)
    return pl.pallas_call(
        functools.partial(_s5_kernel, tt=tt, n_slabs=n_slabs),
        grid=grid,
        in_specs=[u_spec, st_spec, st_spec, full(lb_re), full(lb_im), full(wb), full(wc), full(d_skip)],
        out_specs=[u_spec, st_spec, st_spec],
        out_shape=[jax.ShapeDtypeStruct(u_blocks.shape, BF16),
                   jax.ShapeDtypeStruct(re0.shape, F32), jax.ShapeDtypeStruct(im0.shape, F32)],
        scratch_shapes=[pltpu.VMEM((rows, u_blocks.shape[-1]), F32)] * 2
                       + [pltpu.VMEM((rows, 2 * SLAB_GROUPS * S5_STATE), F32)] * 4,
        compiler_params=pltpu.CompilerParams(dimension_semantics=("parallel", "arbitrary"),
                                             vmem_limit_bytes=VMEM_LIMIT),
        name="s5_scan",
    )(u_blocks, re0, im0, lb_re, lb_im, wb, wc, d_skip)


def _s5_step_kernel(u_ref, re0_ref, im0_ref, lbr_ref, lbi_ref, wb_ref, wc_ref, d_ref,
                    h_ref, re1_ref, im1_ref, *, n_slabs):
    s_w = SLAB_GROUPS * S5_STATE
    u_w = SLAB_GROUPS * S5_GROUP
    u = u_ref[...]
    ys = []
    for s in range(n_slabs):
        us = u[:, s * u_w:(s + 1) * u_w]
        bu = _mm1(us, wb_ref[s])
        st = slice(s * s_w, (s + 1) * s_w)
        lbr, lbi = lbr_ref[:, st], lbi_ref[:, st]
        xr, xi = re0_ref[:, st], im0_ref[:, st]
        nr = lbr * xr - lbi * xi + bu[:, :s_w]
        ni = lbr * xi + lbi * xr + bu[:, s_w:]
        re1_ref[:, st] = nr
        im1_ref[:, st] = ni
        y = _mm1(jnp.concatenate([nr, ni], axis=1), wc_ref[s]) + d_ref[:, s * u_w:(s + 1) * u_w] * us
        ys.append(_gelu_tanh(y))
    h_ref[...] = jnp.concatenate(ys, axis=1).astype(h_ref.dtype)


def _s5_step(u2d, re0, im0, lb_re, lb_im, wb, wc, d_skip):
    full = lambda a: _const_spec(a.shape)
    args = (u2d, re0, im0, lb_re, lb_im, wb, wc, d_skip)
    return pl.pallas_call(
        functools.partial(_s5_step_kernel, n_slabs=wb.shape[0]),
        grid=(1,),
        in_specs=[full(a) for a in args],
        out_specs=[full(u2d), full(re0), full(im0)],
        out_shape=[jax.ShapeDtypeStruct(u2d.shape, BF16),
                   jax.ShapeDtypeStruct(re0.shape, F32), jax.ShapeDtypeStruct(im0.shape, F32)],
        compiler_params=pltpu.CompilerParams(vmem_limit_bytes=VMEM_LIMIT),
        name="s5_step",
    )(*args)


def _block_diag_slabs(m):
    g, a, b = m.shape
    eye = jnp.eye(SLAB_GROUPS, dtype=m.dtype)
    m4 = m.reshape(g // SLAB_GROUPS, SLAB_GROUPS, a, b)
    return jnp.einsum("sgab,gh->sgahb", m4, eye).reshape(g // SLAB_GROUPS, SLAB_GROUPS * a, SLAB_GROUPS * b)


_TAIL_W_NAMES = ("wro", "w1", "b1", "w2", "b2", "wmo", "npm", "nf", "npf", "wg", "wu", "wd")


def _tail_kernel(*refs):
    n_act = 4
    main_in, side_in = refs[:n_act], refs[n_act:2 * n_act]
    wts = dict(zip(_TAIL_W_NAMES, refs[2 * n_act:2 * n_act + len(_TAIL_W_NAMES)]))
    y_ref, ys_ref = refs[-2:]

    def tail(x_ref, oa_ref, hg_ref, gt_ref, y_ref):
        d = x_ref.shape[-1]
        tm = x_ref.shape[0]
        n_sub = 2 if tm % 32 == 0 else 1
        subs = [slice(i * (tm // n_sub), (i + 1) * (tm // n_sub)) for i in range(n_sub)]
        hg = [hg_ref[s, :] for s in subs]
        a_out = [_dot(oa_ref[s, :], wts["wro"][...]) for s in subs]
        b_lin = [_dot(h, wts["w1"][...]) + wts["b1"][...] for h in hg]
        b_gate = [_dot(h, wts["w2"][...]) + wts["b2"][...] for h in hg]
        merged = [(gt_ref[s, :d].astype(F32) * a + gt_ref[s, d:].astype(F32) * (bl * _sigmoid(bg))).astype(BF16)
                  for s, a, bl, bg in zip(subs, a_out, b_lin, b_gate)]
        mix = [_dot(m, wts["wmo"][...]) for m in merged]
        x1 = [x_ref[s, :] + _rms(m, wts["npm"][...]) for s, m in zip(subs, mix)]
        hb = [_rms(x, wts["nf"][...]).astype(BF16) for x in x1]
        gate = [_dot(h, wts["wg"][...]) for h in hb]
        up = [_dot(h, wts["wu"][...]) for h in hb]
        act = [(g * _sigmoid(g) * u).astype(BF16) for g, u in zip(gate, up)]
        f = [_dot(a, wts["wd"][...]) for a in act]
        for s, x, ff in zip(subs, x1, f):
            y_ref[s, :] = x + _rms(ff, wts["npf"][...])

    _on_group(tail, (*main_in, y_ref), (*side_in, ys_ref))


def _tail(acts_main, acts_side, tw, tm):
    d = acts_main[0].shape[1]
    n_main, main_spec = _two_group_grid(acts_main[0].shape[0], tm)
    wargs = [tw[n] for n in _TAIL_W_NAMES]
    return pl.pallas_call(
        _tail_kernel,
        grid=(n_main + 1,),
        in_specs=[main_spec(a.shape[1]) for a in acts_main] + [_const_spec(a.shape) for a in acts_side]
                 + [_const_spec(a.shape, True) for a in wargs],
        out_specs=[main_spec(d), _const_spec((acts_side[0].shape[0], d))],
        out_shape=[jax.ShapeDtypeStruct((a[0].shape[0], d), F32) for a in (acts_main, acts_side)],
        compiler_params=pltpu.CompilerParams(dimension_semantics=("arbitrary",),
                                             vmem_limit_bytes=VMEM_LIMIT),
        name="tail",
    )(*acts_main, *acts_side, *wargs)


def _layer(x_p, x_s, shift0, wkv0, re0, im0, lw, *, chunk, n_seq, s5_tt, row_tile):
    bsz, t, d = x_p.shape
    bs = x_s.shape[0]
    assert x_s.shape[1] == 1 and (bsz * t) % row_tile == 0
    heads = wkv0.shape[1]
    c_shift = shift0.shape[-1]
    c_u = lw["d_skip"].shape[-1]
    n_state = re0.shape[1] * re0.shape[2]
    s5_w = (lw["lb_re"], lw["lb_im"], lw["wb"], lw["wc"], lw["d_skip"])
    xp2d, xs2d = x_p.reshape(bsz * t, d), x_s.reshape(bs, d)
    (pr_p, u_p, gates_p), (pr_s, u_s, gates_s) = _proj(xp2d, xs2d, lw["norm_pre_mix"], lw["w_in"],
                                                       c_shift, c_u, row_tile)

    pr_p3 = pr_p.reshape(bsz, t, c_shift)
    oa_p, wkv_p = _rwkv_chunked(pr_p3, jnp.zeros((bsz, c_shift), F32), jnp.zeros((bsz, heads, HEAD, HEAD), F32),
                                lw["rwkv"], lw["e_mat"], chunk, n_seq)
    zeros_state = jnp.zeros((bsz, n_state), F32)
    hg_p, re_p, im_p = _s5(u_p.reshape(bsz, t, c_u), zeros_state, zeros_state, *s5_w, tt=s5_tt)

    oa_s, wkv_s = _rwkv_step(pr_s, shift0, wkv0, lw["rwkv"], lw["e_mat"])
    hg_s, re_s, im_s = _s5_step(u_s, re0.reshape(bs, n_state), im0.reshape(bs, n_state), *s5_w)

    y_p, y_s = _tail((xp2d, oa_p.reshape(bsz * t, -1), hg_p.reshape(bsz * t, c_u), gates_p),
                     (xs2d, oa_s, hg_s, gates_s), lw["tail"], row_tile)
    st_shape = lambda n: (n,) + re0.shape[1:]
    return ((y_p.reshape(x_p.shape), pr_p3[:, -1], wkv_p, re_p.reshape(st_shape(bsz)), im_p.reshape(st_shape(bsz))),
            (y_s.reshape(x_s.shape), pr_s, wkv_s, re_s.reshape(st_shape(bs)), im_s.reshape(st_shape(bs))))


def _prepare_layer_weights(l, p):
    row = lambda a: a[l][None, :].astype(F32)
    width = p["w0"].shape[-1]
    n_dec, n_aaa, n_gate = p["w_decay_up"].shape[1], p["w_aaa_up"].shape[1], p["w_gate_up"].shape[1]
    assert n_dec + n_aaa + n_gate == LORA_PAD

    def lora_pad(wup, start):
        return jnp.zeros((LORA_PAD, width), F32).at[start:start + wup.shape[0]].set(wup).astype(BF16)

    rwkv = {
        "mu": row(p["mu_shift"]), "w0": row(p["w0"]), "a0": row(p["a0"]), "k_k": row(p["k_k"]),
        "k_a": row(p["k_a"]), "r_k": row(p["r_k"]), "lnx_g": row(p["lnx_g"]), "lnx_b": row(p["lnx_b"]),
        "wd": lora_pad(p["w_decay_up"][l], 0),
        "wa": lora_pad(p["w_aaa_up"][l], n_dec),
        "wg": lora_pad(p["w_gate_up"][l], n_dec + n_aaa),
    }
    head_id = jnp.arange(MXU_DIM) // HEAD
    e_mat = (head_id[:, None] == head_id[None, :]).astype(BF16)

    lb_re, lb_im, bb_re_t, bb_im_t = _s5_discretise(
        p["s5_lam_re"][l], p["s5_lam_im"][l], p["s5_log_dt"][l],
        jnp.swapaxes(p["s5_b_re"][l], 1, 2), jnp.swapaxes(p["s5_b_im"][l], 1, 2))
    n_state = lb_re.shape[0] * lb_re.shape[2]
    to_out = lambda cc: _block_diag_slabs(jnp.swapaxes(cc, 1, 2))
    wb = jnp.concatenate([_block_diag_slabs(bb_re_t), _block_diag_slabs(bb_im_t)], axis=-1).astype(BF16)
    wc = jnp.concatenate([to_out(p["s5_c_re"][l]), -to_out(p["s5_c_im"][l])], axis=1).astype(BF16)

    bf = lambda a: a[l].astype(BF16)
    tail = {
        "wro": bf(p["w_rwkv_out"]), "w1": bf(p["glu_w1"]), "b1": row(p["glu_b1"]), "w2": bf(p["glu_w2"]),
        "b2": row(p["glu_b2"]), "wmo": bf(p["w_merge_out"]), "npm": row(p["norm_post_mix"]),
        "nf": row(p["norm_pre_ffn"]), "npf": row(p["norm_post_ffn"]),
        "wg": bf(p["w_ffn_gate"]), "wu": bf(p["w_ffn_up"]), "wd": bf(p["w_ffn_down"]),
    }
    return {
        "norm_pre_mix": row(p["norm_pre_mix"]), "w_in": bf(p["w_in"]), "rwkv": rwkv, "e_mat": e_mat,
        "lb_re": lb_re.reshape(1, n_state), "lb_im": lb_im.reshape(1, n_state), "wb": wb, "wc": wc,
        "d_skip": row(p["s5_d"]), "tail": tail,
    }


_PARAM_NAMES = ("norm_pre_mix", "norm_post_mix", "norm_pre_ffn", "norm_post_ffn", "w_in", "mu_shift",
                "w0", "w_decay_up", "a0", "w_aaa_up", "w_gate_up", "k_k", "k_a", "r_k", "lnx_g", "lnx_b",
                "w_rwkv_out", "s5_lam_re", "s5_lam_im", "s5_log_dt", "s5_b_re", "s5_b_im", "s5_c_re",
                "s5_c_im", "s5_d", "glu_w1", "glu_b1", "glu_w2", "glu_b2", "w_merge_out",
                "w_ffn_gate", "w_ffn_up", "w_ffn_down")


def _forward(x_prompt, x_sample, state_shift, state_wkv, state_s5_re, state_s5_im, params,
             *, chunk=64, n_seq=4, s5_tt=128, row_tile=256):
    depth = params["w_in"].shape[0]
    yp, ys = x_prompt, x_sample
    outs_p, outs_s = [], []
    for l in range(depth):
        lw = _prepare_layer_weights(l, params)
        (yp, *st_p), (ys, *st_s) = _layer(yp, ys, state_shift[l], state_wkv[l], state_s5_re[l], state_s5_im[l], lw,
                                          chunk=chunk, n_seq=n_seq, s5_tt=s5_tt, row_tile=row_tile)
        outs_p.append(st_p)
        outs_s.append(st_s)
    stack = lambda outs, i, dt: jnp.stack([o[i] for o in outs]).astype(dt)
    dt_p, dt_s = x_prompt.dtype, x_sample.dtype
    return (yp, ys,
            stack(outs_p, 0, dt_p), stack(outs_p, 1, dt_p), stack(outs_p, 2, dt_p), stack(outs_p, 3, dt_p),
            stack(outs_s, 0, dt_s), stack(outs_s, 1, dt_s), stack(outs_s, 2, dt_s), stack(outs_s, 3, dt_s))


def kernel(x_prompt, x_sample, state_shift, state_wkv, state_s5_re, state_s5_im, norm_pre_mix, norm_post_mix, norm_pre_ffn, norm_post_ffn, w_in, mu_shift, w0, w_decay_up, a0, w_aaa_up, w_gate_up, k_k, k_a, r_k, lnx_g, lnx_b, w_rwkv_out, s5_lam_re, s5_lam_im, s5_log_dt, s5_b_re, s5_b_im, s5_c_re, s5_c_im, s5_d, glu_w1, glu_b1, glu_w2, glu_b2, w_merge_out, w_ffn_gate, w_ffn_up, w_ffn_down):
    params = dict(zip(_PARAM_NAMES, (norm_pre_mix, norm_post_mix, norm_pre_ffn, norm_post_ffn, w_in, mu_shift,
                                     w0, w_decay_up, a0, w_aaa_up, w_gate_up, k_k, k_a, r_k, lnx_g, lnx_b,
                                     w_rwkv_out, s5_lam_re, s5_lam_im, s5_log_dt, s5_b_re, s5_b_im, s5_c_re,
                                     s5_c_im, s5_d, glu_w1, glu_b1, glu_w2, glu_b2, w_merge_out,
                                     w_ffn_gate, w_ffn_up, w_ffn_down)))
    return _forward(x_prompt, x_sample, state_shift, state_wkv, state_s5_re, state_s5_im, params)
```

```python
import functools
import math

import jax
import jax.numpy as jnp
from jax import lax
from jax.experimental import pallas as pl
from jax.experimental.pallas import tpu as pltpu

F32 = jnp.float32
BF16 = jnp.bfloat16

NORM_EPS = 1e-6
LNX_EPS = 64e-5
HEAD = 64
GROUP_HEADS = 2
GROUP_W = GROUP_HEADS * HEAD
LORA_PAD = 128
S5_GROUP = 16
S5_STATE = 64
SLAB_GROUPS = 8
SUBLANES = 8
LANES = 128
MXU_DIM = 256
S5_ROW_BLOCK = 256
VMEM_LIMIT =56 * 1024 * 1024

NN = (((1,), (0,)), ((), ()))
NT = (((1,), (1,)), ((), ()))


def _dot(a, b, dims=NN):
    return lax.dot_general(a, b, dims, preferred_element_type=F32)


def _split2(x):
    hi = x.astype(BF16)
    lo = (x - hi.astype(F32)).astype(BF16)
    return hi, lo


def _split3(x):
    hi = x.astype(BF16)
    r1 = x - hi.astype(F32)
    mid = r1.astype(BF16)
    lo = (r1 - mid.astype(F32)).astype(BF16)
    return hi, mid, lo


def _mm1(a, b, dims=NN):
    return _dot(a.astype(BF16), b.astype(BF16), dims)


def _mm_exact_lhs(a_bf16, b):
    h, m, l = _split3(b)
    return _dot(a_bf16, h) + (_dot(a_bf16, m) + _dot(a_bf16, l))


def _rms(x, g):
    return x * lax.rsqrt(jnp.mean(x * x, axis=-1, keepdims=True) + NORM_EPS) * g


def _sigmoid(x):
    return 1.0 / (1.0 + jnp.exp(-x))


def _gelu_tanh(x):
    c = math.sqrt(2.0 / math.pi)
    return 0.5 * x * (1.0 + jnp.tanh(c * (x + 0.044715 * (x * x * x))))


def _const_spec(shape, single_buffer=False):
    idx = lambda *_: (0,) * len(shape)
    if single_buffer:
        return pl.BlockSpec(shape, idx, pipeline_mode=pl.Buffered(1))
    return pl.BlockSpec(shape, idx)


def _two_group_grid(rows_main, tm):
    n_main = rows_main // tm
    main_spec = lambda width: pl.BlockSpec((tm, width), lambda i: (jnp.minimum(i, n_main - 1), 0))
    return n_main, main_spec


def _on_group(body, main_refs, side_refs):
    i, n_main = pl.program_id(0), pl.num_programs(0) - 1
    pl.when(i < n_main)(functools.partial(body, *main_refs))
    pl.when(i == n_main)(functools.partial(body, *side_refs))


def _proj_kernel(x_ref, xs_ref, g_ref, w_ref, pr_ref, u_ref, gt_ref, prs_ref, us_ref, gts_ref, *, c_shift, c_u):
    def project(x_ref, pr_ref, u_ref, gt_ref):
        hb = _rms(x_ref[...], g_ref[...]).astype(BF16)
        pr_ref[...] = _dot(hb, w_ref[:, :c_shift])
        u_ref[...] = _dot(hb, w_ref[:, c_shift:c_shift + c_u])
        gt_ref[...] = _sigmoid(_dot(hb, w_ref[:, c_shift + c_u:])).astype(gt_ref.dtype)

    _on_group(project, (x_ref, pr_ref, u_ref, gt_ref), (xs_ref, prs_ref, us_ref, gts_ref))


def _proj(x_main, x_side, g, w_in_bf16, c_shift, c_u, tm):
    d = x_main.shape[1]
    cols = w_in_bf16.shape[1]
    widths = (c_shift, c_u, cols - c_shift - c_u)
    n_main, main_spec = _two_group_grid(x_main.shape[0], tm)
    side_spec = lambda width: _const_spec((x_side.shape[0], width))
    outs = pl.pallas_call(
        functools.partial(_proj_kernel, c_shift=c_shift, c_u=c_u),
        grid=(n_main + 1,),
        in_specs=[main_spec(d), side_spec(d), _const_spec((1, d)), _const_spec((d, cols), True)],
        out_specs=[main_spec(wd) for wd in widths] + [side_spec(wd) for wd in widths],
        out_shape=[jax.ShapeDtypeStruct((x.shape[0], wd), dt) for x in (x_main, x_side)
                   for wd, dt in zip(widths, (F32, F32, BF16))],
        compiler_params=pltpu.CompilerParams(dimension_semantics=("arbitrary",),
                                             vmem_limit_bytes=VMEM_LIMIT),
        name="proj",
    )(x_main, x_side, g, w_in_bf16)
    return outs[:3], outs[3:]


def _head_sum(x, e_ref):
    rows, width = x.shape
    gw = e_ref.shape[0]
    n_lg = width // gw
    hi, lo = _split2(x)
    stacked = jnp.concatenate([part[:, j * gw:(j + 1) * gw] for part in (hi, lo) for j in range(n_lg)], axis=0)
    sums = _dot(stacked, e_ref[...])
    return jnp.concatenate([sums[j * rows:(j + 1) * rows] + sums[(n_lg + j) * rows:(n_lg + j + 1) * rows]
                            for j in range(n_lg)], axis=1)


def _rwkv_token_prep(xr, w, e_ref):
    width = w["w0"].shape[-1]
    r = xr[:, :width]
    k = xr[:, width:2 * width]
    v = xr[:, 2 * width:3 * width]
    lo = xr[:, 3 * width:3 * width + LORA_PAD]
    wl = w["w0"][...] + _mm1(jnp.tanh(lo), w["wd"][...])
    lw = -math.exp(-0.5) * _sigmoid(wl)
    a = _sigmoid(w["a0"][...] + _mm1(lo, w["wa"][...]))
    g = _mm1(_sigmoid(lo), w["wg"][...])
    kk = k * w["k_k"][...]
    kk = kk * lax.rsqrt(jnp.maximum(_head_sum(kk * kk, e_ref), 1e-24))
    kmod = k * (1.0 + (a - 1.0) * w["k_a"][...])
    return r, kmod, v, kk, kk * a, lw, g


def _rwkv_bonus(r, kmod, v, w, e_ref):
    return _head_sum(r * kmod * w["r_k"][...], e_ref) * v


def _rwkv_post(o, bonus, g, w, e_ref):
    inv_n = 1.0 / HEAD
    mu = _head_sum(o, e_ref) * inv_n
    oc = o - mu
    var = _head_sum(oc * oc, e_ref) * inv_n
    on = oc * lax.rsqrt(var + LNX_EPS) * w["lnx_g"][...] + w["lnx_b"][...]
    return (on + bonus) * g


_RWKV_W_NAMES = ("mu", "w0", "a0", "k_k", "k_a", "r_k", "lnx_g", "lnx_b", "wd", "wa", "wg")


def _rwkv_chunk_kernel(pr_ref, shift0_ref, s0_ref, *rest, chunk, width, n_seq):
    n_w = len(_RWKV_W_NAMES)
    w = dict(zip(_RWKV_W_NAMES, rest[:n_w]))
    e_ref, tri_ref = rest[n_w], rest[n_w + 1]
    oa_ref, s1_ref = rest[n_w + 2], rest[n_w + 3]
    carry_ref, z_ref, ops_ref, aux_ref, gall_ref = rest[n_w + 4:n_w + 9]
    step = pl.program_id(1)
    n_chunks = pl.num_programs(1) - 1
    n_groups = width // GROUP_W
    C = chunk
    GC = GROUP_HEADS * C

    lane_head = lax.broadcasted_iota(jnp.int32, (1, GROUP_W), 1) // HEAD
    head_masks = [lane_head == h for h in range(GROUP_HEADS)]

    def stack(x):
        xb = x.astype(BF16)
        zero = jnp.zeros_like(xb)
        return jnp.concatenate([jnp.where(m, xb, zero) for m in head_masks], axis=0)

    def collapse(xs):
        out = xs[:C]
        for h in range(1, GROUP_HEADS):
            out = out + xs[h * C:(h + 1) * C]
        return out

    @pl.when(step == 0)
    def _():
        carry_ref[...] = shift0_ref[...]
        ops_ref[1] = jnp.zeros(ops_ref.shape[1:], ops_ref.dtype)
        aux_ref[1] = jnp.zeros(aux_ref.shape[1:], aux_ref.dtype)
        gall_ref[1] = jnp.zeros(gall_ref.shape[1:], gall_ref.dtype)
        z_ref[...] = jnp.zeros(z_ref.shape, z_ref.dtype)

    @pl.when(step == 1)
    def _():
        for q in range(n_seq):
            for gi in range(n_groups):
                z_ref[q, gi] = jnp.zeros((GROUP_W, GROUP_W), F32)
                for h in range(GROUP_HEADS):
                    z_ref[q, gi, h * HEAD:(h + 1) * HEAD, h * HEAD:(h + 1) * HEAD] = s0_ref[q, gi * GROUP_HEADS + h]

    ri = lax.broadcasted_iota(jnp.int32, (GC, GC), 0)
    ci = lax.broadcasted_iota(jnp.int32, (GC, GC), 1)
    same_head = (ri // C) == (ci // C)
    strict = same_head & (ri > ci)
    incl = same_head & (ri >= ci)
    eye = jnp.where(ri == ci, 1.0, 0.0).astype(F32)
    zi = lax.broadcasted_iota(jnp.int32, (GROUP_W, GROUP_W), 0) // HEAD
    zj = lax.broadcasted_iota(jnp.int32, (GROUP_W, GROUP_W), 1) // HEAD
    block_diag = zi == zj
    bf = lambda x: x.astype(BF16)

    def recurrence(r_slot):
        chains = [(q, gi) for q in range(n_seq) for gi in range(n_groups)]
        op = lambda i, q, gi: ops_ref[r_slot, i, q * C:(q + 1) * C, gi * GROUP_W:(gi + 1) * GROUP_W]
        kap_t, r_t, v = ([op(i, q, gi) for q, gi in chains] for i in (0, 1, 4))
        kap_s = [stack(x) for x in kap_t]
        v_s = [stack(x) for x in v]
        grams = []
        for i, (q, gi) in enumerate(chains):
            left = jnp.concatenate([kap_s[i], stack(r_t[i])], axis=0)
            right = jnp.concatenate([op(2, q, gi)] * GROUP_HEADS + [op(3, q, gi)] * GROUP_HEADS, axis=0)
            grams.append(_dot(left, right, NT))
        yield
        a_k = [jnp.where(strict, gm[:GC, :GC], 0.0) for gm in grams]
        a_b = [jnp.where(strict, gm[:GC, GC:], 0.0) for gm in grams]
        a_rk = [jnp.where(incl, gm[GC:, :GC], 0.0) for gm in grams]
        a_rb = [jnp.where(incl, gm[GC:, GC:], 0.0) for gm in grams]

        ps = [-a for a in a_b]
        ts = [eye + p for p in ps]
        covered = 2
        while covered < C:
            ps = [_dot(bf(p), bf(p)) for p in ps]
            ts = [t + _dot(bf(t), bf(p)) for t, p in zip(ts, ps)]
            covered *= 2
            yield

        akv = [_dot(bf(a), vs) for a, vs in zip(a_k, v_s)]
        yield
        tw = [_dot(bf(t), jnp.concatenate([ks, bf(x)], axis=1)) for t, ks, x in zip(ts, kap_s, akv)]
        k_hat = [collapse(x[:, :GROUP_W]) for x in tw]
        v_hat = [collapse(x[:, GROUP_W:]) for x in tw]
        arkv = [collapse(_dot(bf(a), vs)) for a, vs in zip(a_rk, v_s)]
        yield
        zs = [z_ref[q, gi] for q, gi in chains]
        pz = [_dot(jnp.concatenate([bf(kh), rt], axis=0), bf(z), NT)
              for kh, rt, z in zip(k_hat, r_t, zs)]
        us = [p[:C] + vh for p, vh in zip(pz, v_hat)]
        yield
        arbu = [collapse(_dot(bf(a), stack(u))) for a, u in zip(a_rb, us)]
        o_blk = [p[C:] + x - y for p, x, y in zip(pz, arkv, arbu)]
        for i, (q, gi) in enumerate(chains):
            vu_t = jnp.concatenate([v[i].astype(F32), us[i]], axis=0).T.astype(BF16)
            kb = jnp.concatenate([op(5, q, gi), op(6, q, gi)], axis=0)
            g_all = gall_ref[r_slot, q][:, gi * GROUP_W:(gi + 1) * GROUP_W]
            z_ref[q, gi] = zs[i] * g_all + jnp.where(block_diag, _dot(vu_t, kb), 0.0)
        yield
        o = jnp.concatenate([jnp.concatenate(o_blk[q * n_groups:(q + 1) * n_groups], axis=1)
                             for q in range(n_seq)], axis=0)
        oa_ref[...] = _rwkv_post(o, aux_ref[r_slot, 0], aux_ref[r_slot, 1], w, e_ref
                                 ).astype(oa_ref.dtype).reshape(oa_ref.shape)

    def prepare(q, w_slot):
        rows = slice(q * C, (q + 1) * C)
        pr = pr_ref[q]
        row_id = lax.broadcasted_iota(jnp.int32, pr.shape, 0)
        pr_prev = jnp.where(row_id == 0, carry_ref[q], pltpu.roll(pr, 1, axis=0))
        carry_ref[q] = pr[C - 1:C, :]
        xr = pr + (pr_prev - pr) * w["mu"][...]
        r, kmod, vv, kk, bvec, lw, g = _rwkv_token_prep(xr, w, e_ref)
        aux_ref[w_slot, 0, rows] = _rwkv_bonus(r, kmod, vv, w, e_ref)
        aux_ref[w_slot, 1, rows] = g
        yield
        cum = _mm_exact_lhs(tri_ref[...], lw)
        cum_last = cum[C - 1:C, :]
        g_neg = jnp.exp(-cum)
        g_end = jnp.exp(cum_last - cum)
        prepared = (kk * jnp.exp(cum - lw),
                    r * jnp.exp(cum),
                    kmod * g_neg, bvec * g_neg,
                    vv, kmod * g_end, -(bvec * g_end))
        for i, x in enumerate(prepared):
            ops_ref[w_slot, i, rows] = x.astype(ops_ref.dtype)
        gall_ref[w_slot, q] = jnp.exp(cum_last)

    w_slot = step % 2
    pending = [gen for gen in [prepare(q, w_slot) for q in range(n_seq)] for _ in range(2)]
    for _ in recurrence(1 - w_slot):
        if pending:
            next(pending.pop(0), None)
    for gen in pending:
        next(gen, None)

    @pl.when(step == n_chunks)
    def _():
        for q in range(n_seq):
            for gi in range(n_groups):
                for h in range(GROUP_HEADS):
                    s1_ref[q, gi * GROUP_HEADS + h] = z_ref[q, gi, h * HEAD:(h + 1) * HEAD, h * HEAD:(h + 1) * HEAD]


_N_OPS = 7


def _rwkv_weight_inputs(wts):
    return [wts[n] for n in _RWKV_W_NAMES]


def _rwkv_weight_specs(wts):
    return [_const_spec(wts[n].shape) for n in _RWKV_W_NAMES]


def _rwkv_chunked(pr3d, shift0, wkv0, wts, e_mat, chunk, n_seq):
    bsz, t, c_shift = pr3d.shape
    heads = wkv0.shape[1]
    width = heads * HEAD
    tri = jnp.tril(jnp.ones((chunk, chunk), F32)).astype(BF16)
    kern = functools.partial(_rwkv_chunk_kernel, chunk=chunk, width=width, n_seq=n_seq)
    n_chunks = t // chunk
    rows = n_seq * chunk
    return pl.pallas_call(
        kern,
        grid=(bsz // n_seq, n_chunks + 1),
        in_specs=[pl.BlockSpec((n_seq, chunk, c_shift), lambda b, s: (b, jnp.minimum(s, n_chunks - 1), 0)),
                  pl.BlockSpec((n_seq, 1, c_shift), lambda b, s: (b, 0, 0)),
                  pl.BlockSpec((n_seq, heads, HEAD, HEAD), lambda b, s: (b, 0, 0, 0))]
                 + _rwkv_weight_specs(wts)
                 + [_const_spec(e_mat.shape), _const_spec(tri.shape)],
        out_specs=[pl.BlockSpec((n_seq, chunk, width), lambda b, s: (b, jnp.maximum(s - 1, 0), 0)),
                   pl.BlockSpec((n_seq, heads, HEAD, HEAD), lambda b, s: (b, 0, 0, 0))],
        out_shape=[jax.ShapeDtypeStruct((bsz, t, width), BF16),
                   jax.ShapeDtypeStruct((bsz, heads, HEAD, HEAD), F32)],
        scratch_shapes=[pltpu.VMEM((n_seq, 1, c_shift), F32),
                        pltpu.VMEM((n_seq, width // GROUP_W, GROUP_W, GROUP_W), F32),
                        pltpu.VMEM((2, _N_OPS, rows, width), BF16),
                        pltpu.VMEM((2, 2, rows, width), F32),
                        pltpu.VMEM((2, n_seq, 1, width), F32)],
        compiler_params=pltpu.CompilerParams(dimension_semantics=("parallel", "arbitrary"),
                                             vmem_limit_bytes=VMEM_LIMIT),
        name="rwkv_chunk",
    )(pr3d, shift0[:, None, :], wkv0, *_rwkv_weight_inputs(wts), e_mat, tri)


_STEP_VECS = 6


def _rwkv_step_kernel(pr_ref, shift0_ref, s_ref, *rest):
    n_w = len(_RWKV_W_NAMES)
    w = dict(zip(_RWKV_W_NAMES, rest[:n_w]))
    e_ref, oa_ref, s1_ref, vec_ref, post_ref, o_ref = rest[n_w:n_w + 6]
    h = pl.program_id(0)

    @pl.when(h == 0)
    def _():
        pr = pr_ref[...]
        xr = pr + (shift0_ref[...] - pr) * w["mu"][...]
        r, kmod, v, kk, bvec, lw, g = _rwkv_token_prep(xr, w, e_ref)
        for i, x in enumerate((kk, bvec, jnp.exp(lw), kmod, r, v)):
            vec_ref[i] = x.T
        post_ref[0] = _rwkv_bonus(r, kmod, v, w, e_ref)
        post_ref[1] = g

    base = pl.multiple_of(h * HEAD, HEAD)
    kk_h, b_h, dec_h, k_h, r_h = [vec_ref[i, pl.ds(base, HEAD), :] for i in range(5)]

    def value_row(vi, carry):
        tile = s_ref[0, vi]
        s_kappa = jnp.sum(tile * kk_h, axis=0, keepdims=True)
        new = tile * dec_h - s_kappa * b_h + vec_ref[5, pl.ds(base + vi, 1), :] * k_h
        s1_ref[0, vi] = new
        o_ref[pl.ds(base + vi, 1), :] = jnp.sum(new * r_h, axis=0, keepdims=True)
        return carry

    lax.fori_loop(0, HEAD, value_row, 0, unroll=4)

    @pl.when(h == pl.num_programs(0) - 1)
    def _():
        oa_ref[...] = _rwkv_post(o_ref[...].T, post_ref[0], post_ref[1], w, e_ref).astype(oa_ref.dtype)


def _rwkv_step(pr2d, shift0, wkv0, wts, e_mat):
    bsz, c_shift = pr2d.shape
    heads = wkv0.shape[1]
    width = heads * HEAD
    full = lambda a: _const_spec(a.shape)
    st = pl.BlockSpec((1, HEAD, HEAD, bsz), lambda hh: (hh, 0, 0, 0))
    oa, s1_t = pl.pallas_call(
        _rwkv_step_kernel,
        grid=(heads,),
        in_specs=[full(pr2d), full(shift0), st] + _rwkv_weight_specs(wts) + [full(e_mat)],
        out_specs=[_const_spec((bsz, width)), st],
        out_shape=[jax.ShapeDtypeStruct((bsz, width), BF16),
                   jax.ShapeDtypeStruct((heads, HEAD, HEAD, bsz), F32)],
        scratch_shapes=[pltpu.VMEM((_STEP_VECS, width, bsz), F32),
                        pltpu.VMEM((2, bsz, width), F32),
                        pltpu.VMEM((width, bsz), F32)],
        compiler_params=pltpu.CompilerParams(dimension_semantics=("arbitrary",), vmem_limit_bytes=VMEM_LIMIT),
        name="rwkv_step",
    )(pr2d, shift0, jnp.transpose(wkv0, (1, 2, 3, 0)), *_rwkv_weight_inputs(wts), e_mat)
    return oa, jnp.transpose(s1_t, (3, 0, 1, 2))


def _s5_disc_kernel(lre_ref, lim_ref, ldt_ref, bre_ref, bim_ref, lbr_ref, lbi_ref, bbr_ref, bbi_ref):
    lam_re, lam_im = lre_ref[...], lim_ref[...]
    dt = jnp.exp(ldt_ref[...])
    mag = jnp.exp(lam_re * dt)
    ang = lam_im * dt
    lb_re, lb_im = mag * jnp.cos(ang), mag * jnp.sin(ang)
    nr, ni = lb_re - 1.0, lb_im
    den = lam_re * lam_re + lam_im * lam_im
    f_re = (nr * lam_re + ni * lam_im) / den
    f_im = (ni * lam_re - nr * lam_im) / den
    b_re, b_im = bre_ref[...], bim_ref[...]
    lbr_ref[...] = lb_re
    lbi_ref[...] = lb_im
    bbr_ref[...] = f_re * b_re - f_im * b_im
    bbi_ref[...] = f_re * b_im + f_im * b_re


def _s5_discretise(lam_re, lam_im, log_dt, b_re_t, b_im_t):
    g, p = lam_re.shape
    full = lambda a: _const_spec(a.shape)
    args = (lam_re[:, None, :], lam_im[:, None, :], log_dt[:, None, None], b_re_t, b_im_t)
    return pl.pallas_call(
        _s5_disc_kernel,
        grid=(1,),
        in_specs=[full(a) for a in args],
        out_specs=[_const_spec((g, 1, p))] * 2 + [full(b_re_t)] * 2,
        out_shape=[jax.ShapeDtypeStruct((g, 1, p), F32)] * 2 + [jax.ShapeDtypeStruct(b_re_t.shape, F32)] * 2,
        name="s5_discretise",
    )(*args)


def _s5_kernel(u_ref, re0_ref, im0_ref, lbr_ref, lbi_ref, wb_ref, wc_ref, d_ref,
               h_ref, re1_ref, im1_ref, u_tm, h_tm, bu0, bu1, xs0, xs1, *, tt, n_slabs):
    t_blk = pl.program_id(1)
    rows = SUBLANES * tt
    s_w = SLAB_GROUPS * S5_STATE
    u_w = SLAB_GROUPS * S5_GROUP
    bu, xs = (bu0, bu1), (xs0, xs1)
    rb = min(rows, S5_ROW_BLOCK)
    steps_per_rb = rb // SUBLANES

    @pl.when(t_blk == 0)
    def _():
        re1_ref[...] = re0_ref[...]
        im1_ref[...] = im0_ref[...]

    u_tm[...] = jnp.swapaxes(u_ref[...], 0, 1).reshape(rows, n_slabs * u_w)

    def project_in(s, j):
        r0 = j * rb
        bu[s % 2][r0:r0 + rb, :] = _mm1(u_tm[r0:r0 + rb, s * u_w:(s + 1) * u_w], wb_ref[s])

    def project_out(s, j):
        r0 = j * rb
        lanes = slice(s * u_w, (s + 1) * u_w)
        y = _mm1(xs[s % 2][r0:r0 + rb, :], wc_ref[s]) + d_ref[:, lanes] * u_tm[r0:r0 + rb, lanes]
        h_tm[r0:r0 + rb, lanes] = _gelu_tanh(y)

    for j in range(rows // rb):
        project_in(0, j)
    for p in range(n_slabs + 1):
        if p < n_slabs:
            st = slice(p * s_w, (p + 1) * s_w)
            lbr = jnp.broadcast_to(lbr_ref[:, st], (SUBLANES, s_w))
            lbi = jnp.broadcast_to(lbi_ref[:, st], (SUBLANES, s_w))
            xr, xi = re1_ref[:, st], im1_ref[:, st]
        for j in range(rows // rb):
            if p < n_slabs:
                for t in range(j * steps_per_rb, (j + 1) * steps_per_rb):
                    at_t = slice(t * SUBLANES, (t + 1) * SUBLANES)
                    xr, xi = (lbr * xr - lbi * xi + bu[p % 2][at_t, :s_w],
                              lbr * xi + lbi * xr + bu[p % 2][at_t, s_w:])
                    xs[p % 2][at_t, :s_w] = xr
                    xs[p % 2][at_t, s_w:] = xi
            if p + 1 < n_slabs:
                project_in(p + 1, j)
            if p >= 1:
                project_out(p - 1, j)
        if p < n_slabs:
            re1_ref[:, st] = xr
            im1_ref[:, st] = xi
    h = h_tm[...].reshape(tt, SUBLANES, n_slabs * u_w)
    h_ref[...] = jnp.swapaxes(h, 0, 1).astype(h_ref.dtype)


def _s5(u_blocks, re0, im0, lb_re, lb_im, wb, wc, d_skip, tt):
    n_slabs = wb.shape[0]
    n_state = re0.shape[1]
    blk = (SUBLANES, tt, u_blocks.shape[-1])
    grid = (u_blocks.shape[0] // SUBLANES, u_blocks.shape[1] // tt)
    rows = SUBLANES * tt
    u_spec = pl.BlockSpec(blk, lambda i, j: (i, j, 0))
    st_spec = pl.BlockSpec((SUBLANES, n_state), lambda i, j: (i, 0))
    full = lambda a: _const_spec(a.shape)
    return pl.pallas_call(
        functools.partial(_s5_kernel, tt=tt, n_slabs=n_slabs),
        grid=grid,
        in_specs=[u_spec, st_spec, st_spec, full(lb_re), full(lb_im), full(wb), full(wc), full(d_skip)],
        out_specs=[u_spec, st_spec, st_spec],
        out_shape=[jax.ShapeDtypeStruct(u_blocks.shape, BF16),
                   jax.ShapeDtypeStruct(re0.shape, F32), jax.ShapeDtypeStruct(im0.shape, F32)],
        scratch_shapes=[pltpu.VMEM((rows, u_blocks.shape[-1]), F32)] * 2
                       + [pltpu.VMEM((rows, 2 * SLAB_GROUPS * S5_STATE), F32)] * 4,
        compiler_params=pltpu.CompilerParams(dimension_semantics=("parallel", "arbitrary"),
                                             vmem_limit_bytes=VMEM_LIMIT),
        name="s5_scan",
    )(u_blocks, re0, im0, lb_re, lb_im, wb, wc, d_skip)


def _s5_step_kernel(u_ref, re0_ref, im0_ref, lbr_ref, lbi_ref, wb_ref, wc_ref, d_ref,
                    h_ref, re1_ref, im1_ref, *, n_slabs):
    s_w = SLAB_GROUPS * S5_STATE
    u_w = SLAB_GROUPS * S5_GROUP
    u = u_ref[...]
    ys = []
    for s in range(n_slabs):
        us = u[:, s * u_w:(s + 1) * u_w]
        bu = _mm1(us, wb_ref[s])
        st = slice(s * s_w, (s + 1) * s_w)
        lbr, lbi = lbr_ref[:, st], lbi_ref[:, st]
        xr, xi = re0_ref[:, st], im0_ref[:, st]
        nr = lbr * xr - lbi * xi + bu[:, :s_w]
        ni = lbr * xi + lbi * xr + bu[:, s_w:]
        re1_ref[:, st] = nr
        im1_ref[:, st] = ni
        y = _mm1(jnp.concatenate([nr, ni], axis=1), wc_ref[s]) + d_ref[:, s * u_w:(s + 1) * u_w] * us
        ys.append(_gelu_tanh(y))
    h_ref[...] = jnp.concatenate(ys, axis=1).astype(h_ref.dtype)


def _s5_step(u2d, re0, im0, lb_re, lb_im, wb, wc, d_skip):
    full = lambda a: _const_spec(a.shape)
    args = (u2d, re0, im0, lb_re, lb_im, wb, wc, d_skip)
    return pl.pallas_call(
        functools.partial(_s5_step_kernel, n_slabs=wb.shape[0]),
        grid=(1,),
        in_specs=[full(a) for a in args],
        out_specs=[full(u2d), full(re0), full(im0)],
        out_shape=[jax.ShapeDtypeStruct(u2d.shape, BF16),
                   jax.ShapeDtypeStruct(re0.shape, F32), jax.ShapeDtypeStruct(im0.shape, F32)],
        compiler_params=pltpu.CompilerParams(vmem_limit_bytes=VMEM_LIMIT),
        name="s5_step",
    )(*args)


def _block_diag_slabs(m):
    g, a, b = m.shape
    eye = jnp.eye(SLAB_GROUPS, dtype=m.dtype)
    m4 = m.reshape(g // SLAB_GROUPS, SLAB_GROUPS, a, b)
    return jnp.einsum("sgab,gh->sgahb", m4, eye).reshape(g // SLAB_GROUPS, SLAB_GROUPS * a, SLAB_GROUPS * b)


_TAIL_W_NAMES = ("wro", "w1", "b1", "w2", "b2", "wmo", "npm", "nf", "npf", "wg", "wu", "wd")


def _tail_kernel(*refs):
    n_act = 4
    main_in, side_in = refs[:n_act], refs[n_act:2 * n_act]
    wts = dict(zip(_TAIL_W_NAMES, refs[2 * n_act:2 * n_act + len(_TAIL_W_NAMES)]))
    y_ref, ys_ref = refs[-2:]

    def tail(x_ref, oa_ref, hg_ref, gt_ref, y_ref):
        d = x_ref.shape[-1]
        tm = x_ref.shape[0]
        n_sub = 2 if tm % 32 == 0 else 1
        subs = [slice(i * (tm // n_sub), (i + 1) * (tm // n_sub)) for i in range(n_sub)]
        hg = [hg_ref[s, :] for s in subs]
        a_out = [_dot(oa_ref[s, :], wts["wro"][...]) for s in subs]
        b_lin = [_dot(h, wts["w1"][...]) + wts["b1"][...] for h in hg]
        b_gate = [_dot(h, wts["w2"][...]) + wts["b2"][...] for h in hg]
        merged = [(gt_ref[s, :d].astype(F32) * a + gt_ref[s, d:].astype(F32) * (bl * _sigmoid(bg))).astype(BF16)
                  for s, a, bl, bg in zip(subs, a_out, b_lin, b_gate)]
        mix = [_dot(m, wts["wmo"][...]) for m in merged]
        x1 = [x_ref[s, :] + _rms(m, wts["npm"][...]) for s, m in zip(subs, mix)]
        hb = [_rms(x, wts["nf"][...]).astype(BF16) for x in x1]
        gate = [_dot(h, wts["wg"][...]) for h in hb]
        up = [_dot(h, wts["wu"][...]) for h in hb]
        act = [(g * _sigmoid(g) * u).astype(BF16) for g, u in zip(gate, up)]
        f = [_dot(a, wts["wd"][...]) for a in act]
        for s, x, ff in zip(subs, x1, f):
            y_ref[s, :] = x + _rms(ff, wts["npf"][...])

    _on_group(tail, (*main_in, y_ref), (*side_in, ys_ref))


def _tail(acts_main, acts_side, tw, tm):
    d = acts_main[0].shape[1]
    n_main, main_spec = _two_group_grid(acts_main[0].shape[0], tm)
    wargs = [tw[n] for n in _TAIL_W_NAMES]
    return pl.pallas_call(
        _tail_kernel,
        grid=(n_main + 1,),
        in_specs=[main_spec(a.shape[1]) for a in acts_main] + [_const_spec(a.shape) for a in acts_side]
                 + [_const_spec(a.shape, True) for a in wargs],
        out_specs=[main_spec(d), _const_spec((acts_side[0].shape[0], d))],
        out_shape=[jax.ShapeDtypeStruct((a[0].shape[0], d), F32) for a in (acts_main, acts_side)],
        compiler_params=pltpu.CompilerParams(dimension_semantics=("arbitrary",),
                                             vmem_limit_bytes=VMEM_LIMIT),
        name="tail",
    )(*acts_main, *acts_side, *wargs)


def _layer(x_p, x_s, shift0, wkv0, re0, im0, lw, *, chunk, n_seq, s5_tt, row_tile):
    bsz, t, d = x_p.shape
    bs = x_s.shape[0]
    assert x_s.shape[1] == 1 and (bsz * t) % row_tile == 0
    heads = wkv0.shape[1]
    c_shift = shift0.shape[-1]
    c_u = lw["d_skip"].shape[-1]
    n_state = re0.shape[1] * re0.shape[2]
    s5_w = (lw["lb_re"], lw["lb_im"], lw["wb"], lw["wc"], lw["d_skip"])
    xp2d, xs2d = x_p.reshape(bsz * t, d), x_s.reshape(bs, d)
    (pr_p, u_p, gates_p), (pr_s, u_s, gates_s) = _proj(xp2d, xs2d, lw["norm_pre_mix"], lw["w_in"],
                                                       c_shift, c_u, row_tile)

    pr_p3 = pr_p.reshape(bsz, t, c_shift)
    oa_p, wkv_p = _rwkv_chunked(pr_p3, jnp.zeros((bsz, c_shift), F32), jnp.zeros((bsz, heads, HEAD, HEAD), F32),
                                lw["rwkv"], lw["e_mat"], chunk, n_seq)
    zeros_state = jnp.zeros((bsz, n_state), F32)
    hg_p, re_p, im_p = _s5(u_p.reshape(bsz, t, c_u), zeros_state, zeros_state, *s5_w, tt=s5_tt)

    oa_s, wkv_s = _rwkv_step(pr_s, shift0, wkv0, lw["rwkv"], lw["e_mat"])
    hg_s, re_s, im_s = _s5_step(u_s, re0.reshape(bs, n_state), im0.reshape(bs, n_state), *s5_w)

    y_p, y_s = _tail((xp2d, oa_p.reshape(bsz * t, -1), hg_p.reshape(bsz * t, c_u), gates_p),
                     (xs2d, oa_s, hg_s, gates_s), lw["tail"], row_tile)
    st_shape = lambda n: (n,) + re0.shape[1:]
    return ((y_p.reshape(x_p.shape), pr_p3[:, -1], wkv_p, re_p.reshape(st_shape(bsz)), im_p.reshape(st_shape(bsz))),
            (y_s.reshape(x_s.shape), pr_s, wkv_s, re_s.reshape(st_shape(bs)), im_s.reshape(st_shape(bs))))


def _prepare_layer_weights(l, p):
    row = lambda a: a[l][None, :].astype(F32)
    width = p["w0"].shape[-1]
    n_dec, n_aaa, n_gate = p["w_decay_up"].shape[1], p["w_aaa_up"].shape[1], p["w_gate_up"].shape[1]
    assert n_dec + n_aaa + n_gate == LORA_PAD

    def lora_pad(wup, start):
        return jnp.zeros((LORA_PAD, width), F32).at[start:start + wup.shape[0]].set(wup).astype(BF16)

    rwkv = {
        "mu": row(p["mu_shift"]), "w0": row(p["w0"]), "a0": row(p["a0"]), "k_k": row(p["k_k"]),
        "k_a": row(p["k_a"]), "r_k": row(p["r_k"]), "lnx_g": row(p["lnx_g"]), "lnx_b": row(p["lnx_b"]),
        "wd": lora_pad(p["w_decay_up"][l], 0),
        "wa": lora_pad(p["w_aaa_up"][l], n_dec),
        "wg": lora_pad(p["w_gate_up"][l], n_dec + n_aaa),
    }
    head_id = jnp.arange(MXU_DIM) // HEAD
    e_mat = (head_id[:, None] == head_id[None, :]).astype(BF16)

    lb_re, lb_im, bb_re_t, bb_im_t = _s5_discretise(
        p["s5_lam_re"][l], p["s5_lam_im"][l], p["s5_log_dt"][l],
        jnp.swapaxes(p["s5_b_re"][l], 1, 2), jnp.swapaxes(p["s5_b_im"][l], 1, 2))
    n_state = lb_re.shape[0] * lb_re.shape[2]
    to_out = lambda cc: _block_diag_slabs(jnp.swapaxes(cc, 1, 2))
    wb = jnp.concatenate([_block_diag_slabs(bb_re_t), _block_diag_slabs(bb_im_t)], axis=-1).astype(BF16)
    wc = jnp.concatenate([to_out(p["s5_c_re"][l]), -to_out(p["s5_c_im"][l])], axis=1).astype(BF16)

    bf = lambda a: a[l].astype(BF16)
    tail = {
        "wro": bf(p["w_rwkv_out"]), "w1": bf(p["glu_w1"]), "b1": row(p["glu_b1"]), "w2": bf(p["glu_w2"]),
        "b2": row(p["glu_b2"]), "wmo": bf(p["w_merge_out"]), "npm": row(p["norm_post_mix"]),
        "nf": row(p["norm_pre_ffn"]), "npf": row(p["norm_post_ffn"]),
        "wg": bf(p["w_ffn_gate"]), "wu": bf(p["w_ffn_up"]), "wd": bf(p["w_ffn_down"]),
    }
    return {
        "norm_pre_mix": row(p["norm_pre_mix"]), "w_in": bf(p["w_in"]), "rwkv": rwkv, "e_mat": e_mat,
        "lb_re": lb_re.reshape(1, n_state), "lb_im": lb_im.reshape(1, n_state), "wb": wb, "wc": wc,
        "d_skip": row(p["s5_d"]), "tail": tail,
    }


_PARAM_NAMES = ("norm_pre_mix", "norm_post_mix", "norm_pre_ffn", "norm_post_ffn", "w_in", "mu_shift",
                "w0", "w_decay_up", "a0", "w_aaa_up", "w_gate_up", "k_k", "k_a", "r_k", "lnx_g", "lnx_b",
                "w_rwkv_out", "s5_lam_re", "s5_lam_im", "s5_log_dt", "s5_b_re", "s5_b_im", "s5_c_re",
                "s5_c_im", "s5_d", "glu_w1", "glu_b1", "glu_w2", "glu_b2", "w_merge_out",
                "w_ffn_gate", "w_ffn_up", "w_ffn_down")


def _forward(x_prompt, x_sample, state_shift, state_wkv, state_s5_re, state_s5_im, params,
             *, chunk=64, n_seq=4, s5_tt=128, row_tile=512):
    depth = params["w_in"].shape[0]
    yp, ys = x_prompt, x_sample
    outs_p, outs_s = [], []
    for l in range(depth):
        lw = _prepare_layer_weights(l, params)
        (yp, *st_p), (ys, *st_s) = _layer(yp, ys, state_shift[l], state_wkv[l], state_s5_re[l], state_s5_im[l], lw,
                                          chunk=chunk, n_seq=n_seq, s5_tt=s5_tt, row_tile=row_tile)
        outs_p.append(st_p)
        outs_s.append(st_s)
    stack = lambda outs, i, dt: jnp.stack([o[i] for o in outs]).astype(dt)
    dt_p, dt_s = x_prompt.dtype, x_sample.dtype
    return (yp, ys,
            stack(outs_p, 0, dt_p), stack(outs_p, 1, dt_p), stack(outs_p, 2, dt_p), stack(outs_p, 3, dt_p),
            stack(outs_s, 0, dt_s), stack(outs_s, 1, dt_s), stack(outs_s, 2, dt_s), stack(outs_s, 3, dt_s))


def kernel(x_prompt, x_sample, state_shift, state_wkv, state_s5_re, state_s5_im, norm_pre_mix, norm_post_mix, norm_pre_ffn, norm_post_ffn, w_in, mu_shift, w0, w_decay_up, a0, w_aaa_up, w_gate_up, k_k, k_a, r_k, lnx_g, lnx_b, w_rwkv_out, s5_lam_re, s5_lam_im, s5_log_dt, s5_b_re, s5_b_im, s5_c_re, s5_c_im, s5_d, glu_w1, glu_b1, glu_w2, glu_b2, w_merge_out, w_ffn_gate, w_ffn_up, w_ffn_down):
    params = dict(zip(_PARAM_NAMES, (norm_pre_mix, norm_post_mix, norm_pre_ffn, norm_post_ffn, w_in, mu_shift,
                                     w0, w_decay_up, a0, w_aaa_up, w_gate_up, k_k, k_a, r_k, lnx_g, lnx_b,
                                     w_rwkv_out, s5_lam_re, s5_lam_im, s5_log_dt, s5_b_re, s5_b_im, s5_c_re,
                                     s5_c_im, s5_d, glu_w1, glu_b1, glu_w2, glu_b2, w_merge_out,
                                     w_ffn_gate, w_ffn_up, w_ffn_down)))
    return _forward(x_prompt, x_sample, state_shift, state_wkv, state_s5_re, state_s5_im, params)
```

```python
import functools
import math

import jax
import jax.numpy as jnp
from jax import lax
from jax.experimental import pallas as pl
from jax.experimental.pallas import tpu as pltpu

F32 = jnp.float32
BF16 = jnp.bfloat16

NORM_EPS = 1e-6
LNX_EPS = 64e-5
HEAD = 64
GROUP_HEADS = 4
GROUP_W = GROUP_HEADS * HEAD
LORA_PAD = 128
S5_GROUP = 16
S5_STATE = 64
SLAB_GROUPS = 8
SUBLANES = 8
LANES = 128
MXU_DIM = 256
S5_ROW_BLOCK = 256
SUB_ROWS = 128
TAIL_SUB_ROWS = 256
VMEM_LIMIT =56 * 1024 * 1024

NN = (((1,), (0,)), ((), ()))
NT = (((1,), (1,)), ((), ()))


def _dot(a, b, dims=NN):
    return lax.dot_general(a, b, dims, preferred_element_type=F32)


def _split2(x):
    hi = x.astype(BF16)
    lo = (x - hi.astype(F32)).astype(BF16)
    return hi, lo


def _split3(x):
    hi = x.astype(BF16)
    r1 = x - hi.astype(F32)
    mid = r1.astype(BF16)
    lo = (r1 - mid.astype(F32)).astype(BF16)
    return hi, mid, lo


def _mm1(a, b, dims=NN):
    return _dot(a.astype(BF16), b.astype(BF16), dims)


def _mm_exact_lhs(a_bf16, b):
    h, m, l = _split3(b)
    return _dot(a_bf16, h) + (_dot(a_bf16, m) + _dot(a_bf16, l))


def _rms(x, g):
    return x * lax.rsqrt(jnp.mean(x * x, axis=-1, keepdims=True) + NORM_EPS) * g


def _sigmoid(x):
    return 1.0 / (1.0 + jnp.exp(-x))


def _gelu_tanh(x):
    c = math.sqrt(2.0 / math.pi)
    return 0.5 * x * (1.0 + jnp.tanh(c * (x + 0.044715 * (x * x * x))))


def _const_spec(shape, single_buffer=False):
    idx = lambda *_: (0,) * len(shape)
    if single_buffer:
        return pl.BlockSpec(shape, idx, pipeline_mode=pl.Buffered(1))
    return pl.BlockSpec(shape, idx)


def _two_group_grid(rows_main, tm):
    n_main = rows_main // tm
    main_spec = lambda width: pl.BlockSpec((tm, width), lambda i: (jnp.minimum(i, n_main - 1), 0))
    return n_main, main_spec


def _on_group(body, main_refs, side_refs):
    i, n_main = pl.program_id(0), pl.num_programs(0) - 1
    pl.when(i < n_main)(functools.partial(body, *main_refs))
    pl.when(i == n_main)(functools.partial(body, *side_refs))


def _proj_kernel(x_ref, xs_ref, g_ref, w_ref, pr_ref, u_ref, gt_ref, prs_ref, us_ref, gts_ref, *, c_shift, c_u):
    def project(x_ref, pr_ref, u_ref, gt_ref):
        tm = x_ref.shape[0]
        sub = min(tm, SUB_ROWS)
        norm = lambda i: _rms(x_ref[i * sub:(i + 1) * sub, :], g_ref[...]).astype(BF16)
        hb_next = norm(0)
        for i in range(tm // sub):
            hb, rows = hb_next, slice(i * sub, (i + 1) * sub)
            if (i + 1) * sub < tm:
                hb_next = norm(i + 1)
            pr_ref[rows, :] = _dot(hb, w_ref[:, :c_shift])
            u_ref[rows, :] = _dot(hb, w_ref[:, c_shift:c_shift + c_u])
            gt_ref[rows, :] = _sigmoid(_dot(hb, w_ref[:, c_shift + c_u:])).astype(gt_ref.dtype)

    _on_group(project, (x_ref, pr_ref, u_ref, gt_ref), (xs_ref, prs_ref, us_ref, gts_ref))


def _proj(x_main, x_side, g, w_in_bf16, c_shift, c_u, tm):
    d = x_main.shape[1]
    cols = w_in_bf16.shape[1]
    widths = (c_shift, c_u, cols - c_shift - c_u)
    n_main, main_spec = _two_group_grid(x_main.shape[0], tm)
    side_spec = lambda width: _const_spec((x_side.shape[0], width))
    outs = pl.pallas_call(
        functools.partial(_proj_kernel, c_shift=c_shift, c_u=c_u),
        grid=(n_main + 1,),
        in_specs=[main_spec(d), side_spec(d), _const_spec((1, d)), _const_spec((d, cols), True)],
        out_specs=[main_spec(wd) for wd in widths] + [side_spec(wd) for wd in widths],
        out_shape=[jax.ShapeDtypeStruct((x.shape[0], wd), dt) for x in (x_main, x_side)
                   for wd, dt in zip(widths, (F32, F32, BF16))],
        compiler_params=pltpu.CompilerParams(dimension_semantics=("arbitrary",),
                                             vmem_limit_bytes=VMEM_LIMIT),
        name="proj",
    )(x_main, x_side, g, w_in_bf16)
    return outs[:3], outs[3:]


def _head_sum(x, e_ref):
    rows, width = x.shape
    gw = e_ref.shape[0]
    n_lg = width // gw
    hi, lo = _split2(x)
    stacked = jnp.concatenate([part[:, j * gw:(j + 1) * gw] for part in (hi, lo) for j in range(n_lg)], axis=0)
    sums = _dot(stacked, e_ref[...])
    return jnp.concatenate([sums[j * rows:(j + 1) * rows] + sums[(n_lg + j) * rows:(n_lg + j + 1) * rows]
                            for j in range(n_lg)], axis=1)


def _rwkv_token_prep(xr, w, e_ref):
    width = w["w0"].shape[-1]
    r = xr[:, :width]
    k = xr[:, width:2 * width]
    v = xr[:, 2 * width:3 * width]
    lo = xr[:, 3 * width:3 * width + LORA_PAD]
    wl = w["w0"][...] + _mm1(jnp.tanh(lo), w["wd"][...])
    lw = -math.exp(-0.5) * _sigmoid(wl)
    a = _sigmoid(w["a0"][...] + _mm1(lo, w["wa"][...]))
    g = _mm1(_sigmoid(lo), w["wg"][...])
    kk = k * w["k_k"][...]
    kk = kk * lax.rsqrt(jnp.maximum(_head_sum(kk * kk, e_ref), 1e-24))
    kmod = k * (1.0 + (a - 1.0) * w["k_a"][...])
    return r, kmod, v, kk, kk * a, lw, g


def _rwkv_bonus(r, kmod, v, w, e_ref):
    return _head_sum(r * kmod * w["r_k"][...], e_ref) * v


def _rwkv_post(o, bonus, g, w, e_ref):
    inv_n = 1.0 / HEAD
    mu = _head_sum(o, e_ref) * inv_n
    oc = o - mu
    var = _head_sum(oc * oc, e_ref) * inv_n
    on = oc * lax.rsqrt(var + LNX_EPS) * w["lnx_g"][...] + w["lnx_b"][...]
    return (on + bonus) * g


_RWKV_W_NAMES = ("mu", "w0", "a0", "k_k", "k_a", "r_k", "lnx_g", "lnx_b", "wd", "wa", "wg")


def _rwkv_chunk_kernel(pr_ref, shift0_ref, s0_ref, *rest, chunk, width, n_seq):
    n_w = len(_RWKV_W_NAMES)
    w = dict(zip(_RWKV_W_NAMES, rest[:n_w]))
    e_ref, tri_ref = rest[n_w], rest[n_w + 1]
    oa_ref, s1_ref = rest[n_w + 2], rest[n_w + 3]
    carry_ref, z_ref, ops_ref, aux_ref, gall_ref = rest[n_w + 4:n_w + 9]
    step = pl.program_id(1)
    n_chunks = pl.num_programs(1) - 1
    n_groups = width // GROUP_W
    C = chunk
    GC = GROUP_HEADS * C

    def lane_block_masks(n_lanes, block):
        lane_block = lax.broadcasted_iota(jnp.int32, (1, n_lanes), 1) // block
        return [lane_block == h for h in range(GROUP_HEADS)]

    vec_masks = lane_block_masks(GROUP_W, HEAD)
    mat_masks = lane_block_masks(GC, C)

    def stack(x, masks):
        xb = x.astype(BF16)
        zero = jnp.zeros_like(xb)
        return jnp.concatenate([jnp.where(m, xb, zero) for m in masks], axis=0)

    @pl.when(step == 0)
    def _():
        carry_ref[...] = shift0_ref[...]
        ops_ref[1] = jnp.zeros(ops_ref.shape[1:], ops_ref.dtype)
        aux_ref[1] = jnp.zeros(aux_ref.shape[1:], aux_ref.dtype)
        gall_ref[1] = jnp.zeros(gall_ref.shape[1:], gall_ref.dtype)
        z_ref[...] = jnp.zeros(z_ref.shape, z_ref.dtype)

    @pl.when(step == 1)
    def _():
        for q in range(n_seq):
            for gi in range(n_groups):
                z_ref[q, gi] = jnp.zeros((GROUP_W, GROUP_W), F32)
                for h in range(GROUP_HEADS):
                    z_ref[q, gi, h * HEAD:(h + 1) * HEAD, h * HEAD:(h + 1) * HEAD] = s0_ref[q, gi * GROUP_HEADS + h]

    tok = lax.broadcasted_iota(jnp.int32, (C, GC), 0)
    col = lax.broadcasted_iota(jnp.int32, (C, GC), 1) % C
    strict = tok > col
    incl = tok >= col
    eye = jnp.where(tok == col, 1.0, 0.0).astype(F32)
    zi = lax.broadcasted_iota(jnp.int32, (GROUP_W, GROUP_W), 0) // HEAD
    zj = lax.broadcasted_iota(jnp.int32, (GROUP_W, GROUP_W), 1) // HEAD
    block_diag = zi == zj
    rows2 = lambda top, bottom: jnp.concatenate([top, bottom], axis=0).astype(BF16)

    def recurrence(r_slot):
        chains = [(q, gi) for q in range(n_seq) for gi in range(n_groups)]
        op = lambda i, q, gi: ops_ref[r_slot, i, q * C:(q + 1) * C, gi * GROUP_W:(gi + 1) * GROUP_W]
        kap_t, r_t, v = ([op(i, q, gi) for q, gi in chains] for i in (0, 1, 4))
        kap_s = [stack(x, vec_masks) for x in kap_t]
        v_s = [stack(x, vec_masks) for x in v]
        grams = []
        for i, (q, gi) in enumerate(chains):
            right = jnp.concatenate([stack(op(2, q, gi), vec_masks), stack(op(3, q, gi), vec_masks)], axis=0)
            grams.append(_dot(rows2(kap_t[i], r_t[i]), right, NT))
        yield
        a_k = [jnp.where(strict, gm[:C, :GC], 0.0) for gm in grams]
        a_b = [jnp.where(strict, gm[:C, GC:], 0.0) for gm in grams]
        a_rk = [jnp.where(incl, gm[C:, :GC], 0.0) for gm in grams]
        a_rb = [jnp.where(incl, gm[C:, GC:], 0.0) for gm in grams]

        ps = [-a for a in a_b]
        ts = [eye + p for p in ps]
        ps = [_dot(p.astype(BF16), stack(p, mat_masks)) for p in ps]
        yield
        covered = 2
        while covered < C:
            powers = [stack(p, mat_masks) for p in ps]
            if 2 * covered < C:
                both = [_dot(rows2(t, p), pw) for t, p, pw in zip(ts, ps, powers)]
                ts = [t + x[:C] for t, x in zip(ts, both)]
                ps = [x[C:] for x in both]
            else:
                ts = [t + _dot(t.astype(BF16), pw) for t, pw in zip(ts, powers)]
            covered *= 2
            yield

        av = [_dot(rows2(a, ar), vs) for a, ar, vs in zip(a_k, a_rk, v_s)]
        yield
        tw = [_dot(t.astype(BF16), jnp.concatenate([ks, stack(x[:C], vec_masks)], axis=1))
              for t, ks, x in zip(ts, kap_s, av)]
        yield
        zs = [z_ref[q, gi] for q, gi in chains]
        pz = [_dot(rows2(x[:, :GROUP_W], rt), z.astype(BF16), NT)
              for x, rt, z in zip(tw, r_t, zs)]
        us = [p[:C] + x[:, GROUP_W:] for p, x in zip(pz, tw)]
        yield
        arbu = [_dot(a.astype(BF16), stack(u, vec_masks)) for a, u in zip(a_rb, us)]
        o_blk = [p[C:] + x[C:] - y for p, x, y in zip(pz, av, arbu)]
        for i, (q, gi) in enumerate(chains):
            vu_t = jnp.concatenate([v[i].astype(F32), us[i]], axis=0).T.astype(BF16)
            kb = jnp.concatenate([op(5, q, gi), op(6, q, gi)], axis=0)
            g_all = gall_ref[r_slot, q][:, gi * GROUP_W:(gi + 1) * GROUP_W]
            z_ref[q, gi] = zs[i] * g_all + jnp.where(block_diag, _dot(vu_t, kb), 0.0)
        yield
        o = jnp.concatenate([jnp.concatenate(o_blk[q * n_groups:(q + 1) * n_groups], axis=1)
                             for q in range(n_seq)], axis=0)
        oa_ref[...] = _rwkv_post(o, aux_ref[r_slot, 0], aux_ref[r_slot, 1], w, e_ref
                                 ).astype(oa_ref.dtype).reshape(oa_ref.shape)

    def prepare(q, w_slot):
        rows = slice(q * C, (q + 1) * C)
        pr = pr_ref[q]
        row_id = lax.broadcasted_iota(jnp.int32, pr.shape, 0)
        pr_prev = jnp.where(row_id == 0, carry_ref[q], pltpu.roll(pr, 1, axis=0))
        carry_ref[q] = pr[C - 1:C, :]
        xr = pr + (pr_prev - pr) * w["mu"][...]
        r, kmod, vv, kk, bvec, lw, g = _rwkv_token_prep(xr, w, e_ref)
        aux_ref[w_slot, 0, rows] = _rwkv_bonus(r, kmod, vv, w, e_ref)
        aux_ref[w_slot, 1, rows] = g
        yield
        cum = _mm_exact_lhs(tri_ref[...], lw)
        cum_last = cum[C - 1:C, :]
        g_neg = jnp.exp(-cum)
        g_end = jnp.exp(cum_last - cum)
        prepared = (kk * jnp.exp(cum - lw),
                    r * jnp.exp(cum),
                    kmod * g_neg, bvec * g_neg,
                    vv, kmod * g_end, -(bvec * g_end))
        for i, x in enumerate(prepared):
            ops_ref[w_slot, i, rows] = x.astype(ops_ref.dtype)
        gall_ref[w_slot, q] = jnp.exp(cum_last)

    w_slot = step % 2
    pending = [gen for gen in [prepare(q, w_slot) for q in range(n_seq)] for _ in range(2)]
    for _ in recurrence(1 - w_slot):
        if pending:
            next(pending.pop(0), None)
    for gen in pending:
        next(gen, None)

    @pl.when(step == n_chunks)
    def _():
        for q in range(n_seq):
            for gi in range(n_groups):
                for h in range(GROUP_HEADS):
                    s1_ref[q, gi * GROUP_HEADS + h] = z_ref[q, gi, h * HEAD:(h + 1) * HEAD, h * HEAD:(h + 1) * HEAD]


_N_OPS = 7


def _rwkv_weight_inputs(wts):
    return [wts[n] for n in _RWKV_W_NAMES]


def _rwkv_weight_specs(wts):
    return [_const_spec(wts[n].shape) for n in _RWKV_W_NAMES]


def _rwkv_chunked(pr3d, shift0, wkv0, wts, e_mat, chunk, n_seq):
    bsz, t, c_shift = pr3d.shape
    heads = wkv0.shape[1]
    width = heads * HEAD
    tri = jnp.tril(jnp.ones((chunk, chunk), F32)).astype(BF16)
    kern = functools.partial(_rwkv_chunk_kernel, chunk=chunk, width=width, n_seq=n_seq)
    n_chunks = t // chunk
    rows = n_seq * chunk
    return pl.pallas_call(
        kern,
        grid=(bsz // n_seq, n_chunks + 1),
        in_specs=[pl.BlockSpec((n_seq, chunk, c_shift), lambda b, s: (b, jnp.minimum(s, n_chunks - 1), 0)),
                  pl.BlockSpec((n_seq, 1, c_shift), lambda b, s: (b, 0, 0)),
                  pl.BlockSpec((n_seq, heads, HEAD, HEAD), lambda b, s: (b, 0, 0, 0))]
                 + _rwkv_weight_specs(wts)
                 + [_const_spec(e_mat.shape), _const_spec(tri.shape)],
        out_specs=[pl.BlockSpec((n_seq, chunk, width), lambda b, s: (b, jnp.maximum(s - 1, 0), 0)),
                   pl.BlockSpec((n_seq, heads, HEAD, HEAD), lambda b, s: (b, 0, 0, 0))],
        out_shape=[jax.ShapeDtypeStruct((bsz, t, width), BF16),
                   jax.ShapeDtypeStruct((bsz, heads, HEAD, HEAD), F32)],
        scratch_shapes=[pltpu.VMEM((n_seq, 1, c_shift), F32),
                        pltpu.VMEM((n_seq, width // GROUP_W, GROUP_W, GROUP_W), F32),
                        pltpu.VMEM((2, _N_OPS, rows, width), BF16),
                        pltpu.VMEM((2, 2, rows, width), F32),
                        pltpu.VMEM((2, n_seq, 1, width), F32)],
        compiler_params=pltpu.CompilerParams(dimension_semantics=("parallel", "arbitrary"),
                                             vmem_limit_bytes=VMEM_LIMIT),
        name="rwkv_chunk",
    )(pr3d, shift0[:, None, :], wkv0, *_rwkv_weight_inputs(wts), e_mat, tri)


_STEP_VECS = 6


def _rwkv_step_kernel(pr_ref, shift0_ref, s_ref, *rest):
    n_w = len(_RWKV_W_NAMES)
    w = dict(zip(_RWKV_W_NAMES, rest[:n_w]))
    e_ref, oa_ref, s1_ref, vec_ref, post_ref, o_ref = rest[n_w:n_w + 6]
    h = pl.program_id(0)

    @pl.when(h == 0)
    def _():
        pr = pr_ref[...]
        xr = pr + (shift0_ref[...] - pr) * w["mu"][...]
        r, kmod, v, kk, bvec, lw, g = _rwkv_token_prep(xr, w, e_ref)
        for i, x in enumerate((kk, bvec, jnp.exp(lw), kmod, r, v)):
            vec_ref[i] = x.T
        post_ref[0] = _rwkv_bonus(r, kmod, v, w, e_ref)
        post_ref[1] = g

    base = pl.multiple_of(h * HEAD, HEAD)
    kk_h, b_h, dec_h, k_h, r_h = [vec_ref[i, pl.ds(base, HEAD), :] for i in range(5)]

    def value_row(vi, carry):
        tile = s_ref[0, vi]
        s_kappa = jnp.sum(tile * kk_h, axis=0, keepdims=True)
        new = tile * dec_h - s_kappa * b_h + vec_ref[5, pl.ds(base + vi, 1), :] * k_h
        s1_ref[0, vi] = new
        o_ref[pl.ds(base + vi, 1), :] = jnp.sum(new * r_h, axis=0, keepdims=True)
        return carry

    lax.fori_loop(0, HEAD, value_row, 0, unroll=4)

    @pl.when(h == pl.num_programs(0) - 1)
    def _():
        oa_ref[...] = _rwkv_post(o_ref[...].T, post_ref[0], post_ref[1], w, e_ref).astype(oa_ref.dtype)


def _rwkv_step(pr2d, shift0, wkv0, wts, e_mat):
    bsz, c_shift = pr2d.shape
    heads = wkv0.shape[1]
    width = heads * HEAD
    full = lambda a: _const_spec(a.shape)
    st = pl.BlockSpec((1, HEAD, HEAD, bsz), lambda hh: (hh, 0, 0, 0))
    oa, s1_t = pl.pallas_call(
        _rwkv_step_kernel,
        grid=(heads,),
        in_specs=[full(pr2d), full(shift0), st] + _rwkv_weight_specs(wts) + [full(e_mat)],
        out_specs=[_const_spec((bsz, width)), st],
        out_shape=[jax.ShapeDtypeStruct((bsz, width), BF16),
                   jax.ShapeDtypeStruct((heads, HEAD, HEAD, bsz), F32)],
        scratch_shapes=[pltpu.VMEM((_STEP_VECS, width, bsz), F32),
                        pltpu.VMEM((2, bsz, width), F32),
                        pltpu.VMEM((width, bsz), F32)],
        compiler_params=pltpu.CompilerParams(dimension_semantics=("arbitrary",), vmem_limit_bytes=VMEM_LIMIT),
        name="rwkv_step",
    )(pr2d, shift0, jnp.transpose(wkv0, (1, 2, 3, 0)), *_rwkv_weight_inputs(wts), e_mat)
    return oa, jnp.transpose(s1_t, (3, 0, 1, 2))


def _s5_disc_kernel(lre_ref, lim_ref, ldt_ref, bre_ref, bim_ref, lbr_ref, lbi_ref, bbr_ref, bbi_ref):
    lam_re, lam_im = lre_ref[...], lim_ref[...]
    dt = jnp.exp(ldt_ref[...])
    mag = jnp.exp(lam_re * dt)
    ang = lam_im * dt
    lb_re, lb_im = mag * jnp.cos(ang), mag * jnp.sin(ang)
    nr, ni = lb_re - 1.0, lb_im
    den = lam_re * lam_re + lam_im * lam_im
    f_re = (nr * lam_re + ni * lam_im) / den
    f_im = (ni * lam_re - nr * lam_im) / den
    b_re, b_im = bre_ref[...], bim_ref[...]
    lbr_ref[...] = lb_re
    lbi_ref[...] = lb_im
    bbr_ref[...] = f_re * b_re - f_im * b_im
    bbi_ref[...] = f_re * b_im + f_im * b_re


def _s5_discretise(lam_re, lam_im, log_dt, b_re_t, b_im_t):
    g, p = lam_re.shape
    full = lambda a: _const_spec(a.shape)
    args = (lam_re[:, None, :], lam_im[:, None, :], log_dt[:, None, None], b_re_t, b_im_t)
    return pl.pallas_call(
        _s5_disc_kernel,
        grid=(1,),
        in_specs=[full(a) for a in args],
        out_specs=[_const_spec((g, 1, p))] * 2 + [full(b_re_t)] * 2,
        out_shape=[jax.ShapeDtypeStruct((g, 1, p), F32)] * 2 + [jax.ShapeDtypeStruct(b_re_t.shape, F32)] * 2,
        name="s5_discretise",
    )(*args)


def _s5_kernel(u_ref, re0_ref, im0_ref, lbr_ref, lbi_ref, wb_ref, wc_ref, d_ref,
               h_ref, re1_ref, im1_ref, u_tm, h_tm, bu0, bu1, xs0, xs1, *, tt, n_slabs):
    t_blk = pl.program_id(1)
    rows = SUBLANES * tt
    s_w = SLAB_GROUPS * S5_STATE
    u_w = SLAB_GROUPS * S5_GROUP
    bu, xs = (bu0, bu1), (xs0, xs1)
    rb = min(rows, S5_ROW_BLOCK)
    steps_per_rb = rb // SUBLANES

    @pl.when(t_blk == 0)
    def _():
        re1_ref[...] = re0_ref[...]
        im1_ref[...] = im0_ref[...]

    u_tm[...] = jnp.swapaxes(u_ref[...], 0, 1).reshape(rows, n_slabs * u_w)

    def project_in(s, j):
        r0 = j * rb
        bu[s % 2][r0:r0 + rb, :] = _mm1(u_tm[r0:r0 + rb, s * u_w:(s + 1) * u_w], wb_ref[s])

    def project_out(s, j):
        r0 = j * rb
        lanes = slice(s * u_w, (s + 1) * u_w)
        y = _mm1(xs[s % 2][r0:r0 + rb, :], wc_ref[s]) + d_ref[:, lanes] * u_tm[r0:r0 + rb, lanes]
        h_tm[r0:r0 + rb, lanes] = _gelu_tanh(y)

    for j in range(rows // rb):
        project_in(0, j)
    for p in range(n_slabs + 1):
        if p < n_slabs:
            st = slice(p * s_w, (p + 1) * s_w)
            lbr = jnp.broadcast_to(lbr_ref[:, st], (SUBLANES, s_w))
            lbi = jnp.broadcast_to(lbi_ref[:, st], (SUBLANES, s_w))
            xr, xi = re1_ref[:, st], im1_ref[:, st]
        for j in range(rows // rb):
            if p < n_slabs:
                for t in range(j * steps_per_rb, (j + 1) * steps_per_rb):
                    at_t = slice(t * SUBLANES, (t + 1) * SUBLANES)
                    xr, xi = (lbr * xr - lbi * xi + bu[p % 2][at_t, :s_w],
                              lbr * xi + lbi * xr + bu[p % 2][at_t, s_w:])
                    xs[p % 2][at_t, :s_w] = xr
                    xs[p % 2][at_t, s_w:] = xi
            if p + 1 < n_slabs:
                project_in(p + 1, j)
            if p >= 1:
                project_out(p - 1, j)
        if p < n_slabs:
            re1_ref[:, st] = xr
            im1_ref[:, st] = xi
    h = h_tm[...].reshape(tt, SUBLANES, n_slabs * u_w)
    h_ref[...] = jnp.swapaxes(h, 0, 1).astype(h_ref.dtype)


def _s5(u_blocks, re0, im0, lb_re, lb_im, wb, wc, d_skip, tt):
    n_slabs = wb.shape[0]
    n_state = re0.shape[1]
    blk = (SUBLANES, tt, u_blocks.shape[-1])
    grid = (u_blocks.shape[0] // SUBLANES, u_blocks.shape[1] // tt)
    rows = SUBLANES * tt
    u_spec = pl.BlockSpec(blk, lambda i, j: (i, j, 0))
    st_spec = pl.BlockSpec((SUBLANES, n_state), lambda i, j: (i, 0))
    full = lambda a: _const_spec(a.shape)
    return pl.pallas_call(
        functools.partial(_s5_kernel, tt=tt, n_slabs=n_slabs),
        grid=grid,
        in_specs=[u_spec, st_spec, st_spec, full(lb_re), full(lb_im), full(wb), full(wc), full(d_skip)],
        out_specs=[u_spec, st_spec, st_spec],
        out_shape=[jax.ShapeDtypeStruct(u_blocks.shape, BF16),
                   jax.ShapeDtypeStruct(re0.shape, F32), jax.ShapeDtypeStruct(im0.shape, F32)],
        scratch_shapes=[pltpu.VMEM((rows, u_blocks.shape[-1]), F32)] * 2
                       + [pltpu.VMEM((rows, 2 * SLAB_GROUPS * S5_STATE), F32)] * 4,
        compiler_params=pltpu.CompilerParams(dimension_semantics=("parallel", "arbitrary"),
                                             vmem_limit_bytes=VMEM_LIMIT),
        name="s5_scan",
    )(u_blocks, re0, im0, lb_re, lb_im, wb, wc, d_skip)


def _s5_step_kernel(u_ref, re0_ref, im0_ref, lbr_ref, lbi_ref, wb_ref, wc_ref, d_ref,
                    h_ref, re1_ref, im1_ref, *, n_slabs):
    s_w = SLAB_GROUPS * S5_STATE
    u_w = SLAB_GROUPS * S5_GROUP
    u = u_ref[...]
    ys = []
    for s in range(n_slabs):
        us = u[:, s * u_w:(s + 1) * u_w]
        bu = _mm1(us, wb_ref[s])
        st = slice(s * s_w, (s + 1) * s_w)
        lbr, lbi = lbr_ref[:, st], lbi_ref[:, st]
        xr, xi = re0_ref[:, st], im0_ref[:, st]
        nr = lbr * xr - lbi * xi + bu[:, :s_w]
        ni = lbr * xi + lbi * xr + bu[:, s_w:]
        re1_ref[:, st] = nr
        im1_ref[:, st] = ni
        y = _mm1(jnp.concatenate([nr, ni], axis=1), wc_ref[s]) + d_ref[:, s * u_w:(s + 1) * u_w] * us
        ys.append(_gelu_tanh(y))
    h_ref[...] = jnp.concatenate(ys, axis=1).astype(h_ref.dtype)


def _s5_step(u2d, re0, im0, lb_re, lb_im, wb, wc, d_skip):
    full = lambda a: _const_spec(a.shape)
    args = (u2d, re0, im0, lb_re, lb_im, wb, wc, d_skip)
    return pl.pallas_call(
        functools.partial(_s5_step_kernel, n_slabs=wb.shape[0]),
        grid=(1,),
        in_specs=[full(a) for a in args],
        out_specs=[full(u2d), full(re0), full(im0)],
        out_shape=[jax.ShapeDtypeStruct(u2d.shape, BF16),
                   jax.ShapeDtypeStruct(re0.shape, F32), jax.ShapeDtypeStruct(im0.shape, F32)],
        compiler_params=pltpu.CompilerParams(vmem_limit_bytes=VMEM_LIMIT),
        name="s5_step",
    )(*args)


def _block_diag_slabs(m):
    g, a, b = m.shape
    eye = jnp.eye(SLAB_GROUPS, dtype=m.dtype)
    m4 = m.reshape(g // SLAB_GROUPS, SLAB_GROUPS, a, b)
    return jnp.einsum("sgab,gh->sgahb", m4, eye).reshape(g // SLAB_GROUPS, SLAB_GROUPS * a, SLAB_GROUPS * b)


_TAIL_W_NAMES = ("wro", "w1", "b1", "w2", "b2", "wmo", "npm", "nf", "npf", "wg", "wu", "wd")


def _tail_kernel(*refs):
    n_act = 4
    main_in, side_in = refs[:n_act], refs[n_act:2 * n_act]
    wts = dict(zip(_TAIL_W_NAMES, refs[2 * n_act:2 * n_act + len(_TAIL_W_NAMES)]))
    y_ref, ys_ref = refs[-2:]

    def tail(x_ref, oa_ref, hg_ref, gt_ref, y_ref):
        d = x_ref.shape[-1]
        tm = x_ref.shape[0]
        n_sub = max(1, tm // TAIL_SUB_ROWS)
        subs = [slice(i * (tm // n_sub), (i + 1) * (tm // n_sub)) for i in range(n_sub)]
        hg = [hg_ref[s, :] for s in subs]
        a_out = [_dot(oa_ref[s, :], wts["wro"][...]) for s in subs]
        b_lin = [_dot(h, wts["w1"][...]) + wts["b1"][...] for h in hg]
        b_gate = [_dot(h, wts["w2"][...]) + wts["b2"][...] for h in hg]
        merged = [(gt_ref[s, :d].astype(F32) * a + gt_ref[s, d:].astype(F32) * (bl * _sigmoid(bg))).astype(BF16)
                  for s, a, bl, bg in zip(subs, a_out, b_lin, b_gate)]
        mix = [_dot(m, wts["wmo"][...]) for m in merged]
        x1 = [x_ref[s, :] + _rms(m, wts["npm"][...]) for s, m in zip(subs, mix)]
        hb = [_rms(x, wts["nf"][...]).astype(BF16) for x in x1]
        gate = [_dot(h, wts["wg"][...]) for h in hb]
        up = [_dot(h, wts["wu"][...]) for h in hb]
        act = [(g * _sigmoid(g) * u).astype(BF16) for g, u in zip(gate, up)]
        f = [_dot(a, wts["wd"][...]) for a in act]
        for s, x, ff in zip(subs, x1, f):
            y_ref[s, :] = x + _rms(ff, wts["npf"][...])

    _on_group(tail, (*main_in, y_ref), (*side_in, ys_ref))


def _tail(acts_main, acts_side, tw, tm):
    d = acts_main[0].shape[1]
    n_main, main_spec = _two_group_grid(acts_main[0].shape[0], tm)
    wargs = [tw[n] for n in _TAIL_W_NAMES]
    return pl.pallas_call(
        _tail_kernel,
        grid=(n_main + 1,),
        in_specs=[main_spec(a.shape[1]) for a in acts_main] + [_const_spec(a.shape) for a in acts_side]
                 + [_const_spec(a.shape, True) for a in wargs],
        out_specs=[main_spec(d), _const_spec((acts_side[0].shape[0], d))],
        out_shape=[jax.ShapeDtypeStruct((a[0].shape[0], d), F32) for a in (acts_main, acts_side)],
        compiler_params=pltpu.CompilerParams(dimension_semantics=("arbitrary",),
                                             vmem_limit_bytes=VMEM_LIMIT),
        name="tail",
    )(*acts_main, *acts_side, *wargs)


def _layer(x_p, x_s, shift0, wkv0, re0, im0, lw, *, chunk, n_seq, s5_tt, row_tile):
    bsz, t, d = x_p.shape
    bs = x_s.shape[0]
    assert x_s.shape[1] == 1 and (bsz * t) % row_tile == 0
    heads = wkv0.shape[1]
    c_shift = shift0.shape[-1]
    c_u = lw["d_skip"].shape[-1]
    n_state = re0.shape[1] * re0.shape[2]
    s5_w = (lw["lb_re"], lw["lb_im"], lw["wb"], lw["wc"], lw["d_skip"])
    xp2d, xs2d = x_p.reshape(bsz * t, d), x_s.reshape(bs, d)
    (pr_p, u_p, gates_p), (pr_s, u_s, gates_s) = _proj(xp2d, xs2d, lw["norm_pre_mix"], lw["w_in"],
                                                       c_shift, c_u, row_tile)

    pr_p3 = pr_p.reshape(bsz, t, c_shift)
    oa_p, wkv_p = _rwkv_chunked(pr_p3, jnp.zeros((bsz, c_shift), F32), jnp.zeros((bsz, heads, HEAD, HEAD), F32),
                                lw["rwkv"], lw["e_mat"], chunk, n_seq)
    zeros_state = jnp.zeros((bsz, n_state), F32)
    hg_p, re_p, im_p = _s5(u_p.reshape(bsz, t, c_u), zeros_state, zeros_state, *s5_w, tt=s5_tt)

    oa_s, wkv_s = _rwkv_step(pr_s, shift0, wkv0, lw["rwkv"], lw["e_mat"])
    hg_s, re_s, im_s = _s5_step(u_s, re0.reshape(bs, n_state), im0.reshape(bs, n_state), *s5_w)

    y_p, y_s = _tail((xp2d, oa_p.reshape(bsz * t, -1), hg_p.reshape(bsz * t, c_u), gates_p),
                     (xs2d, oa_s, hg_s, gates_s), lw["tail"], row_tile)
    st_shape = lambda n: (n,) + re0.shape[1:]
    return ((y_p.reshape(x_p.shape), pr_p3[:, -1], wkv_p, re_p.reshape(st_shape(bsz)), im_p.reshape(st_shape(bsz))),
            (y_s.reshape(x_s.shape), pr_s, wkv_s, re_s.reshape(st_shape(bs)), im_s.reshape(st_shape(bs))))


def _prepare_layer_weights(l, p):
    row = lambda a: a[l][None, :].astype(F32)
    width = p["w0"].shape[-1]
    n_dec, n_aaa, n_gate = p["w_decay_up"].shape[1], p["w_aaa_up"].shape[1], p["w_gate_up"].shape[1]
    assert n_dec + n_aaa + n_gate == LORA_PAD

    def lora_pad(wup, start):
        return jnp.zeros((LORA_PAD, width), F32).at[start:start + wup.shape[0]].set(wup).astype(BF16)

    rwkv = {
        "mu": row(p["mu_shift"]), "w0": row(p["w0"]), "a0": row(p["a0"]), "k_k": row(p["k_k"]),
        "k_a": row(p["k_a"]), "r_k": row(p["r_k"]), "lnx_g": row(p["lnx_g"]), "lnx_b": row(p["lnx_b"]),
        "wd": lora_pad(p["w_decay_up"][l], 0),
        "wa": lora_pad(p["w_aaa_up"][l], n_dec),
        "wg": lora_pad(p["w_gate_up"][l], n_dec + n_aaa),
    }
    head_id = jnp.arange(MXU_DIM) // HEAD
    e_mat = (head_id[:, None] == head_id[None, :]).astype(BF16)

    lb_re, lb_im, bb_re_t, bb_im_t = _s5_discretise(
        p["s5_lam_re"][l], p["s5_lam_im"][l], p["s5_log_dt"][l],
        jnp.swapaxes(p["s5_b_re"][l], 1, 2), jnp.swapaxes(p["s5_b_im"][l], 1, 2))
    n_state = lb_re.shape[0] * lb_re.shape[2]
    to_out = lambda cc: _block_diag_slabs(jnp.swapaxes(cc, 1, 2))
    wb = jnp.concatenate([_block_diag_slabs(bb_re_t), _block_diag_slabs(bb_im_t)], axis=-1).astype(BF16)
    wc = jnp.concatenate([to_out(p["s5_c_re"][l]), -to_out(p["s5_c_im"][l])], axis=1).astype(BF16)

    bf = lambda a: a[l].astype(BF16)
    tail = {
        "wro": bf(p["w_rwkv_out"]), "w1": bf(p["glu_w1"]), "b1": row(p["glu_b1"]), "w2": bf(p["glu_w2"]),
        "b2": row(p["glu_b2"]), "wmo": bf(p["w_merge_out"]), "npm": row(p["norm_post_mix"]),
        "nf": row(p["norm_pre_ffn"]), "npf": row(p["norm_post_ffn"]),
        "wg": bf(p["w_ffn_gate"]), "wu": bf(p["w_ffn_up"]), "wd": bf(p["w_ffn_down"]),
    }
    return {
        "norm_pre_mix": row(p["norm_pre_mix"]), "w_in": bf(p["w_in"]), "rwkv": rwkv, "e_mat": e_mat,
        "lb_re": lb_re.reshape(1, n_state), "lb_im": lb_im.reshape(1, n_state), "wb": wb, "wc": wc,
        "d_skip": row(p["s5_d"]), "tail": tail,
    }


_PARAM_NAMES = ("norm_pre_mix", "norm_post_mix", "norm_pre_ffn", "norm_post_ffn", "w_in", "mu_shift",
                "w0", "w_decay_up", "a0", "w_aaa_up", "w_gate_up", "k_k", "k_a", "r_k", "lnx_g", "lnx_b",
                "w_rwkv_out", "s5_lam_re", "s5_lam_im", "s5_log_dt", "s5_b_re", "s5_b_im", "s5_c_re",
                "s5_c_im", "s5_d", "glu_w1", "glu_b1", "glu_w2", "glu_b2", "w_merge_out",
                "w_ffn_gate", "w_ffn_up", "w_ffn_down")


def _forward(x_prompt, x_sample, state_shift, state_wkv, state_s5_re, state_s5_im, params,
             *, chunk=64, n_seq=4, s5_tt=128, row_tile=512):
    depth = params["w_in"].shape[0]
    yp, ys = x_prompt, x_sample
    outs_p, outs_s = [], []
    for l in range(depth):
        lw = _prepare_layer_weights(l, params)
        (yp, *st_p), (ys, *st_s) = _layer(yp, ys, state_shift[l], state_wkv[l], state_s5_re[l], state_s5_im[l], lw,
                                          chunk=chunk, n_seq=n_seq, s5_tt=s5_tt, row_tile=row_tile)
        outs_p.append(st_p)
        outs_s.append(st_s)
    stack = lambda outs, i, dt: jnp.stack([o[i] for o in outs]).astype(dt)
    dt_p, dt_s = x_prompt.dtype, x_sample.dtype
    return (yp, ys,
            stack(outs_p, 0, dt_p), stack(outs_p, 1, dt_p), stack(outs_p, 2, dt_p), stack(outs_p, 3, dt_p),
            stack(outs_s, 0, dt_s), stack(outs_s, 1, dt_s), stack(outs_s, 2, dt_s), stack(outs_s, 3, dt_s))


def kernel(x_prompt, x_sample, state_shift, state_wkv, state_s5_re, state_s5_im, norm_pre_mix, norm_post_mix, norm_pre_ffn, norm_post_ffn, w_in, mu_shift, w0, w_decay_up, a0, w_aaa_up, w_gate_up, k_k, k_a, r_k, lnx_g, lnx_b, w_rwkv_out, s5_lam_re, s5_lam_im, s5_log_dt, s5_b_re, s5_b_im, s5_c_re, s5_c_im, s5_d, glu_w1, glu_b1, glu_w2, glu_b2, w_merge_out, w_ffn_gate, w_ffn_up, w_ffn_down):
    params = dict(zip(_PARAM_NAMES, (norm_pre_mix, norm_post_mix, norm_pre_ffn, norm_post_ffn, w_in, mu_shift,
                                     w0, w_decay_up, a0, w_aaa_up, w_gate_up, k_k, k_a, r_k, lnx_g, lnx_b,
                                     w_rwkv_out, s5_lam_re, s5_lam_im, s5_log_dt, s5_b_re, s5_b_im, s5_c_re,
                                     s5_c_im, s5_d, glu_w1, glu_b1, glu_w2, glu_b2, w_merge_out,
                                     w_ffn_gate, w_ffn_up, w_ffn_down)))
    return _forward(x_prompt, x_sample, state_shift, state_wkv, state_s5_re, state_s5_im, params)
```

```python
import functools
import math

import jax
import jax.numpy as jnp
from jax import lax
from jax.experimental import pallas as pl
from jax.experimental.pallas import tpu as pltpu

F32 = jnp.float32
BF16 = jnp.bfloat16

NORM_EPS = 1e-6
LNX_EPS = 64e-5
HEAD = 64
GROUP_HEADS = 4
GROUP_W = GROUP_HEADS * HEAD
LORA_PAD = 128
S5_GROUP = 16
S5_STATE = 64
SLAB_GROUPS = 8
SUBLANES = 8
LANES = 128
MXU_DIM = 256
S5_ROW_BLOCK = 256
SUB_ROWS = 128
TAIL_SUB_ROWS = 256
VMEM_LIMIT =56 * 1024 * 1024

NN = (((1,), (0,)), ((), ()))
NT = (((1,), (1,)), ((), ()))


def _dot(a, b, dims=NN):
    return lax.dot_general(a, b, dims, preferred_element_type=F32)


def _split2(x):
    hi = x.astype(BF16)
    lo = (x - hi.astype(F32)).astype(BF16)
    return hi, lo


def _split3(x):
    hi = x.astype(BF16)
    r1 = x - hi.astype(F32)
    mid = r1.astype(BF16)
    lo = (r1 - mid.astype(F32)).astype(BF16)
    return hi, mid, lo


def _mm1(a, b, dims=NN):
    return _dot(a.astype(BF16), b.astype(BF16), dims)


def _mm_exact_lhs(a_bf16, b):
    h, m, l = _split3(b)
    return _dot(a_bf16, h) + (_dot(a_bf16, m) + _dot(a_bf16, l))


def _rms(x, g):
    return x * lax.rsqrt(jnp.mean(x * x, axis=-1, keepdims=True) + NORM_EPS) * g


def _sigmoid(x):
    return 1.0 / (1.0 + jnp.exp(-x))


def _gelu_tanh(x):
    c = math.sqrt(2.0 / math.pi)
    return 0.5 * x * (1.0 + jnp.tanh(c * (x + 0.044715 * (x * x * x))))


def _const_spec(shape, single_buffer=False):
    idx = lambda *_: (0,) * len(shape)
    if single_buffer:
        return pl.BlockSpec(shape, idx, pipeline_mode=pl.Buffered(1))
    return pl.BlockSpec(shape, idx)


def _two_group_grid(rows_main, tm):
    n_main = rows_main // tm
    main_spec = lambda width: pl.BlockSpec((tm, width), lambda i: (jnp.minimum(i, n_main - 1), 0))
    return n_main, main_spec


def _on_group(body, main_refs, side_refs):
    i, n_main = pl.program_id(0), pl.num_programs(0) - 1
    pl.when(i < n_main)(functools.partial(body, *main_refs))
    pl.when(i == n_main)(functools.partial(body, *side_refs))


def _proj_kernel(x_ref, xs_ref, g_ref, w_ref, pr_ref, u_ref, gt_ref, prs_ref, us_ref, gts_ref, *, c_shift, c_u):
    def project(x_ref, pr_ref, u_ref, gt_ref):
        tm = x_ref.shape[0]
        sub = min(tm, SUB_ROWS)
        norm = lambda i: _rms(x_ref[i * sub:(i + 1) * sub, :], g_ref[...]).astype(BF16)
        hb_next = norm(0)
        for i in range(tm // sub):
            hb, rows = hb_next, slice(i * sub, (i + 1) * sub)
            if (i + 1) * sub < tm:
                hb_next = norm(i + 1)
            pr_ref[rows, :] = _dot(hb, w_ref[:, :c_shift])
            u_ref[rows, :] = _dot(hb, w_ref[:, c_shift:c_shift + c_u])
            gt_ref[rows, :] = _sigmoid(_dot(hb, w_ref[:, c_shift + c_u:])).astype(gt_ref.dtype)

    _on_group(project, (x_ref, pr_ref, u_ref, gt_ref), (xs_ref, prs_ref, us_ref, gts_ref))


def _proj(x_main, x_side, g, w_in_bf16, c_shift, c_u, tm):
    d = x_main.shape[1]
    cols = w_in_bf16.shape[1]
    widths = (c_shift, c_u, cols - c_shift - c_u)
    n_main, main_spec = _two_group_grid(x_main.shape[0], tm)
    side_spec = lambda width: _const_spec((x_side.shape[0], width))
    outs = pl.pallas_call(
        functools.partial(_proj_kernel, c_shift=c_shift, c_u=c_u),
        grid=(n_main + 1,),
        in_specs=[main_spec(d), side_spec(d), _const_spec((1, d)), _const_spec((d, cols), True)],
        out_specs=[main_spec(wd) for wd in widths] + [side_spec(wd) for wd in widths],
        out_shape=[jax.ShapeDtypeStruct((x.shape[0], wd), dt) for x in (x_main, x_side)
                   for wd, dt in zip(widths, (F32, F32, BF16))],
        compiler_params=pltpu.CompilerParams(dimension_semantics=("arbitrary",),
                                             vmem_limit_bytes=VMEM_LIMIT),
        name="proj",
    )(x_main, x_side, g, w_in_bf16)
    return outs[:3], outs[3:]


def _head_sum(x, e_ref):
    rows, width = x.shape
    gw = e_ref.shape[0]
    n_lg = width // gw
    hi, lo = _split2(x)
    stacked = jnp.concatenate([part[:, j * gw:(j + 1) * gw] for part in (hi, lo) for j in range(n_lg)], axis=0)
    sums = _dot(stacked, e_ref[...])
    return jnp.concatenate([sums[j * rows:(j + 1) * rows] + sums[(n_lg + j) * rows:(n_lg + j + 1) * rows]
                            for j in range(n_lg)], axis=1)


def _rwkv_token_prep(xr, w, e_ref):
    width = w["w0"].shape[-1]
    r = xr[:, :width]
    k = xr[:, width:2 * width]
    v = xr[:, 2 * width:3 * width]
    lo = xr[:, 3 * width:3 * width + LORA_PAD]
    wl = w["w0"][...] + _mm1(jnp.tanh(lo), w["wd"][...])
    lw = -math.exp(-0.5) * _sigmoid(wl)
    a = _sigmoid(w["a0"][...] + _mm1(lo, w["wa"][...]))
    g = _mm1(_sigmoid(lo), w["wg"][...])
    kk = k * w["k_k"][...]
    kk = kk * lax.rsqrt(jnp.maximum(_head_sum(kk * kk, e_ref), 1e-24))
    kmod = k * (1.0 + (a - 1.0) * w["k_a"][...])
    return r, kmod, v, kk, kk * a, lw, g


def _rwkv_bonus(r, kmod, v, w, e_ref):
    return _head_sum(r * kmod * w["r_k"][...], e_ref) * v


def _rwkv_post(o, bonus, g, w, e_ref):
    inv_n = 1.0 / HEAD
    mu = _head_sum(o, e_ref) * inv_n
    oc = o - mu
    var = _head_sum(oc * oc, e_ref) * inv_n
    on = oc * lax.rsqrt(var + LNX_EPS) * w["lnx_g"][...] + w["lnx_b"][...]
    return (on + bonus) * g


_RWKV_W_NAMES = ("mu", "w0", "a0", "k_k", "k_a", "r_k", "lnx_g", "lnx_b", "wd", "wa", "wg")


def _rwkv_chunk_kernel(pr_ref, shift0_ref, s0_ref, *rest, chunk, width, n_seq):
    n_w = len(_RWKV_W_NAMES)
    w = dict(zip(_RWKV_W_NAMES, rest[:n_w]))
    e_ref, tri_ref = rest[n_w], rest[n_w + 1]
    oa_ref, s1_ref = rest[n_w + 2], rest[n_w + 3]
    carry_ref, z_ref, ops_ref, aux_ref, gall_ref = rest[n_w + 4:n_w + 9]
    step = pl.program_id(1)
    n_chunks = pl.num_programs(1) - 1
    n_groups = width // GROUP_W
    C = chunk
    GC = GROUP_HEADS * C

    def lane_block_masks(n_lanes, block):
        lane_block = lax.broadcasted_iota(jnp.int32, (1, n_lanes), 1) // block
        return [lane_block == h for h in range(GROUP_HEADS)]

    vec_masks = lane_block_masks(GROUP_W, HEAD)
    mat_masks = lane_block_masks(GC, C)

    def stack(x, masks):
        xb = x.astype(BF16)
        zero = jnp.zeros_like(xb)
        return jnp.concatenate([jnp.where(m, xb, zero) for m in masks], axis=0)

    @pl.when(step == 0)
    def _():
        carry_ref[...] = shift0_ref[...]
        ops_ref[1] = jnp.zeros(ops_ref.shape[1:], ops_ref.dtype)
        aux_ref[1] = jnp.zeros(aux_ref.shape[1:], aux_ref.dtype)
        gall_ref[1] = jnp.zeros(gall_ref.shape[1:], gall_ref.dtype)
        z_ref[...] = jnp.zeros(z_ref.shape, z_ref.dtype)

    @pl.when(step == 1)
    def _():
        for q in range(n_seq):
            for gi in range(n_groups):
                z_ref[q, gi] = jnp.zeros((GROUP_W, GROUP_W), F32)
                for h in range(GROUP_HEADS):
                    z_ref[q, gi, h * HEAD:(h + 1) * HEAD, h * HEAD:(h + 1) * HEAD] = s0_ref[q, gi * GROUP_HEADS + h]

    tok = lax.broadcasted_iota(jnp.int32, (C, GC), 0)
    col = lax.broadcasted_iota(jnp.int32, (C, GC), 1) % C
    strict = tok > col
    incl = tok >= col
    eye = jnp.where(tok == col, 1.0, 0.0).astype(F32)
    zi = lax.broadcasted_iota(jnp.int32, (GROUP_W, GROUP_W), 0) // HEAD
    zj = lax.broadcasted_iota(jnp.int32, (GROUP_W, GROUP_W), 1) // HEAD
    block_diag = zi == zj
    rows2 = lambda top, bottom: jnp.concatenate([top, bottom], axis=0).astype(BF16)

    def recurrence(r_slot):
        chains = [(q, gi) for q in range(n_seq) for gi in range(n_groups)]
        op = lambda i, q, gi: ops_ref[r_slot, i, q * C:(q + 1) * C, gi * GROUP_W:(gi + 1) * GROUP_W]
        kap_t, r_t, v = ([op(i, q, gi) for q, gi in chains] for i in (0, 1, 4))
        kap_s = [stack(x, vec_masks) for x in kap_t]
        v_s = [stack(x, vec_masks) for x in v]
        grams = []
        for i, (q, gi) in enumerate(chains):
            right = jnp.concatenate([stack(op(2, q, gi), vec_masks), stack(op(3, q, gi), vec_masks)], axis=0)
            grams.append(_dot(rows2(kap_t[i], r_t[i]), right, NT))
        yield
        a_k = [jnp.where(strict, gm[:C, :GC], 0.0) for gm in grams]
        a_b = [jnp.where(strict, gm[:C, GC:], 0.0) for gm in grams]
        a_rk = [jnp.where(incl, gm[C:, :GC], 0.0) for gm in grams]
        a_rb = [jnp.where(incl, gm[C:, GC:], 0.0) for gm in grams]

        ps = [-a for a in a_b]
        ts = [eye + p for p in ps]
        ps = [_dot(p.astype(BF16), stack(p, mat_masks)) for p in ps]
        yield
        covered = 2
        while covered < C:
            powers = [stack(p, mat_masks) for p in ps]
            if 2 * covered < C:
                both = [_dot(rows2(t, p), pw) for t, p, pw in zip(ts, ps, powers)]
                ts = [t + x[:C] for t, x in zip(ts, both)]
                ps = [x[C:] for x in both]
            else:
                ts = [t + _dot(t.astype(BF16), pw) for t, pw in zip(ts, powers)]
            covered *= 2
            yield

        av = [_dot(rows2(a, ar), vs) for a, ar, vs in zip(a_k, a_rk, v_s)]
        yield
        tw = [_dot(t.astype(BF16), jnp.concatenate([ks, stack(x[:C], vec_masks)], axis=1))
              for t, ks, x in zip(ts, kap_s, av)]
        yield
        zs = [z_ref[q, gi] for q, gi in chains]
        pz = [_dot(rows2(x[:, :GROUP_W], rt), z.astype(BF16), NT)
              for x, rt, z in zip(tw, r_t, zs)]
        us = [p[:C] + x[:, GROUP_W:] for p, x in zip(pz, tw)]
        yield
        arbu = [_dot(a.astype(BF16), stack(u, vec_masks)) for a, u in zip(a_rb, us)]
        o_blk = [p[C:] + x[C:] - y for p, x, y in zip(pz, av, arbu)]
        for i, (q, gi) in enumerate(chains):
            vu_t = jnp.concatenate([v[i].astype(F32), us[i]], axis=0).T.astype(BF16)
            kb = jnp.concatenate([op(5, q, gi), op(6, q, gi)], axis=0)
            g_all = gall_ref[r_slot, q][:, gi * GROUP_W:(gi + 1) * GROUP_W]
            z_ref[q, gi] = zs[i] * g_all + jnp.where(block_diag, _dot(vu_t, kb), 0.0)
        yield
        o = jnp.concatenate([jnp.concatenate(o_blk[q * n_groups:(q + 1) * n_groups], axis=1)
                             for q in range(n_seq)], axis=0)
        oa_ref[...] = _rwkv_post(o, aux_ref[r_slot, 0], aux_ref[r_slot, 1], w, e_ref
                                 ).astype(oa_ref.dtype).reshape(oa_ref.shape)

    def prepare(q, w_slot):
        rows = slice(q * C, (q + 1) * C)
        pr = pr_ref[q]
        row_id = lax.broadcasted_iota(jnp.int32, pr.shape, 0)
        pr_prev = jnp.where(row_id == 0, carry_ref[q], pltpu.roll(pr, 1, axis=0))
        carry_ref[q] = pr[C - 1:C, :]
        xr = pr + (pr_prev - pr) * w["mu"][...]
        r, kmod, vv, kk, bvec, lw, g = _rwkv_token_prep(xr, w, e_ref)
        aux_ref[w_slot, 0, rows] = _rwkv_bonus(r, kmod, vv, w, e_ref)
        aux_ref[w_slot, 1, rows] = g
        yield
        cum = _mm_exact_lhs(tri_ref[...], lw)
        cum_last = cum[C - 1:C, :]
        g_neg = jnp.exp(-cum)
        g_end = jnp.exp(cum_last - cum)
        prepared = (kk * jnp.exp(cum - lw),
                    r * jnp.exp(cum),
                    kmod * g_neg, bvec * g_neg,
                    vv, kmod * g_end, -(bvec * g_end))
        for i, x in enumerate(prepared):
            ops_ref[w_slot, i, rows] = x.astype(ops_ref.dtype)
        gall_ref[w_slot, q] = jnp.exp(cum_last)

    w_slot = step % 2
    pending = [gen for gen in [prepare(q, w_slot) for q in range(n_seq)] for _ in range(2)]
    for _ in recurrence(1 - w_slot):
        if pending:
            next(pending.pop(0), None)
    for gen in pending:
        next(gen, None)

    @pl.when(step == n_chunks)
    def _():
        for q in range(n_seq):
            for gi in range(n_groups):
                for h in range(GROUP_HEADS):
                    s1_ref[q, gi * GROUP_HEADS + h] = z_ref[q, gi, h * HEAD:(h + 1) * HEAD, h * HEAD:(h + 1) * HEAD]


_N_OPS = 7


def _rwkv_weight_inputs(wts):
    return [wts[n] for n in _RWKV_W_NAMES]


def _rwkv_weight_specs(wts):
    return [_const_spec(wts[n].shape) for n in _RWKV_W_NAMES]


def _rwkv_chunked(pr3d, shift0, wkv0, wts, e_mat, chunk, n_seq):
    bsz, t, c_shift = pr3d.shape
    heads = wkv0.shape[1]
    width = heads * HEAD
    tri = jnp.tril(jnp.ones((chunk, chunk), F32)).astype(BF16)
    kern = functools.partial(_rwkv_chunk_kernel, chunk=chunk, width=width, n_seq=n_seq)
    n_chunks = t // chunk
    rows = n_seq * chunk
    return pl.pallas_call(
        kern,
        grid=(bsz // n_seq, n_chunks + 1),
        in_specs=[pl.BlockSpec((n_seq, chunk, c_shift), lambda b, s: (b, jnp.minimum(s, n_chunks - 1), 0)),
                  pl.BlockSpec((n_seq, 1, c_shift), lambda b, s: (b, 0, 0)),
                  pl.BlockSpec((n_seq, heads, HEAD, HEAD), lambda b, s: (b, 0, 0, 0))]
                 + _rwkv_weight_specs(wts)
                 + [_const_spec(e_mat.shape), _const_spec(tri.shape)],
        out_specs=[pl.BlockSpec((n_seq, chunk, width), lambda b, s: (b, jnp.maximum(s - 1, 0), 0)),
                   pl.BlockSpec((n_seq, heads, HEAD, HEAD), lambda b, s: (b, 0, 0, 0))],
        out_shape=[jax.ShapeDtypeStruct((bsz, t, width), BF16),
                   jax.ShapeDtypeStruct((bsz, heads, HEAD, HEAD), F32)],
        scratch_shapes=[pltpu.VMEM((n_seq, 1, c_shift), F32),
                        pltpu.VMEM((n_seq, width // GROUP_W, GROUP_W, GROUP_W), F32),
                        pltpu.VMEM((2, _N_OPS, rows, width), BF16),
                        pltpu.VMEM((2, 2, rows, width), F32),
                        pltpu.VMEM((2, n_seq, 1, width), F32)],
        compiler_params=pltpu.CompilerParams(dimension_semantics=("parallel", "arbitrary"),
                                             vmem_limit_bytes=VMEM_LIMIT),
        name="rwkv_chunk",
    )(pr3d, shift0[:, None, :], wkv0, *_rwkv_weight_inputs(wts), e_mat, tri)


_STEP_VECS = 6


def _rwkv_step_kernel(pr_ref, shift0_ref, s_ref, *rest):
    n_w = len(_RWKV_W_NAMES)
    w = dict(zip(_RWKV_W_NAMES, rest[:n_w]))
    e_ref, oa_ref, s1_ref, vec_ref, post_ref, o_ref = rest[n_w:n_w + 6]
    h = pl.program_id(0)

    @pl.when(h == 0)
    def _():
        pr = pr_ref[...]
        xr = pr + (shift0_ref[...] - pr) * w["mu"][...]
        r, kmod, v, kk, bvec, lw, g = _rwkv_token_prep(xr, w, e_ref)
        for i, x in enumerate((kk, bvec, jnp.exp(lw), kmod, r, v)):
            vec_ref[i] = x.T
        post_ref[0] = _rwkv_bonus(r, kmod, v, w, e_ref)
        post_ref[1] = g

    base = pl.multiple_of(h * HEAD, HEAD)
    kk_h, b_h, dec_h, k_h, r_h = [vec_ref[i, pl.ds(base, HEAD), :] for i in range(5)]

    def value_row(vi, carry):
        tile = s_ref[0, vi]
        s_kappa = jnp.sum(tile * kk_h, axis=0, keepdims=True)
        new = tile * dec_h - s_kappa * b_h + vec_ref[5, pl.ds(base + vi, 1), :] * k_h
        s1_ref[0, vi] = new
        o_ref[pl.ds(base + vi, 1), :] = jnp.sum(new * r_h, axis=0, keepdims=True)
        return carry

    lax.fori_loop(0, HEAD, value_row, 0, unroll=4)

    @pl.when(h == pl.num_programs(0) - 1)
    def _():
        oa_ref[...] = _rwkv_post(o_ref[...].T, post_ref[0], post_ref[1], w, e_ref).astype(oa_ref.dtype)


def _rwkv_step(pr2d, shift0, wkv0, wts, e_mat):
    bsz, c_shift = pr2d.shape
    heads = wkv0.shape[1]
    width = heads * HEAD
    full = lambda a: _const_spec(a.shape)
    st = pl.BlockSpec((1, HEAD, HEAD, bsz), lambda hh: (hh, 0, 0, 0))
    oa, s1_t = pl.pallas_call(
        _rwkv_step_kernel,
        grid=(heads,),
        in_specs=[full(pr2d), full(shift0), st] + _rwkv_weight_specs(wts) + [full(e_mat)],
        out_specs=[_const_spec((bsz, width)), st],
        out_shape=[jax.ShapeDtypeStruct((bsz, width), BF16),
                   jax.ShapeDtypeStruct((heads, HEAD, HEAD, bsz), F32)],
        scratch_shapes=[pltpu.VMEM((_STEP_VECS, width, bsz), F32),
                        pltpu.VMEM((2, bsz, width), F32),
                        pltpu.VMEM((width, bsz), F32)],
        compiler_params=pltpu.CompilerParams(dimension_semantics=("arbitrary",), vmem_limit_bytes=VMEM_LIMIT),
        name="rwkv_step",
    )(pr2d, shift0, jnp.transpose(wkv0, (1, 2, 3, 0)), *_rwkv_weight_inputs(wts), e_mat)
    return oa, jnp.transpose(s1_t, (3, 0, 1, 2))


def _s5_disc_kernel(lre_ref, lim_ref, ldt_ref, bre_ref, bim_ref, lbr_ref, lbi_ref, bbr_ref, bbi_ref):
    lam_re, lam_im = lre_ref[...], lim_ref[...]
    dt = jnp.exp(ldt_ref[...])
    mag = jnp.exp(lam_re * dt)
    ang = lam_im * dt
    lb_re, lb_im = mag * jnp.cos(ang), mag * jnp.sin(ang)
    nr, ni = lb_re - 1.0, lb_im
    den = lam_re * lam_re + lam_im * lam_im
    f_re = (nr * lam_re + ni * lam_im) / den
    f_im = (ni * lam_re - nr * lam_im) / den
    b_re, b_im = bre_ref[...], bim_ref[...]
    lbr_ref[...] = lb_re
    lbi_ref[...] = lb_im
    bbr_ref[...] = f_re * b_re - f_im * b_im
    bbi_ref[...] = f_re * b_im + f_im * b_re


def _s5_discretise(lam_re, lam_im, log_dt, b_re_t, b_im_t):
    g, p = lam_re.shape
    full = lambda a: _const_spec(a.shape)
    args = (lam_re[:, None, :], lam_im[:, None, :], log_dt[:, None, None], b_re_t, b_im_t)
    return pl.pallas_call(
        _s5_disc_kernel,
        grid=(1,),
        in_specs=[full(a) for a in args],
        out_specs=[_const_spec((g, 1, p))] * 2 + [full(b_re_t)] * 2,
        out_shape=[jax.ShapeDtypeStruct((g, 1, p), F32)] * 2 + [jax.ShapeDtypeStruct(b_re_t.shape, F32)] * 2,
        name="s5_discretise",
    )(*args)


def _s5_kernel(u_ref, re0_ref, im0_ref, lbr_ref, lbi_ref, wb_ref, wc_ref, d_ref,
               h_ref, re1_ref, im1_ref, u_tm, h_tm, bu0, bu1, xs0, xs1, *, tt, n_slabs):
    t_blk = pl.program_id(1)
    rows = SUBLANES * tt
    s_w = SLAB_GROUPS * S5_STATE
    u_w = SLAB_GROUPS * S5_GROUP
    bu, xs = (bu0, bu1), (xs0, xs1)
    rb = min(rows, S5_ROW_BLOCK)
    steps_per_rb = rb // SUBLANES

    @pl.when(t_blk == 0)
    def _():
        re1_ref[...] = re0_ref[...]
        im1_ref[...] = im0_ref[...]

    u_tm[...] = jnp.swapaxes(u_ref[...], 0, 1).reshape(rows, n_slabs * u_w)

    def project_in(s, j):
        r0 = j * rb
        bu[s % 2][r0:r0 + rb, :] = _mm1(u_tm[r0:r0 + rb, s * u_w:(s + 1) * u_w], wb_ref[s])

    def project_out(s, j):
        r0 = j * rb
        lanes = slice(s * u_w, (s + 1) * u_w)
        h_tm[r0:r0 + rb, lanes] = (_mm1(xs[s % 2][r0:r0 + rb, :], wc_ref[s])
                                   + d_ref[:, lanes] * u_tm[r0:r0 + rb, lanes])

    for j in range(rows // rb):
        project_in(0, j)
    for p in range(n_slabs + 1):
        if p < n_slabs:
            st = slice(p * s_w, (p + 1) * s_w)
            lbr = jnp.broadcast_to(lbr_ref[:, st], (SUBLANES, s_w))
            lbi = jnp.broadcast_to(lbi_ref[:, st], (SUBLANES, s_w))
            xr, xi = re1_ref[:, st], im1_ref[:, st]
        for j in range(rows // rb):
            if p < n_slabs:
                for t in range(j * steps_per_rb, (j + 1) * steps_per_rb):
                    at_t = slice(t * SUBLANES, (t + 1) * SUBLANES)
                    xr, xi = (lbr * xr - lbi * xi + bu[p % 2][at_t, :s_w],
                              lbr * xi + lbi * xr + bu[p % 2][at_t, s_w:])
                    xs[p % 2][at_t, :s_w] = xr
                    xs[p % 2][at_t, s_w:] = xi
            if p + 1 < n_slabs:
                project_in(p + 1, j)
            if p >= 1:
                project_out(p - 1, j)
        if p < n_slabs:
            re1_ref[:, st] = xr
            im1_ref[:, st] = xi
    h = h_tm[...].reshape(tt, SUBLANES, n_slabs * u_w)
    h_ref[...] = jnp.swapaxes(h, 0, 1).astype(h_ref.dtype)


def _s5(u_blocks, re0, im0, lb_re, lb_im, wb, wc, d_skip, tt):
    n_slabs = wb.shape[0]
    n_state = re0.shape[1]
    blk = (SUBLANES, tt, u_blocks.shape[-1])
    grid = (u_blocks.shape[0] // SUBLANES, u_blocks.shape[1] // tt)
    rows = SUBLANES * tt
    u_spec = pl.BlockSpec(blk, lambda i, j: (i, j, 0))
    st_spec = pl.BlockSpec((SUBLANES, n_state), lambda i, j: (i, 0))
    full = lambda a: _const_spec(a.shape)
    return pl.pallas_call(
        functools.partial(_s5_kernel, tt=tt, n_slabs=n_slabs),
        grid=grid,
        in_specs=[u_spec, st_spec, st_spec, full(lb_re), full(lb_im), full(wb), full(wc), full(d_skip)],
        out_specs=[u_spec, st_spec, st_spec],
        out_shape=[jax.ShapeDtypeStruct(u_blocks.shape, F32),
                   jax.ShapeDtypeStruct(re0.shape, F32), jax.ShapeDtypeStruct(im0.shape, F32)],
        scratch_shapes=[pltpu.VMEM((rows, u_blocks.shape[-1]), F32)] * 2
                       + [pltpu.VMEM((rows, 2 * SLAB_GROUPS * S5_STATE), F32)] * 4,
        compiler_params=pltpu.CompilerParams(dimension_semantics=("parallel", "arbitrary"),
                                             vmem_limit_bytes=VMEM_LIMIT),
        name="s5_scan",
    )(u_blocks, re0, im0, lb_re, lb_im, wb, wc, d_skip)


def _s5_step_kernel(u_ref, re0_ref, im0_ref, lbr_ref, lbi_ref, wb_ref, wc_ref, d_ref,
                    h_ref, re1_ref, im1_ref, *, n_slabs):
    s_w = SLAB_GROUPS * S5_STATE
    u_w = SLAB_GROUPS * S5_GROUP
    u = u_ref[...]
    ys = []
    for s in range(n_slabs):
        us = u[:, s * u_w:(s + 1) * u_w]
        bu = _mm1(us, wb_ref[s])
        st = slice(s * s_w, (s + 1) * s_w)
        lbr, lbi = lbr_ref[:, st], lbi_ref[:, st]
        xr, xi = re0_ref[:, st], im0_ref[:, st]
        nr = lbr * xr - lbi * xi + bu[:, :s_w]
        ni = lbr * xi + lbi * xr + bu[:, s_w:]
        re1_ref[:, st] = nr
        im1_ref[:, st] = ni
        ys.append(_mm1(jnp.concatenate([nr, ni], axis=1), wc_ref[s]) + d_ref[:, s * u_w:(s + 1) * u_w] * us)
    h_ref[...] = jnp.concatenate(ys, axis=1)


def _s5_step(u2d, re0, im0, lb_re, lb_im, wb, wc, d_skip):
    full = lambda a: _const_spec(a.shape)
    args = (u2d, re0, im0, lb_re, lb_im, wb, wc, d_skip)
    return pl.pallas_call(
        functools.partial(_s5_step_kernel, n_slabs=wb.shape[0]),
        grid=(1,),
        in_specs=[full(a) for a in args],
        out_specs=[full(u2d), full(re0), full(im0)],
        out_shape=[jax.ShapeDtypeStruct(u2d.shape, F32),
                   jax.ShapeDtypeStruct(re0.shape, F32), jax.ShapeDtypeStruct(im0.shape, F32)],
        compiler_params=pltpu.CompilerParams(vmem_limit_bytes=VMEM_LIMIT),
        name="s5_step",
    )(*args)


def _block_diag_slabs(m):
    g, a, b = m.shape
    eye = jnp.eye(SLAB_GROUPS, dtype=m.dtype)
    m4 = m.reshape(g // SLAB_GROUPS, SLAB_GROUPS, a, b)
    return jnp.einsum("sgab,gh->sgahb", m4, eye).reshape(g // SLAB_GROUPS, SLAB_GROUPS * a, SLAB_GROUPS * b)


_TAIL_W_NAMES = ("wro", "w1", "b1", "w2", "b2", "wmo", "npm", "nf", "npf", "wg", "wu", "wd")


def _tail_kernel(*refs):
    n_act = 4
    main_in, side_in = refs[:n_act], refs[n_act:2 * n_act]
    wts = dict(zip(_TAIL_W_NAMES, refs[2 * n_act:2 * n_act + len(_TAIL_W_NAMES)]))
    y_ref, ys_ref = refs[-2:]

    def tail(x_ref, oa_ref, y_s5_ref, gt_ref, y_ref):
        d = x_ref.shape[-1]
        tm = x_ref.shape[0]
        n_sub = max(1, tm // TAIL_SUB_ROWS)
        subs = [slice(i * (tm // n_sub), (i + 1) * (tm // n_sub)) for i in range(n_sub)]
        hg = [_gelu_tanh(y_s5_ref[s, :]).astype(BF16) for s in subs]
        a_out = [_dot(oa_ref[s, :], wts["wro"][...]) for s in subs]
        b_lin = [_dot(h, wts["w1"][...]) + wts["b1"][...] for h in hg]
        b_gate = [_dot(h, wts["w2"][...]) + wts["b2"][...] for h in hg]
        merged = [(gt_ref[s, :d].astype(F32) * a + gt_ref[s, d:].astype(F32) * (bl * _sigmoid(bg))).astype(BF16)
                  for s, a, bl, bg in zip(subs, a_out, b_lin, b_gate)]
        mix = [_dot(m, wts["wmo"][...]) for m in merged]
        x1 = [x_ref[s, :] + _rms(m, wts["npm"][...]) for s, m in zip(subs, mix)]
        hb = [_rms(x, wts["nf"][...]).astype(BF16) for x in x1]
        gate = [_dot(h, wts["wg"][...]) for h in hb]
        up = [_dot(h, wts["wu"][...]) for h in hb]
        act = [(g * _sigmoid(g) * u).astype(BF16) for g, u in zip(gate, up)]
        f = [_dot(a, wts["wd"][...]) for a in act]
        for s, x, ff in zip(subs, x1, f):
            y_ref[s, :] = x + _rms(ff, wts["npf"][...])

    _on_group(tail, (*main_in, y_ref), (*side_in, ys_ref))


def _tail(acts_main, acts_side, tw, tm):
    d = acts_main[0].shape[1]
    n_main, main_spec = _two_group_grid(acts_main[0].shape[0], tm)
    wargs = [tw[n] for n in _TAIL_W_NAMES]
    return pl.pallas_call(
        _tail_kernel,
        grid=(n_main + 1,),
        in_specs=[main_spec(a.shape[1]) for a in acts_main] + [_const_spec(a.shape) for a in acts_side]
                 + [_const_spec(a.shape, True) for a in wargs],
        out_specs=[main_spec(d), _const_spec((acts_side[0].shape[0], d))],
        out_shape=[jax.ShapeDtypeStruct((a[0].shape[0], d), F32) for a in (acts_main, acts_side)],
        compiler_params=pltpu.CompilerParams(dimension_semantics=("arbitrary",),
                                             vmem_limit_bytes=VMEM_LIMIT),
        name="tail",
    )(*acts_main, *acts_side, *wargs)


def _layer(x_p, x_s, shift0, wkv0, re0, im0, lw, *, chunk, n_seq, s5_tt, row_tile, proj_tile):
    bsz, t, d = x_p.shape
    bs = x_s.shape[0]
    assert x_s.shape[1] == 1 and (bsz * t) % row_tile == 0 and (bsz * t) % proj_tile == 0
    heads = wkv0.shape[1]
    c_shift = shift0.shape[-1]
    c_u = lw["d_skip"].shape[-1]
    n_state = re0.shape[1] * re0.shape[2]
    s5_w = (lw["lb_re"], lw["lb_im"], lw["wb"], lw["wc"], lw["d_skip"])
    xp2d, xs2d = x_p.reshape(bsz * t, d), x_s.reshape(bs, d)
    (pr_p, u_p, gates_p), (pr_s, u_s, gates_s) = _proj(xp2d, xs2d, lw["norm_pre_mix"], lw["w_in"],
                                                       c_shift, c_u, proj_tile)

    pr_p3 = pr_p.reshape(bsz, t, c_shift)
    oa_p, wkv_p = _rwkv_chunked(pr_p3, jnp.zeros((bsz, c_shift), F32), jnp.zeros((bsz, heads, HEAD, HEAD), F32),
                                lw["rwkv"], lw["e_mat"], chunk, n_seq)
    zeros_state = jnp.zeros((bsz, n_state), F32)
    hg_p, re_p, im_p = _s5(u_p.reshape(bsz, t, c_u), zeros_state, zeros_state, *s5_w, tt=s5_tt)

    oa_s, wkv_s = _rwkv_step(pr_s, shift0, wkv0, lw["rwkv"], lw["e_mat"])
    hg_s, re_s, im_s = _s5_step(u_s, re0.reshape(bs, n_state), im0.reshape(bs, n_state), *s5_w)

    y_p, y_s = _tail((xp2d, oa_p.reshape(bsz * t, -1), hg_p.reshape(bsz * t, c_u), gates_p),
                     (xs2d, oa_s, hg_s, gates_s), lw["tail"], row_tile)
    st_shape = lambda n: (n,) + re0.shape[1:]
    return ((y_p.reshape(x_p.shape), pr_p3[:, -1], wkv_p, re_p.reshape(st_shape(bsz)), im_p.reshape(st_shape(bsz))),
            (y_s.reshape(x_s.shape), pr_s, wkv_s, re_s.reshape(st_shape(bs)), im_s.reshape(st_shape(bs))))


def _prepare_layer_weights(l, p):
    row = lambda a: a[l][None, :].astype(F32)
    width = p["w0"].shape[-1]
    n_dec, n_aaa, n_gate = p["w_decay_up"].shape[1], p["w_aaa_up"].shape[1], p["w_gate_up"].shape[1]
    assert n_dec + n_aaa + n_gate == LORA_PAD

    def lora_pad(wup, start):
        return jnp.zeros((LORA_PAD, width), F32).at[start:start + wup.shape[0]].set(wup).astype(BF16)

    rwkv = {
        "mu": row(p["mu_shift"]), "w0": row(p["w0"]), "a0": row(p["a0"]), "k_k": row(p["k_k"]),
        "k_a": row(p["k_a"]), "r_k": row(p["r_k"]), "lnx_g": row(p["lnx_g"]), "lnx_b": row(p["lnx_b"]),
        "wd": lora_pad(p["w_decay_up"][l], 0),
        "wa": lora_pad(p["w_aaa_up"][l], n_dec),
        "wg": lora_pad(p["w_gate_up"][l], n_dec + n_aaa),
    }
    head_id = jnp.arange(MXU_DIM) // HEAD
    e_mat = (head_id[:, None] == head_id[None, :]).astype(BF16)

    lb_re, lb_im, bb_re_t, bb_im_t = _s5_discretise(
        p["s5_lam_re"][l], p["s5_lam_im"][l], p["s5_log_dt"][l],
        jnp.swapaxes(p["s5_b_re"][l], 1, 2), jnp.swapaxes(p["s5_b_im"][l], 1, 2))
    n_state = lb_re.shape[0] * lb_re.shape[2]
    to_out = lambda cc: _block_diag_slabs(jnp.swapaxes(cc, 1, 2))
    wb = jnp.concatenate([_block_diag_slabs(bb_re_t), _block_diag_slabs(bb_im_t)], axis=-1).astype(BF16)
    wc = jnp.concatenate([to_out(p["s5_c_re"][l]), -to_out(p["s5_c_im"][l])], axis=1).astype(BF16)

    bf = lambda a: a[l].astype(BF16)
    tail = {
        "wro": bf(p["w_rwkv_out"]), "w1": bf(p["glu_w1"]), "b1": row(p["glu_b1"]), "w2": bf(p["glu_w2"]),
        "b2": row(p["glu_b2"]), "wmo": bf(p["w_merge_out"]), "npm": row(p["norm_post_mix"]),
        "nf": row(p["norm_pre_ffn"]), "npf": row(p["norm_post_ffn"]),
        "wg": bf(p["w_ffn_gate"]), "wu": bf(p["w_ffn_up"]), "wd": bf(p["w_ffn_down"]),
    }
    return {
        "norm_pre_mix": row(p["norm_pre_mix"]), "w_in": bf(p["w_in"]), "rwkv": rwkv, "e_mat": e_mat,
        "lb_re": lb_re.reshape(1, n_state), "lb_im": lb_im.reshape(1, n_state), "wb": wb, "wc": wc,
        "d_skip": row(p["s5_d"]), "tail": tail,
    }


_PARAM_NAMES = ("norm_pre_mix", "norm_post_mix", "norm_pre_ffn", "norm_post_ffn", "w_in", "mu_shift",
                "w0", "w_decay_up", "a0", "w_aaa_up", "w_gate_up", "k_k", "k_a", "r_k", "lnx_g", "lnx_b",
                "w_rwkv_out", "s5_lam_re", "s5_lam_im", "s5_log_dt", "s5_b_re", "s5_b_im", "s5_c_re",
                "s5_c_im", "s5_d", "glu_w1", "glu_b1", "glu_w2", "glu_b2", "w_merge_out",
                "w_ffn_gate", "w_ffn_up", "w_ffn_down")


def _forward(x_prompt, x_sample, state_shift, state_wkv, state_s5_re, state_s5_im, params,
             *, chunk=64, n_seq=4, s5_tt=128, row_tile=512, proj_tile=1024):
    depth = params["w_in"].shape[0]
    yp, ys = x_prompt, x_sample
    outs_p, outs_s = [], []
    for l in range(depth):
        lw = _prepare_layer_weights(l, params)
        (yp, *st_p), (ys, *st_s) = _layer(yp, ys, state_shift[l], state_wkv[l], state_s5_re[l], state_s5_im[l], lw,
                                          chunk=chunk, n_seq=n_seq, s5_tt=s5_tt, row_tile=row_tile, proj_tile=proj_tile)
        outs_p.append(st_p)
        outs_s.append(st_s)
    stack = lambda outs, i, dt: jnp.stack([o[i] for o in outs]).astype(dt)
    dt_p, dt_s = x_prompt.dtype, x_sample.dtype
    return (yp, ys,
            stack(outs_p, 0, dt_p), stack(outs_p, 1, dt_p), stack(outs_p, 2, dt_p), stack(outs_p, 3, dt_p),
            stack(outs_s, 0, dt_s), stack(outs_s, 1, dt_s), stack(outs_s, 2, dt_s), stack(outs_s, 3, dt_s))


def kernel(x_prompt, x_sample, state_shift, state_wkv, state_s5_re, state_s5_im, norm_pre_mix, norm_post_mix, norm_pre_ffn, norm_post_ffn, w_in, mu_shift, w0, w_decay_up, a0, w_aaa_up, w_gate_up, k_k, k_a, r_k, lnx_g, lnx_b, w_rwkv_out, s5_lam_re, s5_lam_im, s5_log_dt, s5_b_re, s5_b_im, s5_c_re, s5_c_im, s5_d, glu_w1, glu_b1, glu_w2, glu_b2, w_merge_out, w_ffn_gate, w_ffn_up, w_ffn_down):
    params = dict(zip(_PARAM_NAMES, (norm_pre_mix, norm_post_mix, norm_pre_ffn, norm_post_ffn, w_in, mu_shift,
                                     w0, w_decay_up, a0, w_aaa_up, w_gate_up, k_k, k_a, r_k, lnx_g, lnx_b,
                                     w_rwkv_out, s5_lam_re, s5_lam_im, s5_log_dt, s5_b_re, s5_b_im, s5_c_re,
                                     s5_c_im, s5_d, glu_w1, glu_b1, glu_w2, glu_b2, w_merge_out,
                                     w_ffn_gate, w_ffn_up, w_ffn_down)))
    return _forward(x_prompt, x_sample, state_shift, state_wkv, state_s5_re, state_s5_im, params)
```

```python
import functools
import math

import jax
import jax.numpy as jnp
from jax import lax
from jax.experimental import pallas as pl
from jax.experimental.pallas import tpu as pltpu

F32 = jnp.float32
BF16 = jnp.bfloat16

NORM_EPS = 1e-6
LNX_EPS = 64e-5
HEAD = 64
GROUP_HEADS = 4
GROUP_W = GROUP_HEADS * HEAD
LORA_PAD = 128
S5_GROUP = 16
S5_STATE = 64
SLAB_GROUPS = 8
SUBLANES = 8
LANES = 128
MXU_DIM = 256
S5_ROW_BLOCK = 256
SUB_ROWS = 128
TAIL_SUB_ROWS = 256
VMEM_LIMIT =56 * 1024 * 1024

NN = (((1,), (0,)), ((), ()))
NT = (((1,), (1,)), ((), ()))


def _dot(a, b, dims=NN):
    return lax.dot_general(a, b, dims, preferred_element_type=F32)


def _split2(x):
    hi = x.astype(BF16)
    lo = (x - hi.astype(F32)).astype(BF16)
    return hi, lo


def _split3(x):
    hi = x.astype(BF16)
    r1 = x - hi.astype(F32)
    mid = r1.astype(BF16)
    lo = (r1 - mid.astype(F32)).astype(BF16)
    return hi, mid, lo


def _mm1(a, b, dims=NN):
    return _dot(a.astype(BF16), b.astype(BF16), dims)


def _mm_exact_lhs(a_bf16, b):
    h, m, l = _split3(b)
    return _dot(a_bf16, h) + (_dot(a_bf16, m) + _dot(a_bf16, l))


def _rms(x, g):
    return x * lax.rsqrt(jnp.mean(x * x, axis=-1, keepdims=True) + NORM_EPS) * g


def _sigmoid(x):
    return 1.0 / (1.0 + jnp.exp(-x))


def _gelu_tanh(x):
    c = math.sqrt(2.0 / math.pi)
    return 0.5 * x * (1.0 + jnp.tanh(c * (x + 0.044715 * (x * x * x))))


def _const_spec(shape, single_buffer=False):
    idx = lambda *_: (0,) * len(shape)
    if single_buffer:
        return pl.BlockSpec(shape, idx, pipeline_mode=pl.Buffered(1))
    return pl.BlockSpec(shape, idx)


def _two_group_grid(rows_main, tm):
    n_main = rows_main // tm
    main_spec = lambda width: pl.BlockSpec((tm, width), lambda i: (jnp.minimum(i, n_main - 1), 0))
    return n_main, main_spec


def _on_group(body, main_refs, side_refs):
    i, n_main = pl.program_id(0), pl.num_programs(0) - 1
    pl.when(i < n_main)(functools.partial(body, *main_refs))
    pl.when(i == n_main)(functools.partial(body, *side_refs))


def _proj_kernel(x_ref, xs_ref, g_ref, w_ref, pr_ref, u_ref, gt_ref, prs_ref, us_ref, gts_ref, *, c_shift, c_u):
    def project(x_ref, pr_ref, u_ref, gt_ref):
        tm = x_ref.shape[0]
        sub = min(tm, SUB_ROWS)
        norm = lambda i: _rms(x_ref[i * sub:(i + 1) * sub, :], g_ref[...]).astype(BF16)
        hb_next = norm(0)
        for i in range(tm // sub):
            hb, rows = hb_next, slice(i * sub, (i + 1) * sub)
            if (i + 1) * sub < tm:
                hb_next = norm(i + 1)
            pr_ref[rows, :] = _dot(hb, w_ref[:, :c_shift])
            u_ref[rows, :] = _dot(hb, w_ref[:, c_shift:c_shift + c_u])
            gt_ref[rows, :] = _sigmoid(_dot(hb, w_ref[:, c_shift + c_u:])).astype(gt_ref.dtype)

    _on_group(project, (x_ref, pr_ref, u_ref, gt_ref), (xs_ref, prs_ref, us_ref, gts_ref))


def _proj(x_main, x_side, g, w_in_bf16, c_shift, c_u, tm):
    d = x_main.shape[1]
    cols = w_in_bf16.shape[1]
    widths = (c_shift, c_u, cols - c_shift - c_u)
    n_main, main_spec = _two_group_grid(x_main.shape[0], tm)
    side_spec = lambda width: _const_spec((x_side.shape[0], width))
    outs = pl.pallas_call(
        functools.partial(_proj_kernel, c_shift=c_shift, c_u=c_u),
        grid=(n_main + 1,),
        in_specs=[main_spec(d), side_spec(d), _const_spec((1, d)), _const_spec((d, cols), True)],
        out_specs=[main_spec(wd) for wd in widths] + [side_spec(wd) for wd in widths],
        out_shape=[jax.ShapeDtypeStruct((x.shape[0], wd), dt) for x in (x_main, x_side)
                   for wd, dt in zip(widths, (F32, F32, BF16))],
        compiler_params=pltpu.CompilerParams(dimension_semantics=("arbitrary",),
                                             vmem_limit_bytes=VMEM_LIMIT),
        name="proj",
    )(x_main, x_side, g, w_in_bf16)
    return outs[:3], outs[3:]


def _head_sum(x, e_ref):
    rows, width = x.shape
    gw = e_ref.shape[0]
    n_lg = width // gw
    hi, lo = _split2(x)
    stacked = jnp.concatenate([part[:, j * gw:(j + 1) * gw] for part in (hi, lo) for j in range(n_lg)], axis=0)
    sums = _dot(stacked, e_ref[...])
    return jnp.concatenate([sums[j * rows:(j + 1) * rows] + sums[(n_lg + j) * rows:(n_lg + j + 1) * rows]
                            for j in range(n_lg)], axis=1)


def _rwkv_token_prep(xr, w, e_ref):
    width = w["w0"].shape[-1]
    r = xr[:, :width]
    k = xr[:, width:2 * width]
    v = xr[:, 2 * width:3 * width]
    lo = xr[:, 3 * width:3 * width + LORA_PAD]
    wl = w["w0"][...] + _mm1(jnp.tanh(lo), w["wd"][...])
    lw = -math.exp(-0.5) * _sigmoid(wl)
    a = _sigmoid(w["a0"][...] + _mm1(lo, w["wa"][...]))
    g = _mm1(_sigmoid(lo), w["wg"][...])
    kk = k * w["k_k"][...]
    kk = kk * lax.rsqrt(jnp.maximum(_head_sum(kk * kk, e_ref), 1e-24))
    kmod = k * (1.0 + (a - 1.0) * w["k_a"][...])
    return r, kmod, v, kk, kk * a, lw, g


def _rwkv_bonus(r, kmod, v, w, e_ref):
    return _head_sum(r * kmod * w["r_k"][...], e_ref) * v


def _rwkv_post(o, bonus, g, w, e_ref):
    inv_n = 1.0 / HEAD
    mu = _head_sum(o, e_ref) * inv_n
    oc = o - mu
    var = _head_sum(oc * oc, e_ref) * inv_n
    on = oc * lax.rsqrt(var + LNX_EPS) * w["lnx_g"][...] + w["lnx_b"][...]
    return (on + bonus) * g


_RWKV_W_NAMES = ("mu", "w0", "a0", "k_k", "k_a", "r_k", "lnx_g", "lnx_b", "wd", "wa", "wg")


def _rwkv_chunk_kernel(pr_ref, shift0_ref, s0_ref, *rest, chunk, width, n_seq):
    n_w = len(_RWKV_W_NAMES)
    w = dict(zip(_RWKV_W_NAMES, rest[:n_w]))
    e_ref, tri_ref = rest[n_w], rest[n_w + 1]
    oa_ref, s1_ref = rest[n_w + 2], rest[n_w + 3]
    carry_ref, z_ref, ops_ref, aux_ref, gall_ref = rest[n_w + 4:n_w + 9]
    step = pl.program_id(1)
    n_chunks = pl.num_programs(1) - 1
    n_groups = width // GROUP_W
    C = chunk
    GC = GROUP_HEADS * C

    def lane_block_masks(n_lanes, block):
        lane_block = lax.broadcasted_iota(jnp.int32, (1, n_lanes), 1) // block
        return [lane_block == h for h in range(GROUP_HEADS)]

    vec_masks = lane_block_masks(GROUP_W, HEAD)
    mat_masks = lane_block_masks(GC, C)

    def stack(x, masks):
        xb = x.astype(BF16)
        zero = jnp.zeros_like(xb)
        return jnp.concatenate([jnp.where(m, xb, zero) for m in masks], axis=0)

    @pl.when(step == 0)
    def _():
        carry_ref[...] = shift0_ref[...]
        ops_ref[1] = jnp.zeros(ops_ref.shape[1:], ops_ref.dtype)
        aux_ref[1] = jnp.zeros(aux_ref.shape[1:], aux_ref.dtype)
        gall_ref[1] = jnp.zeros(gall_ref.shape[1:], gall_ref.dtype)
        z_ref[...] = jnp.zeros(z_ref.shape, z_ref.dtype)

    @pl.when(step == 1)
    def _():
        for q in range(n_seq):
            for gi in range(n_groups):
                z_ref[q, gi] = jnp.zeros((GROUP_W, GROUP_W), F32)
                for h in range(GROUP_HEADS):
                    z_ref[q, gi, h * HEAD:(h + 1) * HEAD, h * HEAD:(h + 1) * HEAD] = s0_ref[q, gi * GROUP_HEADS + h]

    tok = lax.broadcasted_iota(jnp.int32, (C, GC), 0)
    col = lax.broadcasted_iota(jnp.int32, (C, GC), 1) % C
    strict = tok > col
    incl = tok >= col
    eye = jnp.where(tok == col, 1.0, 0.0).astype(F32)
    zi = lax.broadcasted_iota(jnp.int32, (GROUP_W, GROUP_W), 0) // HEAD
    zj = lax.broadcasted_iota(jnp.int32, (GROUP_W, GROUP_W), 1) // HEAD
    block_diag = zi == zj
    rows2 = lambda top, bottom: jnp.concatenate([top, bottom], axis=0).astype(BF16)

    def recurrence(r_slot):
        chains = [(q, gi) for q in range(n_seq) for gi in range(n_groups)]
        op = lambda i, q, gi: ops_ref[r_slot, i, q * C:(q + 1) * C, gi * GROUP_W:(gi + 1) * GROUP_W]
        kap_t, r_t, v = ([op(i, q, gi) for q, gi in chains] for i in (0, 1, 4))
        kap_s = [stack(x, vec_masks) for x in kap_t]
        v_s = [stack(x, vec_masks) for x in v]
        grams = []
        for i, (q, gi) in enumerate(chains):
            right = jnp.concatenate([stack(op(2, q, gi), vec_masks), stack(op(3, q, gi), vec_masks)], axis=0)
            grams.append(_dot(rows2(kap_t[i], r_t[i]), right, NT))
        yield
        a_k = [jnp.where(strict, gm[:C, :GC], 0.0) for gm in grams]
        a_b = [jnp.where(strict, gm[:C, GC:], 0.0) for gm in grams]
        a_rk = [jnp.where(incl, gm[C:, :GC], 0.0) for gm in grams]
        a_rb = [jnp.where(incl, gm[C:, GC:], 0.0) for gm in grams]

        ps = [-a for a in a_b]
        ts = [eye + p for p in ps]
        ps = [_dot(p.astype(BF16), stack(p, mat_masks)) for p in ps]
        yield
        covered = 2
        while covered < C:
            powers = [stack(p, mat_masks) for p in ps]
            if 2 * covered < C:
                both = [_dot(rows2(t, p), pw) for t, p, pw in zip(ts, ps, powers)]
                ts = [t + x[:C] for t, x in zip(ts, both)]
                ps = [x[C:] for x in both]
            else:
                ts = [t + _dot(t.astype(BF16), pw) for t, pw in zip(ts, powers)]
            covered *= 2
            yield

        av = [_dot(rows2(a, ar), vs) for a, ar, vs in zip(a_k, a_rk, v_s)]
        yield
        tw = [_dot(t.astype(BF16), jnp.concatenate([ks, stack(x[:C], vec_masks)], axis=1))
              for t, ks, x in zip(ts, kap_s, av)]
        yield
        zs = [z_ref[q, gi] for q, gi in chains]
        pz = [_dot(rows2(x[:, :GROUP_W], rt), z.astype(BF16), NT)
              for x, rt, z in zip(tw, r_t, zs)]
        us = [p[:C] + x[:, GROUP_W:] for p, x in zip(pz, tw)]
        yield
        arbu = [_dot(a.astype(BF16), stack(u, vec_masks)) for a, u in zip(a_rb, us)]
        o_blk = [p[C:] + x[C:] - y for p, x, y in zip(pz, av, arbu)]
        for i, (q, gi) in enumerate(chains):
            vu_t = jnp.concatenate([v[i].astype(F32), us[i]], axis=0).T.astype(BF16)
            kb = jnp.concatenate([op(5, q, gi), op(6, q, gi)], axis=0)
            g_all = gall_ref[r_slot, q][:, gi * GROUP_W:(gi + 1) * GROUP_W]
            z_ref[q, gi] = zs[i] * g_all + jnp.where(block_diag, _dot(vu_t, kb), 0.0)
        yield
        o = jnp.concatenate([jnp.concatenate(o_blk[q * n_groups:(q + 1) * n_groups], axis=1)
                             for q in range(n_seq)], axis=0)
        oa_ref[...] = _rwkv_post(o, aux_ref[r_slot, 0], aux_ref[r_slot, 1], w, e_ref
                                 ).astype(oa_ref.dtype).reshape(oa_ref.shape)

    def prepare(q, w_slot):
        rows = slice(q * C, (q + 1) * C)
        pr = pr_ref[q]
        row_id = lax.broadcasted_iota(jnp.int32, pr.shape, 0)
        pr_prev = jnp.where(row_id == 0, carry_ref[q], pltpu.roll(pr, 1, axis=0))
        carry_ref[q] = pr[C - 1:C, :]
        xr = pr + (pr_prev - pr) * w["mu"][...]
        r, kmod, vv, kk, bvec, lw, g = _rwkv_token_prep(xr, w, e_ref)
        aux_ref[w_slot, 0, rows] = _rwkv_bonus(r, kmod, vv, w, e_ref)
        aux_ref[w_slot, 1, rows] = g
        yield
        cum = _mm_exact_lhs(tri_ref[...], lw)
        cum_last = cum[C - 1:C, :]
        g_neg = jnp.exp(-cum)
        g_end = jnp.exp(cum_last - cum)
        prepared = (kk * jnp.exp(cum - lw),
                    r * jnp.exp(cum),
                    kmod * g_neg, bvec * g_neg,
                    vv, kmod * g_end, -(bvec * g_end))
        for i, x in enumerate(prepared):
            ops_ref[w_slot, i, rows] = x.astype(ops_ref.dtype)
        gall_ref[w_slot, q] = jnp.exp(cum_last)

    w_slot = step % 2
    pending = [gen for gen in [prepare(q, w_slot) for q in range(n_seq)] for _ in range(2)]
    for _ in recurrence(1 - w_slot):
        if pending:
            next(pending.pop(0), None)
    for gen in pending:
        next(gen, None)

    @pl.when(step == n_chunks)
    def _():
        for q in range(n_seq):
            for gi in range(n_groups):
                for h in range(GROUP_HEADS):
                    s1_ref[q, gi * GROUP_HEADS + h] = z_ref[q, gi, h * HEAD:(h + 1) * HEAD, h * HEAD:(h + 1) * HEAD]


_N_OPS = 7


def _rwkv_weight_inputs(wts):
    return [wts[n] for n in _RWKV_W_NAMES]


def _rwkv_weight_specs(wts):
    return [_const_spec(wts[n].shape) for n in _RWKV_W_NAMES]


def _rwkv_chunked(pr3d, shift0, wkv0, wts, e_mat, chunk, n_seq):
    bsz, t, c_shift = pr3d.shape
    heads = wkv0.shape[1]
    width = heads * HEAD
    tri = jnp.tril(jnp.ones((chunk, chunk), F32)).astype(BF16)
    kern = functools.partial(_rwkv_chunk_kernel, chunk=chunk, width=width, n_seq=n_seq)
    n_chunks = t // chunk
    rows = n_seq * chunk
    return pl.pallas_call(
        kern,
        grid=(bsz // n_seq, n_chunks + 1),
        in_specs=[pl.BlockSpec((n_seq, chunk, c_shift), lambda b, s: (b, jnp.minimum(s, n_chunks - 1), 0)),
                  pl.BlockSpec((n_seq, 1, c_shift), lambda b, s: (b, 0, 0)),
                  pl.BlockSpec((n_seq, heads, HEAD, HEAD), lambda b, s: (b, 0, 0, 0))]
                 + _rwkv_weight_specs(wts)
                 + [_const_spec(e_mat.shape), _const_spec(tri.shape)],
        out_specs=[pl.BlockSpec((n_seq, chunk, width), lambda b, s: (b, jnp.maximum(s - 1, 0), 0)),
                   pl.BlockSpec((n_seq, heads, HEAD, HEAD), lambda b, s: (b, 0, 0, 0))],
        out_shape=[jax.ShapeDtypeStruct((bsz, t, width), BF16),
                   jax.ShapeDtypeStruct((bsz, heads, HEAD, HEAD), F32)],
        scratch_shapes=[pltpu.VMEM((n_seq, 1, c_shift), F32),
                        pltpu.VMEM((n_seq, width // GROUP_W, GROUP_W, GROUP_W), F32),
                        pltpu.VMEM((2, _N_OPS, rows, width), BF16),
                        pltpu.VMEM((2, 2, rows, width), F32),
                        pltpu.VMEM((2, n_seq, 1, width), F32)],
        compiler_params=pltpu.CompilerParams(dimension_semantics=("parallel", "arbitrary"),
                                             vmem_limit_bytes=VMEM_LIMIT),
        name="rwkv_chunk",
    )(pr3d, shift0[:, None, :], wkv0, *_rwkv_weight_inputs(wts), e_mat, tri)


_STEP_VECS = 6


def _rwkv_step_kernel(pr_ref, shift0_ref, s_ref, *rest):
    n_w = len(_RWKV_W_NAMES)
    w = dict(zip(_RWKV_W_NAMES, rest[:n_w]))
    e_ref, oa_ref, s1_ref, vec_ref, post_ref, o_ref = rest[n_w:n_w + 6]
    h = pl.program_id(0)

    @pl.when(h == 0)
    def _():
        pr = pr_ref[...]
        xr = pr + (shift0_ref[...] - pr) * w["mu"][...]
        r, kmod, v, kk, bvec, lw, g = _rwkv_token_prep(xr, w, e_ref)
        for i, x in enumerate((kk, bvec, jnp.exp(lw), kmod, r, v)):
            vec_ref[i] = x.T
        post_ref[0] = _rwkv_bonus(r, kmod, v, w, e_ref)
        post_ref[1] = g

    base = pl.multiple_of(h * HEAD, HEAD)
    kk_h, b_h, dec_h, k_h, r_h = [vec_ref[i, pl.ds(base, HEAD), :] for i in range(5)]

    def value_row(vi, carry):
        tile = s_ref[0, vi]
        s_kappa = jnp.sum(tile * kk_h, axis=0, keepdims=True)
        new = tile * dec_h - s_kappa * b_h + vec_ref[5, pl.ds(base + vi, 1), :] * k_h
        s1_ref[0, vi] = new
        o_ref[pl.ds(base + vi, 1), :] = jnp.sum(new * r_h, axis=0, keepdims=True)
        return carry

    lax.fori_loop(0, HEAD, value_row, 0, unroll=4)

    @pl.when(h == pl.num_programs(0) - 1)
    def _():
        oa_ref[...] = _rwkv_post(o_ref[...].T, post_ref[0], post_ref[1], w, e_ref).astype(oa_ref.dtype)


def _rwkv_step(pr2d, shift0, wkv0, wts, e_mat):
    bsz, c_shift = pr2d.shape
    heads = wkv0.shape[1]
    width = heads * HEAD
    full = lambda a: _const_spec(a.shape)
    st = pl.BlockSpec((1, HEAD, HEAD, bsz), lambda hh: (hh, 0, 0, 0))
    oa, s1_t = pl.pallas_call(
        _rwkv_step_kernel,
        grid=(heads,),
        in_specs=[full(pr2d), full(shift0), st] + _rwkv_weight_specs(wts) + [full(e_mat)],
        out_specs=[_const_spec((bsz, width)), st],
        out_shape=[jax.ShapeDtypeStruct((bsz, width), BF16),
                   jax.ShapeDtypeStruct((heads, HEAD, HEAD, bsz), F32)],
        scratch_shapes=[pltpu.VMEM((_STEP_VECS, width, bsz), F32),
                        pltpu.VMEM((2, bsz, width), F32),
                        pltpu.VMEM((width, bsz), F32)],
        compiler_params=pltpu.CompilerParams(dimension_semantics=("arbitrary",), vmem_limit_bytes=VMEM_LIMIT),
        name="rwkv_step",
    )(pr2d, shift0, jnp.transpose(wkv0, (1, 2, 3, 0)), *_rwkv_weight_inputs(wts), e_mat)
    return oa, jnp.transpose(s1_t, (3, 0, 1, 2))


def _s5_disc_kernel(lre_ref, lim_ref, ldt_ref, bre_ref, bim_ref, lbr_ref, lbi_ref, bbr_ref, bbi_ref):
    lam_re, lam_im = lre_ref[...], lim_ref[...]
    dt = jnp.exp(ldt_ref[...])
    mag = jnp.exp(lam_re * dt)
    ang = lam_im * dt
    lb_re, lb_im = mag * jnp.cos(ang), mag * jnp.sin(ang)
    nr, ni = lb_re - 1.0, lb_im
    den = lam_re * lam_re + lam_im * lam_im
    f_re = (nr * lam_re + ni * lam_im) / den
    f_im = (ni * lam_re - nr * lam_im) / den
    b_re, b_im = bre_ref[...], bim_ref[...]
    lbr_ref[...] = lb_re
    lbi_ref[...] = lb_im
    bbr_ref[...] = f_re * b_re - f_im * b_im
    bbi_ref[...] = f_re * b_im + f_im * b_re


def _s5_discretise(lam_re, lam_im, log_dt, b_re_t, b_im_t):
    g, p = lam_re.shape
    full = lambda a: _const_spec(a.shape)
    args = (lam_re[:, None, :], lam_im[:, None, :], log_dt[:, None, None], b_re_t, b_im_t)
    return pl.pallas_call(
        _s5_disc_kernel,
        grid=(1,),
        in_specs=[full(a) for a in args],
        out_specs=[_const_spec((g, 1, p))] * 2 + [full(b_re_t)] * 2,
        out_shape=[jax.ShapeDtypeStruct((g, 1, p), F32)] * 2 + [jax.ShapeDtypeStruct(b_re_t.shape, F32)] * 2,
        name="s5_discretise",
    )(*args)


def _s5_kernel(u_ref, re0_ref, im0_ref, lbr_ref, lbi_ref, wb_ref, wc_ref, d_ref,
               h_ref, re1_ref, im1_ref, u_tm, h_tm, bu0, bu1, xs0, xs1, *, tt, n_slabs):
    t_blk = pl.program_id(1)
    rows = SUBLANES * tt
    s_w = SLAB_GROUPS * S5_STATE
    u_w = SLAB_GROUPS * S5_GROUP
    bu, xs = (bu0, bu1), (xs0, xs1)
    rb = min(rows, S5_ROW_BLOCK)
    steps_per_rb = rb // SUBLANES

    @pl.when(t_blk == 0)
    def _():
        re1_ref[...] = re0_ref[...]
        im1_ref[...] = im0_ref[...]

    u_tm[...] = jnp.swapaxes(u_ref[...], 0, 1).reshape(rows, n_slabs * u_w)

    def project_in(s, j):
        r0 = j * rb
        bu[s % 2][r0:r0 + rb, :] = _mm1(u_tm[r0:r0 + rb, s * u_w:(s + 1) * u_w], wb_ref[s])

    def project_out(s, j):
        r0 = j * rb
        lanes = slice(s * u_w, (s + 1) * u_w)
        h_tm[r0:r0 + rb, lanes] = (_mm1(xs[s % 2][r0:r0 + rb, :], wc_ref[s])
                                   + d_ref[:, lanes] * u_tm[r0:r0 + rb, lanes])

    for j in range(rows // rb):
        project_in(0, j)
    for p in range(n_slabs + 1):
        if p < n_slabs:
            st = slice(p * s_w, (p + 1) * s_w)
            lbr = jnp.broadcast_to(lbr_ref[:, st], (SUBLANES, s_w))
            lbi = jnp.broadcast_to(lbi_ref[:, st], (SUBLANES, s_w))
            xr, xi = re1_ref[:, st], im1_ref[:, st]
        for j in range(rows // rb):
            if p < n_slabs:
                for t in range(j * steps_per_rb, (j + 1) * steps_per_rb):
                    at_t = slice(t * SUBLANES, (t + 1) * SUBLANES)
                    xr, xi = (lbr * xr - lbi * xi + bu[p % 2][at_t, :s_w],
                              lbr * xi + lbi * xr + bu[p % 2][at_t, s_w:])
                    xs[p % 2][at_t, :s_w] = xr
                    xs[p % 2][at_t, s_w:] = xi
            if p + 1 < n_slabs:
                project_in(p + 1, j)
            if p >= 1:
                project_out(p - 1, j)
        if p < n_slabs:
            re1_ref[:, st] = xr
            im1_ref[:, st] = xi
    h = h_tm[...].reshape(tt, SUBLANES, n_slabs * u_w)
    h_ref[...] = jnp.swapaxes(h, 0, 1).astype(h_ref.dtype)


def _s5(u_blocks, re0, im0, lb_re, lb_im, wb, wc, d_skip, tt):
    n_slabs = wb.shape[0]
    n_state = re0.shape[1]
    blk = (SUBLANES, tt, u_blocks.shape[-1])
    grid = (u_blocks.shape[0] // SUBLANES, u_blocks.shape[1] // tt)
    rows = SUBLANES * tt
    u_spec = pl.BlockSpec(blk, lambda i, j: (i, j, 0))
    st_spec = pl.BlockSpec((SUBLANES, n_state), lambda i, j: (i, 0))
    full = lambda a: _const_spec(a.shape)
    return pl.pallas_call(
        functools.partial(_s5_kernel, tt=tt, n_slabs=n_slabs),
        grid=grid,
        in_specs=[u_spec, st_spec, st_spec, full(lb_re), full(lb_im), full(wb), full(wc), full(d_skip)],
        out_specs=[u_spec, st_spec, st_spec],
        out_shape=[jax.ShapeDtypeStruct(u_blocks.shape, F32),
                   jax.ShapeDtypeStruct(re0.shape, F32), jax.ShapeDtypeStruct(im0.shape, F32)],
        scratch_shapes=[pltpu.VMEM((rows, u_blocks.shape[-1]), F32)] * 2
                       + [pltpu.VMEM((rows, 2 * SLAB_GROUPS * S5_STATE), F32)] * 4,
        compiler_params=pltpu.CompilerParams(dimension_semantics=("parallel", "arbitrary"),
                                             vmem_limit_bytes=VMEM_LIMIT),
        name="s5_scan",
    )(u_blocks, re0, im0, lb_re, lb_im, wb, wc, d_skip)


def _s5_step_kernel(u_ref, re0_ref, im0_ref, lbr_ref, lbi_ref, wb_ref, wc_ref, d_ref,
                    h_ref, re1_ref, im1_ref, *, n_slabs):
    s_w = SLAB_GROUPS * S5_STATE
    u_w = SLAB_GROUPS * S5_GROUP
    u = u_ref[...]
    ys = []
    for s in range(n_slabs):
        us = u[:, s * u_w:(s + 1) * u_w]
        bu = _mm1(us, wb_ref[s])
        st = slice(s * s_w, (s + 1) * s_w)
        lbr, lbi = lbr_ref[:, st], lbi_ref[:, st]
        xr, xi = re0_ref[:, st], im0_ref[:, st]
        nr = lbr * xr - lbi * xi + bu[:, :s_w]
        ni = lbr * xi + lbi * xr + bu[:, s_w:]
        re1_ref[:, st] = nr
        im1_ref[:, st] = ni
        ys.append(_mm1(jnp.concatenate([nr, ni], axis=1), wc_ref[s]) + d_ref[:, s * u_w:(s + 1) * u_w] * us)
    h_ref[...] = jnp.concatenate(ys, axis=1)


def _s5_step(u2d, re0, im0, lb_re, lb_im, wb, wc, d_skip):
    full = lambda a: _const_spec(a.shape)
    args = (u2d, re0, im0, lb_re, lb_im, wb, wc, d_skip)
    return pl.pallas_call(
        functools.partial(_s5_step_kernel, n_slabs=wb.shape[0]),
        grid=(1,),
        in_specs=[full(a) for a in args],
        out_specs=[full(u2d), full(re0), full(im0)],
        out_shape=[jax.ShapeDtypeStruct(u2d.shape, F32),
                   jax.ShapeDtypeStruct(re0.shape, F32), jax.ShapeDtypeStruct(im0.shape, F32)],
        compiler_params=pltpu.CompilerParams(vmem_limit_bytes=VMEM_LIMIT),
        name="s5_step",
    )(*args)


def _block_diag_slabs(m):
    g, a, b = m.shape
    eye = jnp.eye(SLAB_GROUPS, dtype=m.dtype)
    m4 = m.reshape(g // SLAB_GROUPS, SLAB_GROUPS, a, b)
    return jnp.einsum("sgab,gh->sgahb", m4, eye).reshape(g // SLAB_GROUPS, SLAB_GROUPS * a, SLAB_GROUPS * b)


_TAIL_W_NAMES = ("wro", "w1", "b1", "w2", "b2", "wmo", "npm", "nf", "npf", "wg", "wu", "wd")


def _tail_kernel(*refs):
    n_act = 4
    main_in, side_in = refs[:n_act], refs[n_act:2 * n_act]
    wts = dict(zip(_TAIL_W_NAMES, refs[2 * n_act:2 * n_act + len(_TAIL_W_NAMES)]))
    y_ref, ys_ref = refs[-2:]

    def tail(x_ref, oa_ref, y_s5_ref, gt_ref, y_ref):
        d = x_ref.shape[-1]
        tm = x_ref.shape[0]
        n_sub = max(1, tm // TAIL_SUB_ROWS)
        subs = [slice(i * (tm // n_sub), (i + 1) * (tm // n_sub)) for i in range(n_sub)]
        hg = [_gelu_tanh(y_s5_ref[s, :]).astype(BF16) for s in subs]
        a_out = [_dot(oa_ref[s, :], wts["wro"][...]) for s in subs]
        b_lin = [_dot(h, wts["w1"][...]) + wts["b1"][...] for h in hg]
        b_gate = [_dot(h, wts["w2"][...]) + wts["b2"][...] for h in hg]
        merged = [(gt_ref[s, :d].astype(F32) * a + gt_ref[s, d:].astype(F32) * (bl * _sigmoid(bg))).astype(BF16)
                  for s, a, bl, bg in zip(subs, a_out, b_lin, b_gate)]
        mix = [_dot(m, wts["wmo"][...]) for m in merged]
        x1 = [x_ref[s, :] + _rms(m, wts["npm"][...]) for s, m in zip(subs, mix)]
        hb = [_rms(x, wts["nf"][...]).astype(BF16) for x in x1]
        gate = [_dot(h, wts["wg"][...]) for h in hb]
        up = [_dot(h, wts["wu"][...]) for h in hb]
        act = [(g * _sigmoid(g) * u).astype(BF16) for g, u in zip(gate, up)]
        f = [_dot(a, wts["wd"][...]) for a in act]
        for s, x, ff in zip(subs, x1, f):
            y_ref[s, :] = x + _rms(ff, wts["npf"][...])

    _on_group(tail, (*main_in, y_ref), (*side_in, ys_ref))


def _tail(acts_main, acts_side, tw, tm):
    d = acts_main[0].shape[1]
    n_main, main_spec = _two_group_grid(acts_main[0].shape[0], tm)
    wargs = [tw[n] for n in _TAIL_W_NAMES]
    return pl.pallas_call(
        _tail_kernel,
        grid=(n_main + 1,),
        in_specs=[main_spec(a.shape[1]) for a in acts_main] + [_const_spec(a.shape) for a in acts_side]
                 + [_const_spec(a.shape, True) for a in wargs],
        out_specs=[main_spec(d), _const_spec((acts_side[0].shape[0], d))],
        out_shape=[jax.ShapeDtypeStruct((a[0].shape[0], d), F32) for a in (acts_main, acts_side)],
        compiler_params=pltpu.CompilerParams(dimension_semantics=("arbitrary",),
                                             vmem_limit_bytes=VMEM_LIMIT),
        name="tail",
    )(*acts_main, *acts_side, *wargs)


def _layer(x_p, x_s, shift0, wkv0, re0, im0, lw, *, chunk, n_seq, s5_tt, row_tile, proj_tile):
    bsz, t, d = x_p.shape
    bs = x_s.shape[0]
    assert x_s.shape[1] == 1 and (bsz * t) % row_tile == 0 and (bsz * t) % proj_tile == 0
    heads = wkv0.shape[1]
    c_shift = shift0.shape[-1]
    c_u = lw["d_skip"].shape[-1]
    n_state = re0.shape[1] * re0.shape[2]
    s5_w = (lw["lb_re"], lw["lb_im"], lw["wb"], lw["wc"], lw["d_skip"])
    xp2d, xs2d = x_p.reshape(bsz * t, d), x_s.reshape(bs, d)
    (pr_p, u_p, gates_p), (pr_s, u_s, gates_s) = _proj(xp2d, xs2d, lw["norm_pre_mix"], lw["w_in"],
                                                       c_shift, c_u, proj_tile)

    pr_p3 = pr_p.reshape(bsz, t, c_shift)
    oa_p, wkv_p = _rwkv_chunked(pr_p3, jnp.zeros((bsz, c_shift), F32), jnp.zeros((bsz, heads, HEAD, HEAD), F32),
                                lw["rwkv"], lw["e_mat"], chunk, n_seq)
    zeros_state = jnp.zeros((bsz, n_state), F32)
    hg_p, re_p, im_p = _s5(u_p.reshape(bsz, t, c_u), zeros_state, zeros_state, *s5_w, tt=s5_tt)

    oa_s, wkv_s = _rwkv_step(pr_s, shift0, wkv0, lw["rwkv"], lw["e_mat"])
    hg_s, re_s, im_s = _s5_step(u_s, re0.reshape(bs, n_state), im0.reshape(bs, n_state), *s5_w)

    y_p, y_s = _tail((xp2d, oa_p.reshape(bsz * t, -1), hg_p.reshape(bsz * t, c_u), gates_p),
                     (xs2d, oa_s, hg_s, gates_s), lw["tail"], row_tile)
    st_shape = lambda n: (n,) + re0.shape[1:]
    return ((y_p.reshape(x_p.shape), pr_p3[:, -1], wkv_p, re_p.reshape(st_shape(bsz)), im_p.reshape(st_shape(bsz))),
            (y_s.reshape(x_s.shape), pr_s, wkv_s, re_s.reshape(st_shape(bs)), im_s.reshape(st_shape(bs))))


def _prepare_layer_weights(l, p):
    row = lambda a: a[l][None, :].astype(F32)
    width = p["w0"].shape[-1]
    n_dec, n_aaa, n_gate = p["w_decay_up"].shape[1], p["w_aaa_up"].shape[1], p["w_gate_up"].shape[1]
    assert n_dec + n_aaa + n_gate == LORA_PAD

    def lora_pad(wup, start):
        return jnp.zeros((LORA_PAD, width), F32).at[start:start + wup.shape[0]].set(wup).astype(BF16)

    rwkv = {
        "mu": row(p["mu_shift"]), "w0": row(p["w0"]), "a0": row(p["a0"]), "k_k": row(p["k_k"]),
        "k_a": row(p["k_a"]), "r_k": row(p["r_k"]), "lnx_g": row(p["lnx_g"]), "lnx_b": row(p["lnx_b"]),
        "wd": lora_pad(p["w_decay_up"][l], 0),
        "wa": lora_pad(p["w_aaa_up"][l], n_dec),
        "wg": lora_pad(p["w_gate_up"][l], n_dec + n_aaa),
    }
    head_id = jnp.arange(MXU_DIM) // HEAD
    e_mat = (head_id[:, None] == head_id[None, :]).astype(BF16)

    lb_re, lb_im, bb_re_t, bb_im_t = _s5_discretise(
        p["s5_lam_re"][l], p["s5_lam_im"][l], p["s5_log_dt"][l],
        jnp.swapaxes(p["s5_b_re"][l], 1, 2), jnp.swapaxes(p["s5_b_im"][l], 1, 2))
    n_state = lb_re.shape[0] * lb_re.shape[2]
    to_out = lambda cc: _block_diag_slabs(jnp.swapaxes(cc, 1, 2))
    wb = jnp.concatenate([_block_diag_slabs(bb_re_t), _block_diag_slabs(bb_im_t)], axis=-1).astype(BF16)
    wc = jnp.concatenate([to_out(p["s5_c_re"][l]), -to_out(p["s5_c_im"][l])], axis=1).astype(BF16)

    bf = lambda a: a[l].astype(BF16)
    tail = {
        "wro": bf(p["w_rwkv_out"]), "w1": bf(p["glu_w1"]), "b1": row(p["glu_b1"]), "w2": bf(p["glu_w2"]),
        "b2": row(p["glu_b2"]), "wmo": bf(p["w_merge_out"]), "npm": row(p["norm_post_mix"]),
        "nf": row(p["norm_pre_ffn"]), "npf": row(p["norm_post_ffn"]),
        "wg": bf(p["w_ffn_gate"]), "wu": bf(p["w_ffn_up"]), "wd": bf(p["w_ffn_down"]),
    }
    return {
        "norm_pre_mix": row(p["norm_pre_mix"]), "w_in": bf(p["w_in"]), "rwkv": rwkv, "e_mat": e_mat,
        "lb_re": lb_re.reshape(1, n_state), "lb_im": lb_im.reshape(1, n_state), "wb": wb, "wc": wc,
        "d_skip": row(p["s5_d"]), "tail": tail,
    }


_PARAM_NAMES = ("norm_pre_mix", "norm_post_mix", "norm_pre_ffn", "norm_post_ffn", "w_in", "mu_shift",
                "w0", "w_decay_up", "a0", "w_aaa_up", "w_gate_up", "k_k", "k_a", "r_k", "lnx_g", "lnx_b",
                "w_rwkv_out", "s5_lam_re", "s5_lam_im", "s5_log_dt", "s5_b_re", "s5_b_im", "s5_c_re",
                "s5_c_im", "s5_d", "glu_w1", "glu_b1", "glu_w2", "glu_b2", "w_merge_out",
                "w_ffn_gate", "w_ffn_up", "w_ffn_down")


def _forward(x_prompt, x_sample, state_shift, state_wkv, state_s5_re, state_s5_im, params,
             *, chunk=64, n_seq=8, s5_tt=128, row_tile=512, proj_tile=1024):
    depth = params["w_in"].shape[0]
    yp, ys = x_prompt, x_sample
    outs_p, outs_s = [], []
    for l in range(depth):
        lw = _prepare_layer_weights(l, params)
        (yp, *st_p), (ys, *st_s) = _layer(yp, ys, state_shift[l], state_wkv[l], state_s5_re[l], state_s5_im[l], lw,
                                          chunk=chunk, n_seq=n_seq, s5_tt=s5_tt, row_tile=row_tile, proj_tile=proj_tile)
        outs_p.append(st_p)
        outs_s.append(st_s)
    stack = lambda outs, i, dt: jnp.stack([o[i] for o in outs]).astype(dt)
    dt_p, dt_s = x_prompt.dtype, x_sample.dtype
    return (yp, ys,
            stack(outs_p, 0, dt_p), stack(outs_p, 1, dt_p), stack(outs_p, 2, dt_p), stack(outs_p, 3, dt_p),
            stack(outs_s, 0, dt_s), stack(outs_s, 1, dt_s), stack(outs_s, 2, dt_s), stack(outs_s, 3, dt_s))


def kernel(x_prompt, x_sample, state_shift, state_wkv, state_s5_re, state_s5_im, norm_pre_mix, norm_post_mix, norm_pre_ffn, norm_post_ffn, w_in, mu_shift, w0, w_decay_up, a0, w_aaa_up, w_gate_up, k_k, k_a, r_k, lnx_g, lnx_b, w_rwkv_out, s5_lam_re, s5_lam_im, s5_log_dt, s5_b_re, s5_b_im, s5_c_re, s5_c_im, s5_d, glu_w1, glu_b1, glu_w2, glu_b2, w_merge_out, w_ffn_gate, w_ffn_up, w_ffn_down):
    params = dict(zip(_PARAM_NAMES, (norm_pre_mix, norm_post_mix, norm_pre_ffn, norm_post_ffn, w_in, mu_shift,
                                     w0, w_decay_up, a0, w_aaa_up, w_gate_up, k_k, k_a, r_k, lnx_g, lnx_b,
                                     w_rwkv_out, s5_lam_re, s5_lam_im, s5_log_dt, s5_b_re, s5_b_im, s5_c_re,
                                     s5_c_im, s5_d, glu_w1, glu_b1, glu_w2, glu_b2, w_merge_out,
                                     w_ffn_gate, w_ffn_up, w_ffn_down)))
    return _forward(x_prompt, x_sample, state_shift, state_wkv, state_s5_re, state_s5_im, params)
```

```python
import functools
import math

import jax
import jax.numpy as jnp
from jax import lax
from jax.experimental import pallas as pl
from jax.experimental.pallas import tpu as pltpu

F32 = jnp.float32
BF16 = jnp.bfloat16

NORM_EPS = 1e-6
LNX_EPS = 64e-5
HEAD = 64
GROUP_HEADS = 4
GROUP_W = GROUP_HEADS * HEAD
LORA_PAD = 128
S5_GROUP = 16
S5_STATE = 64
SLAB_GROUPS = 8
SUBLANES = 8
LANES = 128
MXU_DIM = 256
S5_ROW_BLOCK = 256
SUB_ROWS = 128
TAIL_SUB_ROWS = 256
VMEM_LIMIT =56 * 1024 * 1024

NN = (((1,), (0,)), ((), ()))
NT = (((1,), (1,)), ((), ()))


def _dot(a, b, dims=NN):
    return lax.dot_general(a, b, dims, preferred_element_type=F32)


def _split2(x):
    hi = x.astype(BF16)
    lo = (x - hi.astype(F32)).astype(BF16)
    return hi, lo


def _split3(x):
    hi = x.astype(BF16)
    r1 = x - hi.astype(F32)
    mid = r1.astype(BF16)
    lo = (r1 - mid.astype(F32)).astype(BF16)
    return hi, mid, lo


def _mm1(a, b, dims=NN):
    return _dot(a.astype(BF16), b.astype(BF16), dims)


def _mm_exact_lhs(a_bf16, b):
    h, m, l = _split3(b)
    return _dot(a_bf16, h) + (_dot(a_bf16, m) + _dot(a_bf16, l))


def _rms(x, g):
    return x * lax.rsqrt(jnp.mean(x * x, axis=-1, keepdims=True) + NORM_EPS) * g


def _sigmoid(x):
    return 1.0 / (1.0 + jnp.exp(-x))


def _gelu_tanh(x):
    c = math.sqrt(2.0 / math.pi)
    return 0.5 * x * (1.0 + jnp.tanh(c * (x + 0.044715 * (x * x * x))))


def _const_spec(shape, single_buffer=False):
    idx = lambda *_: (0,) * len(shape)
    if single_buffer:
        return pl.BlockSpec(shape, idx, pipeline_mode=pl.Buffered(1))
    return pl.BlockSpec(shape, idx)


def _two_group_grid(rows_main, tm):
    n_main = rows_main // tm
    main_spec = lambda width: pl.BlockSpec((tm, width), lambda i: (jnp.minimum(i, n_main - 1), 0))
    return n_main, main_spec


def _on_group(body, main_refs, side_refs):
    i, n_main = pl.program_id(0), pl.num_programs(0) - 1
    pl.when(i < n_main)(functools.partial(body, *main_refs))
    pl.when(i == n_main)(functools.partial(body, *side_refs))


def _proj_kernel(x_ref, xs_ref, g_ref, w_ref, pr_ref, u_ref, gt_ref, prs_ref, us_ref, gts_ref, *, c_shift, c_u):
    def project(x_ref, pr_ref, u_ref, gt_ref):
        tm = x_ref.shape[0]
        sub = min(tm, SUB_ROWS)
        norm = lambda i: _rms(x_ref[i * sub:(i + 1) * sub, :], g_ref[...]).astype(BF16)
        hb_next = norm(0)
        for i in range(tm // sub):
            hb, rows = hb_next, slice(i * sub, (i + 1) * sub)
            if (i + 1) * sub < tm:
                hb_next = norm(i + 1)
            pr_ref[rows, :] = _dot(hb, w_ref[:, :c_shift])
            u_ref[rows, :] = _dot(hb, w_ref[:, c_shift:c_shift + c_u])
            gt_ref[rows, :] = _sigmoid(_dot(hb, w_ref[:, c_shift + c_u:])).astype(gt_ref.dtype)

    _on_group(project, (x_ref, pr_ref, u_ref, gt_ref), (xs_ref, prs_ref, us_ref, gts_ref))


def _proj(x_main, x_side, g, w_in_bf16, c_shift, c_u, tm):
    d = x_main.shape[1]
    cols = w_in_bf16.shape[1]
    widths = (c_shift, c_u, cols - c_shift - c_u)
    n_main, main_spec = _two_group_grid(x_main.shape[0], tm)
    side_spec = lambda width: _const_spec((x_side.shape[0], width))
    outs = pl.pallas_call(
        functools.partial(_proj_kernel, c_shift=c_shift, c_u=c_u),
        grid=(n_main + 1,),
        in_specs=[main_spec(d), side_spec(d), _const_spec((1, d)), _const_spec((d, cols), True)],
        out_specs=[main_spec(wd) for wd in widths] + [side_spec(wd) for wd in widths],
        out_shape=[jax.ShapeDtypeStruct((x.shape[0], wd), dt) for x in (x_main, x_side)
                   for wd, dt in zip(widths, (F32, F32, BF16))],
        compiler_params=pltpu.CompilerParams(dimension_semantics=("arbitrary",),
                                             vmem_limit_bytes=VMEM_LIMIT),
        name="proj",
    )(x_main, x_side, g, w_in_bf16)
    return outs[:3], outs[3:]


def _head_sum(x, e_ref, terms=1):
    rows, width = x.shape
    gw = e_ref.shape[0]
    n_lg = width // gw
    parts = _split2(x) if terms == 2 else (x.astype(BF16),)
    stacked = jnp.concatenate([part[:, j * gw:(j + 1) * gw] for part in parts for j in range(n_lg)], axis=0)
    sums = _dot(stacked, e_ref[...])
    blocks = [sums[i * rows:(i + 1) * rows] for i in range(terms * n_lg)]
    return jnp.concatenate([sum(blocks[j::n_lg][1:], blocks[j]) for j in range(n_lg)], axis=1)


def _rwkv_token_prep(xr, w, e_ref):
    width = w["w0"].shape[-1]
    r = xr[:, :width]
    k = xr[:, width:2 * width]
    v = xr[:, 2 * width:3 * width]
    lo = xr[:, 3 * width:3 * width + LORA_PAD]
    wl = w["w0"][...] + _mm1(jnp.tanh(lo), w["wd"][...])
    lw = -math.exp(-0.5) * _sigmoid(wl)
    a = _sigmoid(w["a0"][...] + _mm1(lo, w["wa"][...]))
    g = _mm1(_sigmoid(lo), w["wg"][...])
    kk = k * w["k_k"][...]
    kk = kk * lax.rsqrt(jnp.maximum(_head_sum(kk * kk, e_ref), 1e-24))
    kmod = k * (1.0 + (a - 1.0) * w["k_a"][...])
    return r, kmod, v, kk, kk * a, lw, g


def _rwkv_bonus(r, kmod, v, w, e_ref):
    return _head_sum(r * kmod * w["r_k"][...], e_ref) * v


def _rwkv_post(o, bonus, g, w, e_ref):
    inv_n = 1.0 / HEAD
    mu = _head_sum(o, e_ref, terms=2) * inv_n
    oc = o - mu
    var = _head_sum(oc * oc, e_ref) * inv_n
    on = oc * lax.rsqrt(var + LNX_EPS) * w["lnx_g"][...] + w["lnx_b"][...]
    return (on + bonus) * g


_RWKV_W_NAMES = ("mu", "w0", "a0", "k_k", "k_a", "r_k", "lnx_g", "lnx_b", "wd", "wa", "wg")


def _rwkv_chunk_kernel(pr_ref, shift0_ref, s0_ref, *rest, chunk, width, n_seq):
    n_w = len(_RWKV_W_NAMES)
    w = dict(zip(_RWKV_W_NAMES, rest[:n_w]))
    e_ref, tri_ref = rest[n_w], rest[n_w + 1]
    oa_ref, s1_ref = rest[n_w + 2], rest[n_w + 3]
    carry_ref, z_ref, ops_ref, aux_ref, gall_ref = rest[n_w + 4:n_w + 9]
    step = pl.program_id(1)
    n_chunks = pl.num_programs(1) - 1
    n_groups = width // GROUP_W
    C = chunk
    GC = GROUP_HEADS * C

    def lane_block_masks(n_lanes, block):
        lane_block = lax.broadcasted_iota(jnp.int32, (1, n_lanes), 1) // block
        return [lane_block == h for h in range(GROUP_HEADS)]

    vec_masks = lane_block_masks(GROUP_W, HEAD)
    mat_masks = lane_block_masks(GC, C)

    def stack(x, masks):
        xb = x.astype(BF16)
        zero = jnp.zeros_like(xb)
        return jnp.concatenate([jnp.where(m, xb, zero) for m in masks], axis=0)

    @pl.when(step == 0)
    def _():
        carry_ref[...] = shift0_ref[...]
        ops_ref[1] = jnp.zeros(ops_ref.shape[1:], ops_ref.dtype)
        aux_ref[1] = jnp.zeros(aux_ref.shape[1:], aux_ref.dtype)
        gall_ref[1] = jnp.zeros(gall_ref.shape[1:], gall_ref.dtype)
        z_ref[...] = jnp.zeros(z_ref.shape, z_ref.dtype)

    @pl.when(step == 1)
    def _():
        for q in range(n_seq):
            for gi in range(n_groups):
                z_ref[q, gi] = jnp.zeros((GROUP_W, GROUP_W), F32)
                for h in range(GROUP_HEADS):
                    z_ref[q, gi, h * HEAD:(h + 1) * HEAD, h * HEAD:(h + 1) * HEAD] = s0_ref[q, gi * GROUP_HEADS + h]

    tok = lax.broadcasted_iota(jnp.int32, (C, GC), 0)
    col = lax.broadcasted_iota(jnp.int32, (C, GC), 1) % C
    strict = tok > col
    incl = tok >= col
    eye = jnp.where(tok == col, 1.0, 0.0).astype(F32)
    zi = lax.broadcasted_iota(jnp.int32, (GROUP_W, GROUP_W), 0) // HEAD
    zj = lax.broadcasted_iota(jnp.int32, (GROUP_W, GROUP_W), 1) // HEAD
    block_diag = zi == zj
    rows2 = lambda top, bottom: jnp.concatenate([top, bottom], axis=0).astype(BF16)

    def recurrence(r_slot):
        chains = [(q, gi) for q in range(n_seq) for gi in range(n_groups)]
        op = lambda i, q, gi: ops_ref[r_slot, i, q * C:(q + 1) * C, gi * GROUP_W:(gi + 1) * GROUP_W]
        kap_t, r_t, v = ([op(i, q, gi) for q, gi in chains] for i in (0, 1, 4))
        kap_s = [stack(x, vec_masks) for x in kap_t]
        v_s = [stack(x, vec_masks) for x in v]
        grams = []
        for i, (q, gi) in enumerate(chains):
            right = jnp.concatenate([stack(op(2, q, gi), vec_masks), stack(op(3, q, gi), vec_masks)], axis=0)
            grams.append(_dot(rows2(kap_t[i], r_t[i]), right, NT))
        yield
        a_k = [jnp.where(strict, gm[:C, :GC], 0.0) for gm in grams]
        a_b = [jnp.where(strict, gm[:C, GC:], 0.0) for gm in grams]
        a_rk = [jnp.where(incl, gm[C:, :GC], 0.0) for gm in grams]
        a_rb = [jnp.where(incl, gm[C:, GC:], 0.0) for gm in grams]

        ps = [-a for a in a_b]
        ts = [eye + p for p in ps]
        ps = [_dot(p.astype(BF16), stack(p, mat_masks)) for p in ps]
        yield
        covered = 2
        while covered < C:
            powers = [stack(p, mat_masks) for p in ps]
            if 2 * covered < C:
                both = [_dot(rows2(t, p), pw) for t, p, pw in zip(ts, ps, powers)]
                ts = [t + x[:C] for t, x in zip(ts, both)]
                ps = [x[C:] for x in both]
            else:
                ts = [t + _dot(t.astype(BF16), pw) for t, pw in zip(ts, powers)]
            covered *= 2
            yield

        av = [_dot(rows2(a, ar), vs) for a, ar, vs in zip(a_k, a_rk, v_s)]
        yield
        tw = [_dot(t.astype(BF16), jnp.concatenate([ks, stack(x[:C], vec_masks)], axis=1))
              for t, ks, x in zip(ts, kap_s, av)]
        yield
        zs = [z_ref[q, gi] for q, gi in chains]
        pz = [_dot(rows2(x[:, :GROUP_W], rt), z.astype(BF16), NT)
              for x, rt, z in zip(tw, r_t, zs)]
        us = [p[:C] + x[:, GROUP_W:] for p, x in zip(pz, tw)]
        yield
        arbu = [_dot(a.astype(BF16), stack(u, vec_masks)) for a, u in zip(a_rb, us)]
        o_blk = [p[C:] + x[C:] - y for p, x, y in zip(pz, av, arbu)]
        for i, (q, gi) in enumerate(chains):
            vu_t = jnp.concatenate([v[i].astype(F32), us[i]], axis=0).T.astype(BF16)
            kb = jnp.concatenate([op(5, q, gi), op(6, q, gi)], axis=0)
            g_all = gall_ref[r_slot, q][:, gi * GROUP_W:(gi + 1) * GROUP_W]
            z_ref[q, gi] = zs[i] * g_all + jnp.where(block_diag, _dot(vu_t, kb), 0.0)
        yield
        o = jnp.concatenate([jnp.concatenate(o_blk[q * n_groups:(q + 1) * n_groups], axis=1)
                             for q in range(n_seq)], axis=0)
        oa_ref[...] = _rwkv_post(o, aux_ref[r_slot, 0], aux_ref[r_slot, 1], w, e_ref
                                 ).astype(oa_ref.dtype).reshape(oa_ref.shape)

    def prepare(q, w_slot):
        rows = slice(q * C, (q + 1) * C)
        pr = pr_ref[q]
        row_id = lax.broadcasted_iota(jnp.int32, pr.shape, 0)
        pr_prev = jnp.where(row_id == 0, carry_ref[q], pltpu.roll(pr, 1, axis=0))
        carry_ref[q] = pr[C - 1:C, :]
        xr = pr + (pr_prev - pr) * w["mu"][...]
        r, kmod, vv, kk, bvec, lw, g = _rwkv_token_prep(xr, w, e_ref)
        aux_ref[w_slot, 0, rows] = _rwkv_bonus(r, kmod, vv, w, e_ref)
        aux_ref[w_slot, 1, rows] = g
        yield
        cum = _mm_exact_lhs(tri_ref[...], lw)
        cum_last = cum[C - 1:C, :]
        g_neg = jnp.exp(-cum)
        g_end = jnp.exp(cum_last - cum)
        prepared = (kk * jnp.exp(cum - lw),
                    r * jnp.exp(cum),
                    kmod * g_neg, bvec * g_neg,
                    vv, kmod * g_end, -(bvec * g_end))
        for i, x in enumerate(prepared):
            ops_ref[w_slot, i, rows] = x.astype(ops_ref.dtype)
        gall_ref[w_slot, q] = jnp.exp(cum_last)

    w_slot = step % 2
    pending = [gen for gen in [prepare(q, w_slot) for q in range(n_seq)] for _ in range(2)]
    for _ in recurrence(1 - w_slot):
        if pending:
            next(pending.pop(0), None)
    for gen in pending:
        next(gen, None)

    @pl.when(step == n_chunks)
    def _():
        for q in range(n_seq):
            for gi in range(n_groups):
                for h in range(GROUP_HEADS):
                    s1_ref[q, gi * GROUP_HEADS + h] = z_ref[q, gi, h * HEAD:(h + 1) * HEAD, h * HEAD:(h + 1) * HEAD]


_N_OPS = 7


def _rwkv_weight_inputs(wts):
    return [wts[n] for n in _RWKV_W_NAMES]


def _rwkv_weight_specs(wts):
    return [_const_spec(wts[n].shape) for n in _RWKV_W_NAMES]


def _rwkv_chunked(pr3d, shift0, wkv0, wts, e_mat, chunk, n_seq):
    bsz, t, c_shift = pr3d.shape
    heads = wkv0.shape[1]
    width = heads * HEAD
    tri = jnp.tril(jnp.ones((chunk, chunk), F32)).astype(BF16)
    kern = functools.partial(_rwkv_chunk_kernel, chunk=chunk, width=width, n_seq=n_seq)
    n_chunks = t // chunk
    rows = n_seq * chunk
    return pl.pallas_call(
        kern,
        grid=(bsz // n_seq, n_chunks + 1),
        in_specs=[pl.BlockSpec((n_seq, chunk, c_shift), lambda b, s: (b, jnp.minimum(s, n_chunks - 1), 0)),
                  pl.BlockSpec((n_seq, 1, c_shift), lambda b, s: (b, 0, 0)),
                  pl.BlockSpec((n_seq, heads, HEAD, HEAD), lambda b, s: (b, 0, 0, 0))]
                 + _rwkv_weight_specs(wts)
                 + [_const_spec(e_mat.shape), _const_spec(tri.shape)],
        out_specs=[pl.BlockSpec((n_seq, chunk, width), lambda b, s: (b, jnp.maximum(s - 1, 0), 0)),
                   pl.BlockSpec((n_seq, heads, HEAD, HEAD), lambda b, s: (b, 0, 0, 0))],
        out_shape=[jax.ShapeDtypeStruct((bsz, t, width), BF16),
                   jax.ShapeDtypeStruct((bsz, heads, HEAD, HEAD), F32)],
        scratch_shapes=[pltpu.VMEM((n_seq, 1, c_shift), F32),
                        pltpu.VMEM((n_seq, width // GROUP_W, GROUP_W, GROUP_W), F32),
                        pltpu.VMEM((2, _N_OPS, rows, width), BF16),
                        pltpu.VMEM((2, 2, rows, width), F32),
                        pltpu.VMEM((2, n_seq, 1, width), F32)],
        compiler_params=pltpu.CompilerParams(dimension_semantics=("parallel", "arbitrary"),
                                             vmem_limit_bytes=VMEM_LIMIT),
        name="rwkv_chunk",
    )(pr3d, shift0[:, None, :], wkv0, *_rwkv_weight_inputs(wts), e_mat, tri)


_STEP_VECS = 6


def _rwkv_step_kernel(pr_ref, shift0_ref, s_ref, *rest):
    n_w = len(_RWKV_W_NAMES)
    w = dict(zip(_RWKV_W_NAMES, rest[:n_w]))
    e_ref, oa_ref, s1_ref, vec_ref, post_ref, o_ref = rest[n_w:n_w + 6]
    h = pl.program_id(0)

    @pl.when(h == 0)
    def _():
        pr = pr_ref[...]
        xr = pr + (shift0_ref[...] - pr) * w["mu"][...]
        r, kmod, v, kk, bvec, lw, g = _rwkv_token_prep(xr, w, e_ref)
        for i, x in enumerate((kk, bvec, jnp.exp(lw), kmod, r, v)):
            vec_ref[i] = x.T
        post_ref[0] = _rwkv_bonus(r, kmod, v, w, e_ref)
        post_ref[1] = g

    base = pl.multiple_of(h * HEAD, HEAD)
    kk_h, b_h, dec_h, k_h, r_h = [vec_ref[i, pl.ds(base, HEAD), :] for i in range(5)]

    def value_row(vi, carry):
        tile = s_ref[0, vi]
        s_kappa = jnp.sum(tile * kk_h, axis=0, keepdims=True)
        new = tile * dec_h - s_kappa * b_h + vec_ref[5, pl.ds(base + vi, 1), :] * k_h
        s1_ref[0, vi] = new
        o_ref[pl.ds(base + vi, 1), :] = jnp.sum(new * r_h, axis=0, keepdims=True)
        return carry

    lax.fori_loop(0, HEAD, value_row, 0, unroll=4)

    @pl.when(h == pl.num_programs(0) - 1)
    def _():
        oa_ref[...] = _rwkv_post(o_ref[...].T, post_ref[0], post_ref[1], w, e_ref).astype(oa_ref.dtype)


def _rwkv_step(pr2d, shift0, wkv0, wts, e_mat):
    bsz, c_shift = pr2d.shape
    heads = wkv0.shape[1]
    width = heads * HEAD
    full = lambda a: _const_spec(a.shape)
    st = pl.BlockSpec((1, HEAD, HEAD, bsz), lambda hh: (hh, 0, 0, 0))
    oa, s1_t = pl.pallas_call(
        _rwkv_step_kernel,
        grid=(heads,),
        in_specs=[full(pr2d), full(shift0), st] + _rwkv_weight_specs(wts) + [full(e_mat)],
        out_specs=[_const_spec((bsz, width)), st],
        out_shape=[jax.ShapeDtypeStruct((bsz, width), BF16),
                   jax.ShapeDtypeStruct((heads, HEAD, HEAD, bsz), F32)],
        scratch_shapes=[pltpu.VMEM((_STEP_VECS, width, bsz), F32),
                        pltpu.VMEM((2, bsz, width), F32),
                        pltpu.VMEM((width, bsz), F32)],
        compiler_params=pltpu.CompilerParams(dimension_semantics=("arbitrary",), vmem_limit_bytes=VMEM_LIMIT),
        name="rwkv_step",
    )(pr2d, shift0, jnp.transpose(wkv0, (1, 2, 3, 0)), *_rwkv_weight_inputs(wts), e_mat)
    return oa, jnp.transpose(s1_t, (3, 0, 1, 2))


def _s5_disc_kernel(lre_ref, lim_ref, ldt_ref, bre_ref, bim_ref, lbr_ref, lbi_ref, bbr_ref, bbi_ref):
    lam_re, lam_im = lre_ref[...], lim_ref[...]
    dt = jnp.exp(ldt_ref[...])
    mag = jnp.exp(lam_re * dt)
    ang = lam_im * dt
    lb_re, lb_im = mag * jnp.cos(ang), mag * jnp.sin(ang)
    nr, ni = lb_re - 1.0, lb_im
    den = lam_re * lam_re + lam_im * lam_im
    f_re = (nr * lam_re + ni * lam_im) / den
    f_im = (ni * lam_re - nr * lam_im) / den
    b_re, b_im = bre_ref[...], bim_ref[...]
    lbr_ref[...] = lb_re
    lbi_ref[...] = lb_im
    bbr_ref[...] = f_re * b_re - f_im * b_im
    bbi_ref[...] = f_re * b_im + f_im * b_re


def _s5_discretise(lam_re, lam_im, log_dt, b_re_t, b_im_t):
    g, p = lam_re.shape
    full = lambda a: _const_spec(a.shape)
    args = (lam_re[:, None, :], lam_im[:, None, :], log_dt[:, None, None], b_re_t, b_im_t)
    return pl.pallas_call(
        _s5_disc_kernel,
        grid=(1,),
        in_specs=[full(a) for a in args],
        out_specs=[_const_spec((g, 1, p))] * 2 + [full(b_re_t)] * 2,
        out_shape=[jax.ShapeDtypeStruct((g, 1, p), F32)] * 2 + [jax.ShapeDtypeStruct(b_re_t.shape, F32)] * 2,
        name="s5_discretise",
    )(*args)


def _s5_kernel(u_ref, re0_ref, im0_ref, lbr_ref, lbi_ref, wb_ref, wc_ref, d_ref,
               h_ref, re1_ref, im1_ref, u_tm, h_tm, bu0, bu1, xs0, xs1, *, tt, n_slabs):
    t_blk = pl.program_id(1)
    rows = SUBLANES * tt
    s_w = SLAB_GROUPS * S5_STATE
    u_w = SLAB_GROUPS * S5_GROUP
    bu, xs = (bu0, bu1), (xs0, xs1)
    rb = min(rows, S5_ROW_BLOCK)
    steps_per_rb = rb // SUBLANES

    @pl.when(t_blk == 0)
    def _():
        re1_ref[...] = re0_ref[...]
        im1_ref[...] = im0_ref[...]

    u_tm[...] = jnp.swapaxes(u_ref[...], 0, 1).reshape(rows, n_slabs * u_w)

    def project_in(s, j):
        r0 = j * rb
        bu[s % 2][r0:r0 + rb, :] = _mm1(u_tm[r0:r0 + rb, s * u_w:(s + 1) * u_w], wb_ref[s])

    def project_out(s, j):
        r0 = j * rb
        lanes = slice(s * u_w, (s + 1) * u_w)
        h_tm[r0:r0 + rb, lanes] = (_mm1(xs[s % 2][r0:r0 + rb, :], wc_ref[s])
                                   + d_ref[:, lanes] * u_tm[r0:r0 + rb, lanes])

    for j in range(rows // rb):
        project_in(0, j)
    for p in range(n_slabs + 1):
        if p < n_slabs:
            st = slice(p * s_w, (p + 1) * s_w)
            lbr = jnp.broadcast_to(lbr_ref[:, st], (SUBLANES, s_w))
            lbi = jnp.broadcast_to(lbi_ref[:, st], (SUBLANES, s_w))
            xr, xi = re1_ref[:, st], im1_ref[:, st]
        for j in range(rows // rb):
            if p < n_slabs:
                for t in range(j * steps_per_rb, (j + 1) * steps_per_rb):
                    at_t = slice(t * SUBLANES, (t + 1) * SUBLANES)
                    xr, xi = (lbr * xr - lbi * xi + bu[p % 2][at_t, :s_w],
                              lbr * xi + lbi * xr + bu[p % 2][at_t, s_w:])
                    xs[p % 2][at_t, :s_w] = xr
                    xs[p % 2][at_t, s_w:] = xi
            if p + 1 < n_slabs:
                project_in(p + 1, j)
            if p >= 1:
                project_out(p - 1, j)
        if p < n_slabs:
            re1_ref[:, st] = xr
            im1_ref[:, st] = xi
    h = h_tm[...].reshape(tt, SUBLANES, n_slabs * u_w)
    h_ref[...] = jnp.swapaxes(h, 0, 1).astype(h_ref.dtype)


def _s5(u_blocks, re0, im0, lb_re, lb_im, wb, wc, d_skip, tt):
    n_slabs = wb.shape[0]
    n_state = re0.shape[1]
    blk = (SUBLANES, tt, u_blocks.shape[-1])
    grid = (u_blocks.shape[0] // SUBLANES, u_blocks.shape[1] // tt)
    rows = SUBLANES * tt
    u_spec = pl.BlockSpec(blk, lambda i, j: (i, j, 0))
    st_spec = pl.BlockSpec((SUBLANES, n_state), lambda i, j: (i, 0))
    full = lambda a: _const_spec(a.shape)
    return pl.pallas_call(
        functools.partial(_s5_kernel, tt=tt, n_slabs=n_slabs),
        grid=grid,
        in_specs=[u_spec, st_spec, st_spec, full(lb_re), full(lb_im), full(wb), full(wc), full(d_skip)],
        out_specs=[u_spec, st_spec, st_spec],
        out_shape=[jax.ShapeDtypeStruct(u_blocks.shape, F32),
                   jax.ShapeDtypeStruct(re0.shape, F32), jax.ShapeDtypeStruct(im0.shape, F32)],
        scratch_shapes=[pltpu.VMEM((rows, u_blocks.shape[-1]), F32)] * 2
                       + [pltpu.VMEM((rows, 2 * SLAB_GROUPS * S5_STATE), F32)] * 4,
        compiler_params=pltpu.CompilerParams(dimension_semantics=("parallel", "arbitrary"),
                                             vmem_limit_bytes=VMEM_LIMIT),
        name="s5_scan",
    )(u_blocks, re0, im0, lb_re, lb_im, wb, wc, d_skip)


def _s5_step_kernel(u_ref, re0_ref, im0_ref, lbr_ref, lbi_ref, wb_ref, wc_ref, d_ref,
                    h_ref, re1_ref, im1_ref, *, n_slabs):
    s_w = SLAB_GROUPS * S5_STATE
    u_w = SLAB_GROUPS * S5_GROUP
    u = u_ref[...]
    ys = []
    for s in range(n_slabs):
        us = u[:, s * u_w:(s + 1) * u_w]
        bu = _mm1(us, wb_ref[s])
        st = slice(s * s_w, (s + 1) * s_w)
        lbr, lbi = lbr_ref[:, st], lbi_ref[:, st]
        xr, xi = re0_ref[st, :].T, im0_ref[st, :].T
        nr = lbr * xr - lbi * xi + bu[:, :s_w]
        ni = lbr * xi + lbi * xr + bu[:, s_w:]
        re1_ref[st, :] = nr.T
        im1_ref[st, :] = ni.T
        ys.append(_mm1(jnp.concatenate([nr, ni], axis=1), wc_ref[s]) + d_ref[:, s * u_w:(s + 1) * u_w] * us)
    h_ref[...] = jnp.concatenate(ys, axis=1)


def _s5_step(u2d, re0, im0, lb_re, lb_im, wb, wc, d_skip):
    bs, g, p = re0.shape
    to_minor = lambda x: jnp.transpose(x, (1, 2, 0)).reshape(g * p, bs)
    from_minor = lambda x: jnp.transpose(x.reshape(g, p, bs), (2, 0, 1))
    y, re1, im1 = _s5_step_call(u2d, to_minor(re0), to_minor(im0), lb_re, lb_im, wb, wc, d_skip)
    return y, from_minor(re1), from_minor(im1)


def _s5_step_call(u2d, re0, im0, lb_re, lb_im, wb, wc, d_skip):
    full = lambda a: _const_spec(a.shape)
    args = (u2d, re0, im0, lb_re, lb_im, wb, wc, d_skip)
    return pl.pallas_call(
        functools.partial(_s5_step_kernel, n_slabs=wb.shape[0]),
        grid=(1,),
        in_specs=[full(a) for a in args],
        out_specs=[full(u2d), full(re0), full(im0)],
        out_shape=[jax.ShapeDtypeStruct(u2d.shape, F32),
                   jax.ShapeDtypeStruct(re0.shape, F32), jax.ShapeDtypeStruct(im0.shape, F32)],
        compiler_params=pltpu.CompilerParams(vmem_limit_bytes=VMEM_LIMIT),
        name="s5_step",
    )(*args)


def _block_diag_slabs(m):
    g, a, b = m.shape
    eye = jnp.eye(SLAB_GROUPS, dtype=m.dtype)
    m4 = m.reshape(g // SLAB_GROUPS, SLAB_GROUPS, a, b)
    return jnp.einsum("sgab,gh->sgahb", m4, eye).reshape(g // SLAB_GROUPS, SLAB_GROUPS * a, SLAB_GROUPS * b)


_TAIL_W_NAMES = ("wro", "w1", "b1", "w2", "b2", "wmo", "npm", "nf", "npf", "wg", "wu", "wd")


def _tail_kernel(*refs):
    n_act = 4
    main_in, side_in = refs[:n_act], refs[n_act:2 * n_act]
    wts = dict(zip(_TAIL_W_NAMES, refs[2 * n_act:2 * n_act + len(_TAIL_W_NAMES)]))
    y_ref, ys_ref = refs[-2:]

    def tail(x_ref, oa_ref, y_s5_ref, gt_ref, y_ref):
        d = x_ref.shape[-1]
        tm = x_ref.shape[0]
        n_sub = max(1, tm // TAIL_SUB_ROWS)
        subs = [slice(i * (tm // n_sub), (i + 1) * (tm // n_sub)) for i in range(n_sub)]
        hg = [_gelu_tanh(y_s5_ref[s, :]).astype(BF16) for s in subs]
        a_out = [_dot(oa_ref[s, :], wts["wro"][...]) for s in subs]
        b_lin = [_dot(h, wts["w1"][...]) + wts["b1"][...] for h in hg]
        b_gate = [_dot(h, wts["w2"][...]) + wts["b2"][...] for h in hg]
        merged = [(gt_ref[s, :d].astype(F32) * a + gt_ref[s, d:].astype(F32) * (bl * _sigmoid(bg))).astype(BF16)
                  for s, a, bl, bg in zip(subs, a_out, b_lin, b_gate)]
        mix = [_dot(m, wts["wmo"][...]) for m in merged]
        x1 = [x_ref[s, :] + _rms(m, wts["npm"][...]) for s, m in zip(subs, mix)]
        hb = [_rms(x, wts["nf"][...]).astype(BF16) for x in x1]
        gate = [_dot(h, wts["wg"][...]) for h in hb]
        up = [_dot(h, wts["wu"][...]) for h in hb]
        act = [(g * _sigmoid(g) * u).astype(BF16) for g, u in zip(gate, up)]
        f = [_dot(a, wts["wd"][...]) for a in act]
        for s, x, ff in zip(subs, x1, f):
            y_ref[s, :] = x + _rms(ff, wts["npf"][...])

    _on_group(tail, (*main_in, y_ref), (*side_in, ys_ref))


def _tail(acts_main, acts_side, tw, tm):
    d = acts_main[0].shape[1]
    n_main, main_spec = _two_group_grid(acts_main[0].shape[0], tm)
    wargs = [tw[n] for n in _TAIL_W_NAMES]
    return pl.pallas_call(
        _tail_kernel,
        grid=(n_main + 1,),
        in_specs=[main_spec(a.shape[1]) for a in acts_main] + [_const_spec(a.shape) for a in acts_side]
                 + [_const_spec(a.shape, True) for a in wargs],
        out_specs=[main_spec(d), _const_spec((acts_side[0].shape[0], d))],
        out_shape=[jax.ShapeDtypeStruct((a[0].shape[0], d), F32) for a in (acts_main, acts_side)],
        compiler_params=pltpu.CompilerParams(dimension_semantics=("arbitrary",),
                                             vmem_limit_bytes=VMEM_LIMIT),
        name="tail",
    )(*acts_main, *acts_side, *wargs)


def _layer(x_p, x_s, shift0, wkv0, re0, im0, lw, *, chunk, n_seq, s5_tt, row_tile, proj_tile):
    bsz, t, d = x_p.shape
    bs = x_s.shape[0]
    assert x_s.shape[1] == 1 and (bsz * t) % row_tile == 0 and (bsz * t) % proj_tile == 0
    heads = wkv0.shape[1]
    c_shift = shift0.shape[-1]
    c_u = lw["d_skip"].shape[-1]
    n_state = re0.shape[1] * re0.shape[2]
    s5_w = (lw["lb_re"], lw["lb_im"], lw["wb"], lw["wc"], lw["d_skip"])
    xp2d, xs2d = x_p.reshape(bsz * t, d), x_s.reshape(bs, d)
    (pr_p, u_p, gates_p), (pr_s, u_s, gates_s) = _proj(xp2d, xs2d, lw["norm_pre_mix"], lw["w_in"],
                                                       c_shift, c_u, proj_tile)

    pr_p3 = pr_p.reshape(bsz, t, c_shift)
    oa_p, wkv_p = _rwkv_chunked(pr_p3, jnp.zeros((bsz, c_shift), F32), jnp.zeros((bsz, heads, HEAD, HEAD), F32),
                                lw["rwkv"], lw["e_mat"], chunk, n_seq)
    zeros_state = jnp.zeros((bsz, n_state), F32)
    hg_p, re_p, im_p = _s5(u_p.reshape(bsz, t, c_u), zeros_state, zeros_state, *s5_w, tt=s5_tt)

    oa_s, wkv_s = _rwkv_step(pr_s, shift0, wkv0, lw["rwkv"], lw["e_mat"])
    hg_s, re_s, im_s = _s5_step(u_s, re0, im0, *s5_w)

    y_p, y_s = _tail((xp2d, oa_p.reshape(bsz * t, -1), hg_p.reshape(bsz * t, c_u), gates_p),
                     (xs2d, oa_s, hg_s, gates_s), lw["tail"], row_tile)
    st_shape = lambda n: (n,) + re0.shape[1:]
    return ((y_p.reshape(x_p.shape), pr_p3[:, -1], wkv_p, re_p.reshape(st_shape(bsz)), im_p.reshape(st_shape(bsz))),
            (y_s.reshape(x_s.shape), pr_s, wkv_s, re_s.reshape(st_shape(bs)), im_s.reshape(st_shape(bs))))


def _prepare_layer_weights(l, p):
    row = lambda a: a[l][None, :].astype(F32)
    width = p["w0"].shape[-1]
    n_dec, n_aaa, n_gate = p["w_decay_up"].shape[1], p["w_aaa_up"].shape[1], p["w_gate_up"].shape[1]
    assert n_dec + n_aaa + n_gate == LORA_PAD

    def lora_pad(wup, start):
        return jnp.zeros((LORA_PAD, width), F32).at[start:start + wup.shape[0]].set(wup).astype(BF16)

    rwkv = {
        "mu": row(p["mu_shift"]), "w0": row(p["w0"]), "a0": row(p["a0"]), "k_k": row(p["k_k"]),
        "k_a": row(p["k_a"]), "r_k": row(p["r_k"]), "lnx_g": row(p["lnx_g"]), "lnx_b": row(p["lnx_b"]),
        "wd": lora_pad(p["w_decay_up"][l], 0),
        "wa": lora_pad(p["w_aaa_up"][l], n_dec),
        "wg": lora_pad(p["w_gate_up"][l], n_dec + n_aaa),
    }
    head_id = jnp.arange(MXU_DIM) // HEAD
    e_mat = (head_id[:, None] == head_id[None, :]).astype(BF16)

    lb_re, lb_im, bb_re_t, bb_im_t = _s5_discretise(
        p["s5_lam_re"][l], p["s5_lam_im"][l], p["s5_log_dt"][l],
        jnp.swapaxes(p["s5_b_re"][l], 1, 2), jnp.swapaxes(p["s5_b_im"][l], 1, 2))
    n_state = lb_re.shape[0] * lb_re.shape[2]
    to_out = lambda cc: _block_diag_slabs(jnp.swapaxes(cc, 1, 2))
    wb = jnp.concatenate([_block_diag_slabs(bb_re_t), _block_diag_slabs(bb_im_t)], axis=-1).astype(BF16)
    wc = jnp.concatenate([to_out(p["s5_c_re"][l]), -to_out(p["s5_c_im"][l])], axis=1).astype(BF16)

    bf = lambda a: a[l].astype(BF16)
    tail = {
        "wro": bf(p["w_rwkv_out"]), "w1": bf(p["glu_w1"]), "b1": row(p["glu_b1"]), "w2": bf(p["glu_w2"]),
        "b2": row(p["glu_b2"]), "wmo": bf(p["w_merge_out"]), "npm": row(p["norm_post_mix"]),
        "nf": row(p["norm_pre_ffn"]), "npf": row(p["norm_post_ffn"]),
        "wg": bf(p["w_ffn_gate"]), "wu": bf(p["w_ffn_up"]), "wd": bf(p["w_ffn_down"]),
    }
    return {
        "norm_pre_mix": row(p["norm_pre_mix"]), "w_in": bf(p["w_in"]), "rwkv": rwkv, "e_mat": e_mat,
        "lb_re": lb_re.reshape(1, n_state), "lb_im": lb_im.reshape(1, n_state), "wb": wb, "wc": wc,
        "d_skip": row(p["s5_d"]), "tail": tail,
    }


_PARAM_NAMES = ("norm_pre_mix", "norm_post_mix", "norm_pre_ffn", "norm_post_ffn", "w_in", "mu_shift",
                "w0", "w_decay_up", "a0", "w_aaa_up", "w_gate_up", "k_k", "k_a", "r_k", "lnx_g", "lnx_b",
                "w_rwkv_out", "s5_lam_re", "s5_lam_im", "s5_log_dt", "s5_b_re", "s5_b_im", "s5_c_re",
                "s5_c_im", "s5_d", "glu_w1", "glu_b1", "glu_w2", "glu_b2", "w_merge_out",
                "w_ffn_gate", "w_ffn_up", "w_ffn_down")


def _forward(x_prompt, x_sample, state_shift, state_wkv, state_s5_re, state_s5_im, params,
             *, chunk=64, n_seq=8, s5_tt=128, row_tile=512, proj_tile=1024):
    depth = params["w_in"].shape[0]
    yp, ys = x_prompt, x_sample
    outs_p, outs_s = [], []
    for l in range(depth):
        lw = _prepare_layer_weights(l, params)
        (yp, *st_p), (ys, *st_s) = _layer(yp, ys, state_shift[l], state_wkv[l], state_s5_re[l], state_s5_im[l], lw,
                                          chunk=chunk, n_seq=n_seq, s5_tt=s5_tt, row_tile=row_tile, proj_tile=proj_tile)
        outs_p.append(st_p)
        outs_s.append(st_s)
    stack = lambda outs, i, dt: jnp.stack([o[i] for o in outs]).astype(dt)
    dt_p, dt_s = x_prompt.dtype, x_sample.dtype
    return (yp, ys,
            stack(outs_p, 0, dt_p), stack(outs_p, 1, dt_p), stack(outs_p, 2, dt_p), stack(outs_p, 3, dt_p),
            stack(outs_s, 0, dt_s), stack(outs_s, 1, dt_s), stack(outs_s, 2, dt_s), stack(outs_s, 3, dt_s))


def kernel(x_prompt, x_sample, state_shift, state_wkv, state_s5_re, state_s5_im, norm_pre_mix, norm_post_mix, norm_pre_ffn, norm_post_ffn, w_in, mu_shift, w0, w_decay_up, a0, w_aaa_up, w_gate_up, k_k, k_a, r_k, lnx_g, lnx_b, w_rwkv_out, s5_lam_re, s5_lam_im, s5_log_dt, s5_b_re, s5_b_im, s5_c_re, s5_c_im, s5_d, glu_w1, glu_b1, glu_w2, glu_b2, w_merge_out, w_ffn_gate, w_ffn_up, w_ffn_down):
    params = dict(zip(_PARAM_NAMES, (norm_pre_mix, norm_post_mix, norm_pre_ffn, norm_post_ffn, w_in, mu_shift,
                                     w0, w_decay_up, a0, w_aaa_up, w_gate_up, k_k, k_a, r_k, lnx_g, lnx_b,
                                     w_rwkv_out, s5_lam_re, s5_lam_im, s5_log_dt, s5_b_re, s5_b_im, s5_c_re,
                                     s5_c_im, s5_d, glu_w1, glu_b1, glu_w2, glu_b2, w_merge_out,
                                     w_ffn_gate, w_ffn_up, w_ffn_down)))
    return _forward(x_prompt, x_sample, state_shift, state_wkv, state_s5_re, state_s5_im, params)
```

```python
import functools
import math

import jax
import jax.numpy as jnp
from jax import lax
from jax.experimental import pallas as pl
from jax.experimental.pallas import tpu as pltpu

F32 = jnp.float32
BF16 = jnp.bfloat16

NORM_EPS = 1e-6
LNX_EPS = 64e-5
HEAD = 64
GROUP_HEADS = 4
GROUP_W = GROUP_HEADS * HEAD
LORA_PAD = 128
S5_GROUP = 16
S5_STATE = 64
SLAB_GROUPS = 8
SUBLANES = 8
LANES = 128
MXU_DIM = 256
S5_ROW_BLOCK = 256
SUB_ROWS = 128
TAIL_SUB_ROWS = 256
VMEM_LIMIT =56 * 1024 * 1024

NN = (((1,), (0,)), ((), ()))
NT = (((1,), (1,)), ((), ()))


def _dot(a, b, dims=NN):
    return lax.dot_general(a, b, dims, preferred_element_type=F32)


def _split2(x):
    hi = x.astype(BF16)
    lo = (x - hi.astype(F32)).astype(BF16)
    return hi, lo


def _split3(x):
    hi = x.astype(BF16)
    r1 = x - hi.astype(F32)
    mid = r1.astype(BF16)
    lo = (r1 - mid.astype(F32)).astype(BF16)
    return hi, mid, lo


def _mm1(a, b, dims=NN):
    return _dot(a.astype(BF16), b.astype(BF16), dims)


def _mm_exact_lhs(a_bf16, b):
    h, m, l = _split3(b)
    return _dot(a_bf16, h) + (_dot(a_bf16, m) + _dot(a_bf16, l))


def _rms(x, g):
    return x * lax.rsqrt(jnp.mean(x * x, axis=-1, keepdims=True) + NORM_EPS) * g


def _sigmoid(x):
    return 1.0 / (1.0 + jnp.exp(-x))


def _gelu_tanh(x):
    c = math.sqrt(2.0 / math.pi)
    return 0.5 * x * (1.0 + jnp.tanh(c * (x + 0.044715 * (x * x * x))))


def _const_spec(shape, single_buffer=False):
    idx = lambda *_: (0,) * len(shape)
    if single_buffer:
        return pl.BlockSpec(shape, idx, pipeline_mode=pl.Buffered(1))
    return pl.BlockSpec(shape, idx)


def _two_group_grid(rows_main, tm):
    n_main = rows_main // tm
    main_spec = lambda width: pl.BlockSpec((tm, width), lambda i: (jnp.minimum(i, n_main - 1), 0))
    return n_main, main_spec


def _on_group(body, main_refs, side_refs):
    i, n_main = pl.program_id(0), pl.num_programs(0) - 1
    pl.when(i < n_main)(functools.partial(body, *main_refs))
    pl.when(i == n_main)(functools.partial(body, *side_refs))


def _proj_kernel(x_ref, xs_ref, g_ref, w_ref, pr_ref, u_ref, gt_ref, prs_ref, us_ref, gts_ref, *, c_shift, c_u):
    def project(x_ref, pr_ref, u_ref, gt_ref):
        tm = x_ref.shape[0]
        sub = min(tm, SUB_ROWS)
        norm = lambda i: _rms(x_ref[i * sub:(i + 1) * sub, :], g_ref[...]).astype(BF16)
        hb_next = norm(0)
        for i in range(tm // sub):
            hb, rows = hb_next, slice(i * sub, (i + 1) * sub)
            if (i + 1) * sub < tm:
                hb_next = norm(i + 1)
            pr_ref[rows, :] = _dot(hb, w_ref[:, :c_shift])
            u_ref[rows, :] = _dot(hb, w_ref[:, c_shift:c_shift + c_u])
            gt_ref[rows, :] = _sigmoid(_dot(hb, w_ref[:, c_shift + c_u:])).astype(gt_ref.dtype)

    _on_group(project, (x_ref, pr_ref, u_ref, gt_ref), (xs_ref, prs_ref, us_ref, gts_ref))


def _proj(x_main, x_side, g, w_in_bf16, c_shift, c_u, tm):
    d = x_main.shape[1]
    cols = w_in_bf16.shape[1]
    widths = (c_shift, c_u, cols - c_shift - c_u)
    n_main, main_spec = _two_group_grid(x_main.shape[0], tm)
    side_spec = lambda width: _const_spec((x_side.shape[0], width))
    outs = pl.pallas_call(
        functools.partial(_proj_kernel, c_shift=c_shift, c_u=c_u),
        grid=(n_main + 1,),
        in_specs=[main_spec(d), side_spec(d), _const_spec((1, d)), _const_spec((d, cols), True)],
        out_specs=[main_spec(wd) for wd in widths] + [side_spec(wd) for wd in widths],
        out_shape=[jax.ShapeDtypeStruct((x.shape[0], wd), dt) for x in (x_main, x_side)
                   for wd, dt in zip(widths, (F32, F32, BF16))],
        compiler_params=pltpu.CompilerParams(dimension_semantics=("arbitrary",),
                                             vmem_limit_bytes=VMEM_LIMIT),
        name="proj",
    )(x_main, x_side, g, w_in_bf16)
    return outs[:3], outs[3:]


def _head_sum(x, e_ref, terms=1):
    rows, width = x.shape
    gw = e_ref.shape[0]
    n_lg = width // gw
    parts = _split2(x) if terms == 2 else (x.astype(BF16),)
    stacked = jnp.concatenate([part[:, j * gw:(j + 1) * gw] for part in parts for j in range(n_lg)], axis=0)
    sums = _dot(stacked, e_ref[...])
    blocks = [sums[i * rows:(i + 1) * rows] for i in range(terms * n_lg)]
    return jnp.concatenate([sum(blocks[j::n_lg][1:], blocks[j]) for j in range(n_lg)], axis=1)


def _rwkv_token_prep(xr, w, e_ref):
    width = w["w0"].shape[-1]
    r = xr[:, :width]
    k = xr[:, width:2 * width]
    v = xr[:, 2 * width:3 * width]
    lo = xr[:, 3 * width:3 * width + LORA_PAD]
    wl = w["w0"][...] + _mm1(jnp.tanh(lo), w["wd"][...])
    lw = -math.exp(-0.5) * _sigmoid(wl)
    a = _sigmoid(w["a0"][...] + _mm1(lo, w["wa"][...]))
    g = _mm1(_sigmoid(lo), w["wg"][...])
    kk = k * w["k_k"][...]
    kk = kk * lax.rsqrt(jnp.maximum(_head_sum(kk * kk, e_ref), 1e-24))
    kmod = k * (1.0 + (a - 1.0) * w["k_a"][...])
    return r, kmod, v, kk, kk * a, lw, g


def _rwkv_bonus(r, kmod, v, w, e_ref):
    return _head_sum(r * kmod * w["r_k"][...], e_ref) * v


def _rwkv_post(o, bonus, g, w, e_ref):
    inv_n = 1.0 / HEAD
    mu = _head_sum(o, e_ref, terms=2) * inv_n
    oc = o - mu
    var = _head_sum(oc * oc, e_ref) * inv_n
    on = oc * lax.rsqrt(var + LNX_EPS) * w["lnx_g"][...] + w["lnx_b"][...]
    return (on + bonus) * g


_RWKV_W_NAMES = ("mu", "w0", "a0", "k_k", "k_a", "r_k", "lnx_g", "lnx_b", "wd", "wa", "wg")


def _rwkv_chunk_kernel(pr_ref, shift0_ref, s0_ref, *rest, chunk, width, n_seq):
    n_w = len(_RWKV_W_NAMES)
    w = dict(zip(_RWKV_W_NAMES, rest[:n_w]))
    e_ref, tri_ref = rest[n_w], rest[n_w + 1]
    oa_ref, s1_ref = rest[n_w + 2], rest[n_w + 3]
    carry_ref, z_ref, ops_ref, aux_ref, gall_ref = rest[n_w + 4:n_w + 9]
    step = pl.program_id(1)
    n_chunks = pl.num_programs(1) - 1
    n_groups = width // GROUP_W
    C = chunk
    GC = GROUP_HEADS * C

    HALF = C // 2

    def lane_block_masks(n_lanes, block):
        lane_block = lax.broadcasted_iota(jnp.int32, (1, n_lanes), 1) // block
        return [lane_block == b for b in range(n_lanes // block)]

    vec_masks = lane_block_masks(GROUP_W, HEAD)
    half_masks = lane_block_masks(GC, HALF)

    def stack(x, masks):
        xb = x.astype(BF16)
        zero = jnp.zeros_like(xb)
        return jnp.concatenate([jnp.where(m, xb, zero) for m in masks], axis=0)

    @pl.when(step == 0)
    def _():
        carry_ref[...] = shift0_ref[...]
        ops_ref[1] = jnp.zeros(ops_ref.shape[1:], ops_ref.dtype)
        aux_ref[1] = jnp.zeros(aux_ref.shape[1:], aux_ref.dtype)
        gall_ref[1] = jnp.zeros(gall_ref.shape[1:], gall_ref.dtype)
        z_ref[...] = jnp.zeros(z_ref.shape, z_ref.dtype)

    @pl.when(step == 1)
    def _():
        for q in range(n_seq):
            for gi in range(n_groups):
                z_ref[q, gi] = jnp.zeros((GROUP_W, GROUP_W), F32)
                for h in range(GROUP_HEADS):
                    z_ref[q, gi, h * HEAD:(h + 1) * HEAD, h * HEAD:(h + 1) * HEAD] = s0_ref[q, gi * GROUP_HEADS + h]

    tok = lax.broadcasted_iota(jnp.int32, (C, GC), 0)
    col = lax.broadcasted_iota(jnp.int32, (C, GC), 1) % C
    strict = tok > col
    incl = tok >= col
    lane = lax.broadcasted_iota(jnp.int32, (1, GC), 1)
    in_half0 = (lane % C) < HALF
    eye_half = jnp.where(lax.broadcasted_iota(jnp.int32, (HALF, GC), 0)
                         == lax.broadcasted_iota(jnp.int32, (HALF, GC), 1) % HALF, 1.0, 0.0).astype(F32)
    half0_from_half1 = [(lane // HALF == b - 1) if b % 2 else (lane < 0) for b in range(GC // HALF)]
    zi = lax.broadcasted_iota(jnp.int32, (GROUP_W, GROUP_W), 0) // HEAD
    zj = lax.broadcasted_iota(jnp.int32, (GROUP_W, GROUP_W), 1) // HEAD
    block_diag = zi == zj
    rows2 = lambda top, bottom: jnp.concatenate([top, bottom], axis=0).astype(BF16)
    half_dot = lambda x, right: _dot(x.astype(BF16), right)

    def recurrence(r_slot):
        chains = [(q, gi) for q in range(n_seq) for gi in range(n_groups)]
        op = lambda i, q, gi: ops_ref[r_slot, i, q * C:(q + 1) * C, gi * GROUP_W:(gi + 1) * GROUP_W]
        kap_t, r_t, v = ([op(i, q, gi) for q, gi in chains] for i in (0, 1, 4))
        kap_s = [stack(x, vec_masks) for x in kap_t]
        v_s = [stack(x, vec_masks) for x in v]
        grams = []
        for i, (q, gi) in enumerate(chains):
            right = jnp.concatenate([stack(op(2, q, gi), vec_masks), stack(op(3, q, gi), vec_masks)], axis=0)
            grams.append(_dot(rows2(kap_t[i], r_t[i]), right, NT))
        yield
        a_k = [jnp.where(strict, gm[:C, :GC], 0.0) for gm in grams]
        a_b = [jnp.where(strict, gm[:C, GC:], 0.0) for gm in grams]
        a_rk = [jnp.where(incl, gm[C:, :GC], 0.0) for gm in grams]
        a_rb = [jnp.where(incl, gm[C:, GC:], 0.0) for gm in grams]

        diag = [jnp.where(in_half0, a[:HALF], a[HALF:]) for a in a_b]
        a21 = [jnp.where(in_half0, a[HALF:], 0.0) for a in a_b]
        ps = [-d for d in diag]
        ts = [eye_half + p for p in ps]
        ps = [half_dot(p, stack(p, half_masks)) for p in ps]
        yield
        covered = 2
        while covered < HALF:
            powers = [stack(p, half_masks) for p in ps]
            if 2 * covered < HALF:
                both = [_dot(rows2(t, p), pw) for t, p, pw in zip(ts, ps, powers)]
                ts = [t + x[:HALF] for t, x in zip(ts, both)]
                ps = [x[HALF:] for x in both]
            else:
                ts = [t + half_dot(t, pw) for t, pw in zip(ts, powers)]
            covered *= 2
            yield
        a21_t1 = [half_dot(x, stack(t, half_masks)) for x, t in zip(a21, ts)]
        yield
        t21 = [half_dot(t, stack(x, half0_from_half1)) for t, x in zip(ts, a21_t1)]
        ts = [jnp.concatenate([jnp.where(in_half0, t, 0.0), jnp.where(in_half0, -x, t)], axis=0)
              for t, x in zip(ts, t21)]
        yield

        av = [_dot(rows2(a, ar), vs) for a, ar, vs in zip(a_k, a_rk, v_s)]
        yield
        tw = [_dot(t.astype(BF16), jnp.concatenate([ks, stack(x[:C], vec_masks)], axis=1))
              for t, ks, x in zip(ts, kap_s, av)]
        yield
        zs = [z_ref[q, gi] for q, gi in chains]
        pz = [_dot(rows2(x[:, :GROUP_W], rt), z.astype(BF16), NT)
              for x, rt, z in zip(tw, r_t, zs)]
        us = [p[:C] + x[:, GROUP_W:] for p, x in zip(pz, tw)]
        yield
        arbu = [_dot(a.astype(BF16), stack(u, vec_masks)) for a, u in zip(a_rb, us)]
        o_blk = [p[C:] + x[C:] - y for p, x, y in zip(pz, av, arbu)]
        for i, (q, gi) in enumerate(chains):
            vu_t = jnp.concatenate([v[i].astype(F32), us[i]], axis=0).T.astype(BF16)
            kb = jnp.concatenate([op(5, q, gi), op(6, q, gi)], axis=0)
            g_all = gall_ref[r_slot, q][:, gi * GROUP_W:(gi + 1) * GROUP_W]
            z_ref[q, gi] = zs[i] * g_all + jnp.where(block_diag, _dot(vu_t, kb), 0.0)
        yield
        o = jnp.concatenate([jnp.concatenate(o_blk[q * n_groups:(q + 1) * n_groups], axis=1)
                             for q in range(n_seq)], axis=0)
        oa_ref[...] = _rwkv_post(o, aux_ref[r_slot, 0], aux_ref[r_slot, 1], w, e_ref
                                 ).astype(oa_ref.dtype).reshape(oa_ref.shape)

    def prepare(q, w_slot):
        rows = slice(q * C, (q + 1) * C)
        pr = pr_ref[q]
        row_id = lax.broadcasted_iota(jnp.int32, pr.shape, 0)
        pr_prev = jnp.where(row_id == 0, carry_ref[q], pltpu.roll(pr, 1, axis=0))
        carry_ref[q] = pr[C - 1:C, :]
        xr = pr + (pr_prev - pr) * w["mu"][...]
        r, kmod, vv, kk, bvec, lw, g = _rwkv_token_prep(xr, w, e_ref)
        aux_ref[w_slot, 0, rows] = _rwkv_bonus(r, kmod, vv, w, e_ref)
        aux_ref[w_slot, 1, rows] = g
        yield
        cum = _mm_exact_lhs(tri_ref[...], lw)
        cum_last = cum[C - 1:C, :]
        g_neg = jnp.exp(-cum)
        g_end = jnp.exp(cum_last - cum)
        prepared = (kk * jnp.exp(cum - lw),
                    r * jnp.exp(cum),
                    kmod * g_neg, bvec * g_neg,
                    vv, kmod * g_end, -(bvec * g_end))
        for i, x in enumerate(prepared):
            ops_ref[w_slot, i, rows] = x.astype(ops_ref.dtype)
        gall_ref[w_slot, q] = jnp.exp(cum_last)

    w_slot = step % 2
    pending = [gen for gen in [prepare(q, w_slot) for q in range(n_seq)] for _ in range(2)]
    for _ in recurrence(1 - w_slot):
        if pending:
            next(pending.pop(0), None)
    for gen in pending:
        next(gen, None)

    @pl.when(step == n_chunks)
    def _():
        for q in range(n_seq):
            for gi in range(n_groups):
                for h in range(GROUP_HEADS):
                    s1_ref[q, gi * GROUP_HEADS + h] = z_ref[q, gi, h * HEAD:(h + 1) * HEAD, h * HEAD:(h + 1) * HEAD]


_N_OPS = 7


def _rwkv_weight_inputs(wts):
    return [wts[n] for n in _RWKV_W_NAMES]


def _rwkv_weight_specs(wts):
    return [_const_spec(wts[n].shape) for n in _RWKV_W_NAMES]


def _rwkv_chunked(pr3d, shift0, wkv0, wts, e_mat, chunk, n_seq):
    bsz, t, c_shift = pr3d.shape
    heads = wkv0.shape[1]
    width = heads * HEAD
    tri = jnp.tril(jnp.ones((chunk, chunk), F32)).astype(BF16)
    kern = functools.partial(_rwkv_chunk_kernel, chunk=chunk, width=width, n_seq=n_seq)
    n_chunks = t // chunk
    rows = n_seq * chunk
    return pl.pallas_call(
        kern,
        grid=(bsz // n_seq, n_chunks + 1),
        in_specs=[pl.BlockSpec((n_seq, chunk, c_shift), lambda b, s: (b, jnp.minimum(s, n_chunks - 1), 0)),
                  pl.BlockSpec((n_seq, 1, c_shift), lambda b, s: (b, 0, 0)),
                  pl.BlockSpec((n_seq, heads, HEAD, HEAD), lambda b, s: (b, 0, 0, 0))]
                 + _rwkv_weight_specs(wts)
                 + [_const_spec(e_mat.shape), _const_spec(tri.shape)],
        out_specs=[pl.BlockSpec((n_seq, chunk, width), lambda b, s: (b, jnp.maximum(s - 1, 0), 0)),
                   pl.BlockSpec((n_seq, heads, HEAD, HEAD), lambda b, s: (b, 0, 0, 0))],
        out_shape=[jax.ShapeDtypeStruct((bsz, t, width), BF16),
                   jax.ShapeDtypeStruct((bsz, heads, HEAD, HEAD), F32)],
        scratch_shapes=[pltpu.VMEM((n_seq, 1, c_shift), F32),
                        pltpu.VMEM((n_seq, width // GROUP_W, GROUP_W, GROUP_W), F32),
                        pltpu.VMEM((2, _N_OPS, rows, width), BF16),
                        pltpu.VMEM((2, 2, rows, width), F32),
                        pltpu.VMEM((2, n_seq, 1, width), F32)],
        compiler_params=pltpu.CompilerParams(dimension_semantics=("parallel", "arbitrary"),
                                             vmem_limit_bytes=VMEM_LIMIT),
        name="rwkv_chunk",
    )(pr3d, shift0[:, None, :], wkv0, *_rwkv_weight_inputs(wts), e_mat, tri)


_STEP_VECS = 6


def _rwkv_step_kernel(pr_ref, shift0_ref, s_ref, *rest):
    n_w = len(_RWKV_W_NAMES)
    w = dict(zip(_RWKV_W_NAMES, rest[:n_w]))
    e_ref, oa_ref, s1_ref, vec_ref, post_ref, o_ref = rest[n_w:n_w + 6]
    h = pl.program_id(0)

    @pl.when(h == 0)
    def _():
        pr = pr_ref[...]
        xr = pr + (shift0_ref[...] - pr) * w["mu"][...]
        r, kmod, v, kk, bvec, lw, g = _rwkv_token_prep(xr, w, e_ref)
        for i, x in enumerate((kk, bvec, jnp.exp(lw), kmod, r, v)):
            vec_ref[i] = x.T
        post_ref[0] = _rwkv_bonus(r, kmod, v, w, e_ref)
        post_ref[1] = g

    base = pl.multiple_of(h * HEAD, HEAD)
    kk_h, b_h, dec_h, k_h, r_h = [vec_ref[i, pl.ds(base, HEAD), :] for i in range(5)]

    def value_row(vi, carry):
        tile = s_ref[0, vi]
        s_kappa = jnp.sum(tile * kk_h, axis=0, keepdims=True)
        new = tile * dec_h - s_kappa * b_h + vec_ref[5, pl.ds(base + vi, 1), :] * k_h
        s1_ref[0, vi] = new
        o_ref[pl.ds(base + vi, 1), :] = jnp.sum(new * r_h, axis=0, keepdims=True)
        return carry

    lax.fori_loop(0, HEAD, value_row, 0, unroll=4)

    @pl.when(h == pl.num_programs(0) - 1)
    def _():
        oa_ref[...] = _rwkv_post(o_ref[...].T, post_ref[0], post_ref[1], w, e_ref).astype(oa_ref.dtype)


def _rwkv_step(pr2d, shift0, wkv0, wts, e_mat):
    bsz, c_shift = pr2d.shape
    heads = wkv0.shape[1]
    width = heads * HEAD
    full = lambda a: _const_spec(a.shape)
    st = pl.BlockSpec((1, HEAD, HEAD, bsz), lambda hh: (hh, 0, 0, 0))
    oa, s1_t = pl.pallas_call(
        _rwkv_step_kernel,
        grid=(heads,),
        in_specs=[full(pr2d), full(shift0), st] + _rwkv_weight_specs(wts) + [full(e_mat)],
        out_specs=[_const_spec((bsz, width)), st],
        out_shape=[jax.ShapeDtypeStruct((bsz, width), BF16),
                   jax.ShapeDtypeStruct((heads, HEAD, HEAD, bsz), F32)],
        scratch_shapes=[pltpu.VMEM((_STEP_VECS, width, bsz), F32),
                        pltpu.VMEM((2, bsz, width), F32),
                        pltpu.VMEM((width, bsz), F32)],
        compiler_params=pltpu.CompilerParams(dimension_semantics=("arbitrary",), vmem_limit_bytes=VMEM_LIMIT),
        name="rwkv_step",
    )(pr2d, shift0, jnp.transpose(wkv0, (1, 2, 3, 0)), *_rwkv_weight_inputs(wts), e_mat)
    return oa, jnp.transpose(s1_t, (3, 0, 1, 2))


def _s5_disc_kernel(lre_ref, lim_ref, ldt_ref, bre_ref, bim_ref, lbr_ref, lbi_ref, bbr_ref, bbi_ref):
    lam_re, lam_im = lre_ref[...], lim_ref[...]
    dt = jnp.exp(ldt_ref[...])
    mag = jnp.exp(lam_re * dt)
    ang = lam_im * dt
    lb_re, lb_im = mag * jnp.cos(ang), mag * jnp.sin(ang)
    nr, ni = lb_re - 1.0, lb_im
    den = lam_re * lam_re + lam_im * lam_im
    f_re = (nr * lam_re + ni * lam_im) / den
    f_im = (ni * lam_re - nr * lam_im) / den
    b_re, b_im = bre_ref[...], bim_ref[...]
    lbr_ref[...] = lb_re
    lbi_ref[...] = lb_im
    bbr_ref[...] = f_re * b_re - f_im * b_im
    bbi_ref[...] = f_re * b_im + f_im * b_re


def _s5_discretise(lam_re, lam_im, log_dt, b_re_t, b_im_t):
    g, p = lam_re.shape
    full = lambda a: _const_spec(a.shape)
    args = (lam_re[:, None, :], lam_im[:, None, :], log_dt[:, None, None], b_re_t, b_im_t)
    return pl.pallas_call(
        _s5_disc_kernel,
        grid=(1,),
        in_specs=[full(a) for a in args],
        out_specs=[_const_spec((g, 1, p))] * 2 + [full(b_re_t)] * 2,
        out_shape=[jax.ShapeDtypeStruct((g, 1, p), F32)] * 2 + [jax.ShapeDtypeStruct(b_re_t.shape, F32)] * 2,
        name="s5_discretise",
    )(*args)


def _s5_kernel(u_ref, re0_ref, im0_ref, lbr_ref, lbi_ref, wb_ref, wc_ref, d_ref,
               h_ref, re1_ref, im1_ref, u_tm, h_tm, bu0, bu1, xs0, xs1, *, tt, n_slabs):
    t_blk = pl.program_id(1)
    rows = SUBLANES * tt
    s_w = SLAB_GROUPS * S5_STATE
    u_w = SLAB_GROUPS * S5_GROUP
    bu, xs = (bu0, bu1), (xs0, xs1)
    rb = min(rows, S5_ROW_BLOCK)
    steps_per_rb = rb // SUBLANES

    @pl.when(t_blk == 0)
    def _():
        re1_ref[...] = re0_ref[...]
        im1_ref[...] = im0_ref[...]

    u_tm[...] = jnp.swapaxes(u_ref[...], 0, 1).reshape(rows, n_slabs * u_w)

    def project_in(s, j):
        r0 = j * rb
        bu[s % 2][r0:r0 + rb, :] = _mm1(u_tm[r0:r0 + rb, s * u_w:(s + 1) * u_w], wb_ref[s])

    def project_out(s, j):
        r0 = j * rb
        lanes = slice(s * u_w, (s + 1) * u_w)
        h_tm[r0:r0 + rb, lanes] = (_mm1(xs[s % 2][r0:r0 + rb, :], wc_ref[s])
                                   + d_ref[:, lanes] * u_tm[r0:r0 + rb, lanes])

    for j in range(rows // rb):
        project_in(0, j)
    for p in range(n_slabs + 1):
        if p < n_slabs:
            st = slice(p * s_w, (p + 1) * s_w)
            lbr = jnp.broadcast_to(lbr_ref[:, st], (SUBLANES, s_w))
            lbi = jnp.broadcast_to(lbi_ref[:, st], (SUBLANES, s_w))
            xr, xi = re1_ref[:, st], im1_ref[:, st]
        for j in range(rows // rb):
            if p < n_slabs:
                for t in range(j * steps_per_rb, (j + 1) * steps_per_rb):
                    at_t = slice(t * SUBLANES, (t + 1) * SUBLANES)
                    xr, xi = (lbr * xr - lbi * xi + bu[p % 2][at_t, :s_w],
                              lbr * xi + lbi * xr + bu[p % 2][at_t, s_w:])
                    xs[p % 2][at_t, :s_w] = xr
                    xs[p % 2][at_t, s_w:] = xi
            if p + 1 < n_slabs:
                project_in(p + 1, j)
            if p >= 1:
                project_out(p - 1, j)
        if p < n_slabs:
            re1_ref[:, st] = xr
            im1_ref[:, st] = xi
    h = h_tm[...].reshape(tt, SUBLANES, n_slabs * u_w)
    h_ref[...] = jnp.swapaxes(h, 0, 1).astype(h_ref.dtype)


def _s5(u_blocks, re0, im0, lb_re, lb_im, wb, wc, d_skip, tt):
    n_slabs = wb.shape[0]
    n_state = re0.shape[1]
    blk = (SUBLANES, tt, u_blocks.shape[-1])
    grid = (u_blocks.shape[0] // SUBLANES, u_blocks.shape[1] // tt)
    rows = SUBLANES * tt
    u_spec = pl.BlockSpec(blk, lambda i, j: (i, j, 0))
    st_spec = pl.BlockSpec((SUBLANES, n_state), lambda i, j: (i, 0))
    full = lambda a: _const_spec(a.shape)
    return pl.pallas_call(
        functools.partial(_s5_kernel, tt=tt, n_slabs=n_slabs),
        grid=grid,
        in_specs=[u_spec, st_spec, st_spec, full(lb_re), full(lb_im), full(wb), full(wc), full(d_skip)],
        out_specs=[u_spec, st_spec, st_spec],
        out_shape=[jax.ShapeDtypeStruct(u_blocks.shape, F32),
                   jax.ShapeDtypeStruct(re0.shape, F32), jax.ShapeDtypeStruct(im0.shape, F32)],
        scratch_shapes=[pltpu.VMEM((rows, u_blocks.shape[-1]), F32)] * 2
                       + [pltpu.VMEM((rows, 2 * SLAB_GROUPS * S5_STATE), F32)] * 4,
        compiler_params=pltpu.CompilerParams(dimension_semantics=("parallel", "arbitrary"),
                                             vmem_limit_bytes=VMEM_LIMIT),
        name="s5_scan",
    )(u_blocks, re0, im0, lb_re, lb_im, wb, wc, d_skip)


def _s5_step_kernel(u_ref, re0_ref, im0_ref, lbr_ref, lbi_ref, wb_ref, wc_ref, d_ref,
                    h_ref, re1_ref, im1_ref, *, n_slabs):
    s_w = SLAB_GROUPS * S5_STATE
    u_w = SLAB_GROUPS * S5_GROUP
    u = u_ref[...]
    ys = []
    for s in range(n_slabs):
        us = u[:, s * u_w:(s + 1) * u_w]
        bu = _mm1(us, wb_ref[s])
        st = slice(s * s_w, (s + 1) * s_w)
        lbr, lbi = lbr_ref[:, st], lbi_ref[:, st]
        xr, xi = re0_ref[st, :].T, im0_ref[st, :].T
        nr = lbr * xr - lbi * xi + bu[:, :s_w]
        ni = lbr * xi + lbi * xr + bu[:, s_w:]
        re1_ref[st, :] = nr.T
        im1_ref[st, :] = ni.T
        ys.append(_mm1(jnp.concatenate([nr, ni], axis=1), wc_ref[s]) + d_ref[:, s * u_w:(s + 1) * u_w] * us)
    h_ref[...] = jnp.concatenate(ys, axis=1)


def _s5_step(u2d, re0, im0, lb_re, lb_im, wb, wc, d_skip):
    bs, g, p = re0.shape
    to_minor = lambda x: jnp.transpose(x, (1, 2, 0)).reshape(g * p, bs)
    from_minor = lambda x: jnp.transpose(x.reshape(g, p, bs), (2, 0, 1))
    y, re1, im1 = _s5_step_call(u2d, to_minor(re0), to_minor(im0), lb_re, lb_im, wb, wc, d_skip)
    return y, from_minor(re1), from_minor(im1)


def _s5_step_call(u2d, re0, im0, lb_re, lb_im, wb, wc, d_skip):
    full = lambda a: _const_spec(a.shape)
    args = (u2d, re0, im0, lb_re, lb_im, wb, wc, d_skip)
    return pl.pallas_call(
        functools.partial(_s5_step_kernel, n_slabs=wb.shape[0]),
        grid=(1,),
        in_specs=[full(a) for a in args],
        out_specs=[full(u2d), full(re0), full(im0)],
        out_shape=[jax.ShapeDtypeStruct(u2d.shape, F32),
                   jax.ShapeDtypeStruct(re0.shape, F32), jax.ShapeDtypeStruct(im0.shape, F32)],
        compiler_params=pltpu.CompilerParams(vmem_limit_bytes=VMEM_LIMIT),
        name="s5_step",
    )(*args)


def _block_diag_slabs(m):
    g, a, b = m.shape
    eye = jnp.eye(SLAB_GROUPS, dtype=m.dtype)
    m4 = m.reshape(g // SLAB_GROUPS, SLAB_GROUPS, a, b)
    return jnp.einsum("sgab,gh->sgahb", m4, eye).reshape(g // SLAB_GROUPS, SLAB_GROUPS * a, SLAB_GROUPS * b)


_TAIL_W_NAMES = ("wro", "w1", "b1", "w2", "b2", "wmo", "npm", "nf", "npf", "wg", "wu", "wd")


def _tail_kernel(*refs):
    n_act = 4
    main_in, side_in = refs[:n_act], refs[n_act:2 * n_act]
    wts = dict(zip(_TAIL_W_NAMES, refs[2 * n_act:2 * n_act + len(_TAIL_W_NAMES)]))
    y_ref, ys_ref = refs[-2:]

    def tail(x_ref, oa_ref, y_s5_ref, gt_ref, y_ref):
        d = x_ref.shape[-1]
        tm = x_ref.shape[0]
        n_sub = max(1, tm // TAIL_SUB_ROWS)
        subs = [slice(i * (tm // n_sub), (i + 1) * (tm // n_sub)) for i in range(n_sub)]
        hg = [_gelu_tanh(y_s5_ref[s, :]).astype(BF16) for s in subs]
        a_out = [_dot(oa_ref[s, :], wts["wro"][...]) for s in subs]
        b_lin = [_dot(h, wts["w1"][...]) + wts["b1"][...] for h in hg]
        b_gate = [_dot(h, wts["w2"][...]) + wts["b2"][...] for h in hg]
        merged = [(gt_ref[s, :d].astype(F32) * a + gt_ref[s, d:].astype(F32) * (bl * _sigmoid(bg))).astype(BF16)
                  for s, a, bl, bg in zip(subs, a_out, b_lin, b_gate)]
        mix = [_dot(m, wts["wmo"][...]) for m in merged]
        x1 = [x_ref[s, :] + _rms(m, wts["npm"][...]) for s, m in zip(subs, mix)]
        hb = [_rms(x, wts["nf"][...]).astype(BF16) for x in x1]
        gate = [_dot(h, wts["wg"][...]) for h in hb]
        up = [_dot(h, wts["wu"][...]) for h in hb]
        act = [(g * _sigmoid(g) * u).astype(BF16) for g, u in zip(gate, up)]
        f = [_dot(a, wts["wd"][...]) for a in act]
        for s, x, ff in zip(subs, x1, f):
            y_ref[s, :] = x + _rms(ff, wts["npf"][...])

    _on_group(tail, (*main_in, y_ref), (*side_in, ys_ref))


def _tail(acts_main, acts_side, tw, tm):
    d = acts_main[0].shape[1]
    n_main, main_spec = _two_group_grid(acts_main[0].shape[0], tm)
    wargs = [tw[n] for n in _TAIL_W_NAMES]
    return pl.pallas_call(
        _tail_kernel,
        grid=(n_main + 1,),
        in_specs=[main_spec(a.shape[1]) for a in acts_main] + [_const_spec(a.shape) for a in acts_side]
                 + [_const_spec(a.shape, True) for a in wargs],
        out_specs=[main_spec(d), _const_spec((acts_side[0].shape[0], d))],
        out_shape=[jax.ShapeDtypeStruct((a[0].shape[0], d), F32) for a in (acts_main, acts_side)],
        compiler_params=pltpu.CompilerParams(dimension_semantics=("arbitrary",),
                                             vmem_limit_bytes=VMEM_LIMIT),
        name="tail",
    )(*acts_main, *acts_side, *wargs)


def _layer(x_p, x_s, shift0, wkv0, re0, im0, lw, *, chunk, n_seq, s5_tt, row_tile, proj_tile):
    bsz, t, d = x_p.shape
    bs = x_s.shape[0]
    assert x_s.shape[1] == 1 and (bsz * t) % row_tile == 0 and (bsz * t) % proj_tile == 0
    heads = wkv0.shape[1]
    c_shift = shift0.shape[-1]
    c_u = lw["d_skip"].shape[-1]
    n_state = re0.shape[1] * re0.shape[2]
    s5_w = (lw["lb_re"], lw["lb_im"], lw["wb"], lw["wc"], lw["d_skip"])
    xp2d, xs2d = x_p.reshape(bsz * t, d), x_s.reshape(bs, d)
    (pr_p, u_p, gates_p), (pr_s, u_s, gates_s) = _proj(xp2d, xs2d, lw["norm_pre_mix"], lw["w_in"],
                                                       c_shift, c_u, proj_tile)

    pr_p3 = pr_p.reshape(bsz, t, c_shift)
    oa_p, wkv_p = _rwkv_chunked(pr_p3, jnp.zeros((bsz, c_shift), F32), jnp.zeros((bsz, heads, HEAD, HEAD), F32),
                                lw["rwkv"], lw["e_mat"], chunk, n_seq)
    zeros_state = jnp.zeros((bsz, n_state), F32)
    hg_p, re_p, im_p = _s5(u_p.reshape(bsz, t, c_u), zeros_state, zeros_state, *s5_w, tt=s5_tt)

    oa_s, wkv_s = _rwkv_step(pr_s, shift0, wkv0, lw["rwkv"], lw["e_mat"])
    hg_s, re_s, im_s = _s5_step(u_s, re0, im0, *s5_w)

    y_p, y_s = _tail((xp2d, oa_p.reshape(bsz * t, -1), hg_p.reshape(bsz * t, c_u), gates_p),
                     (xs2d, oa_s, hg_s, gates_s), lw["tail"], row_tile)
    st_shape = lambda n: (n,) + re0.shape[1:]
    return ((y_p.reshape(x_p.shape), pr_p3[:, -1], wkv_p, re_p.reshape(st_shape(bsz)), im_p.reshape(st_shape(bsz))),
            (y_s.reshape(x_s.shape), pr_s, wkv_s, re_s.reshape(st_shape(bs)), im_s.reshape(st_shape(bs))))


def _prepare_layer_weights(l, p):
    row = lambda a: a[l][None, :].astype(F32)
    width = p["w0"].shape[-1]
    n_dec, n_aaa, n_gate = p["w_decay_up"].shape[1], p["w_aaa_up"].shape[1], p["w_gate_up"].shape[1]
    assert n_dec + n_aaa + n_gate == LORA_PAD

    def lora_pad(wup, start):
        return jnp.zeros((LORA_PAD, width), F32).at[start:start + wup.shape[0]].set(wup).astype(BF16)

    rwkv = {
        "mu": row(p["mu_shift"]), "w0": row(p["w0"]), "a0": row(p["a0"]), "k_k": row(p["k_k"]),
        "k_a": row(p["k_a"]), "r_k": row(p["r_k"]), "lnx_g": row(p["lnx_g"]), "lnx_b": row(p["lnx_b"]),
        "wd": lora_pad(p["w_decay_up"][l], 0),
        "wa": lora_pad(p["w_aaa_up"][l], n_dec),
        "wg": lora_pad(p["w_gate_up"][l], n_dec + n_aaa),
    }
    head_id = jnp.arange(MXU_DIM) // HEAD
    e_mat = (head_id[:, None] == head_id[None, :]).astype(BF16)

    lb_re, lb_im, bb_re_t, bb_im_t = _s5_discretise(
        p["s5_lam_re"][l], p["s5_lam_im"][l], p["s5_log_dt"][l],
        jnp.swapaxes(p["s5_b_re"][l], 1, 2), jnp.swapaxes(p["s5_b_im"][l], 1, 2))
    n_state = lb_re.shape[0] * lb_re.shape[2]
    to_out = lambda cc: _block_diag_slabs(jnp.swapaxes(cc, 1, 2))
    wb = jnp.concatenate([_block_diag_slabs(bb_re_t), _block_diag_slabs(bb_im_t)], axis=-1).astype(BF16)
    wc = jnp.concatenate([to_out(p["s5_c_re"][l]), -to_out(p["s5_c_im"][l])], axis=1).astype(BF16)

    bf = lambda a: a[l].astype(BF16)
    tail = {
        "wro": bf(p["w_rwkv_out"]), "w1": bf(p["glu_w1"]), "b1": row(p["glu_b1"]), "w2": bf(p["glu_w2"]),
        "b2": row(p["glu_b2"]), "wmo": bf(p["w_merge_out"]), "npm": row(p["norm_post_mix"]),
        "nf": row(p["norm_pre_ffn"]), "npf": row(p["norm_post_ffn"]),
        "wg": bf(p["w_ffn_gate"]), "wu": bf(p["w_ffn_up"]), "wd": bf(p["w_ffn_down"]),
    }
    return {
        "norm_pre_mix": row(p["norm_pre_mix"]), "w_in": bf(p["w_in"]), "rwkv": rwkv, "e_mat": e_mat,
        "lb_re": lb_re.reshape(1, n_state), "lb_im": lb_im.reshape(1, n_state), "wb": wb, "wc": wc,
        "d_skip": row(p["s5_d"]), "tail": tail,
    }


_PARAM_NAMES = ("norm_pre_mix", "norm_post_mix", "norm_pre_ffn", "norm_post_ffn", "w_in", "mu_shift",
                "w0", "w_decay_up", "a0", "w_aaa_up", "w_gate_up", "k_k", "k_a", "r_k", "lnx_g", "lnx_b",
                "w_rwkv_out", "s5_lam_re", "s5_lam_im", "s5_log_dt", "s5_b_re", "s5_b_im", "s5_c_re",
                "s5_c_im", "s5_d", "glu_w1", "glu_b1", "glu_w2", "glu_b2", "w_merge_out",
                "w_ffn_gate", "w_ffn_up", "w_ffn_down")


def _forward(x_prompt, x_sample, state_shift, state_wkv, state_s5_re, state_s5_im, params,
             *, chunk=64, n_seq=8, s5_tt=128, row_tile=512, proj_tile=1024):
    depth = params["w_in"].shape[0]
    yp, ys = x_prompt, x_sample
    outs_p, outs_s = [], []
    for l in range(depth):
        lw = _prepare_layer_weights(l, params)
        (yp, *st_p), (ys, *st_s) = _layer(yp, ys, state_shift[l], state_wkv[l], state_s5_re[l], state_s5_im[l], lw,
                                          chunk=chunk, n_seq=n_seq, s5_tt=s5_tt, row_tile=row_tile, proj_tile=proj_tile)
        outs_p.append(st_p)
        outs_s.append(st_s)
    stack = lambda outs, i, dt: jnp.stack([o[i] for o in outs]).astype(dt)
    dt_p, dt_s = x_prompt.dtype, x_sample.dtype
    return (yp, ys,
            stack(outs_p, 0, dt_p), stack(outs_p, 1, dt_p), stack(outs_p, 2, dt_p), stack(outs_p, 3, dt_p),
            stack(outs_s, 0, dt_s), stack(outs_s, 1, dt_s), stack(outs_s, 2, dt_s), stack(outs_s, 3, dt_s))


def kernel(x_prompt, x_sample, state_shift, state_wkv, state_s5_re, state_s5_im, norm_pre_mix, norm_post_mix, norm_pre_ffn, norm_post_ffn, w_in, mu_shift, w0, w_decay_up, a0, w_aaa_up, w_gate_up, k_k, k_a, r_k, lnx_g, lnx_b, w_rwkv_out, s5_lam_re, s5_lam_im, s5_log_dt, s5_b_re, s5_b_im, s5_c_re, s5_c_im, s5_d, glu_w1, glu_b1, glu_w2, glu_b2, w_merge_out, w_ffn_gate, w_ffn_up, w_ffn_down):
    params = dict(zip(_PARAM_NAMES, (norm_pre_mix, norm_post_mix, norm_pre_ffn, norm_post_ffn, w_in, mu_shift,
                                     w0, w_decay_up, a0, w_aaa_up, w_gate_up, k_k, k_a, r_k, lnx_g, lnx_b,
                                     w_rwkv_out, s5_lam_re, s5_lam_im, s5_log_dt, s5_b_re, s5_b_im, s5_c_re,
                                     s5_c_im, s5_d, glu_w1, glu_b1, glu_w2, glu_b2, w_merge_out,
                                     w_ffn_gate, w_ffn_up, w_ffn_down)))
    return _forward(x_prompt, x_sample, state_shift, state_wkv, state_s5_re, state_s5_im, params)
```

```python
import functools
import math

import jax
import jax.numpy as jnp
from jax import lax
from jax.experimental import pallas as pl
from jax.experimental.pallas import tpu as pltpu

F32 = jnp.float32
BF16 = jnp.bfloat16

NORM_EPS = 1e-6
LNX_EPS = 64e-5
HEAD = 64
GROUP_HEADS = 4
GROUP_W = GROUP_HEADS * HEAD
LORA_PAD = 128
S5_GROUP = 16
S5_STATE = 64
SLAB_GROUPS = 8
SUBLANES = 8
LANES = 128
MXU_DIM = 256
S5_ROW_BLOCK = 256
SUB_ROWS = 128
TAIL_SUB_ROWS = 256
VMEM_LIMIT =56 * 1024 * 1024

NN = (((1,), (0,)), ((), ()))
NT = (((1,), (1,)), ((), ()))


def _dot(a, b, dims=NN):
    return lax.dot_general(a, b, dims, preferred_element_type=F32)


def _split2(x):
    hi = x.astype(BF16)
    lo = (x - hi.astype(F32)).astype(BF16)
    return hi, lo


def _mm1(a, b, dims=NN):
    return _dot(a.astype(BF16), b.astype(BF16), dims)


def _mm_exact_lhs(a_bf16, b):
    h, l = _split2(b)
    return _dot(a_bf16, h) + _dot(a_bf16, l)


def _rms(x, g):
    return x * lax.rsqrt(jnp.mean(x * x, axis=-1, keepdims=True) + NORM_EPS) * g


def _sigmoid(x):
    return 1.0 / (1.0 + jnp.exp(-x))


def _gelu_tanh(x):
    c = math.sqrt(2.0 / math.pi)
    return 0.5 * x * (1.0 + jnp.tanh(c * (x + 0.044715 * (x * x * x))))


def _const_spec(shape, single_buffer=False):
    idx = lambda *_: (0,) * len(shape)
    if single_buffer:
        return pl.BlockSpec(shape, idx, pipeline_mode=pl.Buffered(1))
    return pl.BlockSpec(shape, idx)


def _two_group_grid(rows_main, tm):
    n_main = rows_main // tm
    main_spec = lambda width: pl.BlockSpec((tm, width), lambda i: (jnp.minimum(i, n_main - 1), 0))
    return n_main, main_spec


def _on_group(body, main_refs, side_refs):
    i, n_main = pl.program_id(0), pl.num_programs(0) - 1
    pl.when(i < n_main)(functools.partial(body, *main_refs))
    pl.when(i == n_main)(functools.partial(body, *side_refs))


def _proj_kernel(x_ref, xs_ref, g_ref, w_ref, pr_ref, u_ref, gt_ref, prs_ref, us_ref, gts_ref, *, c_shift, c_u):
    def project(x_ref, pr_ref, u_ref, gt_ref):
        tm = x_ref.shape[0]
        sub = min(tm, SUB_ROWS)
        norm = lambda i: _rms(x_ref[i * sub:(i + 1) * sub, :], g_ref[...]).astype(BF16)
        hb_next = norm(0)
        for i in range(tm // sub):
            hb, rows = hb_next, slice(i * sub, (i + 1) * sub)
            if (i + 1) * sub < tm:
                hb_next = norm(i + 1)
            pr_ref[rows, :] = _dot(hb, w_ref[:, :c_shift])
            u_ref[rows, :] = _dot(hb, w_ref[:, c_shift:c_shift + c_u])
            gt_ref[rows, :] = _sigmoid(_dot(hb, w_ref[:, c_shift + c_u:])).astype(gt_ref.dtype)

    _on_group(project, (x_ref, pr_ref, u_ref, gt_ref), (xs_ref, prs_ref, us_ref, gts_ref))


def _proj(x_main, x_side, g, w_in_bf16, c_shift, c_u, tm):
    d = x_main.shape[1]
    cols = w_in_bf16.shape[1]
    widths = (c_shift, c_u, cols - c_shift - c_u)
    n_main, main_spec = _two_group_grid(x_main.shape[0], tm)
    side_spec = lambda width: _const_spec((x_side.shape[0], width))
    outs = pl.pallas_call(
        functools.partial(_proj_kernel, c_shift=c_shift, c_u=c_u),
        grid=(n_main + 1,),
        in_specs=[main_spec(d), side_spec(d), _const_spec((1, d)), _const_spec((d, cols), True)],
        out_specs=[main_spec(wd) for wd in widths] + [side_spec(wd) for wd in widths],
        out_shape=[jax.ShapeDtypeStruct((x.shape[0], wd), dt) for x in (x_main, x_side)
                   for wd, dt in zip(widths, (F32, F32, BF16))],
        compiler_params=pltpu.CompilerParams(dimension_semantics=("arbitrary",),
                                             vmem_limit_bytes=VMEM_LIMIT),
        name="proj",
    )(x_main, x_side, g, w_in_bf16)
    return outs[:3], outs[3:]


def _head_sum(x, e_ref, terms=1):
    rows, width = x.shape
    gw = e_ref.shape[0]
    n_lg = width // gw
    parts = _split2(x) if terms == 2 else (x.astype(BF16),)
    stacked = jnp.concatenate([part[:, j * gw:(j + 1) * gw] for part in parts for j in range(n_lg)], axis=0)
    sums = _dot(stacked, e_ref[...])
    blocks = [sums[i * rows:(i + 1) * rows] for i in range(terms * n_lg)]
    return jnp.concatenate([sum(blocks[j::n_lg][1:], blocks[j]) for j in range(n_lg)], axis=1)


def _rwkv_token_prep(xr, w, e_ref):
    width = w["w0"].shape[-1]
    r = xr[:, :width]
    k = xr[:, width:2 * width]
    v = xr[:, 2 * width:3 * width]
    lo = xr[:, 3 * width:3 * width + LORA_PAD]
    wl = w["w0"][...] + _mm1(jnp.tanh(lo), w["wd"][...])
    lw = -math.exp(-0.5) * _sigmoid(wl)
    a = _sigmoid(w["a0"][...] + _mm1(lo, w["wa"][...]))
    g = _mm1(_sigmoid(lo), w["wg"][...])
    kk = k * w["k_k"][...]
    kmod = k * (1.0 + (a - 1.0) * w["k_a"][...])
    rows = xr.shape[0]
    sums = _head_sum(jnp.concatenate([kk * kk, r * kmod * w["r_k"][...]], axis=0), e_ref)
    kk = kk * lax.rsqrt(jnp.maximum(sums[:rows], 1e-24))
    return r, kmod, v, kk, kk * a, lw, g, sums[rows:] * v


def _rwkv_post(o, bonus, g, w, e_ref):
    inv_n = 1.0 / HEAD
    mu = _head_sum(o, e_ref, terms=2) * inv_n
    oc = o - mu
    var = _head_sum(oc * oc, e_ref) * inv_n
    on = oc * lax.rsqrt(var + LNX_EPS) * w["lnx_g"][...] + w["lnx_b"][...]
    return (on + bonus) * g


_RWKV_W_NAMES = ("mu", "w0", "a0", "k_k", "k_a", "r_k", "lnx_g", "lnx_b", "wd", "wa", "wg")


def _rwkv_chunk_kernel(pr_ref, shift0_ref, s0_ref, *rest, chunk, width, n_seq):
    n_w = len(_RWKV_W_NAMES)
    w = dict(zip(_RWKV_W_NAMES, rest[:n_w]))
    e_ref, tri_ref = rest[n_w], rest[n_w + 1]
    oa_ref, s1_ref = rest[n_w + 2], rest[n_w + 3]
    carry_ref, z_ref, ops_ref, aux_ref, gall_ref = rest[n_w + 4:n_w + 9]
    step = pl.program_id(1)
    n_chunks = pl.num_programs(1) - 1
    n_groups = width // GROUP_W
    C = chunk
    GC = GROUP_HEADS * C

    def lane_block_masks(n_lanes, block):
        lane_block = lax.broadcasted_iota(jnp.int32, (1, n_lanes), 1) // block
        return [lane_block == h for h in range(GROUP_HEADS)]

    vec_masks = lane_block_masks(GROUP_W, HEAD)
    mat_masks = lane_block_masks(GC, C)

    def stack(x, masks):
        xb = x.astype(BF16)
        zero = jnp.zeros_like(xb)
        return jnp.concatenate([jnp.where(m, xb, zero) for m in masks], axis=0)

    @pl.when(step == 0)
    def _():
        carry_ref[...] = shift0_ref[...]
        ops_ref[1] = jnp.zeros(ops_ref.shape[1:], ops_ref.dtype)
        aux_ref[1] = jnp.zeros(aux_ref.shape[1:], aux_ref.dtype)
        gall_ref[1] = jnp.zeros(gall_ref.shape[1:], gall_ref.dtype)
        z_ref[...] = jnp.zeros(z_ref.shape, z_ref.dtype)

    @pl.when(step == 1)
    def _():
        for q in range(n_seq):
            for gi in range(n_groups):
                z_ref[q, gi] = jnp.zeros((GROUP_W, GROUP_W), F32)
                for h in range(GROUP_HEADS):
                    z_ref[q, gi, h * HEAD:(h + 1) * HEAD, h * HEAD:(h + 1) * HEAD] = s0_ref[q, gi * GROUP_HEADS + h]

    tok = lax.broadcasted_iota(jnp.int32, (C, GC), 0)
    col = lax.broadcasted_iota(jnp.int32, (C, GC), 1) % C
    strict = tok > col
    incl = tok >= col
    eye = jnp.where(tok == col, 1.0, 0.0).astype(F32)
    zi = lax.broadcasted_iota(jnp.int32, (GROUP_W, GROUP_W), 0) // HEAD
    zj = lax.broadcasted_iota(jnp.int32, (GROUP_W, GROUP_W), 1) // HEAD
    block_diag = zi == zj
    rows2 = lambda top, bottom: jnp.concatenate([top, bottom], axis=0).astype(BF16)

    def recurrence(r_slot):
        chains = [(q, gi) for q in range(n_seq) for gi in range(n_groups)]
        op = lambda i, q, gi: ops_ref[r_slot, i, q * C:(q + 1) * C, gi * GROUP_W:(gi + 1) * GROUP_W]
        kap_t, r_t, v = ([op(i, q, gi) for q, gi in chains] for i in (0, 1, 4))
        kap_s = [stack(x, vec_masks) for x in kap_t]
        v_s = [stack(x, vec_masks) for x in v]
        grams = []
        for i, (q, gi) in enumerate(chains):
            right = jnp.concatenate([stack(op(2, q, gi), vec_masks), stack(op(3, q, gi), vec_masks)], axis=0)
            grams.append(_dot(rows2(kap_t[i], r_t[i]), right, NT))
        yield
        a_k = [jnp.where(strict, gm[:C, :GC], 0.0) for gm in grams]
        a_b = [jnp.where(strict, gm[:C, GC:], 0.0) for gm in grams]
        a_rk = [jnp.where(incl, gm[C:, :GC], 0.0) for gm in grams]
        a_rb = [jnp.where(incl, gm[C:, GC:], 0.0) for gm in grams]

        ps = [-a for a in a_b]
        ts = [eye + p for p in ps]
        ps = [_dot(p.astype(BF16), stack(p, mat_masks)) for p in ps]
        yield
        covered = 2
        while covered < C:
            powers = [stack(p, mat_masks) for p in ps]
            if 2 * covered < C:
                both = [_dot(rows2(t, p), pw) for t, p, pw in zip(ts, ps, powers)]
                ts = [t + x[:C] for t, x in zip(ts, both)]
                ps = [x[C:] for x in both]
            else:
                ts = [t + _dot(t.astype(BF16), pw) for t, pw in zip(ts, powers)]
            covered *= 2
            yield

        av = [_dot(rows2(a, ar), vs) for a, ar, vs in zip(a_k, a_rk, v_s)]
        yield
        tw = [_dot(t.astype(BF16), jnp.concatenate([ks, stack(x[:C], vec_masks)], axis=1))
              for t, ks, x in zip(ts, kap_s, av)]
        yield
        zs = [z_ref[q, gi] for q, gi in chains]
        pz = [_dot(rows2(x[:, :GROUP_W], rt), z.astype(BF16), NT)
              for x, rt, z in zip(tw, r_t, zs)]
        us = [p[:C] + x[:, GROUP_W:] for p, x in zip(pz, tw)]
        yield
        arbu = [_dot(a.astype(BF16), stack(u, vec_masks)) for a, u in zip(a_rb, us)]
        o_blk = [p[C:] + x[C:] - y for p, x, y in zip(pz, av, arbu)]
        for i, (q, gi) in enumerate(chains):
            vu_t = jnp.concatenate([v[i].astype(F32), us[i]], axis=0).T.astype(BF16)
            kb = jnp.concatenate([op(5, q, gi), op(6, q, gi)], axis=0)
            g_all = gall_ref[r_slot, q][:, gi * GROUP_W:(gi + 1) * GROUP_W]
            z_ref[q, gi] = zs[i] * g_all + jnp.where(block_diag, _dot(vu_t, kb), 0.0)
        yield
        o = jnp.concatenate([jnp.concatenate(o_blk[q * n_groups:(q + 1) * n_groups], axis=1)
                             for q in range(n_seq)], axis=0)
        oa_ref[...] = _rwkv_post(o, aux_ref[r_slot, 0], aux_ref[r_slot, 1], w, e_ref
                                 ).astype(oa_ref.dtype).reshape(oa_ref.shape)

    def prepare(q, w_slot):
        rows = slice(q * C, (q + 1) * C)
        pr = pr_ref[q]
        row_id = lax.broadcasted_iota(jnp.int32, pr.shape, 0)
        pr_prev = jnp.where(row_id == 0, carry_ref[q], pltpu.roll(pr, 1, axis=0))
        carry_ref[q] = pr[C - 1:C, :]
        xr = pr + (pr_prev - pr) * w["mu"][...]
        r, kmod, vv, kk, bvec, lw, g, bonus = _rwkv_token_prep(xr, w, e_ref)
        aux_ref[w_slot, 0, rows] = bonus
        aux_ref[w_slot, 1, rows] = g
        yield
        cum = _mm_exact_lhs(tri_ref[...], lw)
        cum_last = cum[C - 1:C, :]
        g_neg = jnp.exp(-cum)
        g_end = jnp.exp(cum_last - cum)
        prepared = (kk * jnp.exp(cum - lw),
                    r * jnp.exp(cum),
                    kmod * g_neg, bvec * g_neg,
                    vv, kmod * g_end, -(bvec * g_end))
        for i, x in enumerate(prepared):
            ops_ref[w_slot, i, rows] = x.astype(ops_ref.dtype)
        gall_ref[w_slot, q] = jnp.exp(cum_last)

    w_slot = step % 2
    pending = [gen for gen in [prepare(q, w_slot) for q in range(n_seq)] for _ in range(2)]
    for _ in recurrence(1 - w_slot):
        if pending:
            next(pending.pop(0), None)
    for gen in pending:
        next(gen, None)

    @pl.when(step == n_chunks)
    def _():
        for q in range(n_seq):
            for gi in range(n_groups):
                for h in range(GROUP_HEADS):
                    s1_ref[q, gi * GROUP_HEADS + h] = z_ref[q, gi, h * HEAD:(h + 1) * HEAD, h * HEAD:(h + 1) * HEAD]


_N_OPS = 7


def _rwkv_weight_inputs(wts):
    return [wts[n] for n in _RWKV_W_NAMES]


def _rwkv_weight_specs(wts):
    return [_const_spec(wts[n].shape) for n in _RWKV_W_NAMES]


def _rwkv_chunked(pr3d, shift0, wkv0, wts, e_mat, chunk, n_seq):
    bsz, t, c_shift = pr3d.shape
    heads = wkv0.shape[1]
    width = heads * HEAD
    tri = jnp.tril(jnp.ones((chunk, chunk), F32)).astype(BF16)
    kern = functools.partial(_rwkv_chunk_kernel, chunk=chunk, width=width, n_seq=n_seq)
    n_chunks = t // chunk
    rows = n_seq * chunk
    return pl.pallas_call(
        kern,
        grid=(bsz // n_seq, n_chunks + 1),
        in_specs=[pl.BlockSpec((n_seq, chunk, c_shift), lambda b, s: (b, jnp.minimum(s, n_chunks - 1), 0)),
                  pl.BlockSpec((n_seq, 1, c_shift), lambda b, s: (b, 0, 0)),
                  pl.BlockSpec((n_seq, heads, HEAD, HEAD), lambda b, s: (b, 0, 0, 0))]
                 + _rwkv_weight_specs(wts)
                 + [_const_spec(e_mat.shape), _const_spec(tri.shape)],
        out_specs=[pl.BlockSpec((n_seq, chunk, width), lambda b, s: (b, jnp.maximum(s - 1, 0), 0)),
                   pl.BlockSpec((n_seq, heads, HEAD, HEAD), lambda b, s: (b, 0, 0, 0))],
        out_shape=[jax.ShapeDtypeStruct((bsz, t, width), BF16),
                   jax.ShapeDtypeStruct((bsz, heads, HEAD, HEAD), F32)],
        scratch_shapes=[pltpu.VMEM((n_seq, 1, c_shift), F32),
                        pltpu.VMEM((n_seq, width // GROUP_W, GROUP_W, GROUP_W), F32),
                        pltpu.VMEM((2, _N_OPS, rows, width), BF16),
                        pltpu.VMEM((2, 2, rows, width), F32),
                        pltpu.VMEM((2, n_seq, 1, width), F32)],
        compiler_params=pltpu.CompilerParams(dimension_semantics=("parallel", "arbitrary"),
                                             vmem_limit_bytes=VMEM_LIMIT),
        name="rwkv_chunk",
    )(pr3d, shift0[:, None, :], wkv0, *_rwkv_weight_inputs(wts), e_mat, tri)


_STEP_VECS = 6


def _rwkv_step_kernel(pr_ref, shift0_ref, s_ref, *rest):
    n_w = len(_RWKV_W_NAMES)
    w = dict(zip(_RWKV_W_NAMES, rest[:n_w]))
    e_ref, oa_ref, s1_ref, vec_ref, post_ref, o_ref = rest[n_w:n_w + 6]
    h = pl.program_id(0)

    @pl.when(h == 0)
    def _():
        pr = pr_ref[...]
        xr = pr + (shift0_ref[...] - pr) * w["mu"][...]
        r, kmod, v, kk, bvec, lw, g, bonus = _rwkv_token_prep(xr, w, e_ref)
        for i, x in enumerate((kk, bvec, jnp.exp(lw), kmod, r, v)):
            vec_ref[i] = x.T
        post_ref[0] = bonus
        post_ref[1] = g

    base = pl.multiple_of(h * HEAD, HEAD)
    kk_h, b_h, dec_h, k_h, r_h = [vec_ref[i, pl.ds(base, HEAD), :] for i in range(5)]

    def value_row(vi, carry):
        tile = s_ref[0, vi]
        s_kappa = jnp.sum(tile * kk_h, axis=0, keepdims=True)
        new = tile * dec_h - s_kappa * b_h + vec_ref[5, pl.ds(base + vi, 1), :] * k_h
        s1_ref[0, vi] = new
        o_ref[pl.ds(base + vi, 1), :] = jnp.sum(new * r_h, axis=0, keepdims=True)
        return carry

    lax.fori_loop(0, HEAD, value_row, 0, unroll=4)

    @pl.when(h == pl.num_programs(0) - 1)
    def _():
        oa_ref[...] = _rwkv_post(o_ref[...].T, post_ref[0], post_ref[1], w, e_ref).astype(oa_ref.dtype)


def _rwkv_step(pr2d, shift0, wkv0, wts, e_mat):
    bsz, c_shift = pr2d.shape
    heads = wkv0.shape[1]
    width = heads * HEAD
    full = lambda a: _const_spec(a.shape)
    st = pl.BlockSpec((1, HEAD, HEAD, bsz), lambda hh: (hh, 0, 0, 0))
    oa, s1_t = pl.pallas_call(
        _rwkv_step_kernel,
        grid=(heads,),
        in_specs=[full(pr2d), full(shift0), st] + _rwkv_weight_specs(wts) + [full(e_mat)],
        out_specs=[_const_spec((bsz, width)), st],
        out_shape=[jax.ShapeDtypeStruct((bsz, width), BF16),
                   jax.ShapeDtypeStruct((heads, HEAD, HEAD, bsz), F32)],
        scratch_shapes=[pltpu.VMEM((_STEP_VECS, width, bsz), F32),
                        pltpu.VMEM((2, bsz, width), F32),
                        pltpu.VMEM((width, bsz), F32)],
        compiler_params=pltpu.CompilerParams(dimension_semantics=("arbitrary",), vmem_limit_bytes=VMEM_LIMIT),
        name="rwkv_step",
    )(pr2d, shift0, jnp.transpose(wkv0, (1, 2, 3, 0)), *_rwkv_weight_inputs(wts), e_mat)
    return oa, jnp.transpose(s1_t, (3, 0, 1, 2))


def _s5_disc_kernel(lre_ref, lim_ref, ldt_ref, bre_ref, bim_ref, lbr_ref, lbi_ref, bbr_ref, bbi_ref):
    lam_re, lam_im = lre_ref[...], lim_ref[...]
    dt = jnp.exp(ldt_ref[...])
    mag = jnp.exp(lam_re * dt)
    ang = lam_im * dt
    lb_re, lb_im = mag * jnp.cos(ang), mag * jnp.sin(ang)
    nr, ni = lb_re - 1.0, lb_im
    den = lam_re * lam_re + lam_im * lam_im
    f_re = (nr * lam_re + ni * lam_im) / den
    f_im = (ni * lam_re - nr * lam_im) / den
    b_re, b_im = bre_ref[...], bim_ref[...]
    lbr_ref[...] = lb_re
    lbi_ref[...] = lb_im
    bbr_ref[...] = f_re * b_re - f_im * b_im
    bbi_ref[...] = f_re * b_im + f_im * b_re


def _s5_discretise(lam_re, lam_im, log_dt, b_re_t, b_im_t):
    g, p = lam_re.shape
    full = lambda a: _const_spec(a.shape)
    args = (lam_re[:, None, :], lam_im[:, None, :], log_dt[:, None, None], b_re_t, b_im_t)
    return pl.pallas_call(
        _s5_disc_kernel,
        grid=(1,),
        in_specs=[full(a) for a in args],
        out_specs=[_const_spec((g, 1, p))] * 2 + [full(b_re_t)] * 2,
        out_shape=[jax.ShapeDtypeStruct((g, 1, p), F32)] * 2 + [jax.ShapeDtypeStruct(b_re_t.shape, F32)] * 2,
        name="s5_discretise",
    )(*args)


def _s5_kernel(u_ref, re0_ref, im0_ref, lbr_ref, lbi_ref, wb_ref, wc_ref, d_ref,
               h_ref, re1_ref, im1_ref, u_tm, h_tm, bu0, bu1, xs0, xs1, *, tt, n_slabs):
    t_blk = pl.program_id(1)
    rows = SUBLANES * tt
    s_w = SLAB_GROUPS * S5_STATE
    u_w = SLAB_GROUPS * S5_GROUP
    bu, xs = (bu0, bu1), (xs0, xs1)
    rb = min(rows, S5_ROW_BLOCK)
    steps_per_rb = rb // SUBLANES

    @pl.when(t_blk == 0)
    def _():
        re1_ref[...] = re0_ref[...]
        im1_ref[...] = im0_ref[...]

    u_tm[...] = jnp.swapaxes(u_ref[...], 0, 1).reshape(rows, n_slabs * u_w)

    def project_in(s, j):
        r0 = j * rb
        bu[s % 2][r0:r0 + rb, :] = _mm1(u_tm[r0:r0 + rb, s * u_w:(s + 1) * u_w], wb_ref[s])

    def project_out(s, j):
        r0 = j * rb
        lanes = slice(s * u_w, (s + 1) * u_w)
        h_tm[r0:r0 + rb, lanes] = (_mm1(xs[s % 2][r0:r0 + rb, :], wc_ref[s])
                                   + d_ref[:, lanes] * u_tm[r0:r0 + rb, lanes])

    for j in range(rows // rb):
        project_in(0, j)
    for p in range(n_slabs + 1):
        if p < n_slabs:
            st = slice(p * s_w, (p + 1) * s_w)
            lbr = jnp.broadcast_to(lbr_ref[:, st], (SUBLANES, s_w))
            lbi = jnp.broadcast_to(lbi_ref[:, st], (SUBLANES, s_w))
            xr, xi = re1_ref[:, st], im1_ref[:, st]
        for j in range(rows // rb):
            if p < n_slabs:
                for t in range(j * steps_per_rb, (j + 1) * steps_per_rb):
                    at_t = slice(t * SUBLANES, (t + 1) * SUBLANES)
                    xr, xi = (lbr * xr - lbi * xi + bu[p % 2][at_t, :s_w],
                              lbr * xi + lbi * xr + bu[p % 2][at_t, s_w:])
                    xs[p % 2][at_t, :s_w] = xr
                    xs[p % 2][at_t, s_w:] = xi
            if p + 1 < n_slabs:
                project_in(p + 1, j)
            if p >= 1:
                project_out(p - 1, j)
        if p < n_slabs:
            re1_ref[:, st] = xr
            im1_ref[:, st] = xi
    h = h_tm[...].reshape(tt, SUBLANES, n_slabs * u_w)
    h_ref[...] = jnp.swapaxes(h, 0, 1).astype(h_ref.dtype)


def _s5(u_blocks, re0, im0, lb_re, lb_im, wb, wc, d_skip, tt):
    n_slabs = wb.shape[0]
    n_state = re0.shape[1]
    blk = (SUBLANES, tt, u_blocks.shape[-1])
    grid = (u_blocks.shape[0] // SUBLANES, u_blocks.shape[1] // tt)
    rows = SUBLANES * tt
    u_spec = pl.BlockSpec(blk, lambda i, j: (i, j, 0))
    st_spec = pl.BlockSpec((SUBLANES, n_state), lambda i, j: (i, 0))
    full = lambda a: _const_spec(a.shape)
    return pl.pallas_call(
        functools.partial(_s5_kernel, tt=tt, n_slabs=n_slabs),
        grid=grid,
        in_specs=[u_spec, st_spec, st_spec, full(lb_re), full(lb_im), full(wb), full(wc), full(d_skip)],
        out_specs=[u_spec, st_spec, st_spec],
        out_shape=[jax.ShapeDtypeStruct(u_blocks.shape, F32),
                   jax.ShapeDtypeStruct(re0.shape, F32), jax.ShapeDtypeStruct(im0.shape, F32)],
        scratch_shapes=[pltpu.VMEM((rows, u_blocks.shape[-1]), F32)] * 2
                       + [pltpu.VMEM((rows, 2 * SLAB_GROUPS * S5_STATE), F32)] * 4,
        compiler_params=pltpu.CompilerParams(dimension_semantics=("parallel", "arbitrary"),
                                             vmem_limit_bytes=VMEM_LIMIT),
        name="s5_scan",
    )(u_blocks, re0, im0, lb_re, lb_im, wb, wc, d_skip)


def _s5_step_kernel(u_ref, re0_ref, im0_ref, lbr_ref, lbi_ref, wb_ref, wc_ref, d_ref,
                    h_ref, re1_ref, im1_ref, *, n_slabs):
    s_w = SLAB_GROUPS * S5_STATE
    u_w = SLAB_GROUPS * S5_GROUP
    u = u_ref[...]
    ys = []
    for s in range(n_slabs):
        us = u[:, s * u_w:(s + 1) * u_w]
        bu = _mm1(us, wb_ref[s])
        st = slice(s * s_w, (s + 1) * s_w)
        lbr, lbi = lbr_ref[:, st], lbi_ref[:, st]
        xr, xi = re0_ref[st, :].T, im0_ref[st, :].T
        nr = lbr * xr - lbi * xi + bu[:, :s_w]
        ni = lbr * xi + lbi * xr + bu[:, s_w:]
        re1_ref[st, :] = nr.T
        im1_ref[st, :] = ni.T
        ys.append(_mm1(jnp.concatenate([nr, ni], axis=1), wc_ref[s]) + d_ref[:, s * u_w:(s + 1) * u_w] * us)
    h_ref[...] = jnp.concatenate(ys, axis=1)


def _s5_step(u2d, re0, im0, lb_re, lb_im, wb, wc, d_skip):
    bs, g, p = re0.shape
    to_minor = lambda x: jnp.transpose(x, (1, 2, 0)).reshape(g * p, bs)
    from_minor = lambda x: jnp.transpose(x.reshape(g, p, bs), (2, 0, 1))
    y, re1, im1 = _s5_step_call(u2d, to_minor(re0), to_minor(im0), lb_re, lb_im, wb, wc, d_skip)
    return y, from_minor(re1), from_minor(im1)


def _s5_step_call(u2d, re0, im0, lb_re, lb_im, wb, wc, d_skip):
    full = lambda a: _const_spec(a.shape)
    args = (u2d, re0, im0, lb_re, lb_im, wb, wc, d_skip)
    return pl.pallas_call(
        functools.partial(_s5_step_kernel, n_slabs=wb.shape[0]),
        grid=(1,),
        in_specs=[full(a) for a in args],
        out_specs=[full(u2d), full(re0), full(im0)],
        out_shape=[jax.ShapeDtypeStruct(u2d.shape, F32),
                   jax.ShapeDtypeStruct(re0.shape, F32), jax.ShapeDtypeStruct(im0.shape, F32)],
        compiler_params=pltpu.CompilerParams(vmem_limit_bytes=VMEM_LIMIT),
        name="s5_step",
    )(*args)


def _block_diag_slabs(m):
    g, a, b = m.shape
    eye = jnp.eye(SLAB_GROUPS, dtype=m.dtype)
    m4 = m.reshape(g // SLAB_GROUPS, SLAB_GROUPS, a, b)
    return jnp.einsum("sgab,gh->sgahb", m4, eye).reshape(g // SLAB_GROUPS, SLAB_GROUPS * a, SLAB_GROUPS * b)


_TAIL_W_NAMES = ("wro", "w1", "b1", "w2", "b2", "wmo", "npm", "nf", "npf", "wg", "wu", "wd")


def _tail_kernel(*refs):
    n_act = 4
    main_in, side_in = refs[:n_act], refs[n_act:2 * n_act]
    wts = dict(zip(_TAIL_W_NAMES, refs[2 * n_act:2 * n_act + len(_TAIL_W_NAMES)]))
    y_ref, ys_ref = refs[-2:]

    def tail(x_ref, oa_ref, y_s5_ref, gt_ref, y_ref):
        d = x_ref.shape[-1]
        tm = x_ref.shape[0]
        n_sub = max(1, tm // TAIL_SUB_ROWS)
        subs = [slice(i * (tm // n_sub), (i + 1) * (tm // n_sub)) for i in range(n_sub)]
        hg = [_gelu_tanh(y_s5_ref[s, :]).astype(BF16) for s in subs]
        a_out = [_dot(oa_ref[s, :], wts["wro"][...]) for s in subs]
        b_lin = [_dot(h, wts["w1"][...]) + wts["b1"][...] for h in hg]
        b_gate = [_dot(h, wts["w2"][...]) + wts["b2"][...] for h in hg]
        merged = [(gt_ref[s, :d].astype(F32) * a + gt_ref[s, d:].astype(F32) * (bl * _sigmoid(bg))).astype(BF16)
                  for s, a, bl, bg in zip(subs, a_out, b_lin, b_gate)]
        mix = [_dot(m, wts["wmo"][...]) for m in merged]
        x1 = [x_ref[s, :] + _rms(m, wts["npm"][...]) for s, m in zip(subs, mix)]
        hb = [_rms(x, wts["nf"][...]).astype(BF16) for x in x1]
        gate = [_dot(h, wts["wg"][...]) for h in hb]
        up = [_dot(h, wts["wu"][...]) for h in hb]
        act = [(g * _sigmoid(g) * u).astype(BF16) for g, u in zip(gate, up)]
        f = [_dot(a, wts["wd"][...]) for a in act]
        for s, x, ff in zip(subs, x1, f):
            y_ref[s, :] = x + _rms(ff, wts["npf"][...])

    _on_group(tail, (*main_in, y_ref), (*side_in, ys_ref))


def _tail(acts_main, acts_side, tw, tm):
    d = acts_main[0].shape[1]
    n_main, main_spec = _two_group_grid(acts_main[0].shape[0], tm)
    wargs = [tw[n] for n in _TAIL_W_NAMES]
    return pl.pallas_call(
        _tail_kernel,
        grid=(n_main + 1,),
        in_specs=[main_spec(a.shape[1]) for a in acts_main] + [_const_spec(a.shape) for a in acts_side]
                 + [_const_spec(a.shape, True) for a in wargs],
        out_specs=[main_spec(d), _const_spec((acts_side[0].shape[0], d))],
        out_shape=[jax.ShapeDtypeStruct((a[0].shape[0], d), F32) for a in (acts_main, acts_side)],
        compiler_params=pltpu.CompilerParams(dimension_semantics=("arbitrary",),
                                             vmem_limit_bytes=VMEM_LIMIT),
        name="tail",
    )(*acts_main, *acts_side, *wargs)


def _layer(x_p, x_s, shift0, wkv0, re0, im0, lw, *, chunk, n_seq, s5_tt, row_tile, proj_tile):
    bsz, t, d = x_p.shape
    bs = x_s.shape[0]
    assert x_s.shape[1] == 1 and (bsz * t) % row_tile == 0 and (bsz * t) % proj_tile == 0
    heads = wkv0.shape[1]
    c_shift = shift0.shape[-1]
    c_u = lw["d_skip"].shape[-1]
    n_state = re0.shape[1] * re0.shape[2]
    s5_w = (lw["lb_re"], lw["lb_im"], lw["wb"], lw["wc"], lw["d_skip"])
    xp2d, xs2d = x_p.reshape(bsz * t, d), x_s.reshape(bs, d)
    (pr_p, u_p, gates_p), (pr_s, u_s, gates_s) = _proj(xp2d, xs2d, lw["norm_pre_mix"], lw["w_in"],
                                                       c_shift, c_u, proj_tile)

    pr_p3 = pr_p.reshape(bsz, t, c_shift)
    oa_p, wkv_p = _rwkv_chunked(pr_p3, jnp.zeros((bsz, c_shift), F32), jnp.zeros((bsz, heads, HEAD, HEAD), F32),
                                lw["rwkv"], lw["e_mat"], chunk, n_seq)
    zeros_state = jnp.zeros((bsz, n_state), F32)
    hg_p, re_p, im_p = _s5(u_p.reshape(bsz, t, c_u), zeros_state, zeros_state, *s5_w, tt=s5_tt)

    oa_s, wkv_s = _rwkv_step(pr_s, shift0, wkv0, lw["rwkv"], lw["e_mat"])
    hg_s, re_s, im_s = _s5_step(u_s, re0, im0, *s5_w)

    y_p, y_s = _tail((xp2d, oa_p.reshape(bsz * t, -1), hg_p.reshape(bsz * t, c_u), gates_p),
                     (xs2d, oa_s, hg_s, gates_s), lw["tail"], row_tile)
    st_shape = lambda n: (n,) + re0.shape[1:]
    return ((y_p.reshape(x_p.shape), pr_p3[:, -1], wkv_p, re_p.reshape(st_shape(bsz)), im_p.reshape(st_shape(bsz))),
            (y_s.reshape(x_s.shape), pr_s, wkv_s, re_s.reshape(st_shape(bs)), im_s.reshape(st_shape(bs))))


def _prepare_layer_weights(l, p):
    row = lambda a: a[l][None, :].astype(F32)
    width = p["w0"].shape[-1]
    n_dec, n_aaa, n_gate = p["w_decay_up"].shape[1], p["w_aaa_up"].shape[1], p["w_gate_up"].shape[1]
    assert n_dec + n_aaa + n_gate == LORA_PAD

    def lora_pad(wup, start):
        return jnp.zeros((LORA_PAD, width), F32).at[start:start + wup.shape[0]].set(wup).astype(BF16)

    rwkv = {
        "mu": row(p["mu_shift"]), "w0": row(p["w0"]), "a0": row(p["a0"]), "k_k": row(p["k_k"]),
        "k_a": row(p["k_a"]), "r_k": row(p["r_k"]), "lnx_g": row(p["lnx_g"]), "lnx_b": row(p["lnx_b"]),
        "wd": lora_pad(p["w_decay_up"][l], 0),
        "wa": lora_pad(p["w_aaa_up"][l], n_dec),
        "wg": lora_pad(p["w_gate_up"][l], n_dec + n_aaa),
    }
    head_id = jnp.arange(MXU_DIM) // HEAD
    e_mat = (head_id[:, None] == head_id[None, :]).astype(BF16)

    lb_re, lb_im, bb_re_t, bb_im_t = _s5_discretise(
        p["s5_lam_re"][l], p["s5_lam_im"][l], p["s5_log_dt"][l],
        jnp.swapaxes(p["s5_b_re"][l], 1, 2), jnp.swapaxes(p["s5_b_im"][l], 1, 2))
    n_state = lb_re.shape[0] * lb_re.shape[2]
    to_out = lambda cc: _block_diag_slabs(jnp.swapaxes(cc, 1, 2))
    wb = jnp.concatenate([_block_diag_slabs(bb_re_t), _block_diag_slabs(bb_im_t)], axis=-1).astype(BF16)
    wc = jnp.concatenate([to_out(p["s5_c_re"][l]), -to_out(p["s5_c_im"][l])], axis=1).astype(BF16)

    bf = lambda a: a[l].astype(BF16)
    tail = {
        "wro": bf(p["w_rwkv_out"]), "w1": bf(p["glu_w1"]), "b1": row(p["glu_b1"]), "w2": bf(p["glu_w2"]),
        "b2": row(p["glu_b2"]), "wmo": bf(p["w_merge_out"]), "npm": row(p["norm_post_mix"]),
        "nf": row(p["norm_pre_ffn"]), "npf": row(p["norm_post_ffn"]),
        "wg": bf(p["w_ffn_gate"]), "wu": bf(p["w_ffn_up"]), "wd": bf(p["w_ffn_down"]),
    }
    return {
        "norm_pre_mix": row(p["norm_pre_mix"]), "w_in": bf(p["w_in"]), "rwkv": rwkv, "e_mat": e_mat,
        "lb_re": lb_re.reshape(1, n_state), "lb_im": lb_im.reshape(1, n_state), "wb": wb, "wc": wc,
        "d_skip": row(p["s5_d"]), "tail": tail,
    }


_PARAM_NAMES = ("norm_pre_mix", "norm_post_mix", "norm_pre_ffn", "norm_post_ffn", "w_in", "mu_shift",
                "w0", "w_decay_up", "a0", "w_aaa_up", "w_gate_up", "k_k", "k_a", "r_k", "lnx_g", "lnx_b",
                "w_rwkv_out", "s5_lam_re", "s5_lam_im", "s5_log_dt", "s5_b_re", "s5_b_im", "s5_c_re",
                "s5_c_im", "s5_d", "glu_w1", "glu_b1", "glu_w2", "glu_b2", "w_merge_out",
                "w_ffn_gate", "w_ffn_up", "w_ffn_down")


def _forward(x_prompt, x_sample, state_shift, state_wkv, state_s5_re, state_s5_im, params,
             *, chunk=64, n_seq=8, s5_tt=128, row_tile=512, proj_tile=1024):
    depth = params["w_in"].shape[0]
    yp, ys = x_prompt, x_sample
    outs_p, outs_s = [], []
    for l in range(depth):
        lw = _prepare_layer_weights(l, params)
        (yp, *st_p), (ys, *st_s) = _layer(yp, ys, state_shift[l], state_wkv[l], state_s5_re[l], state_s5_im[l], lw,
                                          chunk=chunk, n_seq=n_seq, s5_tt=s5_tt, row_tile=row_tile, proj_tile=proj_tile)
        outs_p.append(st_p)
        outs_s.append(st_s)
    stack = lambda outs, i, dt: jnp.stack([o[i] for o in outs]).astype(dt)
    dt_p, dt_s = x_prompt.dtype, x_sample.dtype
    return (yp, ys,
            stack(outs_p, 0, dt_p), stack(outs_p, 1, dt_p), stack(outs_p, 2, dt_p), stack(outs_p, 3, dt_p),
            stack(outs_s, 0, dt_s), stack(outs_s, 1, dt_s), stack(outs_s, 2, dt_s), stack(outs_s, 3, dt_s))


def kernel(x_prompt, x_sample, state_shift, state_wkv, state_s5_re, state_s5_im, norm_pre_mix, norm_post_mix, norm_pre_ffn, norm_post_ffn, w_in, mu_shift, w0, w_decay_up, a0, w_aaa_up, w_gate_up, k_k, k_a, r_k, lnx_g, lnx_b, w_rwkv_out, s5_lam_re, s5_lam_im, s5_log_dt, s5_b_re, s5_b_im, s5_c_re, s5_c_im, s5_d, glu_w1, glu_b1, glu_w2, glu_b2, w_merge_out, w_ffn_gate, w_ffn_up, w_ffn_down):
    params = dict(zip(_PARAM_NAMES, (norm_pre_mix, norm_post_mix, norm_pre_ffn, norm_post_ffn, w_in, mu_shift,
                                     w0, w_decay_up, a0, w_aaa_up, w_gate_up, k_k, k_a, r_k, lnx_g, lnx_b,
                                     w_rwkv_out, s5_lam_re, s5_lam_im, s5_log_dt, s5_b_re, s5_b_im, s5_c_re,
                                     s5_c_im, s5_d, glu_w1, glu_b1, glu_w2, glu_b2, w_merge_out,
                                     w_ffn_gate, w_ffn_up, w_ffn_down)))
    return _forward(x_prompt, x_sample, state_shift, state_wkv, state_s5_re, state_s5_im, params)
```

```python
import functools
import math

import jax
import jax.numpy as jnp
from jax import lax
from jax.experimental import pallas as pl
from jax.experimental.pallas import tpu as pltpu

F32 = jnp.float32
BF16 = jnp.bfloat16

NORM_EPS = 1e-6
LNX_EPS = 64e-5
HEAD = 64
GROUP_HEADS = 4
GROUP_W = GROUP_HEADS * HEAD
LORA_PAD = 128
S5_GROUP = 16
S5_STATE = 64
SLAB_GROUPS = 8
SUBLANES = 8
LANES = 128
MXU_DIM = 256
S5_ROW_BLOCK = 256
SUB_ROWS = 128
TAIL_SUB_ROWS = 256
VMEM_LIMIT =56 * 1024 * 1024

NN = (((1,), (0,)), ((), ()))
NT = (((1,), (1,)), ((), ()))


def _dot(a, b, dims=NN):
    return lax.dot_general(a, b, dims, preferred_element_type=F32)


def _split2(x):
    hi = x.astype(BF16)
    lo = (x - hi.astype(F32)).astype(BF16)
    return hi, lo


def _mm1(a, b, dims=NN):
    return _dot(a.astype(BF16), b.astype(BF16), dims)


def _mm_exact_lhs(a_bf16, b):
    h, l = _split2(b)
    return _dot(a_bf16, h) + _dot(a_bf16, l)


def _rms(x, g):
    return x * lax.rsqrt(jnp.mean(x * x, axis=-1, keepdims=True) + NORM_EPS) * g


def _sigmoid(x):
    return 1.0 / (1.0 + jnp.exp(-x))


def _gelu_tanh(x):
    c = math.sqrt(2.0 / math.pi)
    return 0.5 * x * (1.0 + jnp.tanh(c * (x + 0.044715 * (x * x * x))))


def _const_spec(shape, single_buffer=False):
    idx = lambda *_: (0,) * len(shape)
    if single_buffer:
        return pl.BlockSpec(shape, idx, pipeline_mode=pl.Buffered(1))
    return pl.BlockSpec(shape, idx)


def _two_group_grid(rows_main, tm):
    n_main = rows_main // tm
    main_spec = lambda width: pl.BlockSpec((tm, width), lambda i: (jnp.minimum(i, n_main - 1), 0))
    return n_main, main_spec


def _on_group(body, main_refs, side_refs):
    i, n_main = pl.program_id(0), pl.num_programs(0) - 1
    pl.when(i < n_main)(functools.partial(body, *main_refs))
    pl.when(i == n_main)(functools.partial(body, *side_refs))


def _proj_kernel(x_ref, xs_ref, g_ref, w_ref, pr_ref, u_ref, gt_ref, prs_ref, us_ref, gts_ref, *, c_shift, c_u):
    def project(x_ref, pr_ref, u_ref, gt_ref):
        tm = x_ref.shape[0]
        sub = min(tm, SUB_ROWS)
        norm = lambda i: _rms(x_ref[i * sub:(i + 1) * sub, :], g_ref[...]).astype(BF16)
        hb_next = norm(0)
        for i in range(tm // sub):
            hb, rows = hb_next, slice(i * sub, (i + 1) * sub)
            if (i + 1) * sub < tm:
                hb_next = norm(i + 1)
            pr_ref[rows, :] = _dot(hb, w_ref[:, :c_shift])
            u_ref[rows, :] = _dot(hb, w_ref[:, c_shift:c_shift + c_u])
            gt_ref[rows, :] = _sigmoid(_dot(hb, w_ref[:, c_shift + c_u:])).astype(gt_ref.dtype)

    _on_group(project, (x_ref, pr_ref, u_ref, gt_ref), (xs_ref, prs_ref, us_ref, gts_ref))


def _proj(x_main, x_side, g, w_in_bf16, c_shift, c_u, tm):
    d = x_main.shape[1]
    cols = w_in_bf16.shape[1]
    widths = (c_shift, c_u, cols - c_shift - c_u)
    n_main, main_spec = _two_group_grid(x_main.shape[0], tm)
    side_spec = lambda width: _const_spec((x_side.shape[0], width))
    outs = pl.pallas_call(
        functools.partial(_proj_kernel, c_shift=c_shift, c_u=c_u),
        grid=(n_main + 1,),
        in_specs=[main_spec(d), side_spec(d), _const_spec((1, d)), _const_spec((d, cols), True)],
        out_specs=[main_spec(wd) for wd in widths] + [side_spec(wd) for wd in widths],
        out_shape=[jax.ShapeDtypeStruct((x.shape[0], wd), dt) for x in (x_main, x_side)
                   for wd, dt in zip(widths, (F32, F32, BF16))],
        compiler_params=pltpu.CompilerParams(dimension_semantics=("arbitrary",),
                                             vmem_limit_bytes=VMEM_LIMIT),
        name="proj",
    )(x_main, x_side, g, w_in_bf16)
    return outs[:3], outs[3:]


def _head_sum(x, e_ref, terms=1):
    rows, width = x.shape
    gw = e_ref.shape[0]
    n_lg = width // gw
    parts = _split2(x) if terms == 2 else (x.astype(BF16),)
    stacked = jnp.concatenate([part[:, j * gw:(j + 1) * gw] for part in parts for j in range(n_lg)], axis=0)
    sums = _dot(stacked, e_ref[...])
    blocks = [sums[i * rows:(i + 1) * rows] for i in range(terms * n_lg)]
    return jnp.concatenate([sum(blocks[j::n_lg][1:], blocks[j]) for j in range(n_lg)], axis=1)


def _rwkv_token_prep(xr, w, e_ref):
    width = w["w0"].shape[-1]
    r = xr[:, :width]
    k = xr[:, width:2 * width]
    v = xr[:, 2 * width:3 * width]
    lo = xr[:, 3 * width:3 * width + LORA_PAD]
    wl = w["w0"][...] + _mm1(jnp.tanh(lo), w["wd"][...])
    lw = -math.exp(-0.5) * _sigmoid(wl)
    a = _sigmoid(w["a0"][...] + _mm1(lo, w["wa"][...]))
    g = _mm1(_sigmoid(lo), w["wg"][...])
    kk = k * w["k_k"][...]
    kk = kk * lax.rsqrt(jnp.maximum(_head_sum(kk * kk, e_ref), 1e-24))
    kmod = k * (1.0 + (a - 1.0) * w["k_a"][...])
    return r, kmod, v, kk, kk * a, lw, g


def _rwkv_bonus(r, kmod, v, w, e_ref):
    return _head_sum(r * kmod * w["r_k"][...], e_ref) * v


def _rwkv_post(o, bonus, g, w, e_ref):
    inv_n = 1.0 / HEAD
    mu = _head_sum(o, e_ref, terms=2) * inv_n
    oc = o - mu
    var = _head_sum(oc * oc, e_ref) * inv_n
    on = oc * lax.rsqrt(var + LNX_EPS) * w["lnx_g"][...] + w["lnx_b"][...]
    return (on + bonus) * g


_RWKV_W_NAMES = ("mu", "w0", "a0", "k_k", "k_a", "r_k", "lnx_g", "lnx_b", "wd", "wa", "wg")


def _rwkv_chunk_kernel(pr_ref, shift0_ref, s0_ref, *rest, chunk, width, n_seq):
    n_w = len(_RWKV_W_NAMES)
    w = dict(zip(_RWKV_W_NAMES, rest[:n_w]))
    e_ref, tri_ref = rest[n_w], rest[n_w + 1]
    oa_ref, s1_ref = rest[n_w + 2], rest[n_w + 3]
    carry_ref, z_ref, ops_ref, aux_ref, gall_ref = rest[n_w + 4:n_w + 9]
    step = pl.program_id(1)
    n_chunks = pl.num_programs(1) - 1
    n_groups = width // GROUP_W
    C = chunk
    GC = GROUP_HEADS * C

    def lane_block_masks(n_lanes, block):
        lane_block = lax.broadcasted_iota(jnp.int32, (1, n_lanes), 1) // block
        return [lane_block == h for h in range(GROUP_HEADS)]

    vec_masks = lane_block_masks(GROUP_W, HEAD)
    mat_masks = lane_block_masks(GC, C)

    def stack(x, masks):
        xb = x.astype(BF16)
        zero = jnp.zeros_like(xb)
        return jnp.concatenate([jnp.where(m, xb, zero) for m in masks], axis=0)

    @pl.when(step == 0)
    def _():
        carry_ref[...] = shift0_ref[...]
        ops_ref[1] = jnp.zeros(ops_ref.shape[1:], ops_ref.dtype)
        aux_ref[1] = jnp.zeros(aux_ref.shape[1:], aux_ref.dtype)
        gall_ref[1] = jnp.zeros(gall_ref.shape[1:], gall_ref.dtype)
        z_ref[...] = jnp.zeros(z_ref.shape, z_ref.dtype)

    @pl.when(step == 1)
    def _():
        for q in range(n_seq):
            for gi in range(n_groups):
                z_ref[q, gi] = jnp.zeros((GROUP_W, GROUP_W), F32)
                for h in range(GROUP_HEADS):
                    z_ref[q, gi, h * HEAD:(h + 1) * HEAD, h * HEAD:(h + 1) * HEAD] = s0_ref[q, gi * GROUP_HEADS + h]

    tok = lax.broadcasted_iota(jnp.int32, (C, GC), 0)
    col = lax.broadcasted_iota(jnp.int32, (C, GC), 1) % C
    strict = tok > col
    incl = tok >= col
    eye = jnp.where(tok == col, 1.0, 0.0).astype(F32)
    zi = lax.broadcasted_iota(jnp.int32, (GROUP_W, GROUP_W), 0) // HEAD
    zj = lax.broadcasted_iota(jnp.int32, (GROUP_W, GROUP_W), 1) // HEAD
    block_diag = zi == zj
    rows2 = lambda top, bottom: jnp.concatenate([top, bottom], axis=0).astype(BF16)

    def recurrence(r_slot):
        chains = [(q, gi) for q in range(n_seq) for gi in range(n_groups)]
        op = lambda i, q, gi: ops_ref[r_slot, i, q * C:(q + 1) * C, gi * GROUP_W:(gi + 1) * GROUP_W]
        kap_t, r_t, v = ([op(i, q, gi) for q, gi in chains] for i in (0, 1, 4))
        kap_s = [stack(x, vec_masks) for x in kap_t]
        v_s = [stack(x, vec_masks) for x in v]
        grams = []
        for i, (q, gi) in enumerate(chains):
            right = jnp.concatenate([stack(op(2, q, gi), vec_masks), stack(op(3, q, gi), vec_masks)], axis=0)
            grams.append(_dot(rows2(kap_t[i], r_t[i]), right, NT))
        yield
        a_k = [jnp.where(strict, gm[:C, :GC], 0.0) for gm in grams]
        a_b = [jnp.where(strict, gm[:C, GC:], 0.0) for gm in grams]
        a_rk = [jnp.where(incl, gm[C:, :GC], 0.0) for gm in grams]
        a_rb = [jnp.where(incl, gm[C:, GC:], 0.0) for gm in grams]

        ps = [-a for a in a_b]
        ts = [eye + p for p in ps]
        ps = [_dot(p.astype(BF16), stack(p, mat_masks)) for p in ps]
        yield
        covered = 2
        while covered < C:
            powers = [stack(p, mat_masks) for p in ps]
            if 2 * covered < C:
                both = [_dot(rows2(t, p), pw) for t, p, pw in zip(ts, ps, powers)]
                ts = [t + x[:C] for t, x in zip(ts, both)]
                ps = [x[C:] for x in both]
            else:
                ts = [t + _dot(t.astype(BF16), pw) for t, pw in zip(ts, powers)]
            covered *= 2
            yield

        av = [_dot(rows2(a, ar), vs) for a, ar, vs in zip(a_k, a_rk, v_s)]
        yield
        tw = [_dot(t.astype(BF16), jnp.concatenate([ks, stack(x[:C], vec_masks)], axis=1))
              for t, ks, x in zip(ts, kap_s, av)]
        yield
        zs = [z_ref[q, gi] for q, gi in chains]
        pz = [_dot(rows2(x[:, :GROUP_W], rt), z.astype(BF16), NT)
              for x, rt, z in zip(tw, r_t, zs)]
        us = [p[:C] + x[:, GROUP_W:] for p, x in zip(pz, tw)]
        yield
        arbu = [_dot(a.astype(BF16), stack(u, vec_masks)) for a, u in zip(a_rb, us)]
        o_blk = [p[C:] + x[C:] - y for p, x, y in zip(pz, av, arbu)]
        for i, (q, gi) in enumerate(chains):
            vu_t = jnp.concatenate([v[i].astype(F32), us[i]], axis=0).T.astype(BF16)
            kb = jnp.concatenate([op(5, q, gi), op(6, q, gi)], axis=0)
            g_all = gall_ref[r_slot, q][:, gi * GROUP_W:(gi + 1) * GROUP_W]
            z_ref[q, gi] = zs[i] * g_all + jnp.where(block_diag, _dot(vu_t, kb), 0.0)
        yield
        o = jnp.concatenate([jnp.concatenate(o_blk[q * n_groups:(q + 1) * n_groups], axis=1)
                             for q in range(n_seq)], axis=0)
        oa_ref[...] = _rwkv_post(o, aux_ref[r_slot, 0], aux_ref[r_slot, 1], w, e_ref
                                 ).astype(oa_ref.dtype).reshape(oa_ref.shape)

    def prepare(q, w_slot):
        rows = slice(q * C, (q + 1) * C)
        pr = pr_ref[q]
        row_id = lax.broadcasted_iota(jnp.int32, pr.shape, 0)
        pr_prev = jnp.where(row_id == 0, carry_ref[q], pltpu.roll(pr, 1, axis=0))
        carry_ref[q] = pr[C - 1:C, :]
        xr = pr + (pr_prev - pr) * w["mu"][...]
        r, kmod, vv, kk, bvec, lw, g = _rwkv_token_prep(xr, w, e_ref)
        aux_ref[w_slot, 0, rows] = _rwkv_bonus(r, kmod, vv, w, e_ref)
        aux_ref[w_slot, 1, rows] = g
        yield
        cum = _mm_exact_lhs(tri_ref[...], lw)
        cum_last = cum[C - 1:C, :]
        g_neg = jnp.exp(-cum)
        g_end = jnp.exp(cum_last - cum)
        prepared = (kk * jnp.exp(cum - lw),
                    r * jnp.exp(cum),
                    kmod * g_neg, bvec * g_neg,
                    vv, kmod * g_end, -(bvec * g_end))
        for i, x in enumerate(prepared):
            ops_ref[w_slot, i, rows] = x.astype(ops_ref.dtype)
        gall_ref[w_slot, q] = jnp.exp(cum_last)

    w_slot = step % 2
    pending = [gen for gen in [prepare(q, w_slot) for q in range(n_seq)] for _ in range(2)]
    for _ in recurrence(1 - w_slot):
        if pending:
            next(pending.pop(0), None)
    for gen in pending:
        next(gen, None)

    @pl.when(step == n_chunks)
    def _():
        for q in range(n_seq):
            for gi in range(n_groups):
                for h in range(GROUP_HEADS):
                    s1_ref[q, gi * GROUP_HEADS + h] = z_ref[q, gi, h * HEAD:(h + 1) * HEAD, h * HEAD:(h + 1) * HEAD]


_N_OPS = 7


def _rwkv_weight_inputs(wts):
    return [wts[n] for n in _RWKV_W_NAMES]


def _rwkv_weight_specs(wts):
    return [_const_spec(wts[n].shape) for n in _RWKV_W_NAMES]


def _rwkv_chunked(pr3d, shift0, wkv0, wts, e_mat, chunk, n_seq):
    bsz, t, c_shift = pr3d.shape
    heads = wkv0.shape[1]
    width = heads * HEAD
    tri = jnp.tril(jnp.ones((chunk, chunk), F32)).astype(BF16)
    kern = functools.partial(_rwkv_chunk_kernel, chunk=chunk, width=width, n_seq=n_seq)
    n_chunks = t // chunk
    rows = n_seq * chunk
    return pl.pallas_call(
        kern,
        grid=(bsz // n_seq, n_chunks + 1),
        in_specs=[pl.BlockSpec((n_seq, chunk, c_shift), lambda b, s: (b, jnp.minimum(s, n_chunks - 1), 0)),
                  pl.BlockSpec((n_seq, 1, c_shift), lambda b, s: (b, 0, 0)),
                  pl.BlockSpec((n_seq, heads, HEAD, HEAD), lambda b, s: (b, 0, 0, 0))]
                 + _rwkv_weight_specs(wts)
                 + [_const_spec(e_mat.shape), _const_spec(tri.shape)],
        out_specs=[pl.BlockSpec((n_seq, chunk, width), lambda b, s: (b, jnp.maximum(s - 1, 0), 0)),
                   pl.BlockSpec((n_seq, heads, HEAD, HEAD), lambda b, s: (b, 0, 0, 0))],
        out_shape=[jax.ShapeDtypeStruct((bsz, t, width), BF16),
                   jax.ShapeDtypeStruct((bsz, heads, HEAD, HEAD), F32)],
        scratch_shapes=[pltpu.VMEM((n_seq, 1, c_shift), F32),
                        pltpu.VMEM((n_seq, width // GROUP_W, GROUP_W, GROUP_W), F32),
                        pltpu.VMEM((2, _N_OPS, rows, width), BF16),
                        pltpu.VMEM((2, 2, rows, width), F32),
                        pltpu.VMEM((2, n_seq, 1, width), F32)],
        compiler_params=pltpu.CompilerParams(dimension_semantics=("parallel", "arbitrary"),
                                             vmem_limit_bytes=VMEM_LIMIT),
        name="rwkv_chunk",
    )(pr3d, shift0[:, None, :], wkv0, *_rwkv_weight_inputs(wts), e_mat, tri)


_STEP_VECS = 6


def _rwkv_step_kernel(pr_ref, shift0_ref, s_ref, *rest):
    n_w = len(_RWKV_W_NAMES)
    w = dict(zip(_RWKV_W_NAMES, rest[:n_w]))
    e_ref, oa_ref, s1_ref, vec_ref, post_ref, o_ref = rest[n_w:n_w + 6]
    h = pl.program_id(0)

    @pl.when(h == 0)
    def _():
        pr = pr_ref[...]
        xr = pr + (shift0_ref[...] - pr) * w["mu"][...]
        r, kmod, v, kk, bvec, lw, g = _rwkv_token_prep(xr, w, e_ref)
        for i, x in enumerate((kk, bvec, jnp.exp(lw), kmod, r, v)):
            vec_ref[i] = x.T
        post_ref[0] = _rwkv_bonus(r, kmod, v, w, e_ref)
        post_ref[1] = g

    base = pl.multiple_of(h * HEAD, HEAD)
    kk_h, b_h, dec_h, k_h, r_h = [vec_ref[i, pl.ds(base, HEAD), :] for i in range(5)]

    def value_row(vi, carry):
        tile = s_ref[0, vi]
        s_kappa = jnp.sum(tile * kk_h, axis=0, keepdims=True)
        new = tile * dec_h - s_kappa * b_h + vec_ref[5, pl.ds(base + vi, 1), :] * k_h
        s1_ref[0, vi] = new
        o_ref[pl.ds(base + vi, 1), :] = jnp.sum(new * r_h, axis=0, keepdims=True)
        return carry

    lax.fori_loop(0, HEAD, value_row, 0, unroll=4)

    @pl.when(h == pl.num_programs(0) - 1)
    def _():
        oa_ref[...] = _rwkv_post(o_ref[...].T, post_ref[0], post_ref[1], w, e_ref).astype(oa_ref.dtype)


def _rwkv_step(pr2d, shift0, wkv0, wts, e_mat):
    bsz, c_shift = pr2d.shape
    heads = wkv0.shape[1]
    width = heads * HEAD
    full = lambda a: _const_spec(a.shape)
    st = pl.BlockSpec((1, HEAD, HEAD, bsz), lambda hh: (hh, 0, 0, 0))
    oa, s1_t = pl.pallas_call(
        _rwkv_step_kernel,
        grid=(heads,),
        in_specs=[full(pr2d), full(shift0), st] + _rwkv_weight_specs(wts) + [full(e_mat)],
        out_specs=[_const_spec((bsz, width)), st],
        out_shape=[jax.ShapeDtypeStruct((bsz, width), BF16),
                   jax.ShapeDtypeStruct((heads, HEAD, HEAD, bsz), F32)],
        scratch_shapes=[pltpu.VMEM((_STEP_VECS, width, bsz), F32),
                        pltpu.VMEM((2, bsz, width), F32),
                        pltpu.VMEM((width, bsz), F32)],
        compiler_params=pltpu.CompilerParams(dimension_semantics=("arbitrary",), vmem_limit_bytes=VMEM_LIMIT),
        name="rwkv_step",
    )(pr2d, shift0, jnp.transpose(wkv0, (1, 2, 3, 0)), *_rwkv_weight_inputs(wts), e_mat)
    return oa, jnp.transpose(s1_t, (3, 0, 1, 2))


def _s5_disc_kernel(lre_ref, lim_ref, ldt_ref, bre_ref, bim_ref, lbr_ref, lbi_ref, bbr_ref, bbi_ref):
    lam_re, lam_im = lre_ref[...], lim_ref[...]
    dt = jnp.exp(ldt_ref[...])
    mag = jnp.exp(lam_re * dt)
    ang = lam_im * dt
    lb_re, lb_im = mag * jnp.cos(ang), mag * jnp.sin(ang)
    nr, ni = lb_re - 1.0, lb_im
    den = lam_re * lam_re + lam_im * lam_im
    f_re = (nr * lam_re + ni * lam_im) / den
    f_im = (ni * lam_re - nr * lam_im) / den
    b_re, b_im = bre_ref[...], bim_ref[...]
    lbr_ref[...] = lb_re
    lbi_ref[...] = lb_im
    bbr_ref[...] = f_re * b_re - f_im * b_im
    bbi_ref[...] = f_re * b_im + f_im * b_re


def _s5_discretise(lam_re, lam_im, log_dt, b_re_t, b_im_t):
    g, p = lam_re.shape
    full = lambda a: _const_spec(a.shape)
    args = (lam_re[:, None, :], lam_im[:, None, :], log_dt[:, None, None], b_re_t, b_im_t)
    return pl.pallas_call(
        _s5_disc_kernel,
        grid=(1,),
        in_specs=[full(a) for a in args],
        out_specs=[_const_spec((g, 1, p))] * 2 + [full(b_re_t)] * 2,
        out_shape=[jax.ShapeDtypeStruct((g, 1, p), F32)] * 2 + [jax.ShapeDtypeStruct(b_re_t.shape, F32)] * 2,
        name="s5_discretise",
    )(*args)


def _s5_kernel(u_ref, re0_ref, im0_ref, lbr_ref, lbi_ref, wb_ref, wc_ref, d_ref,
               h_ref, re1_ref, im1_ref, u_tm, h_tm, bu0, bu1, xs0, xs1, *, tt, n_slabs):
    t_blk = pl.program_id(1)
    rows = SUBLANES * tt
    s_w = SLAB_GROUPS * S5_STATE
    u_w = SLAB_GROUPS * S5_GROUP
    bu, xs = (bu0, bu1), (xs0, xs1)
    rb = min(rows, S5_ROW_BLOCK)
    steps_per_rb = rb // SUBLANES

    @pl.when(t_blk == 0)
    def _():
        re1_ref[...] = re0_ref[...]
        im1_ref[...] = im0_ref[...]

    u_tm[...] = jnp.swapaxes(u_ref[...], 0, 1).reshape(rows, n_slabs * u_w)

    def project_in(s, j):
        r0 = j * rb
        bu[s % 2][r0:r0 + rb, :] = _mm1(u_tm[r0:r0 + rb, s * u_w:(s + 1) * u_w], wb_ref[s])

    def project_out(s, j):
        r0 = j * rb
        lanes = slice(s * u_w, (s + 1) * u_w)
        h_tm[r0:r0 + rb, lanes] = (_mm1(xs[s % 2][r0:r0 + rb, :], wc_ref[s])
                                   + d_ref[:, lanes] * u_tm[r0:r0 + rb, lanes])

    for j in range(rows // rb):
        project_in(0, j)
    for p in range(n_slabs + 1):
        if p < n_slabs:
            st = slice(p * s_w, (p + 1) * s_w)
            lbr = jnp.broadcast_to(lbr_ref[:, st], (SUBLANES, s_w))
            lbi = jnp.broadcast_to(lbi_ref[:, st], (SUBLANES, s_w))
            xr, xi = re1_ref[:, st], im1_ref[:, st]
        for j in range(rows // rb):
            if p < n_slabs:
                for t in range(j * steps_per_rb, (j + 1) * steps_per_rb):
                    at_t = slice(t * SUBLANES, (t + 1) * SUBLANES)
                    xr, xi = (lbr * xr - lbi * xi + bu[p % 2][at_t, :s_w],
                              lbr * xi + lbi * xr + bu[p % 2][at_t, s_w:])
                    xs[p % 2][at_t, :s_w] = xr
                    xs[p % 2][at_t, s_w:] = xi
            if p + 1 < n_slabs:
                project_in(p + 1, j)
            if p >= 1:
                project_out(p - 1, j)
        if p < n_slabs:
            re1_ref[:, st] = xr
            im1_ref[:, st] = xi
    h = h_tm[...].reshape(tt, SUBLANES, n_slabs * u_w)
    h_ref[...] = jnp.swapaxes(h, 0, 1).astype(h_ref.dtype)


def _s5(u_blocks, re0, im0, lb_re, lb_im, wb, wc, d_skip, tt):
    n_slabs = wb.shape[0]
    n_state = re0.shape[1]
    blk = (SUBLANES, tt, u_blocks.shape[-1])
    grid = (u_blocks.shape[0] // SUBLANES, u_blocks.shape[1] // tt)
    rows = SUBLANES * tt
    u_spec = pl.BlockSpec(blk, lambda i, j: (i, j, 0))
    st_spec = pl.BlockSpec((SUBLANES, n_state), lambda i, j: (i, 0))
    full = lambda a: _const_spec(a.shape)
    return pl.pallas_call(
        functools.partial(_s5_kernel, tt=tt, n_slabs=n_slabs),
        grid=grid,
        in_specs=[u_spec, st_spec, st_spec, full(lb_re), full(lb_im), full(wb), full(wc), full(d_skip)],
        out_specs=[u_spec, st_spec, st_spec],
        out_shape=[jax.ShapeDtypeStruct(u_blocks.shape, F32),
                   jax.ShapeDtypeStruct(re0.shape, F32), jax.ShapeDtypeStruct(im0.shape, F32)],
        scratch_shapes=[pltpu.VMEM((rows, u_blocks.shape[-1]), F32)] * 2
                       + [pltpu.VMEM((rows, 2 * SLAB_GROUPS * S5_STATE), F32)] * 4,
        compiler_params=pltpu.CompilerParams(dimension_semantics=("parallel", "arbitrary"),
                                             vmem_limit_bytes=VMEM_LIMIT),
        name="s5_scan",
    )(u_blocks, re0, im0, lb_re, lb_im, wb, wc, d_skip)


def _s5_step_kernel(u_ref, re0_ref, im0_ref, lbr_ref, lbi_ref, wb_ref, wc_ref, d_ref,
                    h_ref, re1_ref, im1_ref, *, n_slabs):
    s_w = SLAB_GROUPS * S5_STATE
    u_w = SLAB_GROUPS * S5_GROUP
    u = u_ref[...]
    ys = []
    for s in range(n_slabs):
        us = u[:, s * u_w:(s + 1) * u_w]
        bu = _mm1(us, wb_ref[s])
        st = slice(s * s_w, (s + 1) * s_w)
        lbr, lbi = lbr_ref[:, st], lbi_ref[:, st]
        xr, xi = re0_ref[st, :].T, im0_ref[st, :].T
        nr = lbr * xr - lbi * xi + bu[:, :s_w]
        ni = lbr * xi + lbi * xr + bu[:, s_w:]
        re1_ref[st, :] = nr.T
        im1_ref[st, :] = ni.T
        ys.append(_mm1(jnp.concatenate([nr, ni], axis=1), wc_ref[s]) + d_ref[:, s * u_w:(s + 1) * u_w] * us)
    h_ref[...] = jnp.concatenate(ys, axis=1)


def _s5_step(u2d, re0, im0, lb_re, lb_im, wb, wc, d_skip):
    bs, g, p = re0.shape
    to_minor = lambda x: jnp.transpose(x, (1, 2, 0)).reshape(g * p, bs)
    from_minor = lambda x: jnp.transpose(x.reshape(g, p, bs), (2, 0, 1))
    y, re1, im1 = _s5_step_call(u2d, to_minor(re0), to_minor(im0), lb_re, lb_im, wb, wc, d_skip)
    return y, from_minor(re1), from_minor(im1)


def _s5_step_call(u2d, re0, im0, lb_re, lb_im, wb, wc, d_skip):
    full = lambda a: _const_spec(a.shape)
    args = (u2d, re0, im0, lb_re, lb_im, wb, wc, d_skip)
    return pl.pallas_call(
        functools.partial(_s5_step_kernel, n_slabs=wb.shape[0]),
        grid=(1,),
        in_specs=[full(a) for a in args],
        out_specs=[full(u2d), full(re0), full(im0)],
        out_shape=[jax.ShapeDtypeStruct(u2d.shape, F32),
                   jax.ShapeDtypeStruct(re0.shape, F32), jax.ShapeDtypeStruct(im0.shape, F32)],
        compiler_params=pltpu.CompilerParams(vmem_limit_bytes=VMEM_LIMIT),
        name="s5_step",
    )(*args)


def _block_diag_slabs(m):
    g, a, b = m.shape
    eye = jnp.eye(SLAB_GROUPS, dtype=m.dtype)
    m4 = m.reshape(g // SLAB_GROUPS, SLAB_GROUPS, a, b)
    return jnp.einsum("sgab,gh->sgahb", m4, eye).reshape(g // SLAB_GROUPS, SLAB_GROUPS * a, SLAB_GROUPS * b)


_TAIL_W_NAMES = ("wro", "w1", "b1", "w2", "b2", "wmo", "npm", "nf", "npf", "wg", "wu", "wd")


def _tail_kernel(*refs):
    n_act = 4
    main_in, side_in = refs[:n_act], refs[n_act:2 * n_act]
    wts = dict(zip(_TAIL_W_NAMES, refs[2 * n_act:2 * n_act + len(_TAIL_W_NAMES)]))
    y_ref, ys_ref = refs[-2:]

    def tail(x_ref, oa_ref, y_s5_ref, gt_ref, y_ref):
        d = x_ref.shape[-1]
        tm = x_ref.shape[0]
        n_sub = max(1, tm // TAIL_SUB_ROWS)
        subs = [slice(i * (tm // n_sub), (i + 1) * (tm // n_sub)) for i in range(n_sub)]
        hg = [_gelu_tanh(y_s5_ref[s, :]).astype(BF16) for s in subs]
        a_out = [_dot(oa_ref[s, :], wts["wro"][...]) for s in subs]
        b_lin = [_dot(h, wts["w1"][...]) + wts["b1"][...] for h in hg]
        b_gate = [_dot(h, wts["w2"][...]) + wts["b2"][...] for h in hg]
        merged = [(gt_ref[s, :d].astype(F32) * a + gt_ref[s, d:].astype(F32) * (bl * _sigmoid(bg))).astype(BF16)
                  for s, a, bl, bg in zip(subs, a_out, b_lin, b_gate)]
        mix = [_dot(m, wts["wmo"][...]) for m in merged]
        x1 = [x_ref[s, :] + _rms(m, wts["npm"][...]) for s, m in zip(subs, mix)]
        hb = [_rms(x, wts["nf"][...]).astype(BF16) for x in x1]
        gate = [_dot(h, wts["wg"][...]) for h in hb]
        up = [_dot(h, wts["wu"][...]) for h in hb]
        act = [(g * _sigmoid(g) * u).astype(BF16) for g, u in zip(gate, up)]
        f = [_dot(a, wts["wd"][...]) for a in act]
        for s, x, ff in zip(subs, x1, f):
            y_ref[s, :] = x + _rms(ff, wts["npf"][...])

    _on_group(tail, (*main_in, y_ref), (*side_in, ys_ref))


def _tail(acts_main, acts_side, tw, tm):
    d = acts_main[0].shape[1]
    n_main, main_spec = _two_group_grid(acts_main[0].shape[0], tm)
    wargs = [tw[n] for n in _TAIL_W_NAMES]
    return pl.pallas_call(
        _tail_kernel,
        grid=(n_main + 1,),
        in_specs=[main_spec(a.shape[1]) for a in acts_main] + [_const_spec(a.shape) for a in acts_side]
                 + [_const_spec(a.shape, True) for a in wargs],
        out_specs=[main_spec(d), _const_spec((acts_side[0].shape[0], d))],
        out_shape=[jax.ShapeDtypeStruct((a[0].shape[0], d), F32) for a in (acts_main, acts_side)],
        compiler_params=pltpu.CompilerParams(dimension_semantics=("arbitrary",),
                                             vmem_limit_bytes=VMEM_LIMIT),
        name="tail",
    )(*acts_main, *acts_side, *wargs)


def _layer(x_p, x_s, shift0, wkv0, re0, im0, lw, *, chunk, n_seq, s5_tt, row_tile, proj_tile):
    bsz, t, d = x_p.shape
    bs = x_s.shape[0]
    assert x_s.shape[1] == 1 and (bsz * t) % row_tile == 0 and (bsz * t) % proj_tile == 0
    heads = wkv0.shape[1]
    c_shift = shift0.shape[-1]
    c_u = lw["d_skip"].shape[-1]
    n_state = re0.shape[1] * re0.shape[2]
    s5_w = (lw["lb_re"], lw["lb_im"], lw["wb"], lw["wc"], lw["d_skip"])
    xp2d, xs2d = x_p.reshape(bsz * t, d), x_s.reshape(bs, d)
    (pr_p, u_p, gates_p), (pr_s, u_s, gates_s) = _proj(xp2d, xs2d, lw["norm_pre_mix"], lw["w_in"],
                                                       c_shift, c_u, proj_tile)

    pr_p3 = pr_p.reshape(bsz, t, c_shift)
    oa_p, wkv_p = _rwkv_chunked(pr_p3, jnp.zeros((bsz, c_shift), F32), jnp.zeros((bsz, heads, HEAD, HEAD), F32),
                                lw["rwkv"], lw["e_mat"], chunk, n_seq)
    zeros_state = jnp.zeros((bsz, n_state), F32)
    hg_p, re_p, im_p = _s5(u_p.reshape(bsz, t, c_u), zeros_state, zeros_state, *s5_w, tt=s5_tt)

    oa_s, wkv_s = _rwkv_step(pr_s, shift0, wkv0, lw["rwkv"], lw["e_mat"])
    hg_s, re_s, im_s = _s5_step(u_s, re0, im0, *s5_w)

    y_p, y_s = _tail((xp2d, oa_p.reshape(bsz * t, -1), hg_p.reshape(bsz * t, c_u), gates_p),
                     (xs2d, oa_s, hg_s, gates_s), lw["tail"], row_tile)
    st_shape = lambda n: (n,) + re0.shape[1:]
    return ((y_p.reshape(x_p.shape), pr_p3[:, -1], wkv_p, re_p.reshape(st_shape(bsz)), im_p.reshape(st_shape(bsz))),
            (y_s.reshape(x_s.shape), pr_s, wkv_s, re_s.reshape(st_shape(bs)), im_s.reshape(st_shape(bs))))


def _prepare_layer_weights(l, p):
    row = lambda a: a[l][None, :].astype(F32)
    width = p["w0"].shape[-1]
    n_dec, n_aaa, n_gate = p["w_decay_up"].shape[1], p["w_aaa_up"].shape[1], p["w_gate_up"].shape[1]
    assert n_dec + n_aaa + n_gate == LORA_PAD

    def lora_pad(wup, start):
        return jnp.zeros((LORA_PAD, width), F32).at[start:start + wup.shape[0]].set(wup).astype(BF16)

    rwkv = {
        "mu": row(p["mu_shift"]), "w0": row(p["w0"]), "a0": row(p["a0"]), "k_k": row(p["k_k"]),
        "k_a": row(p["k_a"]), "r_k": row(p["r_k"]), "lnx_g": row(p["lnx_g"]), "lnx_b": row(p["lnx_b"]),
        "wd": lora_pad(p["w_decay_up"][l], 0),
        "wa": lora_pad(p["w_aaa_up"][l], n_dec),
        "wg": lora_pad(p["w_gate_up"][l], n_dec + n_aaa),
    }
    head_id = jnp.arange(MXU_DIM) // HEAD
    e_mat = (head_id[:, None] == head_id[None, :]).astype(BF16)

    lb_re, lb_im, bb_re_t, bb_im_t = _s5_discretise(
        p["s5_lam_re"][l], p["s5_lam_im"][l], p["s5_log_dt"][l],
        jnp.swapaxes(p["s5_b_re"][l], 1, 2), jnp.swapaxes(p["s5_b_im"][l], 1, 2))
    n_state = lb_re.shape[0] * lb_re.shape[2]
    to_out = lambda cc: _block_diag_slabs(jnp.swapaxes(cc, 1, 2))
    wb = jnp.concatenate([_block_diag_slabs(bb_re_t), _block_diag_slabs(bb_im_t)], axis=-1).astype(BF16)
    wc = jnp.concatenate([to_out(p["s5_c_re"][l]), -to_out(p["s5_c_im"][l])], axis=1).astype(BF16)

    bf = lambda a: a[l].astype(BF16)
    tail = {
        "wro": bf(p["w_rwkv_out"]), "w1": bf(p["glu_w1"]), "b1": row(p["glu_b1"]), "w2": bf(p["glu_w2"]),
        "b2": row(p["glu_b2"]), "wmo": bf(p["w_merge_out"]), "npm": row(p["norm_post_mix"]),
        "nf": row(p["norm_pre_ffn"]), "npf": row(p["norm_post_ffn"]),
        "wg": bf(p["w_ffn_gate"]), "wu": bf(p["w_ffn_up"]), "wd": bf(p["w_ffn_down"]),
    }
    return {
        "norm_pre_mix": row(p["norm_pre_mix"]), "w_in": bf(p["w_in"]), "rwkv": rwkv, "e_mat": e_mat,
        "lb_re": lb_re.reshape(1, n_state), "lb_im": lb_im.reshape(1, n_state), "wb": wb, "wc": wc,
        "d_skip": row(p["s5_d"]), "tail": tail,
    }


_PARAM_NAMES = ("norm_pre_mix", "norm_post_mix", "norm_pre_ffn", "norm_post_ffn", "w_in", "mu_shift",
                "w0", "w_decay_up", "a0", "w_aaa_up", "w_gate_up", "k_k", "k_a", "r_k", "lnx_g", "lnx_b",
                "w_rwkv_out", "s5_lam_re", "s5_lam_im", "s5_log_dt", "s5_b_re", "s5_b_im", "s5_c_re",
                "s5_c_im", "s5_d", "glu_w1", "glu_b1", "glu_w2", "glu_b2", "w_merge_out",
                "w_ffn_gate", "w_ffn_up", "w_ffn_down")


def _forward(x_prompt, x_sample, state_shift, state_wkv, state_s5_re, state_s5_im, params,
             *, chunk=64, n_seq=8, s5_tt=128, row_tile=512, proj_tile=1024):
    depth = params["w_in"].shape[0]
    yp, ys = x_prompt, x_sample
    outs_p, outs_s = [], []
    for l in range(depth):
        lw = _prepare_layer_weights(l, params)
        (yp, *st_p), (ys, *st_s) = _layer(yp, ys, state_shift[l], state_wkv[l], state_s5_re[l], state_s5_im[l], lw,
                                          chunk=chunk, n_seq=n_seq, s5_tt=s5_tt, row_tile=row_tile, proj_tile=proj_tile)
        outs_p.append(st_p)
        outs_s.append(st_s)
    stack = lambda outs, i, dt: jnp.stack([o[i] for o in outs]).astype(dt)
    dt_p, dt_s = x_prompt.dtype, x_sample.dtype
    return (yp, ys,
            stack(outs_p, 0, dt_p), stack(outs_p, 1, dt_p), stack(outs_p, 2, dt_p), stack(outs_p, 3, dt_p),
            stack(outs_s, 0, dt_s), stack(outs_s, 1, dt_s), stack(outs_s, 2, dt_s), stack(outs_s, 3, dt_s))


def kernel(x_prompt, x_sample, state_shift, state_wkv, state_s5_re, state_s5_im, norm_pre_mix, norm_post_mix, norm_pre_ffn, norm_post_ffn, w_in, mu_shift, w0, w_decay_up, a0, w_aaa_up, w_gate_up, k_k, k_a, r_k, lnx_g, lnx_b, w_rwkv_out, s5_lam_re, s5_lam_im, s5_log_dt, s5_b_re, s5_b_im, s5_c_re, s5_c_im, s5_d, glu_w1, glu_b1, glu_w2, glu_b2, w_merge_out, w_ffn_gate, w_ffn_up, w_ffn_down):
    params = dict(zip(_PARAM_NAMES, (norm_pre_mix, norm_post_mix, norm_pre_ffn, norm_post_ffn, w_in, mu_shift,
                                     w0, w_decay_up, a0, w_aaa_up, w_gate_up, k_k, k_a, r_k, lnx_g, lnx_b,
                                     w_rwkv_out, s5_lam_re, s5_lam_im, s5_log_dt, s5_b_re, s5_b_im, s5_c_re,
                                     s5_c_im, s5_d, glu_w1, glu_b1, glu_w2, glu_b2, w_merge_out,
                                     w_ffn_gate, w_ffn_up, w_ffn_down)))
    return _forward(x_prompt, x_sample, state_shift, state_wkv, state_s5_re, state_s5_im, params)
```

```python
import functools
import math

import jax
import jax.numpy as jnp
from jax import lax
from jax.experimental import pallas as pl
from jax.experimental.pallas import tpu as pltpu

F32 = jnp.float32
BF16 = jnp.bfloat16

NORM_EPS = 1e-6
LNX_EPS = 64e-5
HEAD = 64
GROUP_HEADS = 4
GROUP_W = GROUP_HEADS * HEAD
LORA_PAD = 128
S5_GROUP = 16
S5_STATE = 64
SLAB_GROUPS = 8
SUBLANES = 8
LANES = 128
MXU_DIM = 256
S5_ROW_BLOCK = 256
SUB_ROWS = 128
TAIL_SUB_ROWS = 256
VMEM_LIMIT =56 * 1024 * 1024

NN = (((1,), (0,)), ((), ()))
NT = (((1,), (1,)), ((), ()))


def _dot(a, b, dims=NN):
    return lax.dot_general(a, b, dims, preferred_element_type=F32)


def _split2(x):
    hi = x.astype(BF16)
    lo = (x - hi.astype(F32)).astype(BF16)
    return hi, lo


def _split3(x):
    hi = x.astype(BF16)
    r1 = x - hi.astype(F32)
    mid = r1.astype(BF16)
    lo = (r1 - mid.astype(F32)).astype(BF16)
    return hi, mid, lo


def _mm1(a, b, dims=NN):
    return _dot(a.astype(BF16), b.astype(BF16), dims)


def _mm_exact_lhs(a_bf16, b):
    h, m, l = _split3(b)
    return _dot(a_bf16, h) + (_dot(a_bf16, m) + _dot(a_bf16, l))


def _rms(x, g):
    return x * lax.rsqrt(jnp.mean(x * x, axis=-1, keepdims=True) + NORM_EPS) * g


def _sigmoid(x):
    return 1.0 / (1.0 + jnp.exp(-x))


def _gelu_tanh(x):
    c = math.sqrt(2.0 / math.pi)
    return 0.5 * x * (1.0 + jnp.tanh(c * (x + 0.044715 * (x * x * x))))


def _const_spec(shape, single_buffer=False):
    idx = lambda *_: (0,) * len(shape)
    if single_buffer:
        return pl.BlockSpec(shape, idx, pipeline_mode=pl.Buffered(1))
    return pl.BlockSpec(shape, idx)


def _two_group_grid(rows_main, tm):
    n_main = rows_main // tm
    main_spec = lambda width: pl.BlockSpec((tm, width), lambda i: (jnp.minimum(i, n_main - 1), 0))
    return n_main, main_spec


def _on_group(body, main_refs, side_refs):
    i, n_main = pl.program_id(0), pl.num_programs(0) - 1
    pl.when(i < n_main)(functools.partial(body, *main_refs))
    pl.when(i == n_main)(functools.partial(body, *side_refs))


def _proj_kernel(x_ref, xs_ref, g_ref, w_ref, pr_ref, u_ref, gt_ref, prs_ref, us_ref, gts_ref, *, c_shift, c_u):
    def project(x_ref, pr_ref, u_ref, gt_ref):
        tm = x_ref.shape[0]
        sub = min(tm, SUB_ROWS)
        norm = lambda i: _rms(x_ref[i * sub:(i + 1) * sub, :], g_ref[...]).astype(BF16)
        hb_next = norm(0)
        for i in range(tm // sub):
            hb, rows = hb_next, slice(i * sub, (i + 1) * sub)
            if (i + 1) * sub < tm:
                hb_next = norm(i + 1)
            pr_ref[rows, :] = _dot(hb, w_ref[:, :c_shift])
            u_ref[rows, :] = _dot(hb, w_ref[:, c_shift:c_shift + c_u])
            gt_ref[rows, :] = _sigmoid(_dot(hb, w_ref[:, c_shift + c_u:])).astype(gt_ref.dtype)

    _on_group(project, (x_ref, pr_ref, u_ref, gt_ref), (xs_ref, prs_ref, us_ref, gts_ref))


def _proj(x_main, x_side, g, w_in_bf16, c_shift, c_u, tm):
    d = x_main.shape[1]
    cols = w_in_bf16.shape[1]
    widths = (c_shift, c_u, cols - c_shift - c_u)
    n_main, main_spec = _two_group_grid(x_main.shape[0], tm)
    side_spec = lambda width: _const_spec((x_side.shape[0], width))
    outs = pl.pallas_call(
        functools.partial(_proj_kernel, c_shift=c_shift, c_u=c_u),
        grid=(n_main + 1,),
        in_specs=[main_spec(d), side_spec(d), _const_spec((1, d)), _const_spec((d, cols), True)],
        out_specs=[main_spec(wd) for wd in widths] + [side_spec(wd) for wd in widths],
        out_shape=[jax.ShapeDtypeStruct((x.shape[0], wd), dt) for x in (x_main, x_side)
                   for wd, dt in zip(widths, (F32, F32, BF16))],
        compiler_params=pltpu.CompilerParams(dimension_semantics=("arbitrary",),
                                             vmem_limit_bytes=VMEM_LIMIT),
        name="proj",
    )(x_main, x_side, g, w_in_bf16)
    return outs[:3], outs[3:]


def _head_sum(x, e_ref, terms=1):
    rows, width = x.shape
    gw = e_ref.shape[0]
    n_lg = width // gw
    parts = _split2(x) if terms == 2 else (x.astype(BF16),)
    stacked = jnp.concatenate([part[:, j * gw:(j + 1) * gw] for part in parts for j in range(n_lg)], axis=0)
    sums = _dot(stacked, e_ref[...])
    blocks = [sums[i * rows:(i + 1) * rows] for i in range(terms * n_lg)]
    return jnp.concatenate([sum(blocks[j::n_lg][1:], blocks[j]) for j in range(n_lg)], axis=1)


def _rwkv_token_prep(xr, w, e_ref):
    width = w["w0"].shape[-1]
    r = xr[:, :width]
    k = xr[:, width:2 * width]
    v = xr[:, 2 * width:3 * width]
    lo = xr[:, 3 * width:3 * width + LORA_PAD]
    wl = w["w0"][...] + _mm1(jnp.tanh(lo), w["wd"][...])
    lw = -math.exp(-0.5) * _sigmoid(wl)
    a = _sigmoid(w["a0"][...] + _mm1(lo, w["wa"][...]))
    g = _mm1(_sigmoid(lo), w["wg"][...])
    kk = k * w["k_k"][...]
    kk = kk * lax.rsqrt(jnp.maximum(_head_sum(kk * kk, e_ref), 1e-24))
    kmod = k * (1.0 + (a - 1.0) * w["k_a"][...])
    return r, kmod, v, kk, kk * a, lw, g


def _rwkv_bonus(r, kmod, v, w, e_ref):
    return _head_sum(r * kmod * w["r_k"][...], e_ref) * v


def _rwkv_post(o, bonus, g, w, e_ref):
    inv_n = 1.0 / HEAD
    mu = _head_sum(o, e_ref, terms=2) * inv_n
    oc = o - mu
    var = _head_sum(oc * oc, e_ref) * inv_n
    on = oc * lax.rsqrt(var + LNX_EPS) * w["lnx_g"][...] + w["lnx_b"][...]
    return (on + bonus) * g


_RWKV_W_NAMES = ("mu", "w0", "a0", "k_k", "k_a", "r_k", "lnx_g", "lnx_b", "wd", "wa", "wg")


def _rwkv_chunk_kernel(pr_ref, shift0_ref, s0_ref, *rest, chunk, width, n_seq):
    n_w = len(_RWKV_W_NAMES)
    w = dict(zip(_RWKV_W_NAMES, rest[:n_w]))
    e_ref, tri_ref = rest[n_w], rest[n_w + 1]
    oa_ref, s1_ref = rest[n_w + 2], rest[n_w + 3]
    carry_ref, z_ref, ops_ref, aux_ref, gall_ref = rest[n_w + 4:n_w + 9]
    step = pl.program_id(1)
    n_chunks = pl.num_programs(1) - 1
    n_groups = width // GROUP_W
    C = chunk
    GC = GROUP_HEADS * C

    def lane_block_masks(n_lanes, block):
        lane_block = lax.broadcasted_iota(jnp.int32, (1, n_lanes), 1) // block
        return [lane_block == h for h in range(GROUP_HEADS)]

    vec_masks = lane_block_masks(GROUP_W, HEAD)
    mat_masks = lane_block_masks(GC, C)

    def stack(x, masks):
        xb = x.astype(BF16)
        zero = jnp.zeros_like(xb)
        return jnp.concatenate([jnp.where(m, xb, zero) for m in masks], axis=0)

    @pl.when(step == 0)
    def _():
        carry_ref[...] = shift0_ref[...]
        ops_ref[1] = jnp.zeros(ops_ref.shape[1:], ops_ref.dtype)
        aux_ref[1] = jnp.zeros(aux_ref.shape[1:], aux_ref.dtype)
        gall_ref[1] = jnp.zeros(gall_ref.shape[1:], gall_ref.dtype)
        z_ref[...] = jnp.zeros(z_ref.shape, z_ref.dtype)

    @pl.when(step == 1)
    def _():
        for q in range(n_seq):
            for gi in range(n_groups):
                z_ref[q, gi] = jnp.zeros((GROUP_W, GROUP_W), F32)
                for h in range(GROUP_HEADS):
                    z_ref[q, gi, h * HEAD:(h + 1) * HEAD, h * HEAD:(h + 1) * HEAD] = s0_ref[q, gi * GROUP_HEADS + h]

    tok = lax.broadcasted_iota(jnp.int32, (C, GC), 0)
    col = lax.broadcasted_iota(jnp.int32, (C, GC), 1) % C
    strict = tok > col
    incl = tok >= col
    eye = jnp.where(tok == col, 1.0, 0.0).astype(F32)
    zi = lax.broadcasted_iota(jnp.int32, (GROUP_W, GROUP_W), 0) // HEAD
    zj = lax.broadcasted_iota(jnp.int32, (GROUP_W, GROUP_W), 1) // HEAD
    block_diag = zi == zj
    rows2 = lambda top, bottom: jnp.concatenate([top, bottom], axis=0).astype(BF16)

    def recurrence(r_slot):
        chains = [(q, gi) for q in range(n_seq) for gi in range(n_groups)]
        op = lambda i, q, gi: ops_ref[r_slot, i, q * C:(q + 1) * C, gi * GROUP_W:(gi + 1) * GROUP_W]
        kap_t, r_t, v = ([op(i, q, gi) for q, gi in chains] for i in (0, 1, 4))
        kap_s = [stack(x, vec_masks) for x in kap_t]
        v_s = [stack(x, vec_masks) for x in v]
        grams = []
        for i, (q, gi) in enumerate(chains):
            right = jnp.concatenate([stack(op(2, q, gi), vec_masks), stack(op(3, q, gi), vec_masks)], axis=0)
            grams.append(_dot(rows2(kap_t[i], r_t[i]), right, NT))
        yield
        a_k = [jnp.where(strict, gm[:C, :GC], 0.0) for gm in grams]
        a_b = [jnp.where(strict, gm[:C, GC:], 0.0) for gm in grams]
        a_rk = [jnp.where(incl, gm[C:, :GC], 0.0) for gm in grams]
        a_rb = [jnp.where(incl, gm[C:, GC:], 0.0) for gm in grams]

        ps = [-a for a in a_b]
        ts = [eye + p for p in ps]
        ps = [_dot(p.astype(BF16), stack(p, mat_masks)) for p in ps]
        yield
        covered = 2
        while covered < C:
            powers = [stack(p, mat_masks) for p in ps]
            if 2 * covered < C:
                both = [_dot(rows2(t, p), pw) for t, p, pw in zip(ts, ps, powers)]
                ts = [t + x[:C] for t, x in zip(ts, both)]
                ps = [x[C:] for x in both]
            else:
                ts = [t + _dot(t.astype(BF16), pw) for t, pw in zip(ts, powers)]
            covered *= 2
            yield

        av = [_dot(rows2(a, ar), vs) for a, ar, vs in zip(a_k, a_rk, v_s)]
        yield
        tw = [_dot(t.astype(BF16), jnp.concatenate([ks, stack(x[:C], vec_masks)], axis=1))
              for t, ks, x in zip(ts, kap_s, av)]
        yield
        zs = [z_ref[q, gi] for q, gi in chains]
        pz = [_dot(rows2(x[:, :GROUP_W], rt), z.astype(BF16), NT)
              for x, rt, z in zip(tw, r_t, zs)]
        us = [p[:C] + x[:, GROUP_W:] for p, x in zip(pz, tw)]
        yield
        arbu = [_dot(a.astype(BF16), stack(u, vec_masks)) for a, u in zip(a_rb, us)]
        o_blk = [p[C:] + x[C:] - y for p, x, y in zip(pz, av, arbu)]
        for i, (q, gi) in enumerate(chains):
            vu_t = jnp.concatenate([v[i].astype(F32), us[i]], axis=0).T.astype(BF16)
            kb = jnp.concatenate([op(5, q, gi), op(6, q, gi)], axis=0)
            g_all = gall_ref[r_slot, q][:, gi * GROUP_W:(gi + 1) * GROUP_W]
            z_ref[q, gi] = zs[i] * g_all + jnp.where(block_diag, _dot(vu_t, kb), 0.0)
        yield
        o = jnp.concatenate([jnp.concatenate(o_blk[q * n_groups:(q + 1) * n_groups], axis=1)
                             for q in range(n_seq)], axis=0)
        oa_ref[...] = _rwkv_post(o, aux_ref[r_slot, 0], aux_ref[r_slot, 1], w, e_ref
                                 ).astype(oa_ref.dtype).reshape(oa_ref.shape)

    def prepare(q, w_slot):
        rows = slice(q * C, (q + 1) * C)
        pr = pr_ref[q]
        row_id = lax.broadcasted_iota(jnp.int32, pr.shape, 0)
        pr_prev = jnp.where(row_id == 0, carry_ref[q], pltpu.roll(pr, 1, axis=0))
        carry_ref[q] = pr[C - 1:C, :]
        xr = pr + (pr_prev - pr) * w["mu"][...]
        r, kmod, vv, kk, bvec, lw, g = _rwkv_token_prep(xr, w, e_ref)
        aux_ref[w_slot, 0, rows] = _rwkv_bonus(r, kmod, vv, w, e_ref)
        aux_ref[w_slot, 1, rows] = g
        yield
        cum = _mm_exact_lhs(tri_ref[...], lw)
        cum_last = cum[C - 1:C, :]
        g_neg = jnp.exp(-cum)
        g_end = jnp.exp(cum_last - cum)
        prepared = (kk * jnp.exp(cum - lw),
                    r * jnp.exp(cum),
                    kmod * g_neg, bvec * g_neg,
                    vv, kmod * g_end, -(bvec * g_end))
        for i, x in enumerate(prepared):
            ops_ref[w_slot, i, rows] = x.astype(ops_ref.dtype)
        gall_ref[w_slot, q] = jnp.exp(cum_last)

    w_slot = step % 2
    pending = [gen for gen in [prepare(q, w_slot) for q in range(n_seq)] for _ in range(2)]
    stages_left = RECURRENCE_YIELDS_BASE + max(C.bit_length() - 2, 0)
    for _ in recurrence(1 - w_slot):
        for _ in range(-(-len(pending) // max(stages_left, 1))):
            next(pending.pop(0), None)
        stages_left -= 1
    for gen in pending:
        next(gen, None)

    @pl.when(step == n_chunks)
    def _():
        for q in range(n_seq):
            for gi in range(n_groups):
                for h in range(GROUP_HEADS):
                    s1_ref[q, gi * GROUP_HEADS + h] = z_ref[q, gi, h * HEAD:(h + 1) * HEAD, h * HEAD:(h + 1) * HEAD]


_N_OPS = 7
RECURRENCE_YIELDS_BASE = 6


def _rwkv_weight_inputs(wts):
    return [wts[n] for n in _RWKV_W_NAMES]


def _rwkv_weight_specs(wts):
    return [_const_spec(wts[n].shape) for n in _RWKV_W_NAMES]


def _rwkv_chunked(pr3d, shift0, wkv0, wts, e_mat, chunk, n_seq):
    bsz, t, c_shift = pr3d.shape
    heads = wkv0.shape[1]
    width = heads * HEAD
    tri = jnp.tril(jnp.ones((chunk, chunk), F32)).astype(BF16)
    kern = functools.partial(_rwkv_chunk_kernel, chunk=chunk, width=width, n_seq=n_seq)
    n_chunks = t // chunk
    rows = n_seq * chunk
    return pl.pallas_call(
        kern,
        grid=(bsz // n_seq, n_chunks + 1),
        in_specs=[pl.BlockSpec((n_seq, chunk, c_shift), lambda b, s: (b, jnp.minimum(s, n_chunks - 1), 0)),
                  pl.BlockSpec((n_seq, 1, c_shift), lambda b, s: (b, 0, 0)),
                  pl.BlockSpec((n_seq, heads, HEAD, HEAD), lambda b, s: (b, 0, 0, 0))]
                 + _rwkv_weight_specs(wts)
                 + [_const_spec(e_mat.shape), _const_spec(tri.shape)],
        out_specs=[pl.BlockSpec((n_seq, chunk, width), lambda b, s: (b, jnp.maximum(s - 1, 0), 0)),
                   pl.BlockSpec((n_seq, heads, HEAD, HEAD), lambda b, s: (b, 0, 0, 0))],
        out_shape=[jax.ShapeDtypeStruct((bsz, t, width), BF16),
                   jax.ShapeDtypeStruct((bsz, heads, HEAD, HEAD), F32)],
        scratch_shapes=[pltpu.VMEM((n_seq, 1, c_shift), F32),
                        pltpu.VMEM((n_seq, width // GROUP_W, GROUP_W, GROUP_W), F32),
                        pltpu.VMEM((2, _N_OPS, rows, width), BF16),
                        pltpu.VMEM((2, 2, rows, width), F32),
                        pltpu.VMEM((2, n_seq, 1, width), F32)],
        compiler_params=pltpu.CompilerParams(dimension_semantics=("parallel", "arbitrary"),
                                             vmem_limit_bytes=VMEM_LIMIT),
        name="rwkv_chunk",
    )(pr3d, shift0[:, None, :], wkv0, *_rwkv_weight_inputs(wts), e_mat, tri)


_STEP_VECS = 6


def _rwkv_step_kernel(pr_ref, shift0_ref, s_ref, *rest):
    n_w = len(_RWKV_W_NAMES)
    w = dict(zip(_RWKV_W_NAMES, rest[:n_w]))
    e_ref, oa_ref, s1_ref, vec_ref, post_ref, o_ref = rest[n_w:n_w + 6]
    h = pl.program_id(0)

    @pl.when(h == 0)
    def _():
        pr = pr_ref[...]
        xr = pr + (shift0_ref[...] - pr) * w["mu"][...]
        r, kmod, v, kk, bvec, lw, g = _rwkv_token_prep(xr, w, e_ref)
        for i, x in enumerate((kk, bvec, jnp.exp(lw), kmod, r, v)):
            vec_ref[i] = x.T
        post_ref[0] = _rwkv_bonus(r, kmod, v, w, e_ref)
        post_ref[1] = g

    base = pl.multiple_of(h * HEAD, HEAD)
    kk_h, b_h, dec_h, k_h, r_h = [vec_ref[i, pl.ds(base, HEAD), :] for i in range(5)]

    def value_row(vi, carry):
        tile = s_ref[0, vi]
        s_kappa = jnp.sum(tile * kk_h, axis=0, keepdims=True)
        new = tile * dec_h - s_kappa * b_h + vec_ref[5, pl.ds(base + vi, 1), :] * k_h
        s1_ref[0, vi] = new
        o_ref[pl.ds(base + vi, 1), :] = jnp.sum(new * r_h, axis=0, keepdims=True)
        return carry

    lax.fori_loop(0, HEAD, value_row, 0, unroll=4)

    @pl.when(h == pl.num_programs(0) - 1)
    def _():
        oa_ref[...] = _rwkv_post(o_ref[...].T, post_ref[0], post_ref[1], w, e_ref).astype(oa_ref.dtype)


def _rwkv_step(pr2d, shift0, wkv0, wts, e_mat):
    bsz, c_shift = pr2d.shape
    heads = wkv0.shape[1]
    width = heads * HEAD
    full = lambda a: _const_spec(a.shape)
    st = pl.BlockSpec((1, HEAD, HEAD, bsz), lambda hh: (hh, 0, 0, 0))
    oa, s1_t = pl.pallas_call(
        _rwkv_step_kernel,
        grid=(heads,),
        in_specs=[full(pr2d), full(shift0), st] + _rwkv_weight_specs(wts) + [full(e_mat)],
        out_specs=[_const_spec((bsz, width)), st],
        out_shape=[jax.ShapeDtypeStruct((bsz, width), BF16),
                   jax.ShapeDtypeStruct((heads, HEAD, HEAD, bsz), F32)],
        scratch_shapes=[pltpu.VMEM((_STEP_VECS, width, bsz), F32),
                        pltpu.VMEM((2, bsz, width), F32),
                        pltpu.VMEM((width, bsz), F32)],
        compiler_params=pltpu.CompilerParams(dimension_semantics=("arbitrary",), vmem_limit_bytes=VMEM_LIMIT),
        name="rwkv_step",
    )(pr2d, shift0, jnp.transpose(wkv0, (1, 2, 3, 0)), *_rwkv_weight_inputs(wts), e_mat)
    return oa, jnp.transpose(s1_t, (3, 0, 1, 2))


def _s5_disc_kernel(lre_ref, lim_ref, ldt_ref, bre_ref, bim_ref, lbr_ref, lbi_ref, bbr_ref, bbi_ref):
    lam_re, lam_im = lre_ref[...], lim_ref[...]
    dt = jnp.exp(ldt_ref[...])
    mag = jnp.exp(lam_re * dt)
    ang = lam_im * dt
    lb_re, lb_im = mag * jnp.cos(ang), mag * jnp.sin(ang)
    nr, ni = lb_re - 1.0, lb_im
    den = lam_re * lam_re + lam_im * lam_im
    f_re = (nr * lam_re + ni * lam_im) / den
    f_im = (ni * lam_re - nr * lam_im) / den
    b_re, b_im = bre_ref[...], bim_ref[...]
    lbr_ref[...] = lb_re
    lbi_ref[...] = lb_im
    bbr_ref[...] = f_re * b_re - f_im * b_im
    bbi_ref[...] = f_re * b_im + f_im * b_re


def _s5_discretise(lam_re, lam_im, log_dt, b_re_t, b_im_t):
    g, p = lam_re.shape
    full = lambda a: _const_spec(a.shape)
    args = (lam_re[:, None, :], lam_im[:, None, :], log_dt[:, None, None], b_re_t, b_im_t)
    return pl.pallas_call(
        _s5_disc_kernel,
        grid=(1,),
        in_specs=[full(a) for a in args],
        out_specs=[_const_spec((g, 1, p))] * 2 + [full(b_re_t)] * 2,
        out_shape=[jax.ShapeDtypeStruct((g, 1, p), F32)] * 2 + [jax.ShapeDtypeStruct(b_re_t.shape, F32)] * 2,
        name="s5_discretise",
    )(*args)


def _s5_kernel(u_ref, re0_ref, im0_ref, lbr_ref, lbi_ref, wb_ref, wc_ref, d_ref,
               h_ref, re1_ref, im1_ref, u_tm, h_tm, bu0, bu1, xs0, xs1, *, tt, n_slabs):
    t_blk = pl.program_id(1)
    rows = SUBLANES * tt
    s_w = SLAB_GROUPS * S5_STATE
    u_w = SLAB_GROUPS * S5_GROUP
    bu, xs = (bu0, bu1), (xs0, xs1)
    rb = min(rows, S5_ROW_BLOCK)
    steps_per_rb = rb // SUBLANES

    @pl.when(t_blk == 0)
    def _():
        re1_ref[...] = re0_ref[...]
        im1_ref[...] = im0_ref[...]

    u_tm[...] = jnp.swapaxes(u_ref[...], 0, 1).reshape(rows, n_slabs * u_w)

    def project_in(s, j):
        r0 = j * rb
        bu[s % 2][r0:r0 + rb, :] = _mm1(u_tm[r0:r0 + rb, s * u_w:(s + 1) * u_w], wb_ref[s])

    def project_out(s, j):
        r0 = j * rb
        lanes = slice(s * u_w, (s + 1) * u_w)
        h_tm[r0:r0 + rb, lanes] = (_mm1(xs[s % 2][r0:r0 + rb, :], wc_ref[s])
                                   + d_ref[:, lanes] * u_tm[r0:r0 + rb, lanes])

    for j in range(rows // rb):
        project_in(0, j)
    for p in range(n_slabs + 1):
        if p < n_slabs:
            st = slice(p * s_w, (p + 1) * s_w)
            lbr = jnp.broadcast_to(lbr_ref[:, st], (SUBLANES, s_w))
            lbi = jnp.broadcast_to(lbi_ref[:, st], (SUBLANES, s_w))
            xr, xi = re1_ref[:, st], im1_ref[:, st]
        for j in range(rows // rb):
            if p < n_slabs:
                for t in range(j * steps_per_rb, (j + 1) * steps_per_rb):
                    at_t = slice(t * SUBLANES, (t + 1) * SUBLANES)
                    xr, xi = (lbr * xr - lbi * xi + bu[p % 2][at_t, :s_w],
                              lbr * xi + lbi * xr + bu[p % 2][at_t, s_w:])
                    xs[p % 2][at_t, :s_w] = xr
                    xs[p % 2][at_t, s_w:] = xi
            if p + 1 < n_slabs:
                project_in(p + 1, j)
            if p >= 1:
                project_out(p - 1, j)
        if p < n_slabs:
            re1_ref[:, st] = xr
            im1_ref[:, st] = xi
    h = h_tm[...].reshape(tt, SUBLANES, n_slabs * u_w)
    h_ref[...] = jnp.swapaxes(h, 0, 1).astype(h_ref.dtype)


def _s5(u_blocks, re0, im0, lb_re, lb_im, wb, wc, d_skip, tt):
    n_slabs = wb.shape[0]
    n_state = re0.shape[1]
    blk = (SUBLANES, tt, u_blocks.shape[-1])
    grid = (u_blocks.shape[0] // SUBLANES, u_blocks.shape[1] // tt)
    rows = SUBLANES * tt
    u_spec = pl.BlockSpec(blk, lambda i, j: (i, j, 0))
    st_spec = pl.BlockSpec((SUBLANES, n_state), lambda i, j: (i, 0))
    full = lambda a: _const_spec(a.shape)
    return pl.pallas_call(
        functools.partial(_s5_kernel, tt=tt, n_slabs=n_slabs),
        grid=grid,
        in_specs=[u_spec, st_spec, st_spec, full(lb_re), full(lb_im), full(wb), full(wc), full(d_skip)],
        out_specs=[u_spec, st_spec, st_spec],
        out_shape=[jax.ShapeDtypeStruct(u_blocks.shape, F32),
                   jax.ShapeDtypeStruct(re0.shape, F32), jax.ShapeDtypeStruct(im0.shape, F32)],
        scratch_shapes=[pltpu.VMEM((rows, u_blocks.shape[-1]), F32)] * 2
                       + [pltpu.VMEM((rows, 2 * SLAB_GROUPS * S5_STATE), F32)] * 4,
        compiler_params=pltpu.CompilerParams(dimension_semantics=("parallel", "arbitrary"),
                                             vmem_limit_bytes=VMEM_LIMIT),
        name="s5_scan",
    )(u_blocks, re0, im0, lb_re, lb_im, wb, wc, d_skip)


def _s5_step_kernel(u_ref, re0_ref, im0_ref, lbr_ref, lbi_ref, wb_ref, wc_ref, d_ref,
                    h_ref, re1_ref, im1_ref, *, n_slabs):
    s_w = SLAB_GROUPS * S5_STATE
    u_w = SLAB_GROUPS * S5_GROUP
    u = u_ref[...]
    ys = []
    for s in range(n_slabs):
        us = u[:, s * u_w:(s + 1) * u_w]
        bu = _mm1(us, wb_ref[s])
        st = slice(s * s_w, (s + 1) * s_w)
        lbr, lbi = lbr_ref[:, st], lbi_ref[:, st]
        xr, xi = re0_ref[st, :].T, im0_ref[st, :].T
        nr = lbr * xr - lbi * xi + bu[:, :s_w]
        ni = lbr * xi + lbi * xr + bu[:, s_w:]
        re1_ref[st, :] = nr.T
        im1_ref[st, :] = ni.T
        ys.append(_mm1(jnp.concatenate([nr, ni], axis=1), wc_ref[s]) + d_ref[:, s * u_w:(s + 1) * u_w] * us)
    h_ref[...] = jnp.concatenate(ys, axis=1)


def _s5_step(u2d, re0, im0, lb_re, lb_im, wb, wc, d_skip):
    bs, g, p = re0.shape
    to_minor = lambda x: jnp.transpose(x, (1, 2, 0)).reshape(g * p, bs)
    from_minor = lambda x: jnp.transpose(x.reshape(g, p, bs), (2, 0, 1))
    y, re1, im1 = _s5_step_call(u2d, to_minor(re0), to_minor(im0), lb_re, lb_im, wb, wc, d_skip)
    return y, from_minor(re1), from_minor(im1)


def _s5_step_call(u2d, re0, im0, lb_re, lb_im, wb, wc, d_skip):
    full = lambda a: _const_spec(a.shape)
    args = (u2d, re0, im0, lb_re, lb_im, wb, wc, d_skip)
    return pl.pallas_call(
        functools.partial(_s5_step_kernel, n_slabs=wb.shape[0]),
        grid=(1,),
        in_specs=[full(a) for a in args],
        out_specs=[full(u2d), full(re0), full(im0)],
        out_shape=[jax.ShapeDtypeStruct(u2d.shape, F32),
                   jax.ShapeDtypeStruct(re0.shape, F32), jax.ShapeDtypeStruct(im0.shape, F32)],
        compiler_params=pltpu.CompilerParams(vmem_limit_bytes=VMEM_LIMIT),
        name="s5_step",
    )(*args)


def _block_diag_slabs(m):
    g, a, b = m.shape
    eye = jnp.eye(SLAB_GROUPS, dtype=m.dtype)
    m4 = m.reshape(g // SLAB_GROUPS, SLAB_GROUPS, a, b)
    return jnp.einsum("sgab,gh->sgahb", m4, eye).reshape(g // SLAB_GROUPS, SLAB_GROUPS * a, SLAB_GROUPS * b)


_TAIL_W_NAMES = ("wro", "w1", "b1", "w2", "b2", "wmo", "npm", "nf", "npf", "wg", "wu", "wd")


def _tail_kernel(*refs):
    n_act = 4
    main_in, side_in = refs[:n_act], refs[n_act:2 * n_act]
    wts = dict(zip(_TAIL_W_NAMES, refs[2 * n_act:2 * n_act + len(_TAIL_W_NAMES)]))
    y_ref, ys_ref = refs[-2:]

    def tail(x_ref, oa_ref, y_s5_ref, gt_ref, y_ref):
        d = x_ref.shape[-1]
        tm = x_ref.shape[0]
        n_sub = max(1, tm // TAIL_SUB_ROWS)
        subs = [slice(i * (tm // n_sub), (i + 1) * (tm // n_sub)) for i in range(n_sub)]
        hg = [_gelu_tanh(y_s5_ref[s, :]).astype(BF16) for s in subs]
        a_out = [_dot(oa_ref[s, :], wts["wro"][...]) for s in subs]
        b_lin = [_dot(h, wts["w1"][...]) + wts["b1"][...] for h in hg]
        b_gate = [_dot(h, wts["w2"][...]) + wts["b2"][...] for h in hg]
        merged = [(gt_ref[s, :d].astype(F32) * a + gt_ref[s, d:].astype(F32) * (bl * _sigmoid(bg))).astype(BF16)
                  for s, a, bl, bg in zip(subs, a_out, b_lin, b_gate)]
        mix = [_dot(m, wts["wmo"][...]) for m in merged]
        x1 = [x_ref[s, :] + _rms(m, wts["npm"][...]) for s, m in zip(subs, mix)]
        hb = [_rms(x, wts["nf"][...]).astype(BF16) for x in x1]
        gate = [_dot(h, wts["wg"][...]) for h in hb]
        up = [_dot(h, wts["wu"][...]) for h in hb]
        act = [(g * _sigmoid(g) * u).astype(BF16) for g, u in zip(gate, up)]
        f = [_dot(a, wts["wd"][...]) for a in act]
        for s, x, ff in zip(subs, x1, f):
            y_ref[s, :] = x + _rms(ff, wts["npf"][...])

    _on_group(tail, (*main_in, y_ref), (*side_in, ys_ref))


def _tail(acts_main, acts_side, tw, tm):
    d = acts_main[0].shape[1]
    n_main, main_spec = _two_group_grid(acts_main[0].shape[0], tm)
    wargs = [tw[n] for n in _TAIL_W_NAMES]
    return pl.pallas_call(
        _tail_kernel,
        grid=(n_main + 1,),
        in_specs=[main_spec(a.shape[1]) for a in acts_main] + [_const_spec(a.shape) for a in acts_side]
                 + [_const_spec(a.shape, True) for a in wargs],
        out_specs=[main_spec(d), _const_spec((acts_side[0].shape[0], d))],
        out_shape=[jax.ShapeDtypeStruct((a[0].shape[0], d), F32) for a in (acts_main, acts_side)],
        compiler_params=pltpu.CompilerParams(dimension_semantics=("arbitrary",),
                                             vmem_limit_bytes=VMEM_LIMIT),
        name="tail",
    )(*acts_main, *acts_side, *wargs)


def _layer(x_p, x_s, shift0, wkv0, re0, im0, lw, *, chunk, n_seq, s5_tt, row_tile, proj_tile):
    bsz, t, d = x_p.shape
    bs = x_s.shape[0]
    assert x_s.shape[1] == 1 and (bsz * t) % row_tile == 0 and (bsz * t) % proj_tile == 0
    heads = wkv0.shape[1]
    c_shift = shift0.shape[-1]
    c_u = lw["d_skip"].shape[-1]
    n_state = re0.shape[1] * re0.shape[2]
    s5_w = (lw["lb_re"], lw["lb_im"], lw["wb"], lw["wc"], lw["d_skip"])
    xp2d, xs2d = x_p.reshape(bsz * t, d), x_s.reshape(bs, d)
    (pr_p, u_p, gates_p), (pr_s, u_s, gates_s) = _proj(xp2d, xs2d, lw["norm_pre_mix"], lw["w_in"],
                                                       c_shift, c_u, proj_tile)

    pr_p3 = pr_p.reshape(bsz, t, c_shift)
    oa_p, wkv_p = _rwkv_chunked(pr_p3, jnp.zeros((bsz, c_shift), F32), jnp.zeros((bsz, heads, HEAD, HEAD), F32),
                                lw["rwkv"], lw["e_mat"], chunk, n_seq)
    zeros_state = jnp.zeros((bsz, n_state), F32)
    hg_p, re_p, im_p = _s5(u_p.reshape(bsz, t, c_u), zeros_state, zeros_state, *s5_w, tt=s5_tt)

    oa_s, wkv_s = _rwkv_step(pr_s, shift0, wkv0, lw["rwkv"], lw["e_mat"])
    hg_s, re_s, im_s = _s5_step(u_s, re0, im0, *s5_w)

    y_p, y_s = _tail((xp2d, oa_p.reshape(bsz * t, -1), hg_p.reshape(bsz * t, c_u), gates_p),
                     (xs2d, oa_s, hg_s, gates_s), lw["tail"], row_tile)
    st_shape = lambda n: (n,) + re0.shape[1:]
    return ((y_p.reshape(x_p.shape), pr_p3[:, -1], wkv_p, re_p.reshape(st_shape(bsz)), im_p.reshape(st_shape(bsz))),
            (y_s.reshape(x_s.shape), pr_s, wkv_s, re_s.reshape(st_shape(bs)), im_s.reshape(st_shape(bs))))


def _prepare_layer_weights(l, p):
    row = lambda a: a[l][None, :].astype(F32)
    width = p["w0"].shape[-1]
    n_dec, n_aaa, n_gate = p["w_decay_up"].shape[1], p["w_aaa_up"].shape[1], p["w_gate_up"].shape[1]
    assert n_dec + n_aaa + n_gate == LORA_PAD

    def lora_pad(wup, start):
        return jnp.zeros((LORA_PAD, width), F32).at[start:start + wup.shape[0]].set(wup).astype(BF16)

    rwkv = {
        "mu": row(p["mu_shift"]), "w0": row(p["w0"]), "a0": row(p["a0"]), "k_k": row(p["k_k"]),
        "k_a": row(p["k_a"]), "r_k": row(p["r_k"]), "lnx_g": row(p["lnx_g"]), "lnx_b": row(p["lnx_b"]),
        "wd": lora_pad(p["w_decay_up"][l], 0),
        "wa": lora_pad(p["w_aaa_up"][l], n_dec),
        "wg": lora_pad(p["w_gate_up"][l], n_dec + n_aaa),
    }
    head_id = jnp.arange(MXU_DIM) // HEAD
    e_mat = (head_id[:, None] == head_id[None, :]).astype(BF16)

    lb_re, lb_im, bb_re_t, bb_im_t = _s5_discretise(
        p["s5_lam_re"][l], p["s5_lam_im"][l], p["s5_log_dt"][l],
        jnp.swapaxes(p["s5_b_re"][l], 1, 2), jnp.swapaxes(p["s5_b_im"][l], 1, 2))
    n_state = lb_re.shape[0] * lb_re.shape[2]
    to_out = lambda cc: _block_diag_slabs(jnp.swapaxes(cc, 1, 2))
    wb = jnp.concatenate([_block_diag_slabs(bb_re_t), _block_diag_slabs(bb_im_t)], axis=-1).astype(BF16)
    wc = jnp.concatenate([to_out(p["s5_c_re"][l]), -to_out(p["s5_c_im"][l])], axis=1).astype(BF16)

    bf = lambda a: a[l].astype(BF16)
    tail = {
        "wro": bf(p["w_rwkv_out"]), "w1": bf(p["glu_w1"]), "b1": row(p["glu_b1"]), "w2": bf(p["glu_w2"]),
        "b2": row(p["glu_b2"]), "wmo": bf(p["w_merge_out"]), "npm": row(p["norm_post_mix"]),
        "nf": row(p["norm_pre_ffn"]), "npf": row(p["norm_post_ffn"]),
        "wg": bf(p["w_ffn_gate"]), "wu": bf(p["w_ffn_up"]), "wd": bf(p["w_ffn_down"]),
    }
    return {
        "norm_pre_mix": row(p["norm_pre_mix"]), "w_in": bf(p["w_in"]), "rwkv": rwkv, "e_mat": e_mat,
        "lb_re": lb_re.reshape(1, n_state), "lb_im": lb_im.reshape(1, n_state), "wb": wb, "wc": wc,
        "d_skip": row(p["s5_d"]), "tail": tail,
    }


_PARAM_NAMES = ("norm_pre_mix", "norm_post_mix", "norm_pre_ffn", "norm_post_ffn", "w_in", "mu_shift",
                "w0", "w_decay_up", "a0", "w_aaa_up", "w_gate_up", "k_k", "k_a", "r_k", "lnx_g", "lnx_b",
                "w_rwkv_out", "s5_lam_re", "s5_lam_im", "s5_log_dt", "s5_b_re", "s5_b_im", "s5_c_re",
                "s5_c_im", "s5_d", "glu_w1", "glu_b1", "glu_w2", "glu_b2", "w_merge_out",
                "w_ffn_gate", "w_ffn_up", "w_ffn_down")


def _forward(x_prompt, x_sample, state_shift, state_wkv, state_s5_re, state_s5_im, params,
             *, chunk=64, n_seq=8, s5_tt=128, row_tile=512, proj_tile=1024):
    depth = params["w_in"].shape[0]
    yp, ys = x_prompt, x_sample
    outs_p, outs_s = [], []
    for l in range(depth):
        lw = _prepare_layer_weights(l, params)
        (yp, *st_p), (ys, *st_s) = _layer(yp, ys, state_shift[l], state_wkv[l], state_s5_re[l], state_s5_im[l], lw,
                                          chunk=chunk, n_seq=n_seq, s5_tt=s5_tt, row_tile=row_tile, proj_tile=proj_tile)
        outs_p.append(st_p)
        outs_s.append(st_s)
    stack = lambda outs, i, dt: jnp.stack([o[i] for o in outs]).astype(dt)
    dt_p, dt_s = x_prompt.dtype, x_sample.dtype
    return (yp, ys,
            stack(outs_p, 0, dt_p), stack(outs_p, 1, dt_p), stack(outs_p, 2, dt_p), stack(outs_p, 3, dt_p),
            stack(outs_s, 0, dt_s), stack(outs_s, 1, dt_s), stack(outs_s, 2, dt_s), stack(outs_s, 3, dt_s))


def kernel(x_prompt, x_sample, state_shift, state_wkv, state_s5_re, state_s5_im, norm_pre_mix, norm_post_mix, norm_pre_ffn, norm_post_ffn, w_in, mu_shift, w0, w_decay_up, a0, w_aaa_up, w_gate_up, k_k, k_a, r_k, lnx_g, lnx_b, w_rwkv_out, s5_lam_re, s5_lam_im, s5_log_dt, s5_b_re, s5_b_im, s5_c_re, s5_c_im, s5_d, glu_w1, glu_b1, glu_w2, glu_b2, w_merge_out, w_ffn_gate, w_ffn_up, w_ffn_down):
    params = dict(zip(_PARAM_NAMES, (norm_pre_mix, norm_post_mix, norm_pre_ffn, norm_post_ffn, w_in, mu_shift,
                                     w0, w_decay_up, a0, w_aaa_up, w_gate_up, k_k, k_a, r_k, lnx_g, lnx_b,
                                     w_rwkv_out, s5_lam_re, s5_lam_im, s5_log_dt, s5_b_re, s5_b_im, s5_c_re,
                                     s5_c_im, s5_d, glu_w1, glu_b1, glu_w2, glu_b2, w_merge_out,
                                     w_ffn_gate, w_ffn_up, w_ffn_down)))
    return _forward(x_prompt, x_sample, state_shift, state_wkv, state_s5_re, state_s5_im, params)
```
